```python
import math
import jax, jax.numpy as jnp
from jax import lax
import numpy as np

D_MODEL = 1024
BATCH = 16
SEQ = 256
DEPTH = 4
DEC_BATCH = 2
DEC_SEQ = 2048
PAST_LEN = 256

GRID_W = 64
Q_BLOCK = 128
ROPE_THETA = 10000.0
EPS = 1e-6
N_EVEN = (DEPTH + 1) // 2
N_ODD = DEPTH // 2

MLA_HEADS = 8
MLA_NOPE = 64
MLA_ROPE = 32
MLA_V = 64
MLA_QK = MLA_NOPE + MLA_ROPE
Q_LORA = 256
KV_LORA = 128
MLA_WIDTH = MLA_HEADS * MLA_V
DIFF_HEADS = 4
DIFF_HD = 64
DIFF_WIDTH = DIFF_HEADS * 2 * DIFF_HD
NA_HEADS = 8
NA_HD = 64
NA_ROWS = 8
NA_COLS = 16
NA_QCOLS = 16
NA_BAND = 2 * NA_COLS
NA_WIDTH = NA_HEADS * NA_HD
GQA_HEADS = 8
GQA_KV = 2
GQA_GROUP = GQA_HEADS // GQA_KV
GQA_HD = 64
GQA_WIDTH = GQA_HEADS * GQA_HD

EVEN_SPLITS = (Q_LORA, KV_LORA + MLA_ROPE, MLA_WIDTH, DIFF_WIDTH, DIFF_WIDTH, DIFF_WIDTH, DIFF_WIDTH)
EVEN_IN = sum(EVEN_SPLITS)
EVEN_OUT = MLA_WIDTH + DIFF_WIDTH
ODD_SPLITS = (3 * NA_WIDTH, NA_WIDTH, GQA_WIDTH, 2 * GQA_KV * GQA_HD, GQA_WIDTH)
ODD_IN = sum(ODD_SPLITS)
ODD_OUT = NA_WIDTH + GQA_WIDTH

kernel_name = 'hybrid_mla_diff_natten_gqa_prefix_dit_step'


def _split(u, sizes):
    idx = [int(v) for v in np.cumsum(sizes)[:-1]]
    return jnp.split(u, idx, axis=-1)


def rmsnorm(x, g):
    x32 = x.astype(jnp.float32)
    y = x32 * lax.rsqrt(jnp.mean(x32 * x32, axis=-1, keepdims=True) + EPS)
    return (y * g.astype(jnp.float32)).astype(x.dtype)


def axial_rope(T, rot_dim):
    pos = jnp.arange(T, dtype=jnp.int32)
    row = (pos // GRID_W).astype(jnp.float32)
    col = (pos % GRID_W).astype(jnp.float32)
    n = rot_dim // 2
    inv = ROPE_THETA ** (-jnp.arange(0, n, 2, dtype=jnp.float32) / n)
    ang = jnp.concatenate([row[:, None] * inv, col[:, None] * inv], axis=-1)
    return jnp.cos(ang), jnp.sin(ang)


def apply_rope(x, cs):
    cos, sin = cs
    shp = (1, cos.shape[0]) + (1,) * (x.ndim - 3) + (cos.shape[-1],)
    cos = cos.reshape(shp)
    sin = sin.reshape(shp)
    x32 = x.astype(jnp.float32)
    half = x.shape[-1] // 2
    x1, x2 = x32[..., :half], x32[..., half:]
    return jnp.concatenate([x1 * cos - x2 * sin, x2 * cos + x1 * sin], axis=-1).astype(x.dtype)


def attend(q, k, v, scale):
    B, Tq, Hk, G, d = q.shape
    nb = Tq // Q_BLOCK
    qb = q.reshape(B, nb, Q_BLOCK, Hk, G, d).swapaxes(0, 1)
    k32 = k.astype(jnp.float32)
    v32 = v.astype(jnp.float32)

    def one(qblk):
        s = jnp.einsum('bqhgd,bkhd->bhgqk', qblk.astype(jnp.float32), k32) * scale
        p = jax.nn.softmax(s, axis=-1)
        return jnp.einsum('bhgqk,bkhe->bqhge', p, v32)

    o = lax.map(one, qb)
    return o.swapaxes(0, 1).reshape(B, Tq, Hk, G, v.shape[-1]).astype(q.dtype)


def na_attend(q, k, v, ck, cv, rpb):
    B, T, H, d = q.shape
    rows = T // GRID_W
    kr = min(NA_ROWS, rows)
    ncb = GRID_W // NA_QCOLS
    scale = d ** -0.5
    qg = q.reshape(B, rows, GRID_W, H, d)
    kg = k.reshape(B, rows, GRID_W, H, d)
    vg = v.reshape(B, rows, GRID_W, H, d)
    row_start = jnp.clip(jnp.arange(rows) - kr // 2, 0, rows - kr)
    qc0 = np.arange(ncb) * NA_QCOLS
    band_start = np.clip(qc0 - NA_COLS // 2, 0, GRID_W - NA_BAND)
    band_idx = band_start[:, None] + np.arange(NA_BAND)
    qcols = qc0[:, None] + np.arange(NA_QCOLS)
    col_start = np.clip(qcols - NA_COLS // 2, 0, GRID_W - NA_COLS)
    kcol = band_idx[:, None, :]
    col_valid = (kcol >= col_start[:, :, None]) & (kcol < col_start[:, :, None] + NA_COLS)
    dc_idx = np.clip(kcol - qcols[:, :, None] + NA_COLS - 1, 0, 2 * NA_COLS - 2)
    mask = jnp.asarray(np.broadcast_to(col_valid[:, :, None, :], (ncb, NA_QCOLS, kr, NA_BAND)).reshape(ncb, NA_QCOLS, kr * NA_BAND))
    rpb_c = rpb.astype(jnp.float32)[:, :, dc_idx]
    ck32 = ck.astype(jnp.float32)
    cv32 = cv.astype(jnp.float32)
    nloc = kr * NA_BAND

    def one_row(args):
        q_row, r = args
        rs = row_start[r]
        kw = lax.dynamic_slice_in_dim(kg, rs, kr, axis=1)
        vw = lax.dynamic_slice_in_dim(vg, rs, kr, axis=1)
        kb = kw[:, :, band_idx].transpose(0, 2, 1, 3, 4, 5).reshape(B, ncb, nloc, H, d).astype(jnp.float32)
        vb = vw[:, :, band_idx].transpose(0, 2, 1, 3, 4, 5).reshape(B, ncb, nloc, H, d).astype(jnp.float32)
        dr_idx = rs + jnp.arange(kr) - r + NA_ROWS - 1
        bias = rpb_c[:, dr_idx].transpose(0, 2, 3, 1, 4).reshape(H, ncb, NA_QCOLS, nloc)
        qr = q_row.reshape(B, ncb, NA_QCOLS, H, d).astype(jnp.float32)
        s_loc = jnp.einsum('bnqhd,bnkhd->bhnqk', qr, kb) * scale + bias[None]
        s_loc = jnp.where(mask[None, None], s_loc, -1e30)
        s_ctx = jnp.einsum('bnqhd,bkhd->bhnqk', qr, ck32) * scale
        p = jax.nn.softmax(jnp.concatenate([s_loc, s_ctx], axis=-1), axis=-1)
        o = jnp.einsum('bhnqk,bnkhd->bnqhd', p[..., :nloc], vb) + jnp.einsum('bhnqk,bkhd->bnqhd', p[..., nloc:], cv32)
        return o.reshape(B, GRID_W, H, d)

    out = lax.map(one_row, (qg.swapaxes(0, 1), jnp.arange(rows)))
    return out.swapaxes(0, 1).reshape(B, T, H, d).astype(q.dtype)


def even_mixers(h, i, lam_init, P, rope, cache):
    B, T, _ = h.shape
    u = h @ P['w_in_even'][i]
    qa, kva, mgate, dq, dk, dv, dgate = _split(u, EVEN_SPLITS)
    q = (rmsnorm(qa, P['mla_qa_norm'][i]) @ P['mla_wqb'][i]).reshape(B, T, MLA_HEADS, MLA_QK)
    q_nope = rmsnorm(q[..., :MLA_NOPE], P['mla_qn_nope'][i])
    q_pe = rmsnorm(q[..., MLA_NOPE:], P['mla_qn_rope'][i])
    c_kv = rmsnorm(kva[..., :KV_LORA], P['mla_kva_norm'][i])
    k_pe = rmsnorm(kva[..., KV_LORA:], P['mla_kn_rope'][i])
    state = [c_kv, k_pe]
    k_pe_use = k_pe
    if rope is not None:
        q_pe = apply_rope(q_pe, rope['mla'])
        k_pe_use = apply_rope(k_pe[:, :, None, :], rope['mla'])[:, :, 0]
    c_all, kpe_all = c_kv, k_pe_use
    if cache is not None:
        c_all = jnp.concatenate([c_kv, cache[0]], axis=1)
        kpe_all = jnp.concatenate([k_pe_use, cache[1]], axis=1)
    Tk = c_all.shape[1]
    kv = (c_all @ P['mla_wkvb'][i]).reshape(B, Tk, MLA_HEADS, MLA_NOPE + MLA_V)
    k_nope = rmsnorm(kv[..., :MLA_NOPE], P['mla_kn_nope'][i])
    v_m = kv[..., MLA_NOPE:]
    k_m = jnp.concatenate([k_nope, jnp.broadcast_to(kpe_all[:, :, None, :], (B, Tk, MLA_HEADS, MLA_ROPE))], axis=-1)
    q_m = jnp.concatenate([q_nope, q_pe], axis=-1)[:, :, :, None, :]
    o_mla = attend(q_m, k_m, v_m, MLA_QK ** -0.5).reshape(B, T, MLA_WIDTH) * jax.nn.silu(mgate)
    dq = rmsnorm(dq.reshape(B, T, DIFF_HEADS, 2, DIFF_HD), P['diff_qn'][i])
    dk = rmsnorm(dk.reshape(B, T, DIFF_HEADS, 2, DIFF_HD), P['diff_kn'][i])
    dv = dv.reshape(B, T, DIFF_HEADS, 2 * DIFF_HD)
    state += [dk, dv]
    dk_use = dk
    if rope is not None:
        dq = apply_rope(dq, rope['diff'])
        dk_use = apply_rope(dk, rope['diff'])
    dk_all, dv_all = dk_use, dv
    if cache is not None:
        dk_all = jnp.concatenate([dk_use, cache[2]], axis=1)
        dv_all = jnp.concatenate([dv, cache[3]], axis=1)
    f32 = jnp.float32
    lam = (jnp.exp(jnp.sum(P['diff_lq1'][i].astype(f32) * P['diff_lk1'][i].astype(f32)))
           - jnp.exp(jnp.sum(P['diff_lq2'][i].astype(f32) * P['diff_lk2'][i].astype(f32))) + lam_init)
    sc = DIFF_HD ** -0.5
    a1 = attend(dq[:, :, :, 0:1], dk_all[:, :, :, 0], dv_all, sc)
    a2 = attend(dq[:, :, :, 1:2], dk_all[:, :, :, 1], dv_all, sc)
    o_d = rmsnorm((a1 - lam.astype(a1.dtype) * a2)[:, :, :, 0], P['diff_subln'][i]) * (1.0 - lam_init)
    o_diff = o_d.reshape(B, T, DIFF_WIDTH) * jax.nn.silu(dgate)
    out = jnp.concatenate([o_mla, o_diff], axis=-1) @ P['w_out_even'][i]
    return out, state


def odd_mixers(h, i, P, rope, cache):
    B, T, _ = h.shape
    u = h @ P['w_in_odd'][i]
    naqkv, nagate, gq, gkv, ggate = _split(u, ODD_SPLITS)
    naqkv = naqkv.reshape(B, T, 3, NA_HEADS, NA_HD)
    nq = rmsnorm(naqkv[:, :, 0], P['na_qn'][i])
    nk = rmsnorm(naqkv[:, :, 1], P['na_kn'][i])
    nv = naqkv[:, :, 2]
    state = [nk, nv]
    if cache is None:
        o_na = attend(nq[:, :, :, None], nk, nv, NA_HD ** -0.5)[:, :, :, 0]
    else:
        o_na = na_attend(nq, nk, nv, cache[0], cache[1], P['na_rpb'][i])
    o_na = o_na.reshape(B, T, NA_WIDTH) * jax.nn.silu(nagate)
    gq = rmsnorm(gq.reshape(B, T, GQA_KV, GQA_GROUP, GQA_HD), P['gqa_qn'][i])
    gkv = gkv.reshape(B, T, 2, GQA_KV, GQA_HD)
    gk = rmsnorm(gkv[:, :, 0], P['gqa_kn'][i])
    gv = gkv[:, :, 1]
    state += [gk, gv]
    gk_use = gk
    if rope is not None:
        gq = apply_rope(gq, rope['gqa'])
        gk_use = apply_rope(gk, rope['gqa'])
    gk_all, gv_all = gk_use, gv
    if cache is not None:
        gk_all = jnp.concatenate([gk_use, cache[2]], axis=1)
        gv_all = jnp.concatenate([gv, cache[3]], axis=1)
    o_g = attend(gq, gk_all, gv_all, GQA_HD ** -0.5).reshape(B, T, GQA_WIDTH) * jax.nn.silu(ggate)
    out = jnp.concatenate([o_na, o_g], axis=-1) @ P['w_out_odd'][i]
    return out, state


def block(x, cvec, l, P, rope, cache):
    mod = jax.nn.silu(cvec) @ P['w_mod'][l] + P['b_mod'][l]
    shift, scale, gate = jnp.split(mod[:, None, :], 3, axis=-1)
    h = rmsnorm(x, P['norm_w'][l]) * (1.0 + scale) + shift
    if l % 2 == 0:
        lam_init = 0.8 - 0.6 * math.exp(-0.3 * l)
        o, st = even_mixers(h, l // 2, lam_init, P, rope, cache)
    else:
        o, st = odd_mixers(h, l // 2, P, rope, cache)
    return x + gate * o, st


def setup_inputs(seed: int = 0) -> dict:
    key = jax.random.key(seed)
    ks = iter(jax.random.split(key, 48))
    f32 = jnp.float32

    def nrm(shape, scale=1.0):
        return jax.random.normal(next(ks), shape, f32) * scale

    def gain(shape):
        return 1.0 + nrm(shape, 0.05)

    return {
        'x_prompt': nrm((BATCH, SEQ, D_MODEL)),
        'x_sample': nrm((DEC_BATCH, DEC_SEQ, D_MODEL)),
        'cache_mla_ckv': nrm((DEC_BATCH, N_EVEN, PAST_LEN, KV_LORA)),
        'cache_mla_kpe': nrm((DEC_BATCH, N_EVEN, PAST_LEN, MLA_ROPE)),
        'cache_diff_k': nrm((DEC_BATCH, N_EVEN, PAST_LEN, DIFF_HEADS, 2, DIFF_HD)),
        'cache_diff_v': nrm((DEC_BATCH, N_EVEN, PAST_LEN, DIFF_HEADS, 2 * DIFF_HD)),
        'cache_na_k': nrm((DEC_BATCH, N_ODD, PAST_LEN, NA_HEADS, NA_HD)),
        'cache_na_v': nrm((DEC_BATCH, N_ODD, PAST_LEN, NA_HEADS, NA_HD)),
        'cache_gqa_k': nrm((DEC_BATCH, N_ODD, PAST_LEN, GQA_KV, GQA_HD)),
        'cache_gqa_v': nrm((DEC_BATCH, N_ODD, PAST_LEN, GQA_KV, GQA_HD)),
        'c': nrm((DEC_BATCH, D_MODEL)),
        'c_ctx': nrm((D_MODEL,)),
        'norm_w': gain((DEPTH, D_MODEL)),
        'w_mod': nrm((DEPTH, D_MODEL, 3 * D_MODEL), 0.5 * D_MODEL ** -0.5),
        'b_mod': nrm((DEPTH, 3 * D_MODEL), 0.01),
        'w_in_even': nrm((N_EVEN, D_MODEL, EVEN_IN), D_MODEL ** -0.5),
        'w_out_even': nrm((N_EVEN, EVEN_OUT, D_MODEL), EVEN_OUT ** -0.5),
        'mla_qa_norm': gain((N_EVEN, Q_LORA)),
        'mla_wqb': nrm((N_EVEN, Q_LORA, MLA_HEADS * MLA_QK), Q_LORA ** -0.5),
        'mla_kva_norm': gain((N_EVEN, KV_LORA)),
        'mla_wkvb': nrm((N_EVEN, KV_LORA, MLA_HEADS * (MLA_NOPE + MLA_V)), KV_LORA ** -0.5),
        'mla_qn_nope': gain((N_EVEN, MLA_NOPE)),
        'mla_qn_rope': gain((N_EVEN, MLA_ROPE)),
        'mla_kn_nope': gain((N_EVEN, MLA_NOPE)),
        'mla_kn_rope': gain((N_EVEN, MLA_ROPE)),
        'diff_qn': gain((N_EVEN, DIFF_HD)),
        'diff_kn': gain((N_EVEN, DIFF_HD)),
        'diff_lq1': nrm((N_EVEN, DIFF_HD), 0.1),
        'diff_lk1': nrm((N_EVEN, DIFF_HD), 0.1),
        'diff_lq2': nrm((N_EVEN, DIFF_HD), 0.1),
        'diff_lk2': nrm((N_EVEN, DIFF_HD), 0.1),
        'diff_subln': gain((N_EVEN, 2 * DIFF_HD)),
        'w_in_odd': nrm((N_ODD, D_MODEL, ODD_IN), D_MODEL ** -0.5),
        'w_out_odd': nrm((N_ODD, ODD_OUT, D_MODEL), ODD_OUT ** -0.5),
        'na_qn': gain((N_ODD, NA_HD)),
        'na_kn': gain((N_ODD, NA_HD)),
        'na_rpb': nrm((N_ODD, NA_HEADS, 2 * NA_ROWS - 1, 2 * NA_COLS - 1), 0.1),
        'gqa_qn': gain((N_ODD, GQA_HD)),
        'gqa_kn': gain((N_ODD, GQA_HD)),
    }


def reference(x_prompt, x_sample, cache_mla_ckv, cache_mla_kpe, cache_diff_k, cache_diff_v,
              cache_na_k, cache_na_v, cache_gqa_k, cache_gqa_v, c, c_ctx,
              norm_w, w_mod, b_mod, w_in_even, w_out_even, mla_qa_norm, mla_wqb, mla_kva_norm, mla_wkvb,
              mla_qn_nope, mla_qn_rope, mla_kn_nope, mla_kn_rope, diff_qn, diff_kn,
              diff_lq1, diff_lk1, diff_lq2, diff_lk2, diff_subln, w_in_odd, w_out_odd,
              na_qn, na_kn, na_rpb, gqa_qn, gqa_kn):
    P = dict(norm_w=norm_w, w_mod=w_mod, b_mod=b_mod, w_in_even=w_in_even, w_out_even=w_out_even,
             mla_qa_norm=mla_qa_norm, mla_wqb=mla_wqb, mla_kva_norm=mla_kva_norm, mla_wkvb=mla_wkvb,
             mla_qn_nope=mla_qn_nope, mla_qn_rope=mla_qn_rope, mla_kn_nope=mla_kn_nope, mla_kn_rope=mla_kn_rope,
             diff_qn=diff_qn, diff_kn=diff_kn, diff_lq1=diff_lq1, diff_lk1=diff_lk1, diff_lq2=diff_lq2,
             diff_lk2=diff_lk2, diff_subln=diff_subln, w_in_odd=w_in_odd, w_out_odd=w_out_odd,
             na_qn=na_qn, na_kn=na_kn, na_rpb=na_rpb, gqa_qn=gqa_qn, gqa_kn=gqa_kn)

    ev_states = [[], [], [], []]
    od_states = [[], [], [], []]
    x = x_prompt
    c_prompt = c_ctx[None, :]
    for l in range(DEPTH):
        x, st = block(x, c_prompt, l, P, None, None)
        target = ev_states if l % 2 == 0 else od_states
        for j in range(4):
            target[j].append(st[j])
    y_prompt = x

    T = x_sample.shape[1]
    rope = {'mla': axial_rope(T, MLA_ROPE), 'diff': axial_rope(T, DIFF_HD), 'gqa': axial_rope(T, GQA_HD)}
    x = x_sample
    for l in range(DEPTH):
        i = l // 2
        if l % 2 == 0:
            cache = (cache_mla_ckv[:, i], cache_mla_kpe[:, i], cache_diff_k[:, i], cache_diff_v[:, i])
        else:
            cache = (cache_na_k[:, i], cache_na_v[:, i], cache_gqa_k[:, i], cache_gqa_v[:, i])
        x, _ = block(x, c, l, P, rope, cache)
    y_sample = x

    new_mla_ckv = jnp.stack(ev_states[0], axis=1)
    new_mla_kpe = jnp.stack(ev_states[1], axis=1)
    new_diff_k = jnp.stack(ev_states[2], axis=1)
    new_diff_v = jnp.stack(ev_states[3], axis=1)
    new_na_k = jnp.stack(od_states[0], axis=1)
    new_na_v = jnp.stack(od_states[1], axis=1)
    new_gqa_k = jnp.stack(od_states[2], axis=1)
    new_gqa_v = jnp.stack(od_states[3], axis=1)
    return (y_prompt, y_sample, new_mla_ckv, new_mla_kpe, new_diff_k, new_diff_v, new_na_k, new_na_v, new_gqa_k, new_gqa_v)
```

```python
import functools
import math

import jax
import jax.numpy as jnp
import numpy as np
from jax import lax
from jax.experimental import pallas as pl
from jax.experimental.pallas import tpu as pltpu

F32 = jnp.float32
BF16 = jnp.bfloat16

D_MODEL = 1024
DEPTH = 4
GRID_W = 64
ROPE_THETA = 10000.0
EPS = 1e-6
MLA_HEADS = 8
MLA_NOPE = 64
MLA_ROPE = 32
MLA_QK = MLA_NOPE + MLA_ROPE
Q_LORA = 256
KV_LORA = 128
DIFF_HEADS = 4
DIFF_HD = 64
NA_HEADS = 8
NA_HD = 64
NA_ROWS = 8
NA_COLS = 16
GQA_HEADS = 8
GQA_KV = 2
GQA_HD = 64
LANES = 128
NA_GROUP_ROWS = 4
NA_WIN_ROWS = NA_ROWS + NA_GROUP_ROWS
NEG = -1e30
VMEM_LIMIT = 48 * 1024 * 1024
GQA_PERM = (0, 4, 1, 5, 2, 6, 3, 7)

ROW_TILE = 512
Q_BLOCK = 256


def _cparams():
    return pltpu.CompilerParams(vmem_limit_bytes=VMEM_LIMIT)


def _dot(a, b):
    return jnp.dot(a, b, preferred_element_type=F32)


def _dot_nt(a, b):
    return lax.dot_general(a, b, (((1,), (1,)), ((), ())), preferred_element_type=F32)


def _rms_full(x, g):
    ms = jnp.mean(x * x, axis=-1, keepdims=True)
    return x * lax.rsqrt(ms + EPS) * g


def _rms_group(x, bd, g):
    ms = _dot((x * x).astype(BF16), bd)
    return x * lax.rsqrt(ms + EPS) * g


def _silu(u):
    return u * (1.0 / (1.0 + jnp.exp(-u)))


def _rope(x, cos, sin, group):
    half = group // 2
    rows, width = x.shape
    lane = lax.broadcasted_iota(jnp.int32, (rows, LANES), 1)
    first = (lane & (group - 1)) < half
    outs = []
    for c in range(width // LANES):
        xc = x[:, c * LANES:(c + 1) * LANES]
        rot = jnp.where(first, pltpu.roll(xc, LANES - half, 1), pltpu.roll(xc, half, 1))
        outs.append(xc * cos + rot * sin)
    return outs[0] if len(outs) == 1 else jnp.concatenate(outs, axis=-1)


def _lane_mask(rows, width, lo, hi):
    lane = lax.broadcasted_iota(jnp.int32, (rows, width), 1)
    return (lane >= lo) & (lane < hi)


def _softmax_av(qm, ksegs, vsegs, bias0=None):
    ss = [_dot_nt(qm, k) for k in ksegs]
    if bias0 is not None:
        ss[0] = ss[0] + bias0
    m = jnp.max(ss[0], axis=-1, keepdims=True)
    for s in ss[1:]:
        m = jnp.maximum(m, jnp.max(s, axis=-1, keepdims=True))
    acc = None
    l = None
    for s, v in zip(ss, vsegs):
        p = jnp.exp(s - m)
        ps = jnp.sum(p, axis=-1, keepdims=True)
        a = _dot(p.astype(BF16), v)
        acc = a if acc is None else acc + a
        l = ps if l is None else l + ps
    return acc / l


def _mod_kernel(c_ref, w_ref, b_ref, o_ref):
    c = c_ref[...]
    o_ref[0] = _dot(_silu(c).astype(BF16), w_ref[0].astype(BF16)) + b_ref[0]


def _modulation(cvecs, w_mod, b_mod):
    tn = 768
    return pl.pallas_call(
        _mod_kernel,
        out_shape=jax.ShapeDtypeStruct((DEPTH, 8, 3 * D_MODEL), F32),
        grid=(DEPTH, 3 * D_MODEL // tn),
        in_specs=[pl.BlockSpec((8, D_MODEL), lambda l, n: (0, 0)),
                  pl.BlockSpec((1, D_MODEL, tn), lambda l, n: (l, 0, n)),
                  pl.BlockSpec((1, 1, tn), lambda l, n: (l, 0, n))],
        out_specs=pl.BlockSpec((1, 8, tn), lambda l, n: (l, 0, n)),
        compiler_params=_cparams(),
        name="modulation",
    )(cvecs, w_mod, b_mod.reshape(DEPTH, 1, 3 * D_MODEL))


def _modulated_norm(x_ref, mod_ref, g_ref, row):
    mod = mod_ref[0, pl.ds(row, 1), :]
    shift = mod[:, :D_MODEL]
    scale = mod[:, D_MODEL:2 * D_MODEL]
    h = _rms_full(x_ref[...], g_ref[0:1, :]) * (1.0 + scale) + shift
    return h.astype(BF16)


def _mla_kv(cn, wkvb_ref, bd64_ref, g_kn, kn_ref, vm_ref):
    kv = _dot(cn.astype(BF16), wkvb_ref[...])
    for c in range(2):
        sl = slice(256 * c, 256 * (c + 1))
        kn_ref[:, sl] = _rms_group(kv[:, sl], bd64_ref[...], g_kn).astype(BF16)
    vm_ref[...] = kv[:, 512:].astype(BF16)


def _in_even_kernel(*refs, rope, states, row_base, tiles_per_batch):
    it = iter(refs)
    x_ref, mod_ref, g_ref, w_ref, wqb_ref, wkvb_ref, bd64_ref, bd32_ref, bdq_ref = (next(it) for _ in range(9))
    if rope:
        c64_ref, s64_ref, c32_ref, s32_ref = (next(it) for _ in range(4))
    qcat_ref, kn_ref, kpe_ref, vm_ref, mg_ref, dq_ref, dk_ref, dv_ref, dg_ref = (next(it) for _ in range(9))
    if states:
        st_ckv_ref, st_kpe_ref, st_dk_ref, st_dv_ref = (next(it) for _ in range(4))

    row = row_base + pl.program_id(0) // tiles_per_batch
    hb = _modulated_norm(x_ref, mod_ref, g_ref, row)
    bd64 = bd64_ref[...]
    mla_scale = MLA_QK ** -0.5
    diff_scale = DIFF_HD ** -0.5

    qa = _dot(hb, w_ref[:, 0:256])
    qa_n = _rms_full(qa, g_ref[1:2, 0:256]).astype(BF16)
    q = _dot(qa_n, wqb_ref[...])
    for p in range(4):
        qp = _rms_group(q[:, 256 * p:256 * (p + 1)], bdq_ref[...], g_ref[2:3, 0:256])
        q_nope = qp[:, :LANES]
        q_pe = qp[:, LANES:]
        if rope:
            q_pe = _rope(q_pe, c32_ref[...], s32_ref[...], MLA_ROPE)
        qcat_ref[:, 256 * p:256 * p + LANES] = (q_nope * mla_scale).astype(BF16)
        qcat_ref[:, 256 * p + LANES:256 * (p + 1)] = (q_pe * mla_scale).astype(BF16)

    kva = _dot(hb, w_ref[:, 256:512])
    c_kv = _rms_full(kva[:, :LANES], g_ref[3:4, 0:LANES])
    k_pe = _rms_group(kva[:, LANES:], bd32_ref[...], g_ref[4:5, 0:LANES])
    if states:
        st_ckv_ref[...] = c_kv
        st_kpe_ref[...] = k_pe[:, :MLA_ROPE]
    if rope:
        k_pe = _rope(k_pe, c32_ref[...], s32_ref[...], MLA_ROPE)
    kpe_ref[...] = k_pe.astype(BF16)
    _mla_kv(c_kv, wkvb_ref, bd64_ref, g_ref[5:6, 0:256], kn_ref, vm_ref)

    mg_ref[...] = _silu(_dot(hb, w_ref[:, 512:1024])).astype(BF16)

    dq = _dot(hb, w_ref[:, 1024:1536])
    dk = _dot(hb, w_ref[:, 1536:2048])
    for c in range(2):
        sl = slice(256 * c, 256 * (c + 1))
        qn = _rms_group(dq[:, sl], bd64, g_ref[6:7, 0:256])
        kn = _rms_group(dk[:, sl], bd64, g_ref[7:8, 0:256])
        if states:
            st_dk_ref[:, sl] = kn
        if rope:
            qn = _rope(qn, c64_ref[...], s64_ref[...], DIFF_HD)
            kn = _rope(kn, c64_ref[...], s64_ref[...], DIFF_HD)
        dq_ref[:, sl] = (qn * diff_scale).astype(BF16)
        dk_ref[:, sl] = kn.astype(BF16)
    dv = _dot(hb, w_ref[:, 2048:2560])
    if states:
        st_dv_ref[...] = dv
    dv_ref[...] = dv.astype(BF16)
    dg_ref[...] = _silu(_dot(hb, w_ref[:, 2560:3072])).astype(BF16)


def _in_odd_kernel(*refs, rope, states, row_base, tiles_per_batch):
    it = iter(refs)
    x_ref, mod_ref, g_ref, w_ref, bd64_ref = (next(it) for _ in range(5))
    if rope:
        c64_ref, s64_ref = (next(it) for _ in range(2))
    nq_ref, nk_ref, nv_ref, ng_ref, gq_ref, gk_ref, gv_ref, gg_ref = (next(it) for _ in range(8))
    if states:
        st_nk_ref, st_nv_ref, st_gk_ref, st_gv_ref = (next(it) for _ in range(4))

    row = row_base + pl.program_id(0) // tiles_per_batch
    hb = _modulated_norm(x_ref, mod_ref, g_ref, row)
    bd64 = bd64_ref[...]
    na_scale = NA_HD ** -0.5
    gqa_scale = GQA_HD ** -0.5

    nq = _dot(hb, w_ref[:, 0:512])
    nk = _dot(hb, w_ref[:, 512:1024])
    for c in range(2):
        sl = slice(256 * c, 256 * (c + 1))
        nq_ref[:, sl] = (_rms_group(nq[:, sl], bd64, g_ref[1:2, 0:256]) * na_scale).astype(BF16)
        kn = _rms_group(nk[:, sl], bd64, g_ref[2:3, 0:256])
        if states:
            st_nk_ref[:, sl] = kn
        nk_ref[:, sl] = kn.astype(BF16)
    nv = _dot(hb, w_ref[:, 1024:1536])
    if states:
        st_nv_ref[...] = nv
    nv_ref[...] = nv.astype(BF16)
    ng_ref[...] = _silu(_dot(hb, w_ref[:, 1536:2048])).astype(BF16)

    gq = _dot(hb, w_ref[:, 2048:2560])
    for c in range(2):
        sl = slice(256 * c, 256 * (c + 1))
        qn = _rms_group(gq[:, sl], bd64, g_ref[3:4, 0:256])
        if rope:
            qn = _rope(qn, c64_ref[...], s64_ref[...], GQA_HD)
        gq_ref[:, sl] = (qn * gqa_scale).astype(BF16)
    gkv = _dot(hb, w_ref[:, 2560:2816])
    gk = _rms_group(gkv[:, :LANES], bd64[:LANES, :LANES], g_ref[4:5, 0:LANES])
    gv = gkv[:, LANES:]
    if states:
        st_gk_ref[...] = gk
        st_gv_ref[...] = gv
    if rope:
        gk = _rope(gk, c64_ref[...], s64_ref[...], GQA_HD)
    gk_ref[...] = gk.astype(BF16)
    gv_ref[...] = gv.astype(BF16)
    gg_ref[...] = _silu(_dot(hb, w_ref[:, 2816:3328])).astype(BF16)


def _full(shape):
    zeros = (0,) * len(shape)
    return pl.BlockSpec(shape, lambda i: zeros)


def _in_proj(kernel, x, mods, layer, gains, weights, consts, ropes, out_widths, state_widths,
             row_base, tokens_per_batch):
    rows = x.shape[0]
    tm = ROW_TILE
    tiles_per_batch = tokens_per_batch // tm
    rope = ropes is not None
    states = state_widths is not None
    in_specs = [pl.BlockSpec((tm, D_MODEL), lambda i: (i, 0)),
                pl.BlockSpec((1, 8, 3 * D_MODEL), lambda i: (layer, 0, 0)),
                _full(gains.shape)]
    in_specs += [_full(w.shape) for w in weights]
    in_specs += [_full(c.shape) for c in consts]
    args = [x, mods, gains, *weights, *consts]
    if rope:
        in_specs += [pl.BlockSpec((tm, LANES), lambda i: (i % tiles_per_batch, 0)) for _ in ropes]
        args += list(ropes)
    out_shape = [jax.ShapeDtypeStruct((rows, w), BF16) for w in out_widths]
    out_specs = [pl.BlockSpec((tm, w), lambda i: (i, 0)) for w in out_widths]
    if states:
        out_shape += [jax.ShapeDtypeStruct((rows, w), F32) for w in state_widths]
        out_specs += [pl.BlockSpec((tm, w), lambda i: (i, 0)) for w in state_widths]
    return pl.pallas_call(
        functools.partial(kernel, rope=rope, states=states, row_base=row_base, tiles_per_batch=tiles_per_batch),
        out_shape=out_shape,
        grid=(rows // tm,),
        in_specs=in_specs,
        out_specs=out_specs,
        compiler_params=_cparams(),
        name=kernel.__name__.strip("_"),
    )(*args)


def _mla_cache_kernel(c_ref, wkvb_ref, bd64_ref, g_ref, kn_ref, vm_ref):
    _mla_kv(c_ref[...], wkvb_ref, bd64_ref, g_ref[5:6, 0:256], kn_ref, vm_ref)


def _mla_cache_kv(ckv, wkvb, bd64, gains):
    rows = ckv.shape[0]
    return pl.pallas_call(
        _mla_cache_kernel,
        out_shape=[jax.ShapeDtypeStruct((rows, 512), BF16)] * 2,
        grid=(1,),
        in_specs=[_full(ckv.shape), _full(wkvb.shape), _full(bd64.shape), _full(gains.shape)],
        out_specs=[_full((rows, 512))] * 2,
        compiler_params=_cparams(),
        name="mla_cache_kv",
    )(ckv, wkvb, bd64, gains)


def _pair_attn_kernel(*refs, mode, n_seg, n_pairs, lam_init):
    it = iter(refs)
    q_ref = next(it)
    segs = []
    for _ in range(n_seg):
        if mode == "mla":
            segs.append((next(it), next(it), next(it)))
        else:
            segs.append((next(it), next(it)))
    gate_ref = next(it)
    if mode == "diff":
        lam_ref, subln_ref = next(it), next(it)
    o_ref = next(it)

    qw = 2 * LANES if mode == "mla" else LANES
    rows = q_ref.shape[0]
    lo = _lane_mask(rows, LANES, 0, LANES // 2)
    if mode == "diff":
        lv = lam_ref[...]
        lam = (jnp.exp(jnp.sum(lv[0:1] * lv[1:2], axis=-1, keepdims=True))
               - jnp.exp(jnp.sum(lv[2:3] * lv[3:4], axis=-1, keepdims=True)) + lam_init)
    for p in range(n_pairs):
        sl = slice(LANES * p, LANES * (p + 1))
        q = q_ref[:, qw * p:qw * (p + 1)]
        ksegs, vsegs = [], []
        for seg in segs:
            if mode == "mla":
                kn, kpe, v = seg
                ksegs.append(jnp.concatenate([kn[:, sl].astype(BF16), kpe[...].astype(BF16)], axis=-1))
            else:
                k, v = seg
                ksegs.append((k[...] if mode == "gqa" else k[:, sl]).astype(BF16))
            vsegs.append((v[...] if mode == "gqa" else v[:, sl]).astype(BF16))
        outs = []
        for j in range(2):
            if mode == "mla":
                keep = (_lane_mask(rows, qw, 64 * j, 64 * (j + 1))
                        | _lane_mask(rows, qw, LANES + 32 * j, LANES + 32 * (j + 1)))
            else:
                keep = _lane_mask(rows, qw, 64 * j, 64 * (j + 1))
            qm = jnp.where(keep, q, jnp.zeros_like(q))
            outs.append(_softmax_av(qm, ksegs, vsegs))
        if mode == "diff":
            d = outs[0] - lam * outs[1]
            o = _rms_full(d, subln_ref[...]) * (1.0 - lam_init)
        else:
            o = jnp.where(lo, outs[0], outs[1])
        o_ref[:, sl] = (o * gate_ref[:, sl].astype(F32)).astype(BF16)


def _pair_attention(mode, q, segs, gate, batch, tq, bq, n_pairs, extra=(), lam_init=0.0):
    qw = 2 * LANES if mode == "mla" else LANES
    nq = tq // bq
    steps_p = 4 // n_pairs
    in_specs = [pl.BlockSpec((bq, qw * n_pairs), lambda b, p, i: (b * nq + i, p))]
    args = [q]
    for seg in segs:
        for arr, spec in seg:
            args.append(arr)
            in_specs.append(spec)
    in_specs.append(pl.BlockSpec((bq, LANES * n_pairs), lambda b, p, i: (b * nq + i, p)))
    args.append(gate)
    for arr in extra:
        args.append(arr)
        in_specs.append(pl.BlockSpec(arr.shape, lambda b, p, i: (0,) * arr.ndim))
    return pl.pallas_call(
        functools.partial(_pair_attn_kernel, mode=mode, n_seg=len(segs), n_pairs=n_pairs, lam_init=lam_init),
        out_shape=jax.ShapeDtypeStruct((batch * tq, 4 * LANES), BF16),
        grid=(batch, steps_p, nq),
        in_specs=in_specs,
        out_specs=pl.BlockSpec((bq, LANES * n_pairs), lambda b, p, i: (b * nq + i, p)),
        compiler_params=_cparams(),
        name=mode + "_attention",
    )(*args)


def _self_seg(arr, tk, n_pairs, shared=False):
    if shared:
        return arr, pl.BlockSpec((tk, arr.shape[1]), lambda b, p, i: (b, 0))
    return arr, pl.BlockSpec((tk, LANES * n_pairs), lambda b, p, i: (b, p))


def _cache_seg(arr, layer, n_pairs, shared=False):
    past = arr.shape[2]
    if shared:
        return arr, pl.BlockSpec((None, None, past, arr.shape[3]), lambda b, p, i: (b, layer, 0, 0))
    return arr, pl.BlockSpec((None, None, past, LANES * n_pairs), lambda b, p, i: (b, layer, 0, p))


def _na_kernel(q_ref, k_ref, v_ref, ck_ref, cv_ref, bias_ref, gate_ref, o_ref):
    g = pl.program_id(1)
    n_groups = pl.num_programs(1)
    first_row = jnp.clip(NA_GROUP_ROWS * g - NA_ROWS // 2, 0, NA_GROUP_ROWS * n_groups - NA_WIN_ROWS)
    start = pl.multiple_of(first_row * GRID_W, GRID_W)
    win = NA_WIN_ROWS * GRID_W
    kwin = k_ref[pl.ds(start, win), :]
    vwin = v_ref[pl.ds(start, win), :]
    ck = ck_ref[...].astype(BF16)
    cv = cv_ref[...].astype(BF16)
    q = q_ref[...]
    rows = q.shape[0]
    lo = _lane_mask(rows, LANES, 0, LANES // 2)
    for p in range(NA_HEADS // 2):
        sl = slice(LANES * p, LANES * (p + 1))
        outs = []
        for j in range(2):
            keep = _lane_mask(rows, LANES, 64 * j, 64 * (j + 1))
            qm = jnp.where(keep, q[:, sl], jnp.zeros_like(q[:, sl]))
            outs.append(_softmax_av(qm, [kwin[:, sl], ck[:, sl]], [vwin[:, sl], cv[:, sl]],
                                    bias0=bias_ref[2 * p + j]))
        o = jnp.where(lo, outs[0], outs[1])
        o_ref[:, sl] = (o * gate_ref[:, sl].astype(F32)).astype(BF16)


def _na_attention(q, k, v, cache_k, cache_v, layer, bias, gate, batch, tq):
    bq = NA_GROUP_ROWS * GRID_W
    n_groups = tq // bq
    past = cache_k.shape[2]
    width = NA_HEADS * NA_HD

    def bias_map(b, g):
        return (jnp.where(g == 0, 0, jnp.where(g == n_groups - 1, 2, 1)), 0, 0, 0)

    tok = pl.BlockSpec((bq, width), lambda b, g: (b * n_groups + g, 0))
    whole = pl.BlockSpec((tq, width), lambda b, g: (b, 0))
    cache = pl.BlockSpec((None, None, past, width), lambda b, g: (b, layer, 0, 0))
    return pl.pallas_call(
        _na_kernel,
        out_shape=jax.ShapeDtypeStruct((batch * tq, width), BF16),
        grid=(batch, n_groups),
        in_specs=[tok, whole, whole, cache, cache,
                  pl.BlockSpec((None, NA_HEADS, bq, NA_WIN_ROWS * GRID_W), bias_map), tok],
        out_specs=tok,
        compiler_params=_cparams(),
        name="na_attention",
    )(q, k, v, cache_k, cache_v, bias, gate)


def _na_bias_tables(rpb):
    qc = np.arange(GRID_W)[:, None]
    kc = np.arange(GRID_W)[None, :]
    cs = np.clip(qc - NA_COLS // 2, 0, GRID_W - NA_COLS)
    col_valid = (kc >= cs) & (kc < cs + NA_COLS)
    dc = kc - qc + NA_COLS - 1
    onehot = np.zeros((2 * NA_COLS - 1, GRID_W, GRID_W), np.float32)
    for d in range(2 * NA_COLS - 1):
        onehot[d] = (col_valid & (dc == d)).astype(np.float32)
    cols = jnp.einsum("hrd,dqk->hrqk", rpb.astype(F32), jnp.asarray(onehot), precision=lax.Precision.HIGHEST)
    cols = jnp.where(jnp.asarray(col_valid)[None, None], cols, NEG)
    outside = jnp.full((NA_HEADS, 1, GRID_W, GRID_W), NEG, F32)
    cols = jnp.concatenate([cols, outside], axis=1)
    n_dr = 2 * NA_ROWS - 1
    idx = np.full((3, NA_GROUP_ROWS, NA_WIN_ROWS), n_dr, np.int32)
    for a in range(NA_GROUP_ROWS):
        for j in range(NA_WIN_ROWS):
            if j < NA_ROWS:
                idx[0, a, j] = j - a + NA_ROWS - 1
            if a <= j < a + NA_ROWS:
                idx[1, a, j] = j - a + NA_ROWS // 2 - 1
            if j >= NA_WIN_ROWS - NA_ROWS:
                idx[2, a, j] = j - a - (NA_WIN_ROWS - NA_ROWS) + NA_ROWS // 2 - 1
    t = jnp.take(cols, jnp.asarray(idx.reshape(-1)), axis=1)
    t = t.reshape(NA_HEADS, 3, NA_GROUP_ROWS, NA_WIN_ROWS, GRID_W, GRID_W)
    t = t.transpose(1, 0, 2, 4, 3, 5)
    return t.reshape(3, NA_HEADS, NA_GROUP_ROWS * GRID_W, NA_WIN_ROWS * GRID_W)


def _out_kernel(oa_ref, ob_ref, x_ref, mod_ref, w_ref, y_ref, *, row_base, tiles_per_batch):
    row = row_base + pl.program_id(0) // tiles_per_batch
    gate = mod_ref[0, pl.ds(row, 1), :][:, 2 * D_MODEL:]
    half = oa_ref.shape[1]
    acc = _dot(oa_ref[...], w_ref[:half, :]) + _dot(ob_ref[...], w_ref[half:, :])
    y_ref[...] = x_ref[...] + gate * acc


def _out_proj(oa, ob, x, mods, layer, w, row_base, tokens_per_batch):
    rows = x.shape[0]
    tm = ROW_TILE
    tiles_per_batch = tokens_per_batch // tm
    return pl.pallas_call(
        functools.partial(_out_kernel, row_base=row_base, tiles_per_batch=tiles_per_batch),
        out_shape=jax.ShapeDtypeStruct((rows, D_MODEL), F32),
        grid=(rows // tm,),
        in_specs=[pl.BlockSpec((tm, oa.shape[1]), lambda i: (i, 0)),
                  pl.BlockSpec((tm, ob.shape[1]), lambda i: (i, 0)),
                  pl.BlockSpec((tm, D_MODEL), lambda i: (i, 0)),
                  pl.BlockSpec((1, 8, 3 * D_MODEL), lambda i: (layer, 0, 0)),
                  _full(w.shape)],
        out_specs=pl.BlockSpec((tm, D_MODEL), lambda i: (i, 0)),
        compiler_params=_cparams(),
        name="out_proj",
    )(oa, ob, x, mods, w)


def _block_diag(width, group):
    idx = np.arange(width) // group
    return jnp.asarray((idx[:, None] == idx[None, :]).astype(np.float32) / group, BF16)


def _rope_tables(t, rot_dim):
    pos = np.arange(t)
    row = (pos // GRID_W).astype(np.float64)
    col = (pos % GRID_W).astype(np.float64)
    n = rot_dim // 2
    inv = ROPE_THETA ** (-np.arange(0, n, 2, dtype=np.float64) / n)
    ang = np.concatenate([row[:, None] * inv, col[:, None] * inv], axis=-1)
    cos = np.concatenate([np.cos(ang), np.cos(ang)], axis=-1)
    sin = np.concatenate([-np.sin(ang), np.sin(ang)], axis=-1)
    reps = LANES // rot_dim
    return (jnp.asarray(np.tile(cos, (1, reps)), F32), jnp.asarray(np.tile(sin, (1, reps)), F32))


def _pad_row(v, width=D_MODEL):
    return jnp.pad(v, (0, width - v.shape[0]))


def _tile_row(v, reps):
    return _pad_row(jnp.tile(v, reps))


def _permute_heads(w, axis):
    shape = w.shape
    split = shape[:axis] + (GQA_HEADS, GQA_HD) + shape[axis + 1:]
    return jnp.take(w.reshape(split), jnp.asarray(GQA_PERM), axis=axis).reshape(shape)


def kernel(x_prompt, x_sample, cache_mla_ckv, cache_mla_kpe, cache_diff_k, cache_diff_v, cache_na_k, cache_na_v, cache_gqa_k, cache_gqa_v, c, c_ctx, norm_w, w_mod, b_mod, w_in_even, w_out_even, mla_qa_norm, mla_wqb, mla_kva_norm, mla_wkvb, mla_qn_nope, mla_qn_rope, mla_kn_nope, mla_kn_rope, diff_qn, diff_kn, diff_lq1, diff_lk1, diff_lq2, diff_lk2, diff_subln, w_in_odd, w_out_odd, na_qn, na_kn, na_rpb, gqa_qn, gqa_kn):
    batch, seq, _ = x_prompt.shape
    dec_batch, dec_seq, _ = x_sample.shape
    past = cache_mla_ckv.shape[2]
    n_even, n_odd = w_in_even.shape[0], w_in_odd.shape[0]

    w_in_e = jnp.concatenate([w_in_even[..., :384], jnp.tile(w_in_even[..., 384:416], (1, 1, 4)),
                              w_in_even[..., 416:]], axis=-1).astype(BF16)
    wqb = mla_wqb.reshape(n_even, Q_LORA, MLA_HEADS, MLA_QK)
    wqb = jnp.concatenate([wqb[..., :MLA_NOPE].reshape(n_even, Q_LORA, 4, 2 * MLA_NOPE),
                           wqb[..., MLA_NOPE:].reshape(n_even, Q_LORA, 4, 2 * MLA_ROPE),
                           jnp.zeros((n_even, Q_LORA, 4, LANES - 2 * MLA_ROPE), F32)], axis=-1)
    wqb = wqb.reshape(n_even, Q_LORA, 4 * 2 * LANES).astype(BF16)
    wkvb = mla_wkvb.reshape(n_even, KV_LORA, MLA_HEADS, 2 * MLA_NOPE)
    wkvb = jnp.concatenate([wkvb[..., :MLA_NOPE].reshape(n_even, KV_LORA, 512),
                            wkvb[..., MLA_NOPE:].reshape(n_even, KV_LORA, 512)], axis=-1).astype(BF16)
    w_out_e = w_out_even.astype(BF16)
    w_in_o = jnp.concatenate([w_in_odd[..., :2048], _permute_heads(w_in_odd[..., 2048:2560], 2),
                              w_in_odd[..., 2560:2816], _permute_heads(w_in_odd[..., 2816:], 2)],
                             axis=-1).astype(BF16)
    w_out_o = jnp.concatenate([w_out_odd[:, :512], _permute_heads(w_out_odd[:, 512:], 1)], axis=1).astype(BF16)

    gains_e = [jnp.stack([norm_w[2 * i], _pad_row(mla_qa_norm[i]),
                          _pad_row(jnp.concatenate([jnp.tile(mla_qn_nope[i], 2), jnp.tile(mla_qn_rope[i], 4)])),
                          _pad_row(mla_kva_norm[i]), _tile_row(mla_kn_rope[i], 4), _tile_row(mla_kn_nope[i], 4),
                          _tile_row(diff_qn[i], 4), _tile_row(diff_kn[i], 4)]) for i in range(n_even)]
    gains_o = [jnp.stack([norm_w[2 * i + 1], _tile_row(na_qn[i], 4), _tile_row(na_kn[i], 4),
                          _tile_row(gqa_qn[i], 4), _tile_row(gqa_kn[i], 2),
                          jnp.zeros((D_MODEL,), F32), jnp.zeros((D_MODEL,), F32), jnp.zeros((D_MODEL,), F32)])
               for i in range(n_odd)]
    lam_vecs = [jnp.stack([diff_lq1[i], diff_lk1[i], diff_lq2[i], diff_lk2[i]]) for i in range(n_even)]

    bd64 = _block_diag(256, 64)
    bd32 = _block_diag(LANES, 32)
    bdq = jnp.concatenate([jnp.concatenate([_block_diag(LANES, 64), jnp.zeros((LANES, LANES), BF16)], axis=1),
                           jnp.concatenate([jnp.zeros((LANES, LANES), BF16), _block_diag(LANES, 32)], axis=1)], axis=0)
    c64, s64 = _rope_tables(dec_seq, 64)
    c32, s32 = _rope_tables(dec_seq, MLA_ROPE)
    na_bias = [_na_bias_tables(na_rpb[i]) for i in range(n_odd)]

    cvecs = jnp.concatenate([c_ctx[None, :], c, jnp.zeros((8 - 1 - dec_batch, D_MODEL), F32)], axis=0)
    mods = _modulation(cvecs, w_mod, b_mod)

    cache_kpe = jnp.tile(cache_mla_kpe, (1, 1, 1, 4))
    cache_dk = cache_diff_k.reshape(dec_batch, n_even, past, 512)
    cache_dv = cache_diff_v.reshape(dec_batch, n_even, past, 512)
    cache_nk = cache_na_k.reshape(dec_batch, n_odd, past, 512)
    cache_nv = cache_na_v.reshape(dec_batch, n_odd, past, 512)
    cache_gk = cache_gqa_k.reshape(dec_batch, n_odd, past, LANES)
    cache_gv = cache_gqa_v.reshape(dec_batch, n_odd, past, LANES)

    even_widths = (1024, 512, LANES, 512, 512, 512, 512, 512, 512)
    odd_widths = (512, 512, 512, 512, 512, LANES, LANES, 512)

    def run_pass(x, nb, t, ctx):
        row_base = 0 if ctx else 1
        states = [[] for _ in range(8)]
        bq = t if ctx else Q_BLOCK
        n_pairs = 4 if ctx else 1
        t_mod = nb * t if ctx else t
        for l in range(DEPTH):
            i = l // 2
            if l % 2 == 0:
                lam_init = 0.8 - 0.6 * math.exp(-0.3 * l)
                outs = _in_proj(_in_even_kernel, x, mods, l, gains_e[i], (w_in_e[i], wqb[i], wkvb[i]),
                                (bd64, bd32, bdq), None if ctx else (c64, s64, c32, s32),
                                even_widths, (LANES, MLA_ROPE, 512, 512) if ctx else None, row_base, t_mod)
                qcat, kn, kpe, vm, mg, dq, dk, dv, dg = outs[:9]
                if ctx:
                    for j in range(4):
                        states[j].append(outs[9 + j])
                mla_segs = [(_self_seg(kn, t, n_pairs), _self_seg(kpe, t, n_pairs, shared=True),
                             _self_seg(vm, t, n_pairs))]
                diff_segs = [(_self_seg(dk, t, n_pairs), _self_seg(dv, t, n_pairs))]
                if not ctx:
                    kn_c, vm_c = _mla_cache_kv(cache_mla_ckv[:, i].reshape(nb * past, KV_LORA), wkvb[i], bd64, gains_e[i])
                    mla_segs.append((_self_seg(kn_c, past, n_pairs), _cache_seg(cache_kpe, i, n_pairs, shared=True),
                                     _self_seg(vm_c, past, n_pairs)))
                    diff_segs.append((_cache_seg(cache_dk, i, n_pairs), _cache_seg(cache_dv, i, n_pairs)))
                oa = _pair_attention("mla", qcat, mla_segs, mg, nb, t, bq, n_pairs)
                ob = _pair_attention("diff", dq, diff_segs, dg, nb, t, bq, n_pairs,
                                     extra=(lam_vecs[i], diff_subln[i][None, :]), lam_init=lam_init)
                x = _out_proj(oa, ob, x, mods, l, w_out_e[i], row_base, t_mod)
            else:
                outs = _in_proj(_in_odd_kernel, x, mods, l, gains_o[i], (w_in_o[i],), (bd64,),
                                None if ctx else (c64, s64), odd_widths,
                                (512, 512, LANES, LANES) if ctx else None, row_base, t_mod)
                nq, nk, nv, ng, gq, gk, gv, gg = outs[:8]
                if ctx:
                    for j in range(4):
                        states[4 + j].append(outs[8 + j])
                    oa = _pair_attention("mha", nq, [(_self_seg(nk, t, n_pairs), _self_seg(nv, t, n_pairs))],
                                         ng, nb, t, bq, n_pairs)
                else:
                    oa = _na_attention(nq, nk, nv, cache_nk, cache_nv, i, na_bias[i], ng, nb, t)
                gqa_segs = [(_self_seg(gk, t, n_pairs, shared=True), _self_seg(gv, t, n_pairs, shared=True))]
                if not ctx:
                    gqa_segs.append((_cache_seg(cache_gk, i, n_pairs, shared=True),
                                     _cache_seg(cache_gv, i, n_pairs, shared=True)))
                ob = _pair_attention("gqa", gq, gqa_segs, gg, nb, t, bq, n_pairs)
                x = _out_proj(oa, ob, x, mods, l, w_out_o[i], row_base, t_mod)
        return x, states

    y_prompt, st = run_pass(x_prompt.reshape(batch * seq, D_MODEL), batch, seq, True)
    y_sample, _ = run_pass(x_sample.reshape(dec_batch * dec_seq, D_MODEL), dec_batch, dec_seq, False)

    def stacked(parts, tail):
        return jnp.stack([p.reshape((batch, seq) + tail) for p in parts], axis=1)

    return (y_prompt.reshape(batch, seq, D_MODEL), y_sample.reshape(dec_batch, dec_seq, D_MODEL),
            stacked(st[0], (KV_LORA,)), stacked(st[1], (MLA_ROPE,)),
            stacked(st[2], (DIFF_HEADS, 2, DIFF_HD)), stacked(st[3], (DIFF_HEADS, 2 * DIFF_HD)),
            stacked(st[4], (NA_HEADS, NA_HD)), stacked(st[5], (NA_HEADS, NA_HD)),
            stacked(st[6], (GQA_KV, GQA_HD)), stacked(st[7], (GQA_KV, GQA_HD)))
```

```python
import functools
import math

import jax
import jax.numpy as jnp
import numpy as np
from jax import lax
from jax.experimental import pallas as pl
from jax.experimental.pallas import tpu as pltpu

F32 = jnp.float32
BF16 = jnp.bfloat16

D_MODEL = 1024
DEPTH = 4
GRID_W = 64
ROPE_THETA = 10000.0
EPS = 1e-6
MLA_HEADS = 8
MLA_NOPE = 64
MLA_ROPE = 32
MLA_QK = MLA_NOPE + MLA_ROPE
Q_LORA = 256
KV_LORA = 128
DIFF_HEADS = 4
DIFF_HD = 64
NA_HEADS = 8
NA_HD = 64
NA_ROWS = 8
NA_COLS = 16
GQA_HEADS = 8
GQA_KV = 2
GQA_HD = 64
LANES = 128
NA_GROUP_ROWS = 4
NA_WIN_ROWS = NA_ROWS + NA_GROUP_ROWS
NEG = -1e30
LOG2E = math.log2(math.e)
VMEM_LIMIT = 48 * 1024 * 1024
GQA_PERM = (0, 4, 1, 5, 2, 6, 3, 7)

ROW_TILE = 512
Q_BLOCK = 256


def _cparams():
    return pltpu.CompilerParams(vmem_limit_bytes=VMEM_LIMIT)


def _dot(a, b):
    return jnp.dot(a, b, preferred_element_type=F32)


def _dot_nt(a, b):
    return lax.dot_general(a, b, (((1,), (1,)), ((), ())), preferred_element_type=F32)


def _rms_full(x, g):
    ms = jnp.mean(x * x, axis=-1, keepdims=True)
    return x * lax.rsqrt(ms + EPS) * g


def _rms_group(x, bd, g):
    ms = _dot((x * x).astype(BF16), bd)
    return x * lax.rsqrt(ms + EPS) * g


def _silu(u):
    return u * (1.0 / (1.0 + jnp.exp(-u)))


def _rope(x, cos, sin, group):
    half = group // 2
    rows, width = x.shape
    lane = lax.broadcasted_iota(jnp.int32, (rows, LANES), 1)
    first = (lane & (group - 1)) < half
    outs = []
    for c in range(width // LANES):
        xc = x[:, c * LANES:(c + 1) * LANES]
        rot = jnp.where(first, pltpu.roll(xc, LANES - half, 1), pltpu.roll(xc, half, 1))
        outs.append(xc * cos + rot * sin)
    return outs[0] if len(outs) == 1 else jnp.concatenate(outs, axis=-1)


def _lane_mask(rows, width, lo, hi):
    lane = lax.broadcasted_iota(jnp.int32, (rows, width), 1)
    return (lane >= lo) & (lane < hi)


def _scores(qm, ksegs, bias0=None):
    ss = [_dot_nt(qm, k) for k in ksegs]
    if bias0 is not None:
        ss[0] = ss[0] + bias0
    return ss


def _softmax_pv(ss, vsegs, ones_lanes=None):
    m = jnp.max(ss[0], axis=-1, keepdims=True)
    for s in ss[1:]:
        m = jnp.maximum(m, jnp.max(s, axis=-1, keepdims=True))
    acc = None
    l = None
    for s, v in zip(ss, vsegs):
        p = jnp.exp2(s - m)
        if ones_lanes is None:
            ps = jnp.sum(p, axis=-1, keepdims=True)
            l = ps if l is None else l + ps
        else:
            v = jnp.where(_lane_mask(v.shape[0], LANES, *ones_lanes), jnp.ones_like(v), v)
        a = _dot(p.astype(BF16), v)
        acc = a if acc is None else acc + a
    if ones_lanes is not None:
        l = pltpu.roll(acc, LANES // 2, 1)
    return acc / l


def _mod_kernel(c_ref, w_ref, b_ref, o_ref):
    c = c_ref[...]
    o_ref[0] = _dot(_silu(c).astype(BF16), w_ref[0].astype(BF16)) + b_ref[0]


def _modulation(cvecs, w_mod, b_mod):
    tn = 768
    return pl.pallas_call(
        _mod_kernel,
        out_shape=jax.ShapeDtypeStruct((DEPTH, 8, 3 * D_MODEL), F32),
        grid=(DEPTH, 3 * D_MODEL // tn),
        in_specs=[pl.BlockSpec((8, D_MODEL), lambda l, n: (0, 0)),
                  pl.BlockSpec((1, D_MODEL, tn), lambda l, n: (l, 0, n)),
                  pl.BlockSpec((1, 1, tn), lambda l, n: (l, 0, n))],
        out_specs=pl.BlockSpec((1, 8, tn), lambda l, n: (l, 0, n)),
        compiler_params=_cparams(),
        name="modulation",
    )(cvecs, w_mod, b_mod.reshape(DEPTH, 1, 3 * D_MODEL))


def _modulated_norm(x_ref, mod_ref, g_ref, row):
    mod = mod_ref[0, pl.ds(row, 1), :]
    shift = mod[:, :D_MODEL]
    scale = mod[:, D_MODEL:2 * D_MODEL]
    h = _rms_full(x_ref[...], g_ref[0:1, :]) * (1.0 + scale) + shift
    return h.astype(BF16)


def _mla_kv(cn, wkvb_ref, bd64_ref, g_kn, kn_ref, vm_ref):
    kv = _dot(cn.astype(BF16), wkvb_ref[...])
    for c in range(2):
        sl = slice(256 * c, 256 * (c + 1))
        kn_ref[:, sl] = _rms_group(kv[:, sl], bd64_ref[...], g_kn).astype(BF16)
    vm_ref[...] = kv[:, 512:].astype(BF16)


def _in_even_kernel(*refs, rope, states, row_base, tiles_per_batch):
    it = iter(refs)
    x_ref, mod_ref, g_ref, w_ref, wqb_ref, wkvb_ref, bd64_ref, bd32_ref, bdq_ref = (next(it) for _ in range(9))
    if rope:
        c64_ref, s64_ref, c32_ref, s32_ref = (next(it) for _ in range(4))
    qcat_ref, kn_ref, kpe_ref, vm_ref, mg_ref, dq_ref, dk_ref, dv_ref, dg_ref = (next(it) for _ in range(9))
    if states:
        st_ckv_ref, st_kpe_ref, st_dk_ref, st_dv_ref = (next(it) for _ in range(4))

    row = row_base + pl.program_id(0) // tiles_per_batch
    hb = _modulated_norm(x_ref, mod_ref, g_ref, row)
    bd64 = bd64_ref[...]
    mla_scale = MLA_QK ** -0.5 * LOG2E
    diff_scale = DIFF_HD ** -0.5 * LOG2E

    qa = _dot(hb, w_ref[:, 0:256])
    qa_n = _rms_full(qa, g_ref[1:2, 0:256]).astype(BF16)
    q = _dot(qa_n, wqb_ref[...])
    for p in range(4):
        qp = _rms_group(q[:, 256 * p:256 * (p + 1)], bdq_ref[...], g_ref[2:3, 0:256])
        q_nope = qp[:, :LANES]
        q_pe = qp[:, LANES:]
        if rope:
            q_pe = _rope(q_pe, c32_ref[...], s32_ref[...], MLA_ROPE)
        qcat_ref[:, 256 * p:256 * p + LANES] = (q_nope * mla_scale).astype(BF16)
        qcat_ref[:, 256 * p + LANES:256 * (p + 1)] = (q_pe * mla_scale).astype(BF16)

    kva = _dot(hb, w_ref[:, 256:512])
    c_kv = _rms_full(kva[:, :LANES], g_ref[3:4, 0:LANES])
    k_pe = _rms_group(kva[:, LANES:], bd32_ref[...], g_ref[4:5, 0:LANES])
    if states:
        st_ckv_ref[...] = c_kv
        st_kpe_ref[...] = k_pe[:, :MLA_ROPE]
    if rope:
        k_pe = _rope(k_pe, c32_ref[...], s32_ref[...], MLA_ROPE)
    kpe_ref[...] = k_pe.astype(BF16)
    _mla_kv(c_kv, wkvb_ref, bd64_ref, g_ref[5:6, 0:256], kn_ref, vm_ref)

    mg_ref[...] = _silu(_dot(hb, w_ref[:, 512:1024])).astype(BF16)

    dq = _dot(hb, w_ref[:, 1024:1536])
    dk = _dot(hb, w_ref[:, 1536:2048])
    for c in range(2):
        sl = slice(256 * c, 256 * (c + 1))
        qn = _rms_group(dq[:, sl], bd64, g_ref[6:7, 0:256])
        kn = _rms_group(dk[:, sl], bd64, g_ref[7:8, 0:256])
        if states:
            st_dk_ref[:, sl] = kn
        if rope:
            qn = _rope(qn, c64_ref[...], s64_ref[...], DIFF_HD)
            kn = _rope(kn, c64_ref[...], s64_ref[...], DIFF_HD)
        dq_ref[:, sl] = (qn * diff_scale).astype(BF16)
        dk_ref[:, sl] = kn.astype(BF16)
    dv = _dot(hb, w_ref[:, 2048:2560])
    if states:
        st_dv_ref[...] = dv
    dv_ref[...] = dv.astype(BF16)
    dg_ref[...] = _silu(_dot(hb, w_ref[:, 2560:3072])).astype(BF16)


def _in_odd_kernel(*refs, rope, states, row_base, tiles_per_batch):
    it = iter(refs)
    x_ref, mod_ref, g_ref, w_ref, bd64_ref = (next(it) for _ in range(5))
    if rope:
        c64_ref, s64_ref = (next(it) for _ in range(2))
    nq_ref, nk_ref, nv_ref, ng_ref, gq_ref, gk_ref, gv_ref, gg_ref = (next(it) for _ in range(8))
    if states:
        st_nk_ref, st_nv_ref, st_gk_ref, st_gv_ref = (next(it) for _ in range(4))

    row = row_base + pl.program_id(0) // tiles_per_batch
    hb = _modulated_norm(x_ref, mod_ref, g_ref, row)
    bd64 = bd64_ref[...]
    na_scale = NA_HD ** -0.5 * LOG2E
    gqa_scale = GQA_HD ** -0.5 * LOG2E

    nq = _dot(hb, w_ref[:, 0:512])
    nk = _dot(hb, w_ref[:, 512:1024])
    for c in range(2):
        sl = slice(256 * c, 256 * (c + 1))
        nq_ref[:, sl] = (_rms_group(nq[:, sl], bd64, g_ref[1:2, 0:256]) * na_scale).astype(BF16)
        kn = _rms_group(nk[:, sl], bd64, g_ref[2:3, 0:256])
        if states:
            st_nk_ref[:, sl] = kn
        nk_ref[:, sl] = kn.astype(BF16)
    nv = _dot(hb, w_ref[:, 1024:1536])
    if states:
        st_nv_ref[...] = nv
    nv_ref[...] = nv.astype(BF16)
    ng_ref[...] = _silu(_dot(hb, w_ref[:, 1536:2048])).astype(BF16)

    gq = _dot(hb, w_ref[:, 2048:2560])
    for c in range(2):
        sl = slice(256 * c, 256 * (c + 1))
        qn = _rms_group(gq[:, sl], bd64, g_ref[3:4, 0:256])
        if rope:
            qn = _rope(qn, c64_ref[...], s64_ref[...], GQA_HD)
        gq_ref[:, sl] = (qn * gqa_scale).astype(BF16)
    gkv = _dot(hb, w_ref[:, 2560:2816])
    gk = _rms_group(gkv[:, :LANES], bd64[:LANES, :LANES], g_ref[4:5, 0:LANES])
    gv = gkv[:, LANES:]
    if states:
        st_gk_ref[...] = gk
        st_gv_ref[...] = gv
    if rope:
        gk = _rope(gk, c64_ref[...], s64_ref[...], GQA_HD)
    gk_ref[...] = gk.astype(BF16)
    gv_ref[...] = gv.astype(BF16)
    gg_ref[...] = _silu(_dot(hb, w_ref[:, 2816:3328])).astype(BF16)


def _full(shape):
    zeros = (0,) * len(shape)
    return pl.BlockSpec(shape, lambda i: zeros)


def _in_proj(kernel, x, mods, layer, gains, weights, consts, ropes, out_widths, state_widths,
             row_base, tokens_per_batch):
    rows = x.shape[0]
    tm = ROW_TILE
    tiles_per_batch = tokens_per_batch // tm
    rope = ropes is not None
    states = state_widths is not None
    in_specs = [pl.BlockSpec((tm, D_MODEL), lambda i: (i, 0)),
                pl.BlockSpec((1, 8, 3 * D_MODEL), lambda i: (layer, 0, 0)),
                _full(gains.shape)]
    in_specs += [_full(w.shape) for w in weights]
    in_specs += [_full(c.shape) for c in consts]
    args = [x, mods, gains, *weights, *consts]
    if rope:
        in_specs += [pl.BlockSpec((tm, LANES), lambda i: (i % tiles_per_batch, 0)) for _ in ropes]
        args += list(ropes)
    out_shape = [jax.ShapeDtypeStruct((rows, w), BF16) for w in out_widths]
    out_specs = [pl.BlockSpec((tm, w), lambda i: (i, 0)) for w in out_widths]
    if states:
        out_shape += [jax.ShapeDtypeStruct((rows, w), F32) for w in state_widths]
        out_specs += [pl.BlockSpec((tm, w), lambda i: (i, 0)) for w in state_widths]
    return pl.pallas_call(
        functools.partial(kernel, rope=rope, states=states, row_base=row_base, tiles_per_batch=tiles_per_batch),
        out_shape=out_shape,
        grid=(rows // tm,),
        in_specs=in_specs,
        out_specs=out_specs,
        compiler_params=_cparams(),
        name=kernel.__name__.strip("_"),
    )(*args)


def _mla_cache_kernel(c_ref, wkvb_ref, bd64_ref, g_ref, kn_ref, vm_ref):
    _mla_kv(c_ref[...], wkvb_ref, bd64_ref, g_ref[5:6, 0:256], kn_ref, vm_ref)


def _mla_cache_kv(ckv, wkvb, bd64, gains):
    rows = ckv.shape[0]
    return pl.pallas_call(
        _mla_cache_kernel,
        out_shape=[jax.ShapeDtypeStruct((rows, 512), BF16)] * 2,
        grid=(1,),
        in_specs=[_full(ckv.shape), _full(wkvb.shape), _full(bd64.shape), _full(gains.shape)],
        out_specs=[_full((rows, 512))] * 2,
        compiler_params=_cparams(),
        name="mla_cache_kv",
    )(ckv, wkvb, bd64, gains)


def _pair_attn_kernel(*refs, mode, n_seg, n_pairs, lam_init):
    it = iter(refs)
    q_ref = next(it)
    segs = []
    for _ in range(n_seg):
        if mode == "mla":
            segs.append((next(it), next(it), next(it)))
        else:
            segs.append((next(it), next(it)))
    gate_ref = next(it)
    if mode == "diff":
        lam_ref, subln_ref = next(it), next(it)
    o_ref = next(it)

    qw = 2 * LANES if mode == "mla" else LANES
    rows = q_ref.shape[0]
    lo = _lane_mask(rows, LANES, 0, LANES // 2)
    if mode == "diff":
        lv = lam_ref[...]
        lam = (jnp.exp(jnp.sum(lv[0:1] * lv[1:2], axis=-1, keepdims=True))
               - jnp.exp(jnp.sum(lv[2:3] * lv[3:4], axis=-1, keepdims=True)) + lam_init)

    def values(p):
        sl = slice(LANES * p, LANES * (p + 1))
        return [(seg[-1][...] if mode == "gqa" else seg[-1][:, sl]).astype(BF16) for seg in segs]

    def scores(p, j):
        sl = slice(LANES * p, LANES * (p + 1))
        q = q_ref[:, qw * p:qw * (p + 1)]
        ksegs = []
        for seg in segs:
            if mode == "mla":
                kn, kpe, _ = seg
                ksegs.append(jnp.concatenate([kn[:, sl].astype(BF16), kpe[...].astype(BF16)], axis=-1))
            else:
                ksegs.append((seg[0][...] if mode == "gqa" else seg[0][:, sl]).astype(BF16))
        keep = _lane_mask(rows, qw, 64 * j, 64 * (j + 1))
        if mode == "mla":
            keep = keep | _lane_mask(rows, qw, LANES + 32 * j, LANES + 32 * (j + 1))
        return _scores(jnp.where(keep, q, jnp.zeros_like(q)), ksegs)

    def finish(p, outs):
        sl = slice(LANES * p, LANES * (p + 1))
        if mode == "diff":
            d = outs[0] - lam * outs[1]
            o = _rms_full(d, subln_ref[...]) * (1.0 - lam_init)
        else:
            o = jnp.where(lo, outs[0], outs[1])
        o_ref[:, sl] = (o * gate_ref[:, sl].astype(F32)).astype(BF16)

    heads = [(p, j) for p in range(n_pairs) for j in range(2)]
    ss = scores(*heads[0])
    outs = []
    for t, (p, j) in enumerate(heads):
        ss_next = scores(*heads[t + 1]) if t + 1 < len(heads) else None
        spare = None if mode == "diff" else (64 * (1 - j), 64 * (2 - j))
        outs.append(_softmax_pv(ss, values(p), ones_lanes=spare))
        if j == 1:
            finish(p, outs)
            outs = []
        ss = ss_next


def _pair_attention(mode, q, segs, gate, batch, tq, bq, n_pairs, extra=(), lam_init=0.0):
    qw = 2 * LANES if mode == "mla" else LANES
    nq = tq // bq
    steps_p = 4 // n_pairs
    in_specs = [pl.BlockSpec((bq, qw * n_pairs), lambda b, p, i: (b * nq + i, p))]
    args = [q]
    for seg in segs:
        for arr, spec in seg:
            args.append(arr)
            in_specs.append(spec)
    in_specs.append(pl.BlockSpec((bq, LANES * n_pairs), lambda b, p, i: (b * nq + i, p)))
    args.append(gate)
    for arr in extra:
        args.append(arr)
        in_specs.append(pl.BlockSpec(arr.shape, lambda b, p, i: (0,) * arr.ndim))
    return pl.pallas_call(
        functools.partial(_pair_attn_kernel, mode=mode, n_seg=len(segs), n_pairs=n_pairs, lam_init=lam_init),
        out_shape=jax.ShapeDtypeStruct((batch * tq, 4 * LANES), BF16),
        grid=(batch, steps_p, nq),
        in_specs=in_specs,
        out_specs=pl.BlockSpec((bq, LANES * n_pairs), lambda b, p, i: (b * nq + i, p)),
        compiler_params=_cparams(),
        name=mode + "_attention",
    )(*args)


def _self_seg(arr, tk, n_pairs, shared=False):
    if shared:
        return arr, pl.BlockSpec((tk, arr.shape[1]), lambda b, p, i: (b, 0))
    return arr, pl.BlockSpec((tk, LANES * n_pairs), lambda b, p, i: (b, p))


def _cache_seg(arr, layer, n_pairs, shared=False):
    past = arr.shape[2]
    if shared:
        return arr, pl.BlockSpec((None, None, past, arr.shape[3]), lambda b, p, i: (b, layer, 0, 0))
    return arr, pl.BlockSpec((None, None, past, LANES * n_pairs), lambda b, p, i: (b, layer, 0, p))


def _na_kernel(q_ref, k_ref, v_ref, ck_ref, cv_ref, bias_ref, gate_ref, o_ref):
    g = pl.program_id(1)
    n_groups = pl.num_programs(1)
    first_row = jnp.clip(NA_GROUP_ROWS * g - NA_ROWS // 2, 0, NA_GROUP_ROWS * n_groups - NA_WIN_ROWS)
    start = pl.multiple_of(first_row * GRID_W, GRID_W)
    win = NA_WIN_ROWS * GRID_W
    kwin = k_ref[pl.ds(start, win), :]
    vwin = v_ref[pl.ds(start, win), :]
    ck = ck_ref[...].astype(BF16)
    cv = cv_ref[...].astype(BF16)
    q = q_ref[...]
    rows = q.shape[0]
    lo = _lane_mask(rows, LANES, 0, LANES // 2)

    def scores(h):
        sl = slice(LANES * (h // 2), LANES * (h // 2 + 1))
        keep = _lane_mask(rows, LANES, 64 * (h % 2), 64 * (h % 2 + 1))
        qm = jnp.where(keep, q[:, sl], jnp.zeros_like(q[:, sl]))
        return _scores(qm, [kwin[:, sl], ck[:, sl]], bias0=bias_ref[h])

    ss = scores(0)
    outs = []
    for h in range(NA_HEADS):
        ss_next = scores(h + 1) if h + 1 < NA_HEADS else None
        sl = slice(LANES * (h // 2), LANES * (h // 2 + 1))
        j = h % 2
        outs.append(_softmax_pv(ss, [vwin[:, sl], cv[:, sl]], ones_lanes=(64 * (1 - j), 64 * (2 - j))))
        if j == 1:
            o = jnp.where(lo, outs[0], outs[1])
            o_ref[:, sl] = (o * gate_ref[:, sl].astype(F32)).astype(BF16)
            outs = []
        ss = ss_next


def _na_attention(q, k, v, cache_k, cache_v, layer, bias, gate, batch, tq):
    bq = NA_GROUP_ROWS * GRID_W
    n_groups = tq // bq
    past = cache_k.shape[2]
    width = NA_HEADS * NA_HD

    def bias_map(b, g):
        return (jnp.where(g == 0, 0, jnp.where(g == n_groups - 1, 2, 1)), 0, 0, 0)

    tok = pl.BlockSpec((bq, width), lambda b, g: (b * n_groups + g, 0))
    whole = pl.BlockSpec((tq, width), lambda b, g: (b, 0))
    cache = pl.BlockSpec((None, None, past, width), lambda b, g: (b, layer, 0, 0))
    return pl.pallas_call(
        _na_kernel,
        out_shape=jax.ShapeDtypeStruct((batch * tq, width), BF16),
        grid=(batch, n_groups),
        in_specs=[tok, whole, whole, cache, cache,
                  pl.BlockSpec((None, NA_HEADS, bq, NA_WIN_ROWS * GRID_W), bias_map), tok],
        out_specs=tok,
        compiler_params=_cparams(),
        name="na_attention",
    )(q, k, v, cache_k, cache_v, bias, gate)


def _na_bias_tables(rpb):
    qc = np.arange(GRID_W)[:, None]
    kc = np.arange(GRID_W)[None, :]
    cs = np.clip(qc - NA_COLS // 2, 0, GRID_W - NA_COLS)
    col_valid = (kc >= cs) & (kc < cs + NA_COLS)
    dc = kc - qc + NA_COLS - 1
    onehot = np.zeros((2 * NA_COLS - 1, GRID_W, GRID_W), np.float32)
    for d in range(2 * NA_COLS - 1):
        onehot[d] = (col_valid & (dc == d)).astype(np.float32)
    cols = jnp.einsum("hrd,dqk->hrqk", rpb.astype(F32), jnp.asarray(onehot), precision=lax.Precision.HIGHEST)
    cols = jnp.where(jnp.asarray(col_valid)[None, None], cols * LOG2E, NEG)
    outside = jnp.full((NA_HEADS, 1, GRID_W, GRID_W), NEG, F32)
    cols = jnp.concatenate([cols, outside], axis=1)
    n_dr = 2 * NA_ROWS - 1
    idx = np.full((3, NA_GROUP_ROWS, NA_WIN_ROWS), n_dr, np.int32)
    for a in range(NA_GROUP_ROWS):
        for j in range(NA_WIN_ROWS):
            if j < NA_ROWS:
                idx[0, a, j] = j - a + NA_ROWS - 1
            if a <= j < a + NA_ROWS:
                idx[1, a, j] = j - a + NA_ROWS // 2 - 1
            if j >= NA_WIN_ROWS - NA_ROWS:
                idx[2, a, j] = j - a - (NA_WIN_ROWS - NA_ROWS) + NA_ROWS // 2 - 1
    t = jnp.take(cols, jnp.asarray(idx.reshape(-1)), axis=1)
    t = t.reshape(NA_HEADS, 3, NA_GROUP_ROWS, NA_WIN_ROWS, GRID_W, GRID_W)
    t = t.transpose(1, 0, 2, 4, 3, 5)
    return t.reshape(3, NA_HEADS, NA_GROUP_ROWS * GRID_W, NA_WIN_ROWS * GRID_W)


def _out_kernel(oa_ref, ob_ref, x_ref, mod_ref, w_ref, y_ref, *, row_base, tiles_per_batch):
    row = row_base + pl.program_id(0) // tiles_per_batch
    gate = mod_ref[0, pl.ds(row, 1), :][:, 2 * D_MODEL:]
    half = oa_ref.shape[1]
    acc = _dot(oa_ref[...], w_ref[:half, :]) + _dot(ob_ref[...], w_ref[half:, :])
    y_ref[...] = x_ref[...] + gate * acc


def _out_proj(oa, ob, x, mods, layer, w, row_base, tokens_per_batch):
    rows = x.shape[0]
    tm = ROW_TILE
    tiles_per_batch = tokens_per_batch // tm
    return pl.pallas_call(
        functools.partial(_out_kernel, row_base=row_base, tiles_per_batch=tiles_per_batch),
        out_shape=jax.ShapeDtypeStruct((rows, D_MODEL), F32),
        grid=(rows // tm,),
        in_specs=[pl.BlockSpec((tm, oa.shape[1]), lambda i: (i, 0)),
                  pl.BlockSpec((tm, ob.shape[1]), lambda i: (i, 0)),
                  pl.BlockSpec((tm, D_MODEL), lambda i: (i, 0)),
                  pl.BlockSpec((1, 8, 3 * D_MODEL), lambda i: (layer, 0, 0)),
                  _full(w.shape)],
        out_specs=pl.BlockSpec((tm, D_MODEL), lambda i: (i, 0)),
        compiler_params=_cparams(),
        name="out_proj",
    )(oa, ob, x, mods, w)


def _block_diag(width, group):
    idx = np.arange(width) // group
    return jnp.asarray((idx[:, None] == idx[None, :]).astype(np.float32) / group, BF16)


def _rope_tables(t, rot_dim):
    pos = np.arange(t)
    row = (pos // GRID_W).astype(np.float64)
    col = (pos % GRID_W).astype(np.float64)
    n = rot_dim // 2
    inv = ROPE_THETA ** (-np.arange(0, n, 2, dtype=np.float64) / n)
    ang = np.concatenate([row[:, None] * inv, col[:, None] * inv], axis=-1)
    cos = np.concatenate([np.cos(ang), np.cos(ang)], axis=-1)
    sin = np.concatenate([-np.sin(ang), np.sin(ang)], axis=-1)
    reps = LANES // rot_dim
    return (jnp.asarray(np.tile(cos, (1, reps)), F32), jnp.asarray(np.tile(sin, (1, reps)), F32))


def _pad_row(v, width=D_MODEL):
    return jnp.pad(v, (0, width - v.shape[0]))


def _tile_row(v, reps):
    return _pad_row(jnp.tile(v, reps))


def _permute_heads(w, axis):
    shape = w.shape
    split = shape[:axis] + (GQA_HEADS, GQA_HD) + shape[axis + 1:]
    return jnp.take(w.reshape(split), jnp.asarray(GQA_PERM), axis=axis).reshape(shape)


def kernel(x_prompt, x_sample, cache_mla_ckv, cache_mla_kpe, cache_diff_k, cache_diff_v, cache_na_k, cache_na_v, cache_gqa_k, cache_gqa_v, c, c_ctx, norm_w, w_mod, b_mod, w_in_even, w_out_even, mla_qa_norm, mla_wqb, mla_kva_norm, mla_wkvb, mla_qn_nope, mla_qn_rope, mla_kn_nope, mla_kn_rope, diff_qn, diff_kn, diff_lq1, diff_lk1, diff_lq2, diff_lk2, diff_subln, w_in_odd, w_out_odd, na_qn, na_kn, na_rpb, gqa_qn, gqa_kn):
    batch, seq, _ = x_prompt.shape
    dec_batch, dec_seq, _ = x_sample.shape
    past = cache_mla_ckv.shape[2]
    n_even, n_odd = w_in_even.shape[0], w_in_odd.shape[0]

    w_in_e = jnp.concatenate([w_in_even[..., :384], jnp.tile(w_in_even[..., 384:416], (1, 1, 4)),
                              w_in_even[..., 416:]], axis=-1).astype(BF16)
    wqb = mla_wqb.reshape(n_even, Q_LORA, MLA_HEADS, MLA_QK)
    wqb = jnp.concatenate([wqb[..., :MLA_NOPE].reshape(n_even, Q_LORA, 4, 2 * MLA_NOPE),
                           wqb[..., MLA_NOPE:].reshape(n_even, Q_LORA, 4, 2 * MLA_ROPE),
                           jnp.zeros((n_even, Q_LORA, 4, LANES - 2 * MLA_ROPE), F32)], axis=-1)
    wqb = wqb.reshape(n_even, Q_LORA, 4 * 2 * LANES).astype(BF16)
    wkvb = mla_wkvb.reshape(n_even, KV_LORA, MLA_HEADS, 2 * MLA_NOPE)
    wkvb = jnp.concatenate([wkvb[..., :MLA_NOPE].reshape(n_even, KV_LORA, 512),
                            wkvb[..., MLA_NOPE:].reshape(n_even, KV_LORA, 512)], axis=-1).astype(BF16)
    w_out_e = w_out_even.astype(BF16)
    w_in_o = jnp.concatenate([w_in_odd[..., :2048], _permute_heads(w_in_odd[..., 2048:2560], 2),
                              w_in_odd[..., 2560:2816], _permute_heads(w_in_odd[..., 2816:], 2)],
                             axis=-1).astype(BF16)
    w_out_o = jnp.concatenate([w_out_odd[:, :512], _permute_heads(w_out_odd[:, 512:], 1)], axis=1).astype(BF16)

    gains_e = [jnp.stack([norm_w[2 * i], _pad_row(mla_qa_norm[i]),
                          _pad_row(jnp.concatenate([jnp.tile(mla_qn_nope[i], 2), jnp.tile(mla_qn_rope[i], 4)])),
                          _pad_row(mla_kva_norm[i]), _tile_row(mla_kn_rope[i], 4), _tile_row(mla_kn_nope[i], 4),
                          _tile_row(diff_qn[i], 4), _tile_row(diff_kn[i], 4)]) for i in range(n_even)]
    gains_o = [jnp.stack([norm_w[2 * i + 1], _tile_row(na_qn[i], 4), _tile_row(na_kn[i], 4),
                          _tile_row(gqa_qn[i], 4), _tile_row(gqa_kn[i], 2),
                          jnp.zeros((D_MODEL,), F32), jnp.zeros((D_MODEL,), F32), jnp.zeros((D_MODEL,), F32)])
               for i in range(n_odd)]
    lam_vecs = [jnp.stack([diff_lq1[i], diff_lk1[i], diff_lq2[i], diff_lk2[i]]) for i in range(n_even)]

    bd64 = _block_diag(256, 64)
    bd32 = _block_diag(LANES, 32)
    bdq = jnp.concatenate([jnp.concatenate([_block_diag(LANES, 64), jnp.zeros((LANES, LANES), BF16)], axis=1),
                           jnp.concatenate([jnp.zeros((LANES, LANES), BF16), _block_diag(LANES, 32)], axis=1)], axis=0)
    c64, s64 = _rope_tables(dec_seq, 64)
    c32, s32 = _rope_tables(dec_seq, MLA_ROPE)
    na_bias = [_na_bias_tables(na_rpb[i]) for i in range(n_odd)]

    cvecs = jnp.concatenate([c_ctx[None, :], c, jnp.zeros((8 - 1 - dec_batch, D_MODEL), F32)], axis=0)
    mods = _modulation(cvecs, w_mod, b_mod)

    cache_kpe = jnp.tile(cache_mla_kpe, (1, 1, 1, 4))
    cache_dk = cache_diff_k.reshape(dec_batch, n_even, past, 512)
    cache_dv = cache_diff_v.reshape(dec_batch, n_even, past, 512)
    cache_nk = cache_na_k.reshape(dec_batch, n_odd, past, 512)
    cache_nv = cache_na_v.reshape(dec_batch, n_odd, past, 512)
    cache_gk = cache_gqa_k.reshape(dec_batch, n_odd, past, LANES)
    cache_gv = cache_gqa_v.reshape(dec_batch, n_odd, past, LANES)

    even_widths = (1024, 512, LANES, 512, 512, 512, 512, 512, 512)
    odd_widths = (512, 512, 512, 512, 512, LANES, LANES, 512)

    def run_pass(x, nb, t, ctx):
        row_base = 0 if ctx else 1
        states = [[] for _ in range(8)]
        bq = t if ctx else Q_BLOCK
        n_pairs = 4
        t_mod = nb * t if ctx else t
        for l in range(DEPTH):
            i = l // 2
            if l % 2 == 0:
                lam_init = 0.8 - 0.6 * math.exp(-0.3 * l)
                outs = _in_proj(_in_even_kernel, x, mods, l, gains_e[i], (w_in_e[i], wqb[i], wkvb[i]),
                                (bd64, bd32, bdq), None if ctx else (c64, s64, c32, s32),
                                even_widths, (LANES, MLA_ROPE, 512, 512) if ctx else None, row_base, t_mod)
                qcat, kn, kpe, vm, mg, dq, dk, dv, dg = outs[:9]
                if ctx:
                    for j in range(4):
                        states[j].append(outs[9 + j])
                mla_segs = [(_self_seg(kn, t, n_pairs), _self_seg(kpe, t, n_pairs, shared=True),
                             _self_seg(vm, t, n_pairs))]
                diff_segs = [(_self_seg(dk, t, n_pairs), _self_seg(dv, t, n_pairs))]
                if not ctx:
                    kn_c, vm_c = _mla_cache_kv(cache_mla_ckv[:, i].reshape(nb * past, KV_LORA), wkvb[i], bd64, gains_e[i])
                    mla_segs.append((_self_seg(kn_c, past, n_pairs), _cache_seg(cache_kpe, i, n_pairs, shared=True),
                                     _self_seg(vm_c, past, n_pairs)))
                    diff_segs.append((_cache_seg(cache_dk, i, n_pairs), _cache_seg(cache_dv, i, n_pairs)))
                oa = _pair_attention("mla", qcat, mla_segs, mg, nb, t, bq, n_pairs)
                ob = _pair_attention("diff", dq, diff_segs, dg, nb, t, bq, n_pairs,
                                     extra=(lam_vecs[i], diff_subln[i][None, :]), lam_init=lam_init)
                x = _out_proj(oa, ob, x, mods, l, w_out_e[i], row_base, t_mod)
            else:
                outs = _in_proj(_in_odd_kernel, x, mods, l, gains_o[i], (w_in_o[i],), (bd64,),
                                None if ctx else (c64, s64), odd_widths,
                                (512, 512, LANES, LANES) if ctx else None, row_base, t_mod)
                nq, nk, nv, ng, gq, gk, gv, gg = outs[:8]
                if ctx:
                    for j in range(4):
                        states[4 + j].append(outs[8 + j])
                    oa = _pair_attention("mha", nq, [(_self_seg(nk, t, n_pairs), _self_seg(nv, t, n_pairs))],
                                         ng, nb, t, bq, n_pairs)
                else:
                    oa = _na_attention(nq, nk, nv, cache_nk, cache_nv, i, na_bias[i], ng, nb, t)
                gqa_segs = [(_self_seg(gk, t, n_pairs, shared=True), _self_seg(gv, t, n_pairs, shared=True))]
                if not ctx:
                    gqa_segs.append((_cache_seg(cache_gk, i, n_pairs, shared=True),
                                     _cache_seg(cache_gv, i, n_pairs, shared=True)))
                ob = _pair_attention("gqa", gq, gqa_segs, gg, nb, t, bq, n_pairs)
                x = _out_proj(oa, ob, x, mods, l, w_out_o[i], row_base, t_mod)
        return x, states

    y_prompt, st = run_pass(x_prompt.reshape(batch * seq, D_MODEL), batch, seq, True)
    y_sample, _ = run_pass(x_sample.reshape(dec_batch * dec_seq, D_MODEL), dec_batch, dec_seq, False)

    def stacked(parts, tail):
        return jnp.stack([p.reshape((batch, seq) + tail) for p in parts], axis=1)

    return (y_prompt.reshape(batch, seq, D_MODEL), y_sample.reshape(dec_batch, dec_seq, D_MODEL),
            stacked(st[0], (KV_LORA,)), stacked(st[1], (MLA_ROPE,)),
            stacked(st[2], (DIFF_HEADS, 2, DIFF_HD)), stacked(st[3], (DIFF_HEADS, 2 * DIFF_HD)),
            stacked(st[4], (NA_HEADS, NA_HD)), stacked(st[5], (NA_HEADS, NA_HD)),
            stacked(st[6], (GQA_KV, GQA_HD)), stacked(st[7], (GQA_KV, GQA_HD)))
```

```python
import functools
import math

import jax
import jax.numpy as jnp
import numpy as np
from jax import lax
from jax.experimental import pallas as pl
from jax.experimental.pallas import tpu as pltpu

F32 = jnp.float32
BF16 = jnp.bfloat16

D_MODEL = 1024
DEPTH = 4
GRID_W = 64
ROPE_THETA = 10000.0
EPS = 1e-6
MLA_HEADS = 8
MLA_NOPE = 64
MLA_ROPE = 32
MLA_QK = MLA_NOPE + MLA_ROPE
Q_LORA = 256
KV_LORA = 128
DIFF_HEADS = 4
DIFF_HD = 64
NA_HEADS = 8
NA_HD = 64
NA_ROWS = 8
NA_COLS = 16
GQA_HEADS = 8
GQA_KV = 2
GQA_HD = 64
LANES = 128
NA_GROUP_ROWS = 4
NA_WIN_ROWS = NA_ROWS + NA_GROUP_ROWS
NEG = -1e30
LOG2E = math.log2(math.e)
VMEM_LIMIT = 48 * 1024 * 1024
GQA_PERM = (0, 4, 1, 5, 2, 6, 3, 7)

ROW_TILE = 512
Q_BLOCK = 256


def _cparams():
    return pltpu.CompilerParams(vmem_limit_bytes=VMEM_LIMIT)


def _dot(a, b):
    return jnp.dot(a, b, preferred_element_type=F32)


def _dot_nt(a, b):
    return lax.dot_general(a, b, (((1,), (1,)), ((), ())), preferred_element_type=F32)


def _rms_full(x, g):
    ms = jnp.mean(x * x, axis=-1, keepdims=True)
    return x * lax.rsqrt(ms + EPS) * g


def _rms_group(x, bd, g):
    ms = _dot((x * x).astype(BF16), bd)
    return x * lax.rsqrt(ms + EPS) * g


def _silu(u):
    return u * (1.0 / (1.0 + jnp.exp(-u)))


def _rope(x, cos, sin, group):
    half = group // 2
    rows, width = x.shape
    lane = lax.broadcasted_iota(jnp.int32, (rows, LANES), 1)
    first = (lane & (group - 1)) < half
    outs = []
    for c in range(width // LANES):
        xc = x[:, c * LANES:(c + 1) * LANES]
        rot = jnp.where(first, pltpu.roll(xc, LANES - half, 1), pltpu.roll(xc, half, 1))
        outs.append(xc * cos + rot * sin)
    return outs[0] if len(outs) == 1 else jnp.concatenate(outs, axis=-1)


def _lane_mask(rows, width, lo, hi):
    lane = lax.broadcasted_iota(jnp.int32, (rows, width), 1)
    return (lane >= lo) & (lane < hi)


def _scores(qm, ksegs, bias0=None):
    ss = [_dot(qm, k) if transposed else _dot_nt(qm, k) for k, transposed in ksegs]
    if bias0 is not None:
        ss[0] = ss[0] + bias0
    return ss


def _softmax_pv(ss, vsegs, ones_lanes=None):
    m = jnp.max(ss[0], axis=-1, keepdims=True)
    for s in ss[1:]:
        m = jnp.maximum(m, jnp.max(s, axis=-1, keepdims=True))
    acc = None
    l = None
    for s, (v, transposed) in zip(ss, vsegs):
        p = jnp.exp2(s - m)
        if ones_lanes is None:
            ps = jnp.sum(p, axis=-1, keepdims=True)
            l = ps if l is None else l + ps
        else:
            width = lax.broadcasted_iota(jnp.int32, v.shape, 0 if transposed else 1)
            v = jnp.where((width >= ones_lanes[0]) & (width < ones_lanes[1]), jnp.ones_like(v), v)
        a = _dot_nt(p.astype(BF16), v) if transposed else _dot(p.astype(BF16), v)
        acc = a if acc is None else acc + a
    if ones_lanes is not None:
        l = pltpu.roll(acc, LANES // 2, 1)
    return acc / l


def _mod_kernel(c_ref, w_ref, b_ref, o_ref):
    c = c_ref[...]
    o_ref[0] = _dot(_silu(c).astype(BF16), w_ref[0].astype(BF16)) + b_ref[0]


def _modulation(cvecs, w_mod, b_mod):
    tn = 768
    return pl.pallas_call(
        _mod_kernel,
        out_shape=jax.ShapeDtypeStruct((DEPTH, 8, 3 * D_MODEL), F32),
        grid=(DEPTH, 3 * D_MODEL // tn),
        in_specs=[pl.BlockSpec((8, D_MODEL), lambda l, n: (0, 0)),
                  pl.BlockSpec((1, D_MODEL, tn), lambda l, n: (l, 0, n)),
                  pl.BlockSpec((1, 1, tn), lambda l, n: (l, 0, n))],
        out_specs=pl.BlockSpec((1, 8, tn), lambda l, n: (l, 0, n)),
        compiler_params=_cparams(),
        name="modulation",
    )(cvecs, w_mod, b_mod.reshape(DEPTH, 1, 3 * D_MODEL))


def _modulated_norm(x_ref, mod_ref, g_ref, row):
    mod = mod_ref[0, pl.ds(row, 1), :]
    shift = mod[:, :D_MODEL]
    scale = mod[:, D_MODEL:2 * D_MODEL]
    h = _rms_full(x_ref[...], g_ref[0:1, :]) * (1.0 + scale) + shift
    return h.astype(BF16)


def _mla_kv(cn, wkvb_ref, bd64_ref, g_kn, kn_ref, vm_ref):
    kv = _dot(cn.astype(BF16), wkvb_ref[...])
    for c in range(2):
        sl = slice(256 * c, 256 * (c + 1))
        kn_ref[:, sl] = _rms_group(kv[:, sl], bd64_ref[...], g_kn).astype(BF16)
    vm_ref[...] = kv[:, 512:].astype(BF16)


def _store_rows(ref, x):
    seq = ref.shape[1]
    for b in range(ref.shape[0]):
        ref[b] = x[b * seq:(b + 1) * seq]


def _store_transposed(ref, x, row0, keep=None):
    seq = ref.shape[2]
    for b in range(ref.shape[0]):
        xt = x[b * seq:(b + 1) * seq].T
        if keep is not None:
            xt = xt[:keep]
        ref[b, row0:row0 + xt.shape[0], :] = xt


def _store_heads4(ref, x):
    seq = ref.shape[1] // 4
    for b in range(ref.shape[0]):
        for h in range(4):
            ref[b, pl.ds(h, seq, stride=4), :] = x[b * seq:(b + 1) * seq, LANES * h:LANES * (h + 1)]


def _in_even_kernel(*refs, rope, states, n_alias, row_base, tiles_per_batch):
    it = iter(refs)
    x_ref, mod_ref, g_ref, w_ref, wqb_ref, wkvb_ref, bd64_ref, bd32_ref, bdq_ref = (next(it) for _ in range(9))
    if rope:
        c64_ref, s64_ref, c32_ref, s32_ref = (next(it) for _ in range(4))
    for _ in range(n_alias):
        next(it)
    qcat_ref, kn_ref, kpe_ref, vm_ref, mg_ref, dq_ref, dk_ref, dv_ref, dg_ref = (next(it) for _ in range(9))
    if states:
        st_ckv_ref, st_kpe_ref, st_dk_ref, st_dv_ref = (next(it) for _ in range(4))

    row = row_base + pl.program_id(0) // tiles_per_batch
    hb = _modulated_norm(x_ref, mod_ref, g_ref, row)
    bd64 = bd64_ref[...]
    mla_scale = MLA_QK ** -0.5 * LOG2E
    diff_scale = DIFF_HD ** -0.5 * LOG2E

    qa = _dot(hb, w_ref[:, 0:256])
    qa_n = _rms_full(qa, g_ref[1:2, 0:256]).astype(BF16)
    q = _dot(qa_n, wqb_ref[...])
    for p in range(4):
        qp = _rms_group(q[:, 256 * p:256 * (p + 1)], bdq_ref[...], g_ref[2:3, 0:256])
        q_nope = qp[:, :LANES]
        q_pe = qp[:, LANES:]
        if rope:
            q_pe = _rope(q_pe, c32_ref[...], s32_ref[...], MLA_ROPE)
        qcat_ref[:, 256 * p:256 * p + LANES] = (q_nope * mla_scale).astype(BF16)
        qcat_ref[:, 256 * p + LANES:256 * (p + 1)] = (q_pe * mla_scale).astype(BF16)

    kva = _dot(hb, w_ref[:, 256:512])
    c_kv = _rms_full(kva[:, :LANES], g_ref[3:4, 0:LANES])
    k_pe = _rms_group(kva[:, LANES:], bd32_ref[...], g_ref[4:5, 0:LANES])
    if states:
        _store_rows(st_ckv_ref, c_kv)
        _store_transposed(st_kpe_ref, k_pe, 0, keep=MLA_ROPE)
    if rope:
        k_pe = _rope(k_pe, c32_ref[...], s32_ref[...], MLA_ROPE)
    kpe_ref[...] = k_pe.astype(BF16)
    _mla_kv(c_kv, wkvb_ref, bd64_ref, g_ref[5:6, 0:256], kn_ref, vm_ref)

    mg_ref[...] = _silu(_dot(hb, w_ref[:, 512:1024])).astype(BF16)

    dq = _dot(hb, w_ref[:, 1024:1536])
    dk = _dot(hb, w_ref[:, 1536:2048])
    for c in range(2):
        sl = slice(256 * c, 256 * (c + 1))
        qn = _rms_group(dq[:, sl], bd64, g_ref[6:7, 0:256])
        kn = _rms_group(dk[:, sl], bd64, g_ref[7:8, 0:256])
        if states:
            _store_transposed(st_dk_ref, kn, 256 * c)
        if rope:
            qn = _rope(qn, c64_ref[...], s64_ref[...], DIFF_HD)
            kn = _rope(kn, c64_ref[...], s64_ref[...], DIFF_HD)
        dq_ref[:, sl] = (qn * diff_scale).astype(BF16)
        dk_ref[:, sl] = kn.astype(BF16)
    dv = _dot(hb, w_ref[:, 2048:2560])
    if states:
        _store_heads4(st_dv_ref, dv)
    dv_ref[...] = dv.astype(BF16)
    dg_ref[...] = _silu(_dot(hb, w_ref[:, 2560:3072])).astype(BF16)


def _in_odd_kernel(*refs, rope, states, n_alias, row_base, tiles_per_batch):
    it = iter(refs)
    x_ref, mod_ref, g_ref, w_ref, bd64_ref = (next(it) for _ in range(5))
    if rope:
        c64_ref, s64_ref = (next(it) for _ in range(2))
    for _ in range(n_alias):
        next(it)
    nq_ref, nk_ref, nv_ref, ng_ref, gq_ref, gk_ref, gv_ref, gg_ref = (next(it) for _ in range(8))
    if states:
        st_nk_ref, st_nv_ref, st_gk_ref, st_gv_ref = (next(it) for _ in range(4))

    row = row_base + pl.program_id(0) // tiles_per_batch
    hb = _modulated_norm(x_ref, mod_ref, g_ref, row)
    bd64 = bd64_ref[...]
    na_scale = NA_HD ** -0.5 * LOG2E
    gqa_scale = GQA_HD ** -0.5 * LOG2E

    nq = _dot(hb, w_ref[:, 0:512])
    nk = _dot(hb, w_ref[:, 512:1024])
    for c in range(2):
        sl = slice(256 * c, 256 * (c + 1))
        nq_ref[:, sl] = (_rms_group(nq[:, sl], bd64, g_ref[1:2, 0:256]) * na_scale).astype(BF16)
        kn = _rms_group(nk[:, sl], bd64, g_ref[2:3, 0:256])
        if states:
            _store_transposed(st_nk_ref, kn, 256 * c)
        nk_ref[:, sl] = kn.astype(BF16)
    nv = _dot(hb, w_ref[:, 1024:1536])
    if states:
        for c in range(2):
            _store_transposed(st_nv_ref, nv[:, 256 * c:256 * (c + 1)], 256 * c)
    nv_ref[...] = nv.astype(BF16)
    ng_ref[...] = _silu(_dot(hb, w_ref[:, 1536:2048])).astype(BF16)

    gq = _dot(hb, w_ref[:, 2048:2560])
    for c in range(2):
        sl = slice(256 * c, 256 * (c + 1))
        qn = _rms_group(gq[:, sl], bd64, g_ref[3:4, 0:256])
        if rope:
            qn = _rope(qn, c64_ref[...], s64_ref[...], GQA_HD)
        gq_ref[:, sl] = (qn * gqa_scale).astype(BF16)
    gkv = _dot(hb, w_ref[:, 2560:2816])
    gk = _rms_group(gkv[:, :LANES], bd64[:LANES, :LANES], g_ref[4:5, 0:LANES])
    gv = gkv[:, LANES:]
    if states:
        _store_transposed(st_gk_ref, gk, 0)
        _store_transposed(st_gv_ref, gv, 0)
    if rope:
        gk = _rope(gk, c64_ref[...], s64_ref[...], GQA_HD)
    gk_ref[...] = gk.astype(BF16)
    gv_ref[...] = gv.astype(BF16)
    gg_ref[...] = _silu(_dot(hb, w_ref[:, 2816:3328])).astype(BF16)


def _full(shape):
    zeros = (0,) * len(shape)
    return pl.BlockSpec(shape, lambda i: zeros)


def _in_proj(kernel, x, mods, layer, gains, weights, consts, ropes, out_widths, state_tails, state_prev,
             row_base, tokens_per_batch, seq):
    rows = x.shape[0]
    tm = ROW_TILE
    tiles_per_batch = tokens_per_batch // tm
    rope = ropes is not None
    states = state_tails is not None
    in_specs = [pl.BlockSpec((tm, D_MODEL), lambda i: (i, 0)),
                pl.BlockSpec((1, 8, 3 * D_MODEL), lambda i: (layer, 0, 0)),
                _full(gains.shape)]
    in_specs += [_full(w.shape) for w in weights]
    in_specs += [_full(c.shape) for c in consts]
    args = [x, mods, gains, *weights, *consts]
    if rope:
        in_specs += [pl.BlockSpec((tm, LANES), lambda i: (i % tiles_per_batch, 0)) for _ in ropes]
        args += list(ropes)
    out_shape = [jax.ShapeDtypeStruct((rows, w), BF16) for w in out_widths]
    out_specs = [pl.BlockSpec((tm, w), lambda i: (i, 0)) for w in out_widths]
    aliases = {}
    if states:
        slot = layer // 2
        out_shape += [jax.ShapeDtypeStruct((rows // seq, DEPTH // 2) + tail, F32) for tail in state_tails]
        out_specs += [pl.BlockSpec((tm // seq, None) + tail, lambda i: (i, slot, 0, 0)) for tail in state_tails]
        if state_prev is not None:
            aliases = {len(args) + j: len(out_widths) + j for j in range(len(state_prev))}
            in_specs += [pl.BlockSpec(memory_space=pl.ANY) for _ in state_prev]
            args += list(state_prev)
    return pl.pallas_call(
        functools.partial(kernel, rope=rope, states=states, n_alias=len(aliases), row_base=row_base,
                          tiles_per_batch=tiles_per_batch),
        out_shape=out_shape,
        grid=(rows // tm,),
        in_specs=in_specs,
        out_specs=out_specs,
        input_output_aliases=aliases,
        compiler_params=_cparams(),
        name=kernel.__name__.strip("_"),
    )(*args)


def _mla_cache_kernel(c_ref, wkvb_ref, bd64_ref, g_ref, kn_ref, vm_ref):
    _mla_kv(c_ref[...], wkvb_ref, bd64_ref, g_ref[5:6, 0:256], kn_ref, vm_ref)


def _mla_cache_kv(ckv, wkvb, bd64, gains):
    rows = ckv.shape[0]
    return pl.pallas_call(
        _mla_cache_kernel,
        out_shape=[jax.ShapeDtypeStruct((rows, 512), BF16)] * 2,
        grid=(1,),
        in_specs=[_full(ckv.shape), _full(wkvb.shape), _full(bd64.shape), _full(gains.shape)],
        out_specs=[_full((rows, 512))] * 2,
        compiler_params=_cparams(),
        name="mla_cache_kv",
    )(ckv, wkvb, bd64, gains)


def _pair_attn_kernel(*refs, mode, kinds, n_pairs, mxu_denominator, lam_init):
    it = iter(refs)
    q_ref = next(it)
    segs = [[next(it) for _ in seg_kinds] for seg_kinds in kinds]
    gate_ref = next(it)
    if mode == "diff":
        lam_ref, subln_ref = next(it), next(it)
    o_ref = next(it)

    qw = 2 * LANES if mode == "mla" else LANES
    rows = q_ref.shape[0]
    lo = _lane_mask(rows, LANES, 0, LANES // 2)
    if mode == "diff":
        lv = lam_ref[...]
        lam = (jnp.exp(jnp.sum(lv[0:1] * lv[1:2], axis=-1, keepdims=True))
               - jnp.exp(jnp.sum(lv[2:3] * lv[3:4], axis=-1, keepdims=True)) + lam_init)

    def load(ref, kind, p):
        if kind == "rows4":
            x = ref[pl.ds(p, ref.shape[0] // 4, stride=4), :]
        elif mode == "gqa":
            x = ref[...]
        else:
            sl = slice(LANES * p, LANES * (p + 1))
            x = ref[sl, :] if kind == "cols" else ref[:, sl]
        return x.astype(BF16), kind == "cols"

    def values(p):
        return [load(seg[-1], seg_kinds[-1], p) for seg, seg_kinds in zip(segs, kinds)]

    def scores(p, j):
        q = q_ref[:, qw * p:qw * (p + 1)]
        ksegs = []
        for seg, seg_kinds in zip(segs, kinds):
            k, transposed = load(seg[0], seg_kinds[0], p)
            if mode == "mla":
                k = jnp.concatenate([k, seg[1][...].astype(BF16)], axis=-1)
            ksegs.append((k, transposed))
        keep = _lane_mask(rows, qw, 64 * j, 64 * (j + 1))
        if mode == "mla":
            keep = keep | _lane_mask(rows, qw, LANES + 32 * j, LANES + 32 * (j + 1))
        return _scores(jnp.where(keep, q, jnp.zeros_like(q)), ksegs)

    def finish(p, outs):
        sl = slice(LANES * p, LANES * (p + 1))
        if mode == "diff":
            d = outs[0] - lam * outs[1]
            o = _rms_full(d, subln_ref[...]) * (1.0 - lam_init)
        else:
            o = jnp.where(lo, outs[0], outs[1])
        o_ref[:, sl] = (o * gate_ref[:, sl].astype(F32)).astype(BF16)

    heads = [(p, j) for p in range(n_pairs) for j in range(2)]
    ss = scores(*heads[0])
    outs = []
    for t, (p, j) in enumerate(heads):
        ss_next = scores(*heads[t + 1]) if t + 1 < len(heads) else None
        spare = (64 * (1 - j), 64 * (2 - j)) if mxu_denominator and mode != "diff" else None
        outs.append(_softmax_pv(ss, values(p), ones_lanes=spare))
        ss = ss_next
        if j == 1:
            finish(p, outs)
            outs = []


def _pair_attention(mode, q, segs, gate, batch, tq, bq, n_pairs, mxu_denominator, extra=(), lam_init=0.0):
    qw = 2 * LANES if mode == "mla" else LANES
    nq = tq // bq
    steps_p = 4 // n_pairs
    in_specs = [pl.BlockSpec((bq, qw * n_pairs), lambda b, p, i: (b * nq + i, p))]
    args = [q]
    for seg in segs:
        for arr, spec, _ in seg:
            args.append(arr)
            in_specs.append(spec)
    kinds = tuple(tuple(kind for _, _, kind in seg) for seg in segs)
    in_specs.append(pl.BlockSpec((bq, LANES * n_pairs), lambda b, p, i: (b * nq + i, p)))
    args.append(gate)
    for arr in extra:
        args.append(arr)
        in_specs.append(pl.BlockSpec(arr.shape, lambda b, p, i: (0,) * arr.ndim))
    return pl.pallas_call(
        functools.partial(_pair_attn_kernel, mode=mode, kinds=kinds, n_pairs=n_pairs,
                          mxu_denominator=mxu_denominator, lam_init=lam_init),
        out_shape=jax.ShapeDtypeStruct((batch * tq, 4 * LANES), BF16),
        grid=(batch, steps_p, nq),
        in_specs=in_specs,
        out_specs=pl.BlockSpec((bq, LANES * n_pairs), lambda b, p, i: (b * nq + i, p)),
        compiler_params=_cparams(),
        name=mode + "_attention",
    )(*args)


def _self_seg(arr, tk, n_pairs, shared=False):
    if shared:
        return arr, pl.BlockSpec((tk, arr.shape[1]), lambda b, p, i: (b, 0)), "rows"
    return arr, pl.BlockSpec((tk, LANES * n_pairs), lambda b, p, i: (b, p)), "rows"


def _cache_seg(arr, layer, n_pairs, kind, shared=False):
    r, c = arr.shape[2:]
    if shared or kind == "rows4":
        return arr, pl.BlockSpec((None, None, r, c), lambda b, p, i: (b, layer, 0, 0)), kind
    if kind == "cols":
        return arr, pl.BlockSpec((None, None, LANES * n_pairs, c), lambda b, p, i: (b, layer, p, 0)), kind
    return arr, pl.BlockSpec((None, None, r, LANES * n_pairs), lambda b, p, i: (b, layer, 0, p)), kind


def _na_kernel(q_ref, k_ref, v_ref, ck_ref, cv_ref, bias_ref, gate_ref, o_ref):
    g = pl.program_id(1)
    n_groups = pl.num_programs(1)
    first_row = jnp.clip(NA_GROUP_ROWS * g - NA_ROWS // 2, 0, NA_GROUP_ROWS * n_groups - NA_WIN_ROWS)
    start = pl.multiple_of(first_row * GRID_W, GRID_W)
    win = NA_WIN_ROWS * GRID_W
    kwin = k_ref[pl.ds(start, win), :]
    vwin = v_ref[pl.ds(start, win), :]
    ck = ck_ref[...].astype(BF16)
    cv = cv_ref[...].astype(BF16)
    q = q_ref[...]
    rows = q.shape[0]
    lo = _lane_mask(rows, LANES, 0, LANES // 2)

    def scores(h):
        sl = slice(LANES * (h // 2), LANES * (h // 2 + 1))
        keep = _lane_mask(rows, LANES, 64 * (h % 2), 64 * (h % 2 + 1))
        qm = jnp.where(keep, q[:, sl], jnp.zeros_like(q[:, sl]))
        return _scores(qm, [(kwin[:, sl], False), (ck[sl, :], True)], bias0=bias_ref[h])

    ss = scores(0)
    outs = []
    for h in range(NA_HEADS):
        ss_next = scores(h + 1) if h + 1 < NA_HEADS else None
        sl = slice(LANES * (h // 2), LANES * (h // 2 + 1))
        j = h % 2
        outs.append(_softmax_pv(ss, [(vwin[:, sl], False), (cv[sl, :], True)],
                                ones_lanes=(64 * (1 - j), 64 * (2 - j))))
        if j == 1:
            o = jnp.where(lo, outs[0], outs[1])
            o_ref[:, sl] = (o * gate_ref[:, sl].astype(F32)).astype(BF16)
            outs = []
        ss = ss_next


def _na_attention(q, k, v, cache_k, cache_v, layer, bias, gate, batch, tq):
    bq = NA_GROUP_ROWS * GRID_W
    n_groups = tq // bq
    past = cache_k.shape[3]
    width = NA_HEADS * NA_HD

    def bias_map(b, g):
        return (jnp.where(g == 0, 0, jnp.where(g == n_groups - 1, 2, 1)), 0, 0, 0)

    tok = pl.BlockSpec((bq, width), lambda b, g: (b * n_groups + g, 0))
    whole = pl.BlockSpec((tq, width), lambda b, g: (b, 0))
    cache = pl.BlockSpec((None, None, width, past), lambda b, g: (b, layer, 0, 0))
    return pl.pallas_call(
        _na_kernel,
        out_shape=jax.ShapeDtypeStruct((batch * tq, width), BF16),
        grid=(batch, n_groups),
        in_specs=[tok, whole, whole, cache, cache,
                  pl.BlockSpec((None, NA_HEADS, bq, NA_WIN_ROWS * GRID_W), bias_map), tok],
        out_specs=tok,
        compiler_params=_cparams(),
        name="na_attention",
    )(q, k, v, cache_k, cache_v, bias, gate)


def _na_bias_tables(rpb):
    qc = np.arange(GRID_W)[:, None]
    kc = np.arange(GRID_W)[None, :]
    cs = np.clip(qc - NA_COLS // 2, 0, GRID_W - NA_COLS)
    col_valid = (kc >= cs) & (kc < cs + NA_COLS)
    dc = kc - qc + NA_COLS - 1
    onehot = np.zeros((2 * NA_COLS - 1, GRID_W, GRID_W), np.float32)
    for d in range(2 * NA_COLS - 1):
        onehot[d] = (col_valid & (dc == d)).astype(np.float32)
    cols = jnp.einsum("hrd,dqk->hrqk", rpb.astype(F32), jnp.asarray(onehot), precision=lax.Precision.HIGHEST)
    cols = jnp.where(jnp.asarray(col_valid)[None, None], cols * LOG2E, NEG)
    outside = jnp.full((NA_HEADS, 1, GRID_W, GRID_W), NEG, F32)
    cols = jnp.concatenate([cols, outside], axis=1)
    n_dr = 2 * NA_ROWS - 1
    idx = np.full((3, NA_GROUP_ROWS, NA_WIN_ROWS), n_dr, np.int32)
    for a in range(NA_GROUP_ROWS):
        for j in range(NA_WIN_ROWS):
            if j < NA_ROWS:
                idx[0, a, j] = j - a + NA_ROWS - 1
            if a <= j < a + NA_ROWS:
                idx[1, a, j] = j - a + NA_ROWS // 2 - 1
            if j >= NA_WIN_ROWS - NA_ROWS:
                idx[2, a, j] = j - a - (NA_WIN_ROWS - NA_ROWS) + NA_ROWS // 2 - 1
    return pl.pallas_call(
        _na_bias_kernel,
        out_shape=jax.ShapeDtypeStruct((3, NA_HEADS, NA_GROUP_ROWS * GRID_W, NA_WIN_ROWS * GRID_W), F32),
        grid_spec=pltpu.PrefetchScalarGridSpec(
            num_scalar_prefetch=1,
            grid=(3, NA_HEADS),
            in_specs=[pl.BlockSpec((None, n_dr + 1, GRID_W, GRID_W), lambda t, h, idx: (h, 0, 0, 0))],
            out_specs=pl.BlockSpec((None, None, NA_GROUP_ROWS * GRID_W, NA_WIN_ROWS * GRID_W),
                                   lambda t, h, idx: (t, h, 0, 0))),
        compiler_params=_cparams(),
        name="na_bias",
    )(jnp.asarray(idx.reshape(-1)), cols)


def _na_bias_kernel(idx_ref, cols_ref, out_ref):
    t = pl.program_id(0)
    for a in range(NA_GROUP_ROWS):
        for j in range(0, NA_WIN_ROWS, 2):
            base = (t * NA_GROUP_ROWS + a) * NA_WIN_ROWS + j
            pair = jnp.concatenate([cols_ref[idx_ref[base]], cols_ref[idx_ref[base + 1]]], axis=-1)
            out_ref[a * GRID_W:(a + 1) * GRID_W, j * GRID_W:(j + 2) * GRID_W] = pair


def _out_kernel(oa_ref, ob_ref, x_ref, mod_ref, w_ref, y_ref, *, row_base, tiles_per_batch):
    row = row_base + pl.program_id(0) // tiles_per_batch
    gate = mod_ref[0, pl.ds(row, 1), :][:, 2 * D_MODEL:]
    half = oa_ref.shape[1]
    acc = _dot(oa_ref[...], w_ref[:half, :]) + _dot(ob_ref[...], w_ref[half:, :])
    y_ref[...] = x_ref[...] + gate * acc


def _out_proj(oa, ob, x, mods, layer, w, row_base, tokens_per_batch):
    rows = x.shape[0]
    tm = ROW_TILE
    tiles_per_batch = tokens_per_batch // tm
    return pl.pallas_call(
        functools.partial(_out_kernel, row_base=row_base, tiles_per_batch=tiles_per_batch),
        out_shape=jax.ShapeDtypeStruct((rows, D_MODEL), F32),
        grid=(rows // tm,),
        in_specs=[pl.BlockSpec((tm, oa.shape[1]), lambda i: (i, 0)),
                  pl.BlockSpec((tm, ob.shape[1]), lambda i: (i, 0)),
                  pl.BlockSpec((tm, D_MODEL), lambda i: (i, 0)),
                  pl.BlockSpec((1, 8, 3 * D_MODEL), lambda i: (layer, 0, 0)),
                  _full(w.shape)],
        out_specs=pl.BlockSpec((tm, D_MODEL), lambda i: (i, 0)),
        compiler_params=_cparams(),
        name="out_proj",
    )(oa, ob, x, mods, w)


def _block_diag(width, group):
    idx = np.arange(width) // group
    return jnp.asarray((idx[:, None] == idx[None, :]).astype(np.float32) / group, BF16)


def _rope_tables(t, rot_dim):
    pos = np.arange(t)
    row = (pos // GRID_W).astype(np.float64)
    col = (pos % GRID_W).astype(np.float64)
    n = rot_dim // 2
    inv = ROPE_THETA ** (-np.arange(0, n, 2, dtype=np.float64) / n)
    ang = np.concatenate([row[:, None] * inv, col[:, None] * inv], axis=-1)
    cos = np.concatenate([np.cos(ang), np.cos(ang)], axis=-1)
    sin = np.concatenate([-np.sin(ang), np.sin(ang)], axis=-1)
    reps = LANES // rot_dim
    return (jnp.asarray(np.tile(cos, (1, reps)), F32), jnp.asarray(np.tile(sin, (1, reps)), F32))


def _pad_row(v, width=D_MODEL):
    return jnp.pad(v, (0, width - v.shape[0]))


def _tile_row(v, reps):
    return _pad_row(jnp.tile(v, reps))


def _permute_heads(w, axis):
    shape = w.shape
    split = shape[:axis] + (GQA_HEADS, GQA_HD) + shape[axis + 1:]
    return jnp.take(w.reshape(split), jnp.asarray(GQA_PERM), axis=axis).reshape(shape)


def kernel(x_prompt, x_sample, cache_mla_ckv, cache_mla_kpe, cache_diff_k, cache_diff_v, cache_na_k, cache_na_v, cache_gqa_k, cache_gqa_v, c, c_ctx, norm_w, w_mod, b_mod, w_in_even, w_out_even, mla_qa_norm, mla_wqb, mla_kva_norm, mla_wkvb, mla_qn_nope, mla_qn_rope, mla_kn_nope, mla_kn_rope, diff_qn, diff_kn, diff_lq1, diff_lk1, diff_lq2, diff_lk2, diff_subln, w_in_odd, w_out_odd, na_qn, na_kn, na_rpb, gqa_qn, gqa_kn):
    batch, seq, _ = x_prompt.shape
    dec_batch, dec_seq, _ = x_sample.shape
    past = cache_mla_ckv.shape[2]
    n_even, n_odd = w_in_even.shape[0], w_in_odd.shape[0]

    w_in_e = jnp.concatenate([w_in_even[..., :384], jnp.tile(w_in_even[..., 384:416], (1, 1, 4)),
                              w_in_even[..., 416:]], axis=-1).astype(BF16)
    wqb = mla_wqb.reshape(n_even, Q_LORA, MLA_HEADS, MLA_QK)
    wqb = jnp.concatenate([wqb[..., :MLA_NOPE].reshape(n_even, Q_LORA, 4, 2 * MLA_NOPE),
                           wqb[..., MLA_NOPE:].reshape(n_even, Q_LORA, 4, 2 * MLA_ROPE),
                           jnp.zeros((n_even, Q_LORA, 4, LANES - 2 * MLA_ROPE), F32)], axis=-1)
    wqb = wqb.reshape(n_even, Q_LORA, 4 * 2 * LANES).astype(BF16)
    wkvb = mla_wkvb.reshape(n_even, KV_LORA, MLA_HEADS, 2 * MLA_NOPE)
    wkvb = jnp.concatenate([wkvb[..., :MLA_NOPE].reshape(n_even, KV_LORA, 512),
                            wkvb[..., MLA_NOPE:].reshape(n_even, KV_LORA, 512)], axis=-1).astype(BF16)
    w_out_e = w_out_even.astype(BF16)
    w_in_o = jnp.concatenate([w_in_odd[..., :2048], _permute_heads(w_in_odd[..., 2048:2560], 2),
                              w_in_odd[..., 2560:2816], _permute_heads(w_in_odd[..., 2816:], 2)],
                             axis=-1).astype(BF16)
    w_out_o = jnp.concatenate([w_out_odd[:, :512], _permute_heads(w_out_odd[:, 512:], 1)], axis=1).astype(BF16)

    gains_e = [jnp.stack([norm_w[2 * i], _pad_row(mla_qa_norm[i]),
                          _pad_row(jnp.concatenate([jnp.tile(mla_qn_nope[i], 2), jnp.tile(mla_qn_rope[i], 4)])),
                          _pad_row(mla_kva_norm[i]), _tile_row(mla_kn_rope[i], 4), _tile_row(mla_kn_nope[i], 4),
                          _tile_row(diff_qn[i], 4), _tile_row(diff_kn[i], 4)]) for i in range(n_even)]
    gains_o = [jnp.stack([norm_w[2 * i + 1], _tile_row(na_qn[i], 4), _tile_row(na_kn[i], 4),
                          _tile_row(gqa_qn[i], 4), _tile_row(gqa_kn[i], 2),
                          jnp.zeros((D_MODEL,), F32), jnp.zeros((D_MODEL,), F32), jnp.zeros((D_MODEL,), F32)])
               for i in range(n_odd)]
    lam_vecs = [jnp.stack([diff_lq1[i], diff_lk1[i], diff_lq2[i], diff_lk2[i]]) for i in range(n_even)]

    bd64 = _block_diag(256, 64)
    bd32 = _block_diag(LANES, 32)
    bdq = jnp.concatenate([jnp.concatenate([_block_diag(LANES, 64), jnp.zeros((LANES, LANES), BF16)], axis=1),
                           jnp.concatenate([jnp.zeros((LANES, LANES), BF16), _block_diag(LANES, 32)], axis=1)], axis=0)
    c64, s64 = _rope_tables(dec_seq, 64)
    c32, s32 = _rope_tables(dec_seq, MLA_ROPE)
    na_bias = [_na_bias_tables(na_rpb[i]) for i in range(n_odd)]

    cvecs = jnp.concatenate([c_ctx[None, :], c, jnp.zeros((8 - 1 - dec_batch, D_MODEL), F32)], axis=0)
    mods = _modulation(cvecs, w_mod, b_mod)

    cache_kpe = jnp.tile(cache_mla_kpe, (1, 1, 1, 4))
    cache_dk = cache_diff_k.transpose(0, 1, 3, 4, 5, 2).reshape(dec_batch, n_even, 512, past)
    cache_dv = cache_diff_v.reshape(dec_batch, n_even, 4 * past, LANES)
    cache_nk = cache_na_k.transpose(0, 1, 3, 4, 2).reshape(dec_batch, n_odd, 512, past)
    cache_nv = cache_na_v.transpose(0, 1, 3, 4, 2).reshape(dec_batch, n_odd, 512, past)
    cache_gk = cache_gqa_k.transpose(0, 1, 3, 4, 2).reshape(dec_batch, n_odd, LANES, past)
    cache_gv = cache_gqa_v.transpose(0, 1, 3, 4, 2).reshape(dec_batch, n_odd, LANES, past)

    even_widths = (1024, 512, LANES, 512, 512, 512, 512, 512, 512)
    odd_widths = (512, 512, 512, 512, 512, LANES, LANES, 512)
    even_states = ((seq, KV_LORA), (MLA_ROPE, seq), (512, seq), (4 * seq, LANES))
    odd_states = ((512, seq), (512, seq), (LANES, seq), (LANES, seq))

    def run_pass(x, nb, t, ctx):
        row_base = 0 if ctx else 1
        states = [None, None]
        bq = t if ctx else Q_BLOCK
        n_pairs = 4
        mxu_den = not ctx
        t_mod = nb * t if ctx else t
        for l in range(DEPTH):
            i = l // 2
            if l % 2 == 0:
                lam_init = 0.8 - 0.6 * math.exp(-0.3 * l)
                outs = _in_proj(_in_even_kernel, x, mods, l, gains_e[i], (w_in_e[i], wqb[i], wkvb[i]),
                                (bd64, bd32, bdq), None if ctx else (c64, s64, c32, s32),
                                even_widths, even_states if ctx else None, states[0], row_base, t_mod, t)
                qcat, kn, kpe, vm, mg, dq, dk, dv, dg = outs[:9]
                if ctx:
                    states[0] = outs[9:]
                mla_segs = [(_self_seg(kn, t, n_pairs), _self_seg(kpe, t, n_pairs, shared=True),
                             _self_seg(vm, t, n_pairs))]
                diff_segs = [(_self_seg(dk, t, n_pairs), _self_seg(dv, t, n_pairs))]
                if not ctx:
                    kn_c, vm_c = _mla_cache_kv(cache_mla_ckv[:, i].reshape(nb * past, KV_LORA), wkvb[i], bd64, gains_e[i])
                    mla_segs.append((_self_seg(kn_c, past, n_pairs),
                                     _cache_seg(cache_kpe, i, n_pairs, "rows", shared=True),
                                     _self_seg(vm_c, past, n_pairs)))
                    diff_segs.append((_cache_seg(cache_dk, i, n_pairs, "cols"),
                                      _cache_seg(cache_dv, i, n_pairs, "rows4")))
                oa = _pair_attention("mla", qcat, mla_segs, mg, nb, t, bq, n_pairs, mxu_den)
                ob = _pair_attention("diff", dq, diff_segs, dg, nb, t, bq, n_pairs, mxu_den,
                                     extra=(lam_vecs[i], diff_subln[i][None, :]), lam_init=lam_init)
                x = _out_proj(oa, ob, x, mods, l, w_out_e[i], row_base, t_mod)
            else:
                outs = _in_proj(_in_odd_kernel, x, mods, l, gains_o[i], (w_in_o[i],), (bd64,),
                                None if ctx else (c64, s64), odd_widths,
                                odd_states if ctx else None, states[1], row_base, t_mod, t)
                nq, nk, nv, ng, gq, gk, gv, gg = outs[:8]
                if ctx:
                    states[1] = outs[8:]
                    oa = _pair_attention("mha", nq, [(_self_seg(nk, t, n_pairs), _self_seg(nv, t, n_pairs))],
                                         ng, nb, t, bq, n_pairs, mxu_den)
                else:
                    oa = _na_attention(nq, nk, nv, cache_nk, cache_nv, i, na_bias[i], ng, nb, t)
                gqa_segs = [(_self_seg(gk, t, n_pairs, shared=True), _self_seg(gv, t, n_pairs, shared=True))]
                if not ctx:
                    gqa_segs.append((_cache_seg(cache_gk, i, n_pairs, "cols", shared=True),
                                     _cache_seg(cache_gv, i, n_pairs, "cols", shared=True)))
                ob = _pair_attention("gqa", gq, gqa_segs, gg, nb, t, bq, n_pairs, mxu_den)
                x = _out_proj(oa, ob, x, mods, l, w_out_o[i], row_base, t_mod)
        return x, states

    y_prompt, st = run_pass(x_prompt.reshape(batch * seq, D_MODEL), batch, seq, True)
    y_sample, _ = run_pass(x_sample.reshape(dec_batch * dec_seq, D_MODEL), dec_batch, dec_seq, False)

    def token_major(a, heads):
        a = a.reshape((batch, n_even) + heads + (a.shape[2] // math.prod(heads), seq))
        return jnp.moveaxis(a, -1, 2)

    ckv, kpe_t, dk_t, dv4 = st[0]
    nk_t, nv_t, gk_t, gv_t = st[1]
    return (y_prompt.reshape(batch, seq, D_MODEL), y_sample.reshape(dec_batch, dec_seq, D_MODEL),
            ckv, token_major(kpe_t, ()), token_major(dk_t, (DIFF_HEADS, 2)),
            dv4.reshape(batch, n_even, seq, DIFF_HEADS, 2 * DIFF_HD),
            token_major(nk_t, (NA_HEADS,)), token_major(nv_t, (NA_HEADS,)),
            token_major(gk_t, (GQA_KV,)), token_major(gv_t, (GQA_KV,)))
```

```python
import functools
import math

import jax
import jax.numpy as jnp
import numpy as np
from jax import lax
from jax.experimental import pallas as pl
from jax.experimental.pallas import tpu as pltpu

F32 = jnp.float32
BF16 = jnp.bfloat16

D_MODEL = 1024
DEPTH = 4
GRID_W = 64
ROPE_THETA = 10000.0
EPS = 1e-6
MLA_HEADS = 8
MLA_NOPE = 64
MLA_ROPE = 32
MLA_QK = MLA_NOPE + MLA_ROPE
Q_LORA = 256
KV_LORA = 128
DIFF_HEADS = 4
DIFF_HD = 64
NA_HEADS = 8
NA_HD = 64
NA_ROWS = 8
NA_COLS = 16
GQA_HEADS = 8
GQA_KV = 2
GQA_HD = 64
LANES = 128
NA_GROUP_ROWS = 4
NA_WIN_ROWS = NA_ROWS + NA_GROUP_ROWS
NEG = -1e30
LOG2E = math.log2(math.e)
VMEM_LIMIT = 48 * 1024 * 1024
GQA_PERM = (0, 4, 1, 5, 2, 6, 3, 7)

ROW_TILE = 512
Q_BLOCK = 512


def _cparams():
    return pltpu.CompilerParams(vmem_limit_bytes=VMEM_LIMIT)


def _dot(a, b):
    return jnp.dot(a, b, preferred_element_type=F32)


def _dot_nt(a, b):
    return lax.dot_general(a, b, (((1,), (1,)), ((), ())), preferred_element_type=F32)


def _rms_full(x, g):
    ms = jnp.mean(x * x, axis=-1, keepdims=True)
    return x * lax.rsqrt(ms + EPS) * g


def _rms_group(x, bd, g):
    ms = _dot((x * x).astype(BF16), bd)
    return x * lax.rsqrt(ms + EPS) * g


def _silu(u):
    return u * (1.0 / (1.0 + jnp.exp(-u)))


def _rope(x, cos, sin, group):
    half = group // 2
    rows, width = x.shape
    lane = lax.broadcasted_iota(jnp.int32, (rows, LANES), 1)
    first = (lane & (group - 1)) < half
    outs = []
    for c in range(width // LANES):
        xc = x[:, c * LANES:(c + 1) * LANES]
        rot = jnp.where(first, pltpu.roll(xc, LANES - half, 1), pltpu.roll(xc, half, 1))
        outs.append(xc * cos + rot * sin)
    return outs[0] if len(outs) == 1 else jnp.concatenate(outs, axis=-1)


def _lane_mask(rows, width, lo, hi):
    lane = lax.broadcasted_iota(jnp.int32, (rows, width), 1)
    return (lane >= lo) & (lane < hi)


def _scores(qm, ksegs, bias0=None):
    ss = [_dot(qm, k) if transposed else _dot_nt(qm, k) for k, transposed in ksegs]
    if bias0 is not None:
        ss[0] = ss[0] + bias0
    return ss


def _softmax_pv(ss, vsegs, ones_lanes=None):
    m = jnp.max(ss[0], axis=-1, keepdims=True)
    for s in ss[1:]:
        m = jnp.maximum(m, jnp.max(s, axis=-1, keepdims=True))
    acc = None
    l = None
    for s, (v, transposed) in zip(ss, vsegs):
        p = jnp.exp2(s - m)
        if ones_lanes is None:
            ps = jnp.sum(p, axis=-1, keepdims=True)
            l = ps if l is None else l + ps
        else:
            width = lax.broadcasted_iota(jnp.int32, v.shape, 0 if transposed else 1)
            v = jnp.where((width >= ones_lanes[0]) & (width < ones_lanes[1]), jnp.ones_like(v), v)
        a = _dot_nt(p.astype(BF16), v) if transposed else _dot(p.astype(BF16), v)
        acc = a if acc is None else acc + a
    if ones_lanes is not None:
        l = pltpu.roll(acc, LANES // 2, 1)
    return acc / l


def _mod_kernel(c_ref, w_ref, b_ref, o_ref):
    c = c_ref[...]
    o_ref[0] = _dot(_silu(c).astype(BF16), w_ref[0].astype(BF16)) + b_ref[0]


def _modulation(cvecs, w_mod, b_mod):
    tn = 768
    return pl.pallas_call(
        _mod_kernel,
        out_shape=jax.ShapeDtypeStruct((DEPTH, 8, 3 * D_MODEL), F32),
        grid=(DEPTH, 3 * D_MODEL // tn),
        in_specs=[pl.BlockSpec((8, D_MODEL), lambda l, n: (0, 0)),
                  pl.BlockSpec((1, D_MODEL, tn), lambda l, n: (l, 0, n)),
                  pl.BlockSpec((1, 1, tn), lambda l, n: (l, 0, n))],
        out_specs=pl.BlockSpec((1, 8, tn), lambda l, n: (l, 0, n)),
        compiler_params=_cparams(),
        name="modulation",
    )(cvecs, w_mod, b_mod.reshape(DEPTH, 1, 3 * D_MODEL))


def _modulated_norm(x_ref, mod_ref, g_ref, row):
    mod = mod_ref[0, pl.ds(row, 1), :]
    shift = mod[:, :D_MODEL]
    scale = mod[:, D_MODEL:2 * D_MODEL]
    h = _rms_full(x_ref[...], g_ref[0:1, :]) * (1.0 + scale) + shift
    return h.astype(BF16)


def _mla_kv(cn, wkvb_ref, bd64_ref, g_kn, kn_ref, vm_ref):
    kv = _dot(cn.astype(BF16), wkvb_ref[...])
    for c in range(2):
        sl = slice(256 * c, 256 * (c + 1))
        kn_ref[:, sl] = _rms_group(kv[:, sl], bd64_ref[...], g_kn).astype(BF16)
    vm_ref[...] = kv[:, 512:].astype(BF16)


def _store_rows(ref, x):
    seq = ref.shape[1]
    for b in range(ref.shape[0]):
        ref[b] = x[b * seq:(b + 1) * seq]


def _store_transposed(ref, x, row0, keep=None):
    seq = ref.shape[2]
    for b in range(ref.shape[0]):
        xt = x[b * seq:(b + 1) * seq].T
        if keep is not None:
            xt = xt[:keep]
        ref[b, row0:row0 + xt.shape[0], :] = xt


def _store_heads4(ref, x):
    seq = ref.shape[1] // 4
    for b in range(ref.shape[0]):
        for h in range(4):
            ref[b, pl.ds(h, seq, stride=4), :] = x[b * seq:(b + 1) * seq, LANES * h:LANES * (h + 1)]


def _in_even_kernel(*refs, rope, states, n_alias, row_base, tiles_per_batch):
    it = iter(refs)
    x_ref, mod_ref, g_ref, w_ref, wqb_ref, wkvb_ref, bd64_ref, bd32_ref, bdq_ref = (next(it) for _ in range(9))
    if rope:
        c64_ref, s64_ref, c32_ref, s32_ref = (next(it) for _ in range(4))
    for _ in range(n_alias):
        next(it)
    qcat_ref, kn_ref, kpe_ref, vm_ref, mg_ref, dq_ref, dk_ref, dv_ref, dg_ref = (next(it) for _ in range(9))
    if states:
        st_ckv_ref, st_kpe_ref, st_dk_ref, st_dv_ref = (next(it) for _ in range(4))

    row = row_base + pl.program_id(0) // tiles_per_batch
    hb = _modulated_norm(x_ref, mod_ref, g_ref, row)
    bd64 = bd64_ref[...]
    mla_scale = MLA_QK ** -0.5 * LOG2E
    diff_scale = DIFF_HD ** -0.5 * LOG2E

    qa = _dot(hb, w_ref[:, 0:256])
    qa_n = _rms_full(qa, g_ref[1:2, 0:256]).astype(BF16)
    q = _dot(qa_n, wqb_ref[...])
    for p in range(4):
        qp = _rms_group(q[:, 256 * p:256 * (p + 1)], bdq_ref[...], g_ref[2:3, 0:256])
        q_nope = qp[:, :LANES]
        q_pe = qp[:, LANES:]
        if rope:
            q_pe = _rope(q_pe, c32_ref[...], s32_ref[...], MLA_ROPE)
        qcat_ref[:, 256 * p:256 * p + LANES] = (q_nope * mla_scale).astype(BF16)
        qcat_ref[:, 256 * p + LANES:256 * (p + 1)] = (q_pe * mla_scale).astype(BF16)

    kva = _dot(hb, w_ref[:, 256:512])
    c_kv = _rms_full(kva[:, :LANES], g_ref[3:4, 0:LANES])
    k_pe = _rms_group(kva[:, LANES:], bd32_ref[...], g_ref[4:5, 0:LANES])
    if states:
        _store_rows(st_ckv_ref, c_kv)
        _store_transposed(st_kpe_ref, k_pe, 0, keep=MLA_ROPE)
    if rope:
        k_pe = _rope(k_pe, c32_ref[...], s32_ref[...], MLA_ROPE)
    kpe_ref[...] = k_pe.astype(BF16)
    _mla_kv(c_kv, wkvb_ref, bd64_ref, g_ref[5:6, 0:256], kn_ref, vm_ref)

    mg_ref[...] = _silu(_dot(hb, w_ref[:, 512:1024])).astype(BF16)

    dq = _dot(hb, w_ref[:, 1024:1536])
    dk = _dot(hb, w_ref[:, 1536:2048])
    for c in range(2):
        sl = slice(256 * c, 256 * (c + 1))
        qn = _rms_group(dq[:, sl], bd64, g_ref[6:7, 0:256])
        kn = _rms_group(dk[:, sl], bd64, g_ref[7:8, 0:256])
        if states:
            _store_transposed(st_dk_ref, kn, 256 * c)
        if rope:
            qn = _rope(qn, c64_ref[...], s64_ref[...], DIFF_HD)
            kn = _rope(kn, c64_ref[...], s64_ref[...], DIFF_HD)
        dq_ref[:, sl] = (qn * diff_scale).astype(BF16)
        dk_ref[:, sl] = kn.astype(BF16)
    dv = _dot(hb, w_ref[:, 2048:2560])
    if states:
        _store_heads4(st_dv_ref, dv)
    dv_ref[...] = dv.astype(BF16)
    dg_ref[...] = _silu(_dot(hb, w_ref[:, 2560:3072])).astype(BF16)


def _in_odd_kernel(*refs, rope, states, n_alias, row_base, tiles_per_batch):
    it = iter(refs)
    x_ref, mod_ref, g_ref, w_ref, bd64_ref = (next(it) for _ in range(5))
    if rope:
        c64_ref, s64_ref = (next(it) for _ in range(2))
    for _ in range(n_alias):
        next(it)
    nq_ref, nk_ref, nv_ref, ng_ref, gq_ref, gk_ref, gv_ref, gg_ref = (next(it) for _ in range(8))
    if states:
        st_nk_ref, st_nv_ref, st_gk_ref, st_gv_ref = (next(it) for _ in range(4))

    row = row_base + pl.program_id(0) // tiles_per_batch
    hb = _modulated_norm(x_ref, mod_ref, g_ref, row)
    bd64 = bd64_ref[...]
    na_scale = NA_HD ** -0.5 * LOG2E
    gqa_scale = GQA_HD ** -0.5 * LOG2E

    nq = _dot(hb, w_ref[:, 0:512])
    nk = _dot(hb, w_ref[:, 512:1024])
    for c in range(2):
        sl = slice(256 * c, 256 * (c + 1))
        nq_ref[:, sl] = (_rms_group(nq[:, sl], bd64, g_ref[1:2, 0:256]) * na_scale).astype(BF16)
        kn = _rms_group(nk[:, sl], bd64, g_ref[2:3, 0:256])
        if states:
            _store_transposed(st_nk_ref, kn, 256 * c)
        nk_ref[:, sl] = kn.astype(BF16)
    nv = _dot(hb, w_ref[:, 1024:1536])
    if states:
        for c in range(2):
            _store_transposed(st_nv_ref, nv[:, 256 * c:256 * (c + 1)], 256 * c)
    nv_ref[...] = nv.astype(BF16)
    ng_ref[...] = _silu(_dot(hb, w_ref[:, 1536:2048])).astype(BF16)

    gq = _dot(hb, w_ref[:, 2048:2560])
    for c in range(2):
        sl = slice(256 * c, 256 * (c + 1))
        qn = _rms_group(gq[:, sl], bd64, g_ref[3:4, 0:256])
        if rope:
            qn = _rope(qn, c64_ref[...], s64_ref[...], GQA_HD)
        gq_ref[:, sl] = (qn * gqa_scale).astype(BF16)
    gkv = _dot(hb, w_ref[:, 2560:2816])
    gk = _rms_group(gkv[:, :LANES], bd64[:LANES, :LANES], g_ref[4:5, 0:LANES])
    gv = gkv[:, LANES:]
    if states:
        _store_transposed(st_gk_ref, gk, 0)
        _store_transposed(st_gv_ref, gv, 0)
    if rope:
        gk = _rope(gk, c64_ref[...], s64_ref[...], GQA_HD)
    gk_ref[...] = gk.astype(BF16)
    gv_ref[...] = gv.astype(BF16)
    gg_ref[...] = _silu(_dot(hb, w_ref[:, 2816:3328])).astype(BF16)


def _full(shape):
    zeros = (0,) * len(shape)
    return pl.BlockSpec(shape, lambda i: zeros)


def _in_proj(kernel, x, mods, layer, gains, weights, consts, ropes, out_widths, state_tails, state_prev,
             row_base, tokens_per_batch, seq):
    rows = x.shape[0]
    tm = ROW_TILE
    tiles_per_batch = tokens_per_batch // tm
    rope = ropes is not None
    states = state_tails is not None
    in_specs = [pl.BlockSpec((tm, D_MODEL), lambda i: (i, 0)),
                pl.BlockSpec((1, 8, 3 * D_MODEL), lambda i: (layer, 0, 0)),
                _full(gains.shape)]
    in_specs += [_full(w.shape) for w in weights]
    in_specs += [_full(c.shape) for c in consts]
    args = [x, mods, gains, *weights, *consts]
    if rope:
        in_specs += [pl.BlockSpec((tm, LANES), lambda i: (i % tiles_per_batch, 0)) for _ in ropes]
        args += list(ropes)
    out_shape = [jax.ShapeDtypeStruct((rows, w), BF16) for w in out_widths]
    out_specs = [pl.BlockSpec((tm, w), lambda i: (i, 0)) for w in out_widths]
    aliases = {}
    if states:
        slot = layer // 2
        out_shape += [jax.ShapeDtypeStruct((rows // seq, DEPTH // 2) + tail, F32) for tail in state_tails]
        out_specs += [pl.BlockSpec((tm // seq, None) + tail, lambda i: (i, slot, 0, 0)) for tail in state_tails]
        if state_prev is not None:
            aliases = {len(args) + j: len(out_widths) + j for j in range(len(state_prev))}
            in_specs += [pl.BlockSpec(memory_space=pl.ANY) for _ in state_prev]
            args += list(state_prev)
    return pl.pallas_call(
        functools.partial(kernel, rope=rope, states=states, n_alias=len(aliases), row_base=row_base,
                          tiles_per_batch=tiles_per_batch),
        out_shape=out_shape,
        grid=(rows // tm,),
        in_specs=in_specs,
        out_specs=out_specs,
        input_output_aliases=aliases,
        compiler_params=_cparams(),
        name=kernel.__name__.strip("_"),
    )(*args)


def _mla_cache_kernel(c_ref, wkvb_ref, bd64_ref, g_ref, kn_ref, vm_ref):
    _mla_kv(c_ref[...], wkvb_ref, bd64_ref, g_ref[5:6, 0:256], kn_ref, vm_ref)


def _mla_cache_kv(ckv, wkvb, bd64, gains):
    rows = ckv.shape[0]
    return pl.pallas_call(
        _mla_cache_kernel,
        out_shape=[jax.ShapeDtypeStruct((rows, 512), BF16)] * 2,
        grid=(1,),
        in_specs=[_full(ckv.shape), _full(wkvb.shape), _full(bd64.shape), _full(gains.shape)],
        out_specs=[_full((rows, 512))] * 2,
        compiler_params=_cparams(),
        name="mla_cache_kv",
    )(ckv, wkvb, bd64, gains)


def _pair_attn_kernel(*refs, mode, kinds, n_pairs, mxu_denominator, lam_init):
    it = iter(refs)
    q_ref = next(it)
    segs = [[next(it) for _ in seg_kinds] for seg_kinds in kinds]
    gate_ref = next(it)
    if mode == "diff":
        lam_ref, subln_ref = next(it), next(it)
    o_ref = next(it)

    qw = 2 * LANES if mode == "mla" else LANES
    rows = q_ref.shape[0]
    lo = _lane_mask(rows, LANES, 0, LANES // 2)
    if mode == "diff":
        lv = lam_ref[...]
        lam = (jnp.exp(jnp.sum(lv[0:1] * lv[1:2], axis=-1, keepdims=True))
               - jnp.exp(jnp.sum(lv[2:3] * lv[3:4], axis=-1, keepdims=True)) + lam_init)

    def load(ref, kind, p):
        if kind == "rows4":
            x = ref[pl.ds(p, ref.shape[0] // 4, stride=4), :]
        elif mode == "gqa":
            x = ref[...]
        else:
            sl = slice(LANES * p, LANES * (p + 1))
            x = ref[sl, :] if kind == "cols" else ref[:, sl]
        return x.astype(BF16), kind == "cols"

    def values(p):
        return [load(seg[-1], seg_kinds[-1], p) for seg, seg_kinds in zip(segs, kinds)]

    def scores(p, j):
        q = q_ref[:, qw * p:qw * (p + 1)]
        ksegs = []
        for seg, seg_kinds in zip(segs, kinds):
            k, transposed = load(seg[0], seg_kinds[0], p)
            if mode == "mla":
                k = jnp.concatenate([k, seg[1][...].astype(BF16)], axis=-1)
            ksegs.append((k, transposed))
        keep = _lane_mask(rows, qw, 64 * j, 64 * (j + 1))
        if mode == "mla":
            keep = keep | _lane_mask(rows, qw, LANES + 32 * j, LANES + 32 * (j + 1))
        return _scores(jnp.where(keep, q, jnp.zeros_like(q)), ksegs)

    def finish(p, outs):
        sl = slice(LANES * p, LANES * (p + 1))
        if mode == "diff":
            d = outs[0] - lam * outs[1]
            o = _rms_full(d, subln_ref[...]) * (1.0 - lam_init)
        else:
            o = jnp.where(lo, outs[0], outs[1])
        o_ref[:, sl] = (o * gate_ref[:, sl].astype(F32)).astype(BF16)

    heads = [(p, j) for p in range(n_pairs) for j in range(2)]
    ss = scores(*heads[0])
    outs = []
    for t, (p, j) in enumerate(heads):
        ss_next = scores(*heads[t + 1]) if t + 1 < len(heads) else None
        spare = (64 * (1 - j), 64 * (2 - j)) if mxu_denominator and mode != "diff" else None
        outs.append(_softmax_pv(ss, values(p), ones_lanes=spare))
        ss = ss_next
        if j == 1:
            finish(p, outs)
            outs = []


def _pair_attention(mode, q, segs, gate, batch, tq, bq, n_pairs, mxu_denominator, extra=(), lam_init=0.0):
    qw = 2 * LANES if mode == "mla" else LANES
    nq = tq // bq
    steps_p = 4 // n_pairs
    in_specs = [pl.BlockSpec((bq, qw * n_pairs), lambda b, p, i: (b * nq + i, p))]
    args = [q]
    for seg in segs:
        for arr, spec, _ in seg:
            args.append(arr)
            in_specs.append(spec)
    kinds = tuple(tuple(kind for _, _, kind in seg) for seg in segs)
    in_specs.append(pl.BlockSpec((bq, LANES * n_pairs), lambda b, p, i: (b * nq + i, p)))
    args.append(gate)
    for arr in extra:
        args.append(arr)
        in_specs.append(pl.BlockSpec(arr.shape, lambda b, p, i: (0,) * arr.ndim))
    return pl.pallas_call(
        functools.partial(_pair_attn_kernel, mode=mode, kinds=kinds, n_pairs=n_pairs,
                          mxu_denominator=mxu_denominator, lam_init=lam_init),
        out_shape=jax.ShapeDtypeStruct((batch * tq, 4 * LANES), BF16),
        grid=(batch, steps_p, nq),
        in_specs=in_specs,
        out_specs=pl.BlockSpec((bq, LANES * n_pairs), lambda b, p, i: (b * nq + i, p)),
        compiler_params=_cparams(),
        name=mode + "_attention",
    )(*args)


def _self_seg(arr, tk, n_pairs, shared=False):
    if shared:
        return arr, pl.BlockSpec((tk, arr.shape[1]), lambda b, p, i: (b, 0)), "rows"
    return arr, pl.BlockSpec((tk, LANES * n_pairs), lambda b, p, i: (b, p)), "rows"


def _cache_seg(arr, layer, n_pairs, kind, shared=False):
    r, c = arr.shape[2:]
    if shared or kind == "rows4":
        return arr, pl.BlockSpec((None, None, r, c), lambda b, p, i: (b, layer, 0, 0)), kind
    if kind == "cols":
        return arr, pl.BlockSpec((None, None, LANES * n_pairs, c), lambda b, p, i: (b, layer, p, 0)), kind
    return arr, pl.BlockSpec((None, None, r, LANES * n_pairs), lambda b, p, i: (b, layer, 0, p)), kind


def _na_kernel(q_ref, k_ref, v_ref, ck_ref, cv_ref, bias_ref, gate_ref, o_ref):
    g = pl.program_id(1)
    n_groups = pl.num_programs(1)
    first_row = jnp.clip(NA_GROUP_ROWS * g - NA_ROWS // 2, 0, NA_GROUP_ROWS * n_groups - NA_WIN_ROWS)
    start = pl.multiple_of(first_row * GRID_W, GRID_W)
    win = NA_WIN_ROWS * GRID_W
    kwin = k_ref[pl.ds(start, win), :]
    vwin = v_ref[pl.ds(start, win), :]
    ck = ck_ref[...].astype(BF16)
    cv = cv_ref[...].astype(BF16)
    q = q_ref[...]
    rows = q.shape[0]
    lo = _lane_mask(rows, LANES, 0, LANES // 2)

    def scores(h):
        sl = slice(LANES * (h // 2), LANES * (h // 2 + 1))
        keep = _lane_mask(rows, LANES, 64 * (h % 2), 64 * (h % 2 + 1))
        qm = jnp.where(keep, q[:, sl], jnp.zeros_like(q[:, sl]))
        return _scores(qm, [(kwin[:, sl], False), (ck[sl, :], True)], bias0=bias_ref[h])

    ss = scores(0)
    outs = []
    for h in range(NA_HEADS):
        ss_next = scores(h + 1) if h + 1 < NA_HEADS else None
        sl = slice(LANES * (h // 2), LANES * (h // 2 + 1))
        j = h % 2
        outs.append(_softmax_pv(ss, [(vwin[:, sl], False), (cv[sl, :], True)],
                                ones_lanes=(64 * (1 - j), 64 * (2 - j))))
        if j == 1:
            o = jnp.where(lo, outs[0], outs[1])
            o_ref[:, sl] = (o * gate_ref[:, sl].astype(F32)).astype(BF16)
            outs = []
        ss = ss_next


def _na_attention(q, k, v, cache_k, cache_v, layer, bias, gate, batch, tq):
    bq = NA_GROUP_ROWS * GRID_W
    n_groups = tq // bq
    past = cache_k.shape[3]
    width = NA_HEADS * NA_HD

    def bias_map(b, g):
        return (layer, jnp.where(g == 0, 0, jnp.where(g == n_groups - 1, 2, 1)), 0, 0, 0)

    tok = pl.BlockSpec((bq, width), lambda b, g: (b * n_groups + g, 0))
    whole = pl.BlockSpec((tq, width), lambda b, g: (b, 0))
    cache = pl.BlockSpec((None, None, width, past), lambda b, g: (b, layer, 0, 0))
    return pl.pallas_call(
        _na_kernel,
        out_shape=jax.ShapeDtypeStruct((batch * tq, width), BF16),
        grid=(batch, n_groups),
        in_specs=[tok, whole, whole, cache, cache,
                  pl.BlockSpec((None, None, NA_HEADS, bq, NA_WIN_ROWS * GRID_W), bias_map), tok],
        out_specs=tok,
        compiler_params=_cparams(),
        name="na_attention",
    )(q, k, v, cache_k, cache_v, bias, gate)


def _na_bias_tables(rpb):
    qc = np.arange(GRID_W)[:, None]
    kc = np.arange(GRID_W)[None, :]
    cs = np.clip(qc - NA_COLS // 2, 0, GRID_W - NA_COLS)
    col_valid = (kc >= cs) & (kc < cs + NA_COLS)
    dc = kc - qc + NA_COLS - 1
    onehot = np.zeros((2 * NA_COLS - 1, GRID_W, GRID_W), np.float32)
    for d in range(2 * NA_COLS - 1):
        onehot[d] = (col_valid & (dc == d)).astype(np.float32)
    cols = jnp.einsum("lhrd,dqk->lhrqk", rpb.astype(F32), jnp.asarray(onehot), precision=lax.Precision.HIGHEST)
    cols = jnp.where(jnp.asarray(col_valid), cols * LOG2E, NEG)
    outside = jnp.full(cols.shape[:2] + (1, GRID_W, GRID_W), NEG, F32)
    cols = jnp.concatenate([cols, outside], axis=2)
    n_dr = 2 * NA_ROWS - 1
    idx = np.full((3, NA_GROUP_ROWS, NA_WIN_ROWS), n_dr, np.int32)
    for a in range(NA_GROUP_ROWS):
        for j in range(NA_WIN_ROWS):
            if j < NA_ROWS:
                idx[0, a, j] = j - a + NA_ROWS - 1
            if a <= j < a + NA_ROWS:
                idx[1, a, j] = j - a + NA_ROWS // 2 - 1
            if j >= NA_WIN_ROWS - NA_ROWS:
                idx[2, a, j] = j - a - (NA_WIN_ROWS - NA_ROWS) + NA_ROWS // 2 - 1
    layers = rpb.shape[0]
    return pl.pallas_call(
        functools.partial(_na_bias_kernel, idx=idx),
        out_shape=jax.ShapeDtypeStruct((layers, 3, NA_HEADS, NA_GROUP_ROWS * GRID_W, NA_WIN_ROWS * GRID_W), F32),
        grid=(layers, NA_HEADS),
        in_specs=[pl.BlockSpec((None, None, n_dr + 1, GRID_W, GRID_W), lambda l, h: (l, h, 0, 0, 0))],
        out_specs=pl.BlockSpec((None, 3, None, NA_GROUP_ROWS * GRID_W, NA_WIN_ROWS * GRID_W),
                               lambda l, h: (l, 0, h, 0, 0)),
        compiler_params=_cparams(),
        name="na_bias",
    )(cols)


def _na_bias_kernel(cols_ref, out_ref, *, idx):
    for t in range(3):
        for a in range(NA_GROUP_ROWS):
            for j in range(0, NA_WIN_ROWS, 2):
                pair = jnp.concatenate([cols_ref[int(idx[t, a, j])], cols_ref[int(idx[t, a, j + 1])]], axis=-1)
                out_ref[t, a * GRID_W:(a + 1) * GRID_W, j * GRID_W:(j + 2) * GRID_W] = pair


def _out_kernel(oa_ref, ob_ref, x_ref, mod_ref, w_ref, y_ref, *, row_base, tiles_per_batch):
    row = row_base + pl.program_id(0) // tiles_per_batch
    gate = mod_ref[0, pl.ds(row, 1), :][:, 2 * D_MODEL:]
    half = oa_ref.shape[1]
    acc = _dot(oa_ref[...], w_ref[:half, :]) + _dot(ob_ref[...], w_ref[half:, :])
    y_ref[...] = x_ref[...] + gate * acc


def _out_proj(oa, ob, x, mods, layer, w, row_base, tokens_per_batch):
    rows = x.shape[0]
    tm = ROW_TILE
    tiles_per_batch = tokens_per_batch // tm
    return pl.pallas_call(
        functools.partial(_out_kernel, row_base=row_base, tiles_per_batch=tiles_per_batch),
        out_shape=jax.ShapeDtypeStruct((rows, D_MODEL), F32),
        grid=(rows // tm,),
        in_specs=[pl.BlockSpec((tm, oa.shape[1]), lambda i: (i, 0)),
                  pl.BlockSpec((tm, ob.shape[1]), lambda i: (i, 0)),
                  pl.BlockSpec((tm, D_MODEL), lambda i: (i, 0)),
                  pl.BlockSpec((1, 8, 3 * D_MODEL), lambda i: (layer, 0, 0)),
                  _full(w.shape)],
        out_specs=pl.BlockSpec((tm, D_MODEL), lambda i: (i, 0)),
        compiler_params=_cparams(),
        name="out_proj",
    )(oa, ob, x, mods, w)


def _block_diag(width, group):
    idx = np.arange(width) // group
    return jnp.asarray((idx[:, None] == idx[None, :]).astype(np.float32) / group, BF16)


def _rope_tables(t, rot_dim):
    pos = np.arange(t)
    row = (pos // GRID_W).astype(np.float64)
    col = (pos % GRID_W).astype(np.float64)
    n = rot_dim // 2
    inv = ROPE_THETA ** (-np.arange(0, n, 2, dtype=np.float64) / n)
    ang = np.concatenate([row[:, None] * inv, col[:, None] * inv], axis=-1)
    cos = np.concatenate([np.cos(ang), np.cos(ang)], axis=-1)
    sin = np.concatenate([-np.sin(ang), np.sin(ang)], axis=-1)
    reps = LANES // rot_dim
    return (jnp.asarray(np.tile(cos, (1, reps)), F32), jnp.asarray(np.tile(sin, (1, reps)), F32))


def _pad_row(v, width=D_MODEL):
    return jnp.pad(v, (0, width - v.shape[0]))


def _tile_row(v, reps):
    return _pad_row(jnp.tile(v, reps))


def _permute_heads(w, axis):
    shape = w.shape
    split = shape[:axis] + (GQA_HEADS, GQA_HD) + shape[axis + 1:]
    return jnp.take(w.reshape(split), jnp.asarray(GQA_PERM), axis=axis).reshape(shape)


def kernel(x_prompt, x_sample, cache_mla_ckv, cache_mla_kpe, cache_diff_k, cache_diff_v, cache_na_k, cache_na_v, cache_gqa_k, cache_gqa_v, c, c_ctx, norm_w, w_mod, b_mod, w_in_even, w_out_even, mla_qa_norm, mla_wqb, mla_kva_norm, mla_wkvb, mla_qn_nope, mla_qn_rope, mla_kn_nope, mla_kn_rope, diff_qn, diff_kn, diff_lq1, diff_lk1, diff_lq2, diff_lk2, diff_subln, w_in_odd, w_out_odd, na_qn, na_kn, na_rpb, gqa_qn, gqa_kn):
    batch, seq, _ = x_prompt.shape
    dec_batch, dec_seq, _ = x_sample.shape
    past = cache_mla_ckv.shape[2]
    n_even, n_odd = w_in_even.shape[0], w_in_odd.shape[0]

    w_in_e = jnp.concatenate([w_in_even[..., :384], jnp.tile(w_in_even[..., 384:416], (1, 1, 4)),
                              w_in_even[..., 416:]], axis=-1).astype(BF16)
    wqb = mla_wqb.reshape(n_even, Q_LORA, MLA_HEADS, MLA_QK)
    wqb = jnp.concatenate([wqb[..., :MLA_NOPE].reshape(n_even, Q_LORA, 4, 2 * MLA_NOPE),
                           wqb[..., MLA_NOPE:].reshape(n_even, Q_LORA, 4, 2 * MLA_ROPE),
                           jnp.zeros((n_even, Q_LORA, 4, LANES - 2 * MLA_ROPE), F32)], axis=-1)
    wqb = wqb.reshape(n_even, Q_LORA, 4 * 2 * LANES).astype(BF16)
    wkvb = mla_wkvb.reshape(n_even, KV_LORA, MLA_HEADS, 2 * MLA_NOPE)
    wkvb = jnp.concatenate([wkvb[..., :MLA_NOPE].reshape(n_even, KV_LORA, 512),
                            wkvb[..., MLA_NOPE:].reshape(n_even, KV_LORA, 512)], axis=-1).astype(BF16)
    w_out_e = w_out_even.astype(BF16)
    w_in_o = jnp.concatenate([w_in_odd[..., :2048], _permute_heads(w_in_odd[..., 2048:2560], 2),
                              w_in_odd[..., 2560:2816], _permute_heads(w_in_odd[..., 2816:], 2)],
                             axis=-1).astype(BF16)
    w_out_o = jnp.concatenate([w_out_odd[:, :512], _permute_heads(w_out_odd[:, 512:], 1)], axis=1).astype(BF16)

    gains_e = [jnp.stack([norm_w[2 * i], _pad_row(mla_qa_norm[i]),
                          _pad_row(jnp.concatenate([jnp.tile(mla_qn_nope[i], 2), jnp.tile(mla_qn_rope[i], 4)])),
                          _pad_row(mla_kva_norm[i]), _tile_row(mla_kn_rope[i], 4), _tile_row(mla_kn_nope[i], 4),
                          _tile_row(diff_qn[i], 4), _tile_row(diff_kn[i], 4)]) for i in range(n_even)]
    gains_o = [jnp.stack([norm_w[2 * i + 1], _tile_row(na_qn[i], 4), _tile_row(na_kn[i], 4),
                          _tile_row(gqa_qn[i], 4), _tile_row(gqa_kn[i], 2),
                          jnp.zeros((D_MODEL,), F32), jnp.zeros((D_MODEL,), F32), jnp.zeros((D_MODEL,), F32)])
               for i in range(n_odd)]
    lam_vecs = [jnp.stack([diff_lq1[i], diff_lk1[i], diff_lq2[i], diff_lk2[i]]) for i in range(n_even)]

    bd64 = _block_diag(256, 64)
    bd32 = _block_diag(LANES, 32)
    bdq = jnp.concatenate([jnp.concatenate([_block_diag(LANES, 64), jnp.zeros((LANES, LANES), BF16)], axis=1),
                           jnp.concatenate([jnp.zeros((LANES, LANES), BF16), _block_diag(LANES, 32)], axis=1)], axis=0)
    c64, s64 = _rope_tables(dec_seq, 64)
    c32, s32 = _rope_tables(dec_seq, MLA_ROPE)
    na_bias = _na_bias_tables(na_rpb)

    cvecs = jnp.concatenate([c_ctx[None, :], c, jnp.zeros((8 - 1 - dec_batch, D_MODEL), F32)], axis=0)
    mods = _modulation(cvecs, w_mod, b_mod)

    cache_kpe = jnp.tile(cache_mla_kpe, (1, 1, 1, 4))
    cache_dk = cache_diff_k.transpose(0, 1, 3, 4, 5, 2).reshape(dec_batch, n_even, 512, past)
    cache_dv = cache_diff_v.reshape(dec_batch, n_even, 4 * past, LANES)
    cache_nk = cache_na_k.transpose(0, 1, 3, 4, 2).reshape(dec_batch, n_odd, 512, past)
    cache_nv = cache_na_v.transpose(0, 1, 3, 4, 2).reshape(dec_batch, n_odd, 512, past)
    cache_gk = cache_gqa_k.transpose(0, 1, 3, 4, 2).reshape(dec_batch, n_odd, LANES, past)
    cache_gv = cache_gqa_v.transpose(0, 1, 3, 4, 2).reshape(dec_batch, n_odd, LANES, past)

    even_widths = (1024, 512, LANES, 512, 512, 512, 512, 512, 512)
    odd_widths = (512, 512, 512, 512, 512, LANES, LANES, 512)
    even_states = ((seq, KV_LORA), (MLA_ROPE, seq), (512, seq), (4 * seq, LANES))
    odd_states = ((512, seq), (512, seq), (LANES, seq), (LANES, seq))

    def run_pass(x, nb, t, ctx):
        row_base = 0 if ctx else 1
        states = [None, None]
        bq = t if ctx else Q_BLOCK
        n_pairs = 4
        mxu_den = not ctx
        t_mod = nb * t if ctx else t
        for l in range(DEPTH):
            i = l // 2
            if l % 2 == 0:
                lam_init = 0.8 - 0.6 * math.exp(-0.3 * l)
                outs = _in_proj(_in_even_kernel, x, mods, l, gains_e[i], (w_in_e[i], wqb[i], wkvb[i]),
                                (bd64, bd32, bdq), None if ctx else (c64, s64, c32, s32),
                                even_widths, even_states if ctx else None, states[0], row_base, t_mod, t)
                qcat, kn, kpe, vm, mg, dq, dk, dv, dg = outs[:9]
                if ctx:
                    states[0] = outs[9:]
                mla_segs = [(_self_seg(kn, t, n_pairs), _self_seg(kpe, t, n_pairs, shared=True),
                             _self_seg(vm, t, n_pairs))]
                diff_segs = [(_self_seg(dk, t, n_pairs), _self_seg(dv, t, n_pairs))]
                if not ctx:
                    kn_c, vm_c = _mla_cache_kv(cache_mla_ckv[:, i].reshape(nb * past, KV_LORA), wkvb[i], bd64, gains_e[i])
                    mla_segs.append((_self_seg(kn_c, past, n_pairs),
                                     _cache_seg(cache_kpe, i, n_pairs, "rows", shared=True),
                                     _self_seg(vm_c, past, n_pairs)))
                    diff_segs.append((_cache_seg(cache_dk, i, n_pairs, "cols"),
                                      _cache_seg(cache_dv, i, n_pairs, "rows4")))
                oa = _pair_attention("mla", qcat, mla_segs, mg, nb, t, bq, n_pairs, mxu_den)
                ob = _pair_attention("diff", dq, diff_segs, dg, nb, t, bq, n_pairs, mxu_den,
                                     extra=(lam_vecs[i], diff_subln[i][None, :]), lam_init=lam_init)
                x = _out_proj(oa, ob, x, mods, l, w_out_e[i], row_base, t_mod)
            else:
                outs = _in_proj(_in_odd_kernel, x, mods, l, gains_o[i], (w_in_o[i],), (bd64,),
                                None if ctx else (c64, s64), odd_widths,
                                odd_states if ctx else None, states[1], row_base, t_mod, t)
                nq, nk, nv, ng, gq, gk, gv, gg = outs[:8]
                if ctx:
                    states[1] = outs[8:]
                    oa = _pair_attention("mha", nq, [(_self_seg(nk, t, n_pairs), _self_seg(nv, t, n_pairs))],
                                         ng, nb, t, bq, n_pairs, mxu_den)
                else:
                    oa = _na_attention(nq, nk, nv, cache_nk, cache_nv, i, na_bias, ng, nb, t)
                gqa_segs = [(_self_seg(gk, t, n_pairs, shared=True), _self_seg(gv, t, n_pairs, shared=True))]
                if not ctx:
                    gqa_segs.append((_cache_seg(cache_gk, i, n_pairs, "cols", shared=True),
                                     _cache_seg(cache_gv, i, n_pairs, "cols", shared=True)))
                ob = _pair_attention("gqa", gq, gqa_segs, gg, nb, t, bq, n_pairs, mxu_den)
                x = _out_proj(oa, ob, x, mods, l, w_out_o[i], row_base, t_mod)
        return x, states

    y_prompt, st = run_pass(x_prompt.reshape(batch * seq, D_MODEL), batch, seq, True)
    y_sample, _ = run_pass(x_sample.reshape(dec_batch * dec_seq, D_MODEL), dec_batch, dec_seq, False)

    def token_major(a, heads):
        a = a.reshape((batch, n_even) + heads + (a.shape[2] // math.prod(heads), seq))
        return jnp.moveaxis(a, -1, 2)

    ckv, kpe_t, dk_t, dv4 = st[0]
    nk_t, nv_t, gk_t, gv_t = st[1]
    return (y_prompt.reshape(batch, seq, D_MODEL), y_sample.reshape(dec_batch, dec_seq, D_MODEL),
            ckv, token_major(kpe_t, ()), token_major(dk_t, (DIFF_HEADS, 2)),
            dv4.reshape(batch, n_even, seq, DIFF_HEADS, 2 * DIFF_HD),
            token_major(nk_t, (NA_HEADS,)), token_major(nv_t, (NA_HEADS,)),
            token_major(gk_t, (GQA_KV,)), token_major(gv_t, (GQA_KV,)))
```

```python
import functools
import math

import jax
import jax.numpy as jnp
import numpy as np
from jax import lax
from jax.experimental import pallas as pl
from jax.experimental.pallas import tpu as pltpu

F32 = jnp.float32
BF16 = jnp.bfloat16

D_MODEL = 1024
DEPTH = 4
GRID_W = 64
ROPE_THETA = 10000.0
EPS = 1e-6
MLA_HEADS = 8
MLA_NOPE = 64
MLA_ROPE = 32
MLA_QK = MLA_NOPE + MLA_ROPE
Q_LORA = 256
KV_LORA = 128
DIFF_HEADS = 4
DIFF_HD = 64
NA_HEADS = 8
NA_HD = 64
NA_ROWS = 8
NA_COLS = 16
GQA_HEADS = 8
GQA_KV = 2
GQA_HD = 64
LANES = 128
NA_GROUP_ROWS = 4
NA_WIN_ROWS = NA_ROWS + NA_GROUP_ROWS
NEG = -1e30
LOG2E = math.log2(math.e)
VMEM_LIMIT = 48 * 1024 * 1024
GQA_PERM = (0, 4, 1, 5, 2, 6, 3, 7)

ROW_TILE = 512
Q_BLOCK = 512


def _cparams():
    return pltpu.CompilerParams(vmem_limit_bytes=VMEM_LIMIT)


def _dot(a, b):
    return jnp.dot(a, b, preferred_element_type=F32)


def _dot_nt(a, b):
    return lax.dot_general(a, b, (((1,), (1,)), ((), ())), preferred_element_type=F32)


def _rms_full(x, g):
    ms = jnp.mean(x * x, axis=-1, keepdims=True)
    return x * lax.rsqrt(ms + EPS) * g


def _rms_group(x, bd, g):
    ms = _dot((x * x).astype(BF16), bd)
    return x * lax.rsqrt(ms + EPS) * g


def _silu(u):
    return u * (1.0 / (1.0 + jnp.exp(-u)))


def _rope(x, cos, sin, group):
    half = group // 2
    rows, width = x.shape
    lane = lax.broadcasted_iota(jnp.int32, (rows, LANES), 1)
    first = (lane & (group - 1)) < half
    outs = []
    for c in range(width // LANES):
        xc = x[:, c * LANES:(c + 1) * LANES]
        rot = jnp.where(first, pltpu.roll(xc, LANES - half, 1), pltpu.roll(xc, half, 1))
        outs.append(xc * cos + rot * sin)
    return outs[0] if len(outs) == 1 else jnp.concatenate(outs, axis=-1)


def _lane_mask(rows, width, lo, hi):
    lane = lax.broadcasted_iota(jnp.int32, (rows, width), 1)
    return (lane >= lo) & (lane < hi)


def _scores(qm, ksegs, bias0=None):
    ss = [_dot(qm, k) if transposed else _dot_nt(qm, k) for k, transposed in ksegs]
    if bias0 is not None:
        ss[0] = ss[0] + bias0
    return ss


def _softmax_pv(ss, vsegs, ones_lanes=None):
    m = jnp.max(ss[0], axis=-1, keepdims=True)
    for s in ss[1:]:
        m = jnp.maximum(m, jnp.max(s, axis=-1, keepdims=True))
    acc = None
    l = None
    for s, (v, transposed) in zip(ss, vsegs):
        p = jnp.exp2(s - m)
        if ones_lanes is None:
            ps = jnp.sum(p, axis=-1, keepdims=True)
            l = ps if l is None else l + ps
        else:
            width = lax.broadcasted_iota(jnp.int32, v.shape, 0 if transposed else 1)
            v = jnp.where((width >= ones_lanes[0]) & (width < ones_lanes[1]), jnp.ones_like(v), v)
        a = _dot_nt(p.astype(BF16), v) if transposed else _dot(p.astype(BF16), v)
        acc = a if acc is None else acc + a
    if ones_lanes is not None:
        l = pltpu.roll(acc, LANES // 2, 1)
    return acc / l


def _mod_kernel(c_ref, w_ref, b_ref, o_ref):
    c = c_ref[...]
    o_ref[0] = _dot(_silu(c).astype(BF16), w_ref[0].astype(BF16)) + b_ref[0]


def _modulation(cvecs, w_mod, b_mod):
    tn = 768
    return pl.pallas_call(
        _mod_kernel,
        out_shape=jax.ShapeDtypeStruct((DEPTH, 8, 3 * D_MODEL), F32),
        grid=(DEPTH, 3 * D_MODEL // tn),
        in_specs=[pl.BlockSpec((8, D_MODEL), lambda l, n: (0, 0)),
                  pl.BlockSpec((1, D_MODEL, tn), lambda l, n: (l, 0, n)),
                  pl.BlockSpec((1, 1, tn), lambda l, n: (l, 0, n))],
        out_specs=pl.BlockSpec((1, 8, tn), lambda l, n: (l, 0, n)),
        compiler_params=_cparams(),
        name="modulation",
    )(cvecs, w_mod, b_mod.reshape(DEPTH, 1, 3 * D_MODEL))


def _modulated_norm(x_ref, mod_ref, g_ref, row):
    mod = mod_ref[0, pl.ds(row, 1), :]
    shift = mod[:, :D_MODEL]
    scale = mod[:, D_MODEL:2 * D_MODEL]
    h = _rms_full(x_ref[...], g_ref[0:1, :]) * (1.0 + scale) + shift
    return h.astype(BF16)


def _mla_kv(cn, wkvb_ref, bd64_ref, g_kn, kn_ref, vm_ref):
    kv = _dot(cn.astype(BF16), wkvb_ref[...])
    for c in range(2):
        sl = slice(256 * c, 256 * (c + 1))
        kn_ref[:, sl] = _rms_group(kv[:, sl], bd64_ref[...], g_kn).astype(BF16)
    vm_ref[...] = kv[:, 512:].astype(BF16)


def _store_rows(ref, x):
    seq = ref.shape[1]
    for b in range(ref.shape[0]):
        ref[b] = x[b * seq:(b + 1) * seq]


def _store_transposed(ref, x, row0, keep=None):
    seq = ref.shape[2]
    for b in range(ref.shape[0]):
        xt = x[b * seq:(b + 1) * seq].T
        if keep is not None:
            xt = xt[:keep]
        ref[b, row0:row0 + xt.shape[0], :] = xt


def _store_heads4(ref, x):
    seq = ref.shape[1] // 4
    for b in range(ref.shape[0]):
        for h in range(4):
            ref[b, pl.ds(h, seq, stride=4), :] = x[b * seq:(b + 1) * seq, LANES * h:LANES * (h + 1)]


def _in_even_kernel(*refs, rope, states, n_alias, row_base, tiles_per_batch):
    it = iter(refs)
    x_ref, mod_ref, g_ref, w_ref, wqb_ref, wkvb_ref, bd64_ref, bd32_ref, bdq_ref = (next(it) for _ in range(9))
    if rope:
        c64_ref, s64_ref, c32_ref, s32_ref = (next(it) for _ in range(4))
    for _ in range(n_alias):
        next(it)
    qcat_ref, kn_ref, kpe_ref, vm_ref, mg_ref, dq_ref, dk_ref, dv_ref, dg_ref = (next(it) for _ in range(9))
    if states:
        st_ckv_ref, st_kpe_ref, st_dk_ref, st_dv_ref = (next(it) for _ in range(4))

    row = row_base + pl.program_id(0) // tiles_per_batch
    hb = _modulated_norm(x_ref, mod_ref, g_ref, row)
    bd64 = bd64_ref[...]
    mla_scale = MLA_QK ** -0.5 * LOG2E
    diff_scale = DIFF_HD ** -0.5 * LOG2E

    qa = _dot(hb, w_ref[:, 0:256])
    qa_n = _rms_full(qa, g_ref[1:2, 0:256]).astype(BF16)
    q = _dot(qa_n, wqb_ref[...])
    for p in range(4):
        qp = _rms_group(q[:, 256 * p:256 * (p + 1)], bdq_ref[...], g_ref[2:3, 0:256])
        q_nope = qp[:, :LANES]
        q_pe = qp[:, LANES:]
        if rope:
            q_pe = _rope(q_pe, c32_ref[...], s32_ref[...], MLA_ROPE)
        qcat_ref[:, 256 * p:256 * p + LANES] = (q_nope * mla_scale).astype(BF16)
        qcat_ref[:, 256 * p + LANES:256 * (p + 1)] = (q_pe * mla_scale).astype(BF16)

    kva = _dot(hb, w_ref[:, 256:512])
    c_kv = _rms_full(kva[:, :LANES], g_ref[3:4, 0:LANES])
    k_pe = _rms_group(kva[:, LANES:], bd32_ref[...], g_ref[4:5, 0:LANES])
    if states:
        _store_rows(st_ckv_ref, c_kv)
        _store_transposed(st_kpe_ref, k_pe, 0, keep=MLA_ROPE)
    if rope:
        k_pe = _rope(k_pe, c32_ref[...], s32_ref[...], MLA_ROPE)
    kpe_ref[...] = k_pe.astype(BF16)
    _mla_kv(c_kv, wkvb_ref, bd64_ref, g_ref[5:6, 0:256], kn_ref, vm_ref)

    mg_ref[...] = _silu(_dot(hb, w_ref[:, 512:1024])).astype(BF16)

    dq = _dot(hb, w_ref[:, 1024:1536])
    dk = _dot(hb, w_ref[:, 1536:2048])
    for c in range(2):
        sl = slice(256 * c, 256 * (c + 1))
        qn = _rms_group(dq[:, sl], bd64, g_ref[6:7, 0:256])
        kn = _rms_group(dk[:, sl], bd64, g_ref[7:8, 0:256])
        if states:
            _store_transposed(st_dk_ref, kn, 256 * c)
        if rope:
            qn = _rope(qn, c64_ref[...], s64_ref[...], DIFF_HD)
            kn = _rope(kn, c64_ref[...], s64_ref[...], DIFF_HD)
        dq_ref[:, sl] = (qn * diff_scale).astype(BF16)
        dk_ref[:, sl] = kn.astype(BF16)
    dv = _dot(hb, w_ref[:, 2048:2560])
    if states:
        _store_heads4(st_dv_ref, dv)
    dv_ref[...] = dv.astype(BF16)
    dg_ref[...] = _silu(_dot(hb, w_ref[:, 2560:3072])).astype(BF16)


def _in_odd_kernel(*refs, rope, states, n_alias, row_base, tiles_per_batch):
    it = iter(refs)
    x_ref, mod_ref, g_ref, w_ref, bd64_ref = (next(it) for _ in range(5))
    if rope:
        c64_ref, s64_ref = (next(it) for _ in range(2))
    for _ in range(n_alias):
        next(it)
    nq_ref, nk_ref, nv_ref, ng_ref, gq_ref, gk_ref, gv_ref, gg_ref = (next(it) for _ in range(8))
    if states:
        st_nk_ref, st_nv_ref, st_gk_ref, st_gv_ref = (next(it) for _ in range(4))

    row = row_base + pl.program_id(0) // tiles_per_batch
    hb = _modulated_norm(x_ref, mod_ref, g_ref, row)
    bd64 = bd64_ref[...]
    na_scale = NA_HD ** -0.5 * LOG2E
    gqa_scale = GQA_HD ** -0.5 * LOG2E

    nq = _dot(hb, w_ref[:, 0:512])
    nk = _dot(hb, w_ref[:, 512:1024])
    for c in range(2):
        sl = slice(256 * c, 256 * (c + 1))
        nq_ref[:, sl] = (_rms_group(nq[:, sl], bd64, g_ref[1:2, 0:256]) * na_scale).astype(BF16)
        kn = _rms_group(nk[:, sl], bd64, g_ref[2:3, 0:256])
        if states:
            _store_transposed(st_nk_ref, kn, 256 * c)
        nk_ref[:, sl] = kn.astype(BF16)
    nv = _dot(hb, w_ref[:, 1024:1536])
    if states:
        for c in range(2):
            _store_transposed(st_nv_ref, nv[:, 256 * c:256 * (c + 1)], 256 * c)
    nv_ref[...] = nv.astype(BF16)
    ng_ref[...] = _silu(_dot(hb, w_ref[:, 1536:2048])).astype(BF16)

    gq = _dot(hb, w_ref[:, 2048:2560])
    for c in range(2):
        sl = slice(256 * c, 256 * (c + 1))
        qn = _rms_group(gq[:, sl], bd64, g_ref[3:4, 0:256])
        if rope:
            qn = _rope(qn, c64_ref[...], s64_ref[...], GQA_HD)
        gq_ref[:, sl] = (qn * gqa_scale).astype(BF16)
    gkv = _dot(hb, w_ref[:, 2560:2816])
    gk = _rms_group(gkv[:, :LANES], bd64[:LANES, :LANES], g_ref[4:5, 0:LANES])
    gv = gkv[:, LANES:]
    if states:
        _store_transposed(st_gk_ref, gk, 0)
        _store_transposed(st_gv_ref, gv, 0)
    if rope:
        gk = _rope(gk, c64_ref[...], s64_ref[...], GQA_HD)
    gk_ref[...] = gk.astype(BF16)
    gv_ref[...] = gv.astype(BF16)
    gg_ref[...] = _silu(_dot(hb, w_ref[:, 2816:3328])).astype(BF16)


def _full(shape):
    zeros = (0,) * len(shape)
    return pl.BlockSpec(shape, lambda i: zeros)


def _in_proj(kernel, x, mods, layer, gains, weights, consts, ropes, out_widths, state_tails, state_prev,
             row_base, tokens_per_batch, seq):
    rows = x.shape[0]
    tm = ROW_TILE
    tiles_per_batch = tokens_per_batch // tm
    rope = ropes is not None
    states = state_tails is not None
    in_specs = [pl.BlockSpec((tm, D_MODEL), lambda i: (i, 0)),
                pl.BlockSpec((1, 8, 3 * D_MODEL), lambda i: (layer, 0, 0)),
                _full(gains.shape)]
    in_specs += [_full(w.shape) for w in weights]
    in_specs += [_full(c.shape) for c in consts]
    args = [x, mods, gains, *weights, *consts]
    if rope:
        in_specs += [pl.BlockSpec((tm, LANES), lambda i: (i % tiles_per_batch, 0)) for _ in ropes]
        args += list(ropes)
    out_shape = [jax.ShapeDtypeStruct((rows, w), BF16) for w in out_widths]
    out_specs = [pl.BlockSpec((tm, w), lambda i: (i, 0)) for w in out_widths]
    aliases = {}
    if states:
        slot = layer // 2
        out_shape += [jax.ShapeDtypeStruct((rows // seq, DEPTH // 2) + tail, F32) for tail in state_tails]
        out_specs += [pl.BlockSpec((tm // seq, None) + tail, lambda i: (i, slot, 0, 0)) for tail in state_tails]
        if state_prev is not None:
            aliases = {len(args) + j: len(out_widths) + j for j in range(len(state_prev))}
            in_specs += [pl.BlockSpec(memory_space=pl.ANY) for _ in state_prev]
            args += list(state_prev)
    return pl.pallas_call(
        functools.partial(kernel, rope=rope, states=states, n_alias=len(aliases), row_base=row_base,
                          tiles_per_batch=tiles_per_batch),
        out_shape=out_shape,
        grid=(rows // tm,),
        in_specs=in_specs,
        out_specs=out_specs,
        input_output_aliases=aliases,
        compiler_params=_cparams(),
        name=kernel.__name__.strip("_"),
    )(*args)


def _mla_cache_kernel(c_ref, wkvb_ref, bd64_ref, g_ref, kn_ref, vm_ref):
    _mla_kv(c_ref[...], wkvb_ref, bd64_ref, g_ref[5:6, 0:256], kn_ref, vm_ref)


def _mla_cache_kv(ckv, wkvb, bd64, gains):
    rows = ckv.shape[0]
    return pl.pallas_call(
        _mla_cache_kernel,
        out_shape=[jax.ShapeDtypeStruct((rows, 512), BF16)] * 2,
        grid=(1,),
        in_specs=[_full(ckv.shape), _full(wkvb.shape), _full(bd64.shape), _full(gains.shape)],
        out_specs=[_full((rows, 512))] * 2,
        compiler_params=_cparams(),
        name="mla_cache_kv",
    )(ckv, wkvb, bd64, gains)


def _pair_attn_kernel(*refs, mode, kinds, n_pairs, mxu_denominator, lam_init):
    it = iter(refs)
    q_ref = next(it)
    segs = [[next(it) for _ in seg_kinds] for seg_kinds in kinds]
    gate_ref = next(it)
    if mode == "diff":
        lam_ref, subln_ref = next(it), next(it)
    o_ref = next(it)

    qw = 2 * LANES if mode == "mla" else LANES
    rows = q_ref.shape[0]
    lo = _lane_mask(rows, LANES, 0, LANES // 2)
    if mode == "diff":
        lv = lam_ref[...]
        lam = (jnp.exp(jnp.sum(lv[0:1] * lv[1:2], axis=-1, keepdims=True))
               - jnp.exp(jnp.sum(lv[2:3] * lv[3:4], axis=-1, keepdims=True)) + lam_init)

    def load(ref, kind, p):
        if kind == "rows4":
            x = ref[pl.ds(p, ref.shape[0] // 4, stride=4), :]
        elif mode == "gqa":
            x = ref[...]
        else:
            sl = slice(LANES * p, LANES * (p + 1))
            x = ref[sl, :] if kind == "cols" else ref[:, sl]
        return x.astype(BF16), kind == "cols"

    def values(p):
        return [load(seg[-1], seg_kinds[-1], p) for seg, seg_kinds in zip(segs, kinds)]

    def scores(p, j):
        q = q_ref[:, qw * p:qw * (p + 1)]
        ksegs = []
        for seg, seg_kinds in zip(segs, kinds):
            k, transposed = load(seg[0], seg_kinds[0], p)
            if mode == "mla":
                k = jnp.concatenate([k, seg[1][...].astype(BF16)], axis=-1)
            ksegs.append((k, transposed))
        keep = _lane_mask(rows, qw, 64 * j, 64 * (j + 1))
        if mode == "mla":
            keep = keep | _lane_mask(rows, qw, LANES + 32 * j, LANES + 32 * (j + 1))
        return _scores(jnp.where(keep, q, jnp.zeros_like(q)), ksegs)

    def finish(p, outs):
        sl = slice(LANES * p, LANES * (p + 1))
        if mode == "diff":
            d = outs[0] - lam * outs[1]
            o = _rms_full(d, subln_ref[...]) * (1.0 - lam_init)
        else:
            o = jnp.where(lo, outs[0], outs[1])
        o_ref[:, sl] = (o * gate_ref[:, sl].astype(F32)).astype(BF16)

    heads = [(p, j) for p in range(n_pairs) for j in range(2)]
    ss = scores(*heads[0])
    outs = []
    for t, (p, j) in enumerate(heads):
        ss_next = scores(*heads[t + 1]) if t + 1 < len(heads) else None
        spare = (64 * (1 - j), 64 * (2 - j)) if mxu_denominator and mode != "diff" else None
        outs.append(_softmax_pv(ss, values(p), ones_lanes=spare))
        ss = ss_next
        if j == 1:
            finish(p, outs)
            outs = []


def _pair_attention(mode, q, segs, gate, batch, tq, bq, n_pairs, mxu_denominator, extra=(), lam_init=0.0):
    qw = 2 * LANES if mode == "mla" else LANES
    nq = tq // bq
    steps_p = 4 // n_pairs
    in_specs = [pl.BlockSpec((bq, qw * n_pairs), lambda b, p, i: (b * nq + i, p))]
    args = [q]
    for seg in segs:
        for arr, spec, _ in seg:
            args.append(arr)
            in_specs.append(spec)
    kinds = tuple(tuple(kind for _, _, kind in seg) for seg in segs)
    in_specs.append(pl.BlockSpec((bq, LANES * n_pairs), lambda b, p, i: (b * nq + i, p)))
    args.append(gate)
    for arr in extra:
        args.append(arr)
        in_specs.append(pl.BlockSpec(arr.shape, lambda b, p, i: (0,) * arr.ndim))
    return pl.pallas_call(
        functools.partial(_pair_attn_kernel, mode=mode, kinds=kinds, n_pairs=n_pairs,
                          mxu_denominator=mxu_denominator, lam_init=lam_init),
        out_shape=jax.ShapeDtypeStruct((batch * tq, 4 * LANES), BF16),
        grid=(batch, steps_p, nq),
        in_specs=in_specs,
        out_specs=pl.BlockSpec((bq, LANES * n_pairs), lambda b, p, i: (b * nq + i, p)),
        compiler_params=_cparams(),
        name=mode + "_attention",
    )(*args)


def _self_seg(arr, tk, n_pairs, shared=False):
    if shared:
        return arr, pl.BlockSpec((tk, arr.shape[1]), lambda b, p, i: (b, 0)), "rows"
    return arr, pl.BlockSpec((tk, LANES * n_pairs), lambda b, p, i: (b, p)), "rows"


def _cache_seg(arr, layer, n_pairs, kind, shared=False):
    r, c = arr.shape[2:]
    if shared or kind == "rows4":
        return arr, pl.BlockSpec((None, None, r, c), lambda b, p, i: (b, layer, 0, 0)), kind
    if kind == "cols":
        return arr, pl.BlockSpec((None, None, LANES * n_pairs, c), lambda b, p, i: (b, layer, p, 0)), kind
    return arr, pl.BlockSpec((None, None, r, LANES * n_pairs), lambda b, p, i: (b, layer, 0, p)), kind


def _na_kernel(q_ref, k_ref, v_ref, ck_ref, cv_ref, bias_ref, gate_ref, o_ref):
    g = pl.program_id(1)
    n_groups = pl.num_programs(1)
    first_row = jnp.clip(NA_GROUP_ROWS * g - NA_ROWS // 2, 0, NA_GROUP_ROWS * n_groups - NA_WIN_ROWS)
    start = pl.multiple_of(first_row * GRID_W, GRID_W)
    win = NA_WIN_ROWS * GRID_W
    kwin = k_ref[pl.ds(start, win), :]
    vwin = v_ref[pl.ds(start, win), :]
    ck = ck_ref[...].astype(BF16)
    cv = cv_ref[...].astype(BF16)
    q = q_ref[...]
    rows = q.shape[0]
    lo = _lane_mask(rows, LANES, 0, LANES // 2)

    def scores(h):
        sl = slice(LANES * (h // 2), LANES * (h // 2 + 1))
        keep = _lane_mask(rows, LANES, 64 * (h % 2), 64 * (h % 2 + 1))
        qm = jnp.where(keep, q[:, sl], jnp.zeros_like(q[:, sl]))
        return _scores(qm, [(kwin[:, sl], False), (ck[sl, :], True)], bias0=bias_ref[h])

    ss = scores(0)
    outs = []
    for h in range(NA_HEADS):
        ss_next = scores(h + 1) if h + 1 < NA_HEADS else None
        sl = slice(LANES * (h // 2), LANES * (h // 2 + 1))
        j = h % 2
        outs.append(_softmax_pv(ss, [(vwin[:, sl], False), (cv[sl, :], True)],
                                ones_lanes=(64 * (1 - j), 64 * (2 - j))))
        if j == 1:
            o = jnp.where(lo, outs[0], outs[1])
            o_ref[:, sl] = (o * gate_ref[:, sl].astype(F32)).astype(BF16)
            outs = []
        ss = ss_next


def _na_attention(q, k, v, cache_k, cache_v, layer, bias, gate, batch, tq):
    bq = NA_GROUP_ROWS * GRID_W
    n_groups = tq // bq
    past = cache_k.shape[3]
    width = NA_HEADS * NA_HD

    def bias_map(b, g):
        return (layer, jnp.where(g == 0, 0, jnp.where(g == n_groups - 1, 2, 1)), 0, 0, 0)

    tok = pl.BlockSpec((bq, width), lambda b, g: (b * n_groups + g, 0))
    whole = pl.BlockSpec((tq, width), lambda b, g: (b, 0))
    cache = pl.BlockSpec((None, None, width, past), lambda b, g: (b, layer, 0, 0))
    return pl.pallas_call(
        _na_kernel,
        out_shape=jax.ShapeDtypeStruct((batch * tq, width), BF16),
        grid=(batch, n_groups),
        in_specs=[tok, whole, whole, cache, cache,
                  pl.BlockSpec((None, None, NA_HEADS, bq, NA_WIN_ROWS * GRID_W), bias_map), tok],
        out_specs=tok,
        compiler_params=_cparams(),
        name="na_attention",
    )(q, k, v, cache_k, cache_v, bias, gate)


def _na_bias_tables(rpb):
    qc = np.arange(GRID_W)[:, None]
    kc = np.arange(GRID_W)[None, :]
    cs = np.clip(qc - NA_COLS // 2, 0, GRID_W - NA_COLS)
    col_valid = (kc >= cs) & (kc < cs + NA_COLS)
    dc = kc - qc + NA_COLS - 1
    onehot = np.zeros((2 * NA_COLS - 1, GRID_W, GRID_W), np.float32)
    for d in range(2 * NA_COLS - 1):
        onehot[d] = (col_valid & (dc == d)).astype(np.float32)
    cols = jnp.einsum("lhrd,dqk->lhrqk", rpb.astype(F32), jnp.asarray(onehot), precision=lax.Precision.HIGHEST)
    cols = jnp.where(jnp.asarray(col_valid), cols * LOG2E, NEG)
    outside = jnp.full(cols.shape[:2] + (1, GRID_W, GRID_W), NEG, F32)
    cols = jnp.concatenate([cols, outside], axis=2)
    n_dr = 2 * NA_ROWS - 1
    idx = np.full((3, NA_GROUP_ROWS, NA_WIN_ROWS), n_dr, np.int32)
    for a in range(NA_GROUP_ROWS):
        for j in range(NA_WIN_ROWS):
            if j < NA_ROWS:
                idx[0, a, j] = j - a + NA_ROWS - 1
            if a <= j < a + NA_ROWS:
                idx[1, a, j] = j - a + NA_ROWS // 2 - 1
            if j >= NA_WIN_ROWS - NA_ROWS:
                idx[2, a, j] = j - a - (NA_WIN_ROWS - NA_ROWS) + NA_ROWS // 2 - 1
    layers = rpb.shape[0]
    return pl.pallas_call(
        functools.partial(_na_bias_kernel, idx=idx),
        out_shape=jax.ShapeDtypeStruct((layers, 3, NA_HEADS, NA_GROUP_ROWS * GRID_W, NA_WIN_ROWS * GRID_W), F32),
        grid=(layers, NA_HEADS),
        in_specs=[pl.BlockSpec((None, None, n_dr + 1, GRID_W, GRID_W), lambda l, h: (l, h, 0, 0, 0))],
        out_specs=pl.BlockSpec((None, 3, None, NA_GROUP_ROWS * GRID_W, NA_WIN_ROWS * GRID_W),
                               lambda l, h: (l, 0, h, 0, 0)),
        compiler_params=_cparams(),
        name="na_bias",
    )(cols)


def _na_bias_kernel(cols_ref, out_ref, *, idx):
    for t in range(3):
        for a in range(NA_GROUP_ROWS):
            for j in range(0, NA_WIN_ROWS, 2):
                pair = jnp.concatenate([cols_ref[int(idx[t, a, j])], cols_ref[int(idx[t, a, j + 1])]], axis=-1)
                out_ref[t, a * GRID_W:(a + 1) * GRID_W, j * GRID_W:(j + 2) * GRID_W] = pair


def _out_kernel(oa_ref, ob_ref, x_ref, mod_ref, w_ref, y_ref, *, row_base, tiles_per_batch):
    row = row_base + pl.program_id(0) // tiles_per_batch
    gate = mod_ref[0, pl.ds(row, 1), :][:, 2 * D_MODEL:]
    half = oa_ref.shape[1]
    acc = _dot(oa_ref[...], w_ref[:half, :]) + _dot(ob_ref[...], w_ref[half:, :])
    y_ref[...] = x_ref[...] + gate * acc


def _out_proj(oa, ob, x, mods, layer, w, row_base, tokens_per_batch):
    rows = x.shape[0]
    tm = ROW_TILE
    tiles_per_batch = tokens_per_batch // tm
    return pl.pallas_call(
        functools.partial(_out_kernel, row_base=row_base, tiles_per_batch=tiles_per_batch),
        out_shape=jax.ShapeDtypeStruct((rows, D_MODEL), F32),
        grid=(rows // tm,),
        in_specs=[pl.BlockSpec((tm, oa.shape[1]), lambda i: (i, 0)),
                  pl.BlockSpec((tm, ob.shape[1]), lambda i: (i, 0)),
                  pl.BlockSpec((tm, D_MODEL), lambda i: (i, 0)),
                  pl.BlockSpec((1, 8, 3 * D_MODEL), lambda i: (layer, 0, 0)),
                  _full(w.shape)],
        out_specs=pl.BlockSpec((tm, D_MODEL), lambda i: (i, 0)),
        compiler_params=_cparams(),
        name="out_proj",
    )(oa, ob, x, mods, w)


def _ctx_layer_kernel(*refs, even, n_in, n_alias, n_slabs, seq, lam_init, tiles_per_batch):
    in_refs = refs[:n_in]
    it = iter(refs[n_in:])
    wout_ref = next(it)
    if even:
        lam_ref, subln_ref = next(it), next(it)
    alias_refs = [next(it) for _ in range(n_alias)]
    y_ref = next(it)
    state_refs = [next(it) for _ in range(4)]
    slabs = [next(it) for _ in range(n_slabs)]
    oa_ref, ob_ref = next(it), next(it)
    x_ref, mod_ref = in_refs[0], in_refs[1]

    proj = _in_even_kernel if even else _in_odd_kernel
    proj(*in_refs, *alias_refs, *slabs, *state_refs, rope=False, states=True, n_alias=n_alias, row_base=0,
         tiles_per_batch=tiles_per_batch)
    for b in range(x_ref.shape[0] // seq):
        def own(ref):
            return ref.at[pl.ds(b * seq, seq), :]
        common = dict(n_pairs=4, mxu_denominator=False)
        if even:
            qcat, kn, kpe, vm, mg, dq, dk, dv, dg = (own(r) for r in slabs)
            _pair_attn_kernel(qcat, kn, kpe, vm, mg, own(oa_ref), mode="mla", kinds=(("rows",) * 3,),
                              lam_init=0.0, **common)
            _pair_attn_kernel(dq, dk, dv, dg, lam_ref, subln_ref, own(ob_ref), mode="diff", kinds=(("rows",) * 2,),
                              lam_init=lam_init, **common)
        else:
            nq, nk, nv, ng, gq, gk, gv, gg = (own(r) for r in slabs)
            _pair_attn_kernel(nq, nk, nv, ng, own(oa_ref), mode="mha", kinds=(("rows",) * 2,), lam_init=0.0, **common)
            _pair_attn_kernel(gq, gk, gv, gg, own(ob_ref), mode="gqa", kinds=(("rows",) * 2,), lam_init=0.0, **common)
    _out_kernel(oa_ref, ob_ref, x_ref, mod_ref, wout_ref, y_ref, row_base=0, tiles_per_batch=tiles_per_batch)


def _ctx_layer(even, x, mods, layer, gains, weights, consts, w_out, extras, slab_widths, state_tails, state_prev,
               seq, lam_init=0.0):
    rows = x.shape[0]
    tm = ROW_TILE
    slot = layer // 2
    in_specs = [pl.BlockSpec((tm, D_MODEL), lambda i: (i, 0)),
                pl.BlockSpec((1, 8, 3 * D_MODEL), lambda i: (layer, 0, 0)),
                _full(gains.shape)]
    in_specs += [_full(a.shape) for a in (*weights, *consts)]
    args = [x, mods, gains, *weights, *consts]
    n_in = len(args)
    in_specs += [_full(a.shape) for a in (w_out, *extras)]
    args += [w_out, *extras]
    aliases = {}
    if state_prev is not None:
        aliases = {len(args) + j: 1 + j for j in range(len(state_prev))}
        in_specs += [pl.BlockSpec(memory_space=pl.ANY) for _ in state_prev]
        args += list(state_prev)
    out_shape = [jax.ShapeDtypeStruct((rows, D_MODEL), F32)]
    out_specs = [pl.BlockSpec((tm, D_MODEL), lambda i: (i, 0))]
    out_shape += [jax.ShapeDtypeStruct((rows // seq, DEPTH // 2) + tail, F32) for tail in state_tails]
    out_specs += [pl.BlockSpec((tm // seq, None) + tail, lambda i: (i, slot, 0, 0)) for tail in state_tails]
    scratch = [pltpu.VMEM((tm, w), BF16) for w in (*slab_widths, 4 * LANES, 4 * LANES)]
    return pl.pallas_call(
        functools.partial(_ctx_layer_kernel, even=even, n_in=n_in, n_alias=len(aliases), n_slabs=len(slab_widths),
                          seq=seq, lam_init=lam_init, tiles_per_batch=rows // tm),
        out_shape=out_shape,
        grid=(rows // tm,),
        in_specs=in_specs,
        out_specs=out_specs,
        scratch_shapes=scratch,
        input_output_aliases=aliases,
        compiler_params=_cparams(),
        name="ctx_layer_even" if even else "ctx_layer_odd",
    )(*args)


def _block_diag(width, group):
    idx = np.arange(width) // group
    return jnp.asarray((idx[:, None] == idx[None, :]).astype(np.float32) / group, BF16)


def _rope_tables(t, rot_dim):
    pos = np.arange(t)
    row = (pos // GRID_W).astype(np.float64)
    col = (pos % GRID_W).astype(np.float64)
    n = rot_dim // 2
    inv = ROPE_THETA ** (-np.arange(0, n, 2, dtype=np.float64) / n)
    ang = np.concatenate([row[:, None] * inv, col[:, None] * inv], axis=-1)
    cos = np.concatenate([np.cos(ang), np.cos(ang)], axis=-1)
    sin = np.concatenate([-np.sin(ang), np.sin(ang)], axis=-1)
    reps = LANES // rot_dim
    return (jnp.asarray(np.tile(cos, (1, reps)), F32), jnp.asarray(np.tile(sin, (1, reps)), F32))


def _pad_row(v, width=D_MODEL):
    return jnp.pad(v, (0, width - v.shape[0]))


def _tile_row(v, reps):
    return _pad_row(jnp.tile(v, reps))


def _permute_heads(w, axis):
    shape = w.shape
    split = shape[:axis] + (GQA_HEADS, GQA_HD) + shape[axis + 1:]
    return jnp.take(w.reshape(split), jnp.asarray(GQA_PERM), axis=axis).reshape(shape)


def kernel(x_prompt, x_sample, cache_mla_ckv, cache_mla_kpe, cache_diff_k, cache_diff_v, cache_na_k, cache_na_v, cache_gqa_k, cache_gqa_v, c, c_ctx, norm_w, w_mod, b_mod, w_in_even, w_out_even, mla_qa_norm, mla_wqb, mla_kva_norm, mla_wkvb, mla_qn_nope, mla_qn_rope, mla_kn_nope, mla_kn_rope, diff_qn, diff_kn, diff_lq1, diff_lk1, diff_lq2, diff_lk2, diff_subln, w_in_odd, w_out_odd, na_qn, na_kn, na_rpb, gqa_qn, gqa_kn):
    batch, seq, _ = x_prompt.shape
    dec_batch, dec_seq, _ = x_sample.shape
    past = cache_mla_ckv.shape[2]
    n_even, n_odd = w_in_even.shape[0], w_in_odd.shape[0]

    w_in_e = jnp.concatenate([w_in_even[..., :384], jnp.tile(w_in_even[..., 384:416], (1, 1, 4)),
                              w_in_even[..., 416:]], axis=-1).astype(BF16)
    wqb = mla_wqb.reshape(n_even, Q_LORA, MLA_HEADS, MLA_QK)
    wqb = jnp.concatenate([wqb[..., :MLA_NOPE].reshape(n_even, Q_LORA, 4, 2 * MLA_NOPE),
                           wqb[..., MLA_NOPE:].reshape(n_even, Q_LORA, 4, 2 * MLA_ROPE),
                           jnp.zeros((n_even, Q_LORA, 4, LANES - 2 * MLA_ROPE), F32)], axis=-1)
    wqb = wqb.reshape(n_even, Q_LORA, 4 * 2 * LANES).astype(BF16)
    wkvb = mla_wkvb.reshape(n_even, KV_LORA, MLA_HEADS, 2 * MLA_NOPE)
    wkvb = jnp.concatenate([wkvb[..., :MLA_NOPE].reshape(n_even, KV_LORA, 512),
                            wkvb[..., MLA_NOPE:].reshape(n_even, KV_LORA, 512)], axis=-1).astype(BF16)
    w_out_e = w_out_even.astype(BF16)
    w_in_o = jnp.concatenate([w_in_odd[..., :2048], _permute_heads(w_in_odd[..., 2048:2560], 2),
                              w_in_odd[..., 2560:2816], _permute_heads(w_in_odd[..., 2816:], 2)],
                             axis=-1).astype(BF16)
    w_out_o = jnp.concatenate([w_out_odd[:, :512], _permute_heads(w_out_odd[:, 512:], 1)], axis=1).astype(BF16)

    gains_e = [jnp.stack([norm_w[2 * i], _pad_row(mla_qa_norm[i]),
                          _pad_row(jnp.concatenate([jnp.tile(mla_qn_nope[i], 2), jnp.tile(mla_qn_rope[i], 4)])),
                          _pad_row(mla_kva_norm[i]), _tile_row(mla_kn_rope[i], 4), _tile_row(mla_kn_nope[i], 4),
                          _tile_row(diff_qn[i], 4), _tile_row(diff_kn[i], 4)]) for i in range(n_even)]
    gains_o = [jnp.stack([norm_w[2 * i + 1], _tile_row(na_qn[i], 4), _tile_row(na_kn[i], 4),
                          _tile_row(gqa_qn[i], 4), _tile_row(gqa_kn[i], 2),
                          jnp.zeros((D_MODEL,), F32), jnp.zeros((D_MODEL,), F32), jnp.zeros((D_MODEL,), F32)])
               for i in range(n_odd)]
    lam_vecs = [jnp.stack([diff_lq1[i], diff_lk1[i], diff_lq2[i], diff_lk2[i]]) for i in range(n_even)]

    bd64 = _block_diag(256, 64)
    bd32 = _block_diag(LANES, 32)
    bdq = jnp.concatenate([jnp.concatenate([_block_diag(LANES, 64), jnp.zeros((LANES, LANES), BF16)], axis=1),
                           jnp.concatenate([jnp.zeros((LANES, LANES), BF16), _block_diag(LANES, 32)], axis=1)], axis=0)
    c64, s64 = _rope_tables(dec_seq, 64)
    c32, s32 = _rope_tables(dec_seq, MLA_ROPE)
    na_bias = _na_bias_tables(na_rpb)

    cvecs = jnp.concatenate([c_ctx[None, :], c, jnp.zeros((8 - 1 - dec_batch, D_MODEL), F32)], axis=0)
    mods = _modulation(cvecs, w_mod, b_mod)

    cache_kpe = jnp.tile(cache_mla_kpe, (1, 1, 1, 4))
    cache_dk = cache_diff_k.transpose(0, 1, 3, 4, 5, 2).reshape(dec_batch, n_even, 512, past)
    cache_dv = cache_diff_v.reshape(dec_batch, n_even, 4 * past, LANES)
    cache_nk = cache_na_k.transpose(0, 1, 3, 4, 2).reshape(dec_batch, n_odd, 512, past)
    cache_nv = cache_na_v.transpose(0, 1, 3, 4, 2).reshape(dec_batch, n_odd, 512, past)
    cache_gk = cache_gqa_k.transpose(0, 1, 3, 4, 2).reshape(dec_batch, n_odd, LANES, past)
    cache_gv = cache_gqa_v.transpose(0, 1, 3, 4, 2).reshape(dec_batch, n_odd, LANES, past)

    even_widths = (1024, 512, LANES, 512, 512, 512, 512, 512, 512)
    odd_widths = (512, 512, 512, 512, 512, LANES, LANES, 512)
    even_states = ((seq, KV_LORA), (MLA_ROPE, seq), (512, seq), (4 * seq, LANES))
    odd_states = ((512, seq), (512, seq), (LANES, seq), (LANES, seq))

    def lam_init(l):
        return 0.8 - 0.6 * math.exp(-0.3 * l)

    def context_pass(x):
        states = [None, None]
        for l in range(DEPTH):
            i = l // 2
            if l % 2 == 0:
                x, *states[0] = _ctx_layer(True, x, mods, l, gains_e[i], (w_in_e[i], wqb[i], wkvb[i]),
                                           (bd64, bd32, bdq), w_out_e[i], (lam_vecs[i], diff_subln[i][None, :]),
                                           even_widths, even_states, states[0], seq, lam_init(l))
            else:
                x, *states[1] = _ctx_layer(False, x, mods, l, gains_o[i], (w_in_o[i],), (bd64,), w_out_o[i], (),
                                           odd_widths, odd_states, states[1], seq)
        return x, states

    def latent_pass(x, nb, t):
        row_base = 1
        bq = Q_BLOCK
        n_pairs = 4
        for l in range(DEPTH):
            i = l // 2
            if l % 2 == 0:
                outs = _in_proj(_in_even_kernel, x, mods, l, gains_e[i], (w_in_e[i], wqb[i], wkvb[i]),
                                (bd64, bd32, bdq), (c64, s64, c32, s32), even_widths, None, None, row_base, t, t)
                qcat, kn, kpe, vm, mg, dq, dk, dv, dg = outs
                kn_c, vm_c = _mla_cache_kv(cache_mla_ckv[:, i].reshape(nb * past, KV_LORA), wkvb[i], bd64, gains_e[i])
                mla_segs = [(_self_seg(kn, t, n_pairs), _self_seg(kpe, t, n_pairs, shared=True),
                             _self_seg(vm, t, n_pairs)),
                            (_self_seg(kn_c, past, n_pairs), _cache_seg(cache_kpe, i, n_pairs, "rows", shared=True),
                             _self_seg(vm_c, past, n_pairs))]
                diff_segs = [(_self_seg(dk, t, n_pairs), _self_seg(dv, t, n_pairs)),
                             (_cache_seg(cache_dk, i, n_pairs, "cols"), _cache_seg(cache_dv, i, n_pairs, "rows4"))]
                oa = _pair_attention("mla", qcat, mla_segs, mg, nb, t, bq, n_pairs, True)
                ob = _pair_attention("diff", dq, diff_segs, dg, nb, t, bq, n_pairs, True,
                                     extra=(lam_vecs[i], diff_subln[i][None, :]), lam_init=lam_init(l))
                x = _out_proj(oa, ob, x, mods, l, w_out_e[i], row_base, t)
            else:
                outs = _in_proj(_in_odd_kernel, x, mods, l, gains_o[i], (w_in_o[i],), (bd64,), (c64, s64),
                                odd_widths, None, None, row_base, t, t)
                nq, nk, nv, ng, gq, gk, gv, gg = outs
                oa = _na_attention(nq, nk, nv, cache_nk, cache_nv, i, na_bias, ng, nb, t)
                gqa_segs = [(_self_seg(gk, t, n_pairs, shared=True), _self_seg(gv, t, n_pairs, shared=True)),
                            (_cache_seg(cache_gk, i, n_pairs, "cols", shared=True),
                             _cache_seg(cache_gv, i, n_pairs, "cols", shared=True))]
                ob = _pair_attention("gqa", gq, gqa_segs, gg, nb, t, bq, n_pairs, True)
                x = _out_proj(oa, ob, x, mods, l, w_out_o[i], row_base, t)
        return x

    y_prompt, st = context_pass(x_prompt.reshape(batch * seq, D_MODEL))
    y_sample = latent_pass(x_sample.reshape(dec_batch * dec_seq, D_MODEL), dec_batch, dec_seq)

    def token_major(a, heads):
        a = a.reshape((batch, n_even) + heads + (a.shape[2] // math.prod(heads), seq))
        return jnp.moveaxis(a, -1, 2)

    ckv, kpe_t, dk_t, dv4 = st[0]
    nk_t, nv_t, gk_t, gv_t = st[1]
    return (y_prompt.reshape(batch, seq, D_MODEL), y_sample.reshape(dec_batch, dec_seq, D_MODEL),
            ckv, token_major(kpe_t, ()), token_major(dk_t, (DIFF_HEADS, 2)),
            dv4.reshape(batch, n_even, seq, DIFF_HEADS, 2 * DIFF_HD),
            token_major(nk_t, (NA_HEADS,)), token_major(nv_t, (NA_HEADS,)),
            token_major(gk_t, (GQA_KV,)), token_major(gv_t, (GQA_KV,)))
```

```python
import functools
import math

import jax
import jax.numpy as jnp
import numpy as np
from jax import lax
from jax.experimental import pallas as pl
from jax.experimental.pallas import tpu as pltpu

F32 = jnp.float32
BF16 = jnp.bfloat16

D_MODEL = 1024
DEPTH = 4
GRID_W = 64
ROPE_THETA = 10000.0
EPS = 1e-6
MLA_HEADS = 8
MLA_NOPE = 64
MLA_ROPE = 32
MLA_QK = MLA_NOPE + MLA_ROPE
Q_LORA = 256
KV_LORA = 128
DIFF_HEADS = 4
DIFF_HD = 64
NA_HEADS = 8
NA_HD = 64
NA_ROWS = 8
NA_COLS = 16
GQA_HEADS = 8
GQA_KV = 2
GQA_HD = 64
LANES = 128
NA_GROUP_ROWS = 4
NA_WIN_ROWS = NA_ROWS + NA_GROUP_ROWS
NEG = -1e30
LOG2E = math.log2(math.e)
VMEM_LIMIT = 48 * 1024 * 1024
GQA_PERM = (0, 4, 1, 5, 2, 6, 3, 7)

ROW_TILE = 512
Q_BLOCK = 512


def _cparams():
    return pltpu.CompilerParams(vmem_limit_bytes=VMEM_LIMIT)


def _dot(a, b):
    return jnp.dot(a, b, preferred_element_type=F32)


def _dot_nt(a, b):
    return lax.dot_general(a, b, (((1,), (1,)), ((), ())), preferred_element_type=F32)


def _rms_full(x, g):
    ms = jnp.mean(x * x, axis=-1, keepdims=True)
    return x * lax.rsqrt(ms + EPS) * g


def _rms_group(x, bd, g):
    ms = _dot((x * x).astype(BF16), bd)
    return x * lax.rsqrt(ms + EPS) * g


def _silu(u):
    return u * (1.0 / (1.0 + jnp.exp(-u)))


def _rope(x, cos, sin, group):
    half = group // 2
    rows, width = x.shape
    lane = lax.broadcasted_iota(jnp.int32, (rows, LANES), 1)
    first = (lane & (group - 1)) < half
    outs = []
    for c in range(width // LANES):
        xc = x[:, c * LANES:(c + 1) * LANES]
        rot = jnp.where(first, pltpu.roll(xc, LANES - half, 1), pltpu.roll(xc, half, 1))
        outs.append(xc * cos + rot * sin)
    return outs[0] if len(outs) == 1 else jnp.concatenate(outs, axis=-1)


def _lane_mask(rows, width, lo, hi):
    lane = lax.broadcasted_iota(jnp.int32, (rows, width), 1)
    return (lane >= lo) & (lane < hi)


def _scores(qm, ksegs, bias0=None):
    ss = [_dot(qm, k) if transposed else _dot_nt(qm, k) for k, transposed in ksegs]
    if bias0 is not None:
        ss[0] = ss[0] + bias0
    return ss


def _softmax_pv(ss, vsegs, ones_lanes=None):
    m = jnp.max(ss[0], axis=-1, keepdims=True)
    for s in ss[1:]:
        m = jnp.maximum(m, jnp.max(s, axis=-1, keepdims=True))
    acc = None
    l = None
    for s, (v, transposed) in zip(ss, vsegs):
        p = jnp.exp2(s - m)
        if ones_lanes is None:
            ps = jnp.sum(p, axis=-1, keepdims=True)
            l = ps if l is None else l + ps
        else:
            width = lax.broadcasted_iota(jnp.int32, v.shape, 0 if transposed else 1)
            v = jnp.where((width >= ones_lanes[0]) & (width < ones_lanes[1]), jnp.ones_like(v), v)
        a = _dot_nt(p.astype(BF16), v) if transposed else _dot(p.astype(BF16), v)
        acc = a if acc is None else acc + a
    if ones_lanes is not None:
        l = pltpu.roll(acc, LANES // 2, 1)
    return acc / l


def _mod_kernel(c_ref, w_ref, b_ref, o_ref):
    c = c_ref[...]
    o_ref[0] = _dot(_silu(c).astype(BF16), w_ref[0].astype(BF16)) + b_ref[0]


def _modulation(cvecs, w_mod, b_mod):
    tn = 768
    return pl.pallas_call(
        _mod_kernel,
        out_shape=jax.ShapeDtypeStruct((DEPTH, 8, 3 * D_MODEL), F32),
        grid=(DEPTH, 3 * D_MODEL // tn),
        in_specs=[pl.BlockSpec((8, D_MODEL), lambda l, n: (0, 0)),
                  pl.BlockSpec((1, D_MODEL, tn), lambda l, n: (l, 0, n)),
                  pl.BlockSpec((1, 1, tn), lambda l, n: (l, 0, n))],
        out_specs=pl.BlockSpec((1, 8, tn), lambda l, n: (l, 0, n)),
        compiler_params=_cparams(),
        name="modulation",
    )(cvecs, w_mod, b_mod.reshape(DEPTH, 1, 3 * D_MODEL))


def _modulated_norm(x_ref, mod_ref, g_ref, row):
    mod = mod_ref[0, pl.ds(row, 1), :]
    shift = mod[:, :D_MODEL]
    scale = mod[:, D_MODEL:2 * D_MODEL]
    h = _rms_full(x_ref[...], g_ref[0:1, :]) * (1.0 + scale) + shift
    return h.astype(BF16)


def _mla_kv(cn, wkvb_ref, bd64_ref, g_kn, kn_ref, vm_ref):
    kv = _dot(cn.astype(BF16), wkvb_ref[...])
    for c in range(2):
        sl = slice(256 * c, 256 * (c + 1))
        kn_ref[:, sl] = _rms_group(kv[:, sl], bd64_ref[...], g_kn).astype(BF16)
    vm_ref[...] = kv[:, 512:].astype(BF16)


def _store_rows(ref, x):
    seq = ref.shape[1]
    for b in range(ref.shape[0]):
        ref[b] = x[b * seq:(b + 1) * seq]


def _store_transposed(ref, x, row0, keep=None):
    seq = ref.shape[2]
    for b in range(ref.shape[0]):
        xt = x[b * seq:(b + 1) * seq].T
        if keep is not None:
            xt = xt[:keep]
        ref[b, row0:row0 + xt.shape[0], :] = xt


def _store_heads4(ref, x):
    seq = ref.shape[1] // 4
    for b in range(ref.shape[0]):
        for h in range(4):
            ref[b, pl.ds(h, seq, stride=4), :] = x[b * seq:(b + 1) * seq, LANES * h:LANES * (h + 1)]


def _in_even_kernel(*refs, rope, states, n_alias, row_base, tiles_per_batch):
    it = iter(refs)
    x_ref, mod_ref, g_ref, w_ref, wqb_ref, wkvb_ref, bd64_ref, bd32_ref, bdq_ref = (next(it) for _ in range(9))
    if rope:
        c64_ref, s64_ref, c32_ref, s32_ref = (next(it) for _ in range(4))
    for _ in range(n_alias):
        next(it)
    qcat_ref, kn_ref, kpe_ref, vm_ref, mg_ref, dq_ref, dk_ref, dv_ref, dg_ref = (next(it) for _ in range(9))
    if states:
        st_ckv_ref, st_kpe_ref, st_dk_ref, st_dv_ref = (next(it) for _ in range(4))

    row = row_base + pl.program_id(0) // tiles_per_batch
    hb = _modulated_norm(x_ref, mod_ref, g_ref, row)
    bd64 = bd64_ref[...]
    mla_scale = MLA_QK ** -0.5 * LOG2E
    diff_scale = DIFF_HD ** -0.5 * LOG2E

    qa = _dot(hb, w_ref[:, 0:256])
    qa_n = _rms_full(qa, g_ref[1:2, 0:256]).astype(BF16)
    q = _dot(qa_n, wqb_ref[...])
    for p in range(4):
        qp = _rms_group(q[:, 256 * p:256 * (p + 1)], bdq_ref[...], g_ref[2:3, 0:256])
        q_nope = qp[:, :LANES]
        q_pe = qp[:, LANES:]
        if rope:
            q_pe = _rope(q_pe, c32_ref[...], s32_ref[...], MLA_ROPE)
        qcat_ref[:, 256 * p:256 * p + LANES] = (q_nope * mla_scale).astype(BF16)
        qcat_ref[:, 256 * p + LANES:256 * (p + 1)] = (q_pe * mla_scale).astype(BF16)

    kva = _dot(hb, w_ref[:, 256:512])
    c_kv = _rms_full(kva[:, :LANES], g_ref[3:4, 0:LANES])
    k_pe = _rms_group(kva[:, LANES:], bd32_ref[...], g_ref[4:5, 0:LANES])
    if states:
        _store_rows(st_ckv_ref, c_kv)
        _store_transposed(st_kpe_ref, k_pe, 0, keep=MLA_ROPE)
    if rope:
        k_pe = _rope(k_pe, c32_ref[...], s32_ref[...], MLA_ROPE)
    kpe_ref[...] = k_pe.astype(BF16)
    _mla_kv(c_kv, wkvb_ref, bd64_ref, g_ref[5:6, 0:256], kn_ref, vm_ref)

    mg_ref[...] = _silu(_dot(hb, w_ref[:, 512:1024])).astype(BF16)

    dq = _dot(hb, w_ref[:, 1024:1536])
    dk = _dot(hb, w_ref[:, 1536:2048])
    for c in range(2):
        sl = slice(256 * c, 256 * (c + 1))
        qn = _rms_group(dq[:, sl], bd64, g_ref[6:7, 0:256])
        kn = _rms_group(dk[:, sl], bd64, g_ref[7:8, 0:256])
        if states:
            _store_transposed(st_dk_ref, kn, 256 * c)
        if rope:
            qn = _rope(qn, c64_ref[...], s64_ref[...], DIFF_HD)
            kn = _rope(kn, c64_ref[...], s64_ref[...], DIFF_HD)
        dq_ref[:, sl] = (qn * diff_scale).astype(BF16)
        dk_ref[:, sl] = kn.astype(BF16)
    dv = _dot(hb, w_ref[:, 2048:2560])
    if states:
        _store_heads4(st_dv_ref, dv)
    dv_ref[...] = dv.astype(BF16)
    dg_ref[...] = _silu(_dot(hb, w_ref[:, 2560:3072])).astype(BF16)


def _in_odd_kernel(*refs, rope, states, n_alias, row_base, tiles_per_batch):
    it = iter(refs)
    x_ref, mod_ref, g_ref, w_ref, bd64_ref = (next(it) for _ in range(5))
    if rope:
        c64_ref, s64_ref = (next(it) for _ in range(2))
    for _ in range(n_alias):
        next(it)
    nq_ref, nk_ref, nv_ref, ng_ref, gq_ref, gk_ref, gv_ref, gg_ref = (next(it) for _ in range(8))
    if states:
        st_nk_ref, st_nv_ref, st_gk_ref, st_gv_ref = (next(it) for _ in range(4))

    row = row_base + pl.program_id(0) // tiles_per_batch
    hb = _modulated_norm(x_ref, mod_ref, g_ref, row)
    bd64 = bd64_ref[...]
    na_scale = NA_HD ** -0.5 * LOG2E
    gqa_scale = GQA_HD ** -0.5 * LOG2E

    nq = _dot(hb, w_ref[:, 0:512])
    nk = _dot(hb, w_ref[:, 512:1024])
    for c in range(2):
        sl = slice(256 * c, 256 * (c + 1))
        nq_ref[:, sl] = (_rms_group(nq[:, sl], bd64, g_ref[1:2, 0:256]) * na_scale).astype(BF16)
        kn = _rms_group(nk[:, sl], bd64, g_ref[2:3, 0:256])
        if states:
            _store_transposed(st_nk_ref, kn, 256 * c)
        nk_ref[:, sl] = kn.astype(BF16)
    nv = _dot(hb, w_ref[:, 1024:1536])
    if states:
        for c in range(2):
            _store_transposed(st_nv_ref, nv[:, 256 * c:256 * (c + 1)], 256 * c)
    nv_ref[...] = nv.astype(BF16)
    ng_ref[...] = _silu(_dot(hb, w_ref[:, 1536:2048])).astype(BF16)

    gq = _dot(hb, w_ref[:, 2048:2560])
    for c in range(2):
        sl = slice(256 * c, 256 * (c + 1))
        qn = _rms_group(gq[:, sl], bd64, g_ref[3:4, 0:256])
        if rope:
            qn = _rope(qn, c64_ref[...], s64_ref[...], GQA_HD)
        gq_ref[:, sl] = (qn * gqa_scale).astype(BF16)
    gkv = _dot(hb, w_ref[:, 2560:2816])
    gk = _rms_group(gkv[:, :LANES], bd64[:LANES, :LANES], g_ref[4:5, 0:LANES])
    gv = gkv[:, LANES:]
    if states:
        _store_transposed(st_gk_ref, gk, 0)
        _store_transposed(st_gv_ref, gv, 0)
    if rope:
        gk = _rope(gk, c64_ref[...], s64_ref[...], GQA_HD)
    gk_ref[...] = gk.astype(BF16)
    gv_ref[...] = gv.astype(BF16)
    gg_ref[...] = _silu(_dot(hb, w_ref[:, 2816:3328])).astype(BF16)


def _full(shape):
    zeros = (0,) * len(shape)
    return pl.BlockSpec(shape, lambda i: zeros)


def _in_proj(kernel, x, mods, layer, gains, weights, consts, ropes, out_widths, row_base, tokens_per_batch):
    rows = x.shape[0]
    tm = ROW_TILE
    tiles_per_batch = tokens_per_batch // tm
    in_specs = [pl.BlockSpec((tm, D_MODEL), lambda i: (i, 0)),
                pl.BlockSpec((1, 8, 3 * D_MODEL), lambda i: (layer, 0, 0)),
                _full(gains.shape)]
    in_specs += [_full(a.shape) for a in (*weights, *consts)]
    in_specs += [pl.BlockSpec((tm, LANES), lambda i: (i % tiles_per_batch, 0)) for _ in ropes]
    return pl.pallas_call(
        functools.partial(kernel, rope=True, states=False, n_alias=0, row_base=row_base,
                          tiles_per_batch=tiles_per_batch),
        out_shape=[jax.ShapeDtypeStruct((rows, w), BF16) for w in out_widths],
        grid=(rows // tm,),
        in_specs=in_specs,
        out_specs=[pl.BlockSpec((tm, w), lambda i: (i, 0)) for w in out_widths],
        compiler_params=_cparams(),
        name=kernel.__name__.strip("_"),
    )(x, mods, gains, *weights, *consts, *ropes)


def _mla_cache_kernel(c_ref, wkvb_ref, bd64_ref, g_ref, kn_ref, vm_ref):
    _mla_kv(c_ref[...], wkvb_ref, bd64_ref, g_ref[5:6, 0:256], kn_ref, vm_ref)


def _mla_cache_kv(ckv, wkvb, bd64, gains):
    rows = ckv.shape[0]
    return pl.pallas_call(
        _mla_cache_kernel,
        out_shape=[jax.ShapeDtypeStruct((rows, 512), BF16)] * 2,
        grid=(1,),
        in_specs=[_full(ckv.shape), _full(wkvb.shape), _full(bd64.shape), _full(gains.shape)],
        out_specs=[_full((rows, 512))] * 2,
        compiler_params=_cparams(),
        name="mla_cache_kv",
    )(ckv, wkvb, bd64, gains)


def _pair_attn_kernel(*refs, mode, kinds, n_pairs, mxu_denominator, lam_init):
    it = iter(refs)
    q_ref = next(it)
    segs = [[next(it) for _ in seg_kinds] for seg_kinds in kinds]
    gate_ref = next(it)
    if mode == "diff":
        lam_ref, subln_ref = next(it), next(it)
    o_ref = next(it)

    qw = 2 * LANES if mode == "mla" else LANES
    rows = q_ref.shape[0]
    lo = _lane_mask(rows, LANES, 0, LANES // 2)
    if mode == "diff":
        lv = lam_ref[...]
        lam = (jnp.exp(jnp.sum(lv[0:1] * lv[1:2], axis=-1, keepdims=True))
               - jnp.exp(jnp.sum(lv[2:3] * lv[3:4], axis=-1, keepdims=True)) + lam_init)

    def load(ref, kind, p):
        if kind == "rows4":
            x = ref[pl.ds(p, ref.shape[0] // 4, stride=4), :]
        elif mode == "gqa":
            x = ref[...]
        else:
            sl = slice(LANES * p, LANES * (p + 1))
            x = ref[sl, :] if kind == "cols" else ref[:, sl]
        return x.astype(BF16), kind == "cols"

    def values(p):
        return [load(seg[-1], seg_kinds[-1], p) for seg, seg_kinds in zip(segs, kinds)]

    def scores(p, j):
        q = q_ref[:, qw * p:qw * (p + 1)]
        ksegs = []
        for seg, seg_kinds in zip(segs, kinds):
            k, transposed = load(seg[0], seg_kinds[0], p)
            if mode == "mla":
                k = jnp.concatenate([k, seg[1][...].astype(BF16)], axis=-1)
            ksegs.append((k, transposed))
        keep = _lane_mask(rows, qw, 64 * j, 64 * (j + 1))
        if mode == "mla":
            keep = keep | _lane_mask(rows, qw, LANES + 32 * j, LANES + 32 * (j + 1))
        return _scores(jnp.where(keep, q, jnp.zeros_like(q)), ksegs)

    def finish(p, outs):
        sl = slice(LANES * p, LANES * (p + 1))
        if mode == "diff":
            d = outs[0] - lam * outs[1]
            o = _rms_full(d, subln_ref[...]) * (1.0 - lam_init)
        else:
            o = jnp.where(lo, outs[0], outs[1])
        o_ref[:, sl] = (o * gate_ref[:, sl].astype(F32)).astype(BF16)

    heads = [(p, j) for p in range(n_pairs) for j in range(2)]
    ss = scores(*heads[0])
    outs = []
    for t, (p, j) in enumerate(heads):
        ss_next = scores(*heads[t + 1]) if t + 1 < len(heads) else None
        spare = (64 * (1 - j), 64 * (2 - j)) if mxu_denominator and mode != "diff" else None
        outs.append(_softmax_pv(ss, values(p), ones_lanes=spare))
        ss = ss_next
        if j == 1:
            finish(p, outs)
            outs = []


def _pair_attention(mode, q, segs, gate, batch, tq, bq, n_pairs, mxu_denominator, extra=(), lam_init=0.0):
    qw = 2 * LANES if mode == "mla" else LANES
    nq = tq // bq
    steps_p = 4 // n_pairs
    in_specs = [pl.BlockSpec((bq, qw * n_pairs), lambda b, p, i: (b * nq + i, p))]
    args = [q]
    for seg in segs:
        for arr, spec, _ in seg:
            args.append(arr)
            in_specs.append(spec)
    kinds = tuple(tuple(kind for _, _, kind in seg) for seg in segs)
    in_specs.append(pl.BlockSpec((bq, LANES * n_pairs), lambda b, p, i: (b * nq + i, p)))
    args.append(gate)
    for arr in extra:
        args.append(arr)
        in_specs.append(pl.BlockSpec(arr.shape, lambda b, p, i: (0,) * arr.ndim))
    return pl.pallas_call(
        functools.partial(_pair_attn_kernel, mode=mode, kinds=kinds, n_pairs=n_pairs,
                          mxu_denominator=mxu_denominator, lam_init=lam_init),
        out_shape=jax.ShapeDtypeStruct((batch * tq, 4 * LANES), BF16),
        grid=(batch, steps_p, nq),
        in_specs=in_specs,
        out_specs=pl.BlockSpec((bq, LANES * n_pairs), lambda b, p, i: (b * nq + i, p)),
        compiler_params=_cparams(),
        name=mode + "_attention",
    )(*args)


def _self_seg(arr, tk, n_pairs, shared=False):
    if shared:
        return arr, pl.BlockSpec((tk, arr.shape[1]), lambda b, p, i: (b, 0)), "rows"
    return arr, pl.BlockSpec((tk, LANES * n_pairs), lambda b, p, i: (b, p)), "rows"


def _cache_seg(arr, layer, n_pairs, kind, shared=False):
    r, c = arr.shape[2:]
    if shared or kind == "rows4":
        return arr, pl.BlockSpec((None, None, r, c), lambda b, p, i: (b, layer, 0, 0)), kind
    if kind == "cols":
        return arr, pl.BlockSpec((None, None, LANES * n_pairs, c), lambda b, p, i: (b, layer, p, 0)), kind
    return arr, pl.BlockSpec((None, None, r, LANES * n_pairs), lambda b, p, i: (b, layer, 0, p)), kind


def _na_kernel(q_ref, k_ref, v_ref, ck_ref, cv_ref, bias_ref, gate_ref, o_ref):
    g = pl.program_id(1)
    n_groups = pl.num_programs(1)
    first_row = jnp.clip(NA_GROUP_ROWS * g - NA_ROWS // 2, 0, NA_GROUP_ROWS * n_groups - NA_WIN_ROWS)
    start = pl.multiple_of(first_row * GRID_W, GRID_W)
    win = NA_WIN_ROWS * GRID_W
    kwin = k_ref[pl.ds(start, win), :]
    vwin = v_ref[pl.ds(start, win), :]
    ck = ck_ref[...].astype(BF16)
    cv = cv_ref[...].astype(BF16)
    q = q_ref[...]
    rows = q.shape[0]
    lo = _lane_mask(rows, LANES, 0, LANES // 2)

    def scores(h):
        sl = slice(LANES * (h // 2), LANES * (h // 2 + 1))
        keep = _lane_mask(rows, LANES, 64 * (h % 2), 64 * (h % 2 + 1))
        qm = jnp.where(keep, q[:, sl], jnp.zeros_like(q[:, sl]))
        return _scores(qm, [(kwin[:, sl], False), (ck[sl, :], True)], bias0=bias_ref[h])

    ss = scores(0)
    outs = []
    for h in range(NA_HEADS):
        ss_next = scores(h + 1) if h + 1 < NA_HEADS else None
        sl = slice(LANES * (h // 2), LANES * (h // 2 + 1))
        j = h % 2
        outs.append(_softmax_pv(ss, [(vwin[:, sl], False), (cv[sl, :], True)],
                                ones_lanes=(64 * (1 - j), 64 * (2 - j))))
        if j == 1:
            o = jnp.where(lo, outs[0], outs[1])
            o_ref[:, sl] = (o * gate_ref[:, sl].astype(F32)).astype(BF16)
            outs = []
        ss = ss_next


def _na_attention(q, k, v, cache_k, cache_v, layer, bias, gate, batch, tq):
    bq = NA_GROUP_ROWS * GRID_W
    n_groups = tq // bq
    past = cache_k.shape[3]
    width = NA_HEADS * NA_HD

    def bias_map(b, g):
        return (layer, jnp.where(g == 0, 0, jnp.where(g == n_groups - 1, 2, 1)), 0, 0, 0)

    tok = pl.BlockSpec((bq, width), lambda b, g: (b * n_groups + g, 0))
    whole = pl.BlockSpec((tq, width), lambda b, g: (b, 0))
    cache = pl.BlockSpec((None, None, width, past), lambda b, g: (b, layer, 0, 0))
    return pl.pallas_call(
        _na_kernel,
        out_shape=jax.ShapeDtypeStruct((batch * tq, width), BF16),
        grid=(batch, n_groups),
        in_specs=[tok, whole, whole, cache, cache,
                  pl.BlockSpec((None, None, NA_HEADS, bq, NA_WIN_ROWS * GRID_W), bias_map), tok],
        out_specs=tok,
        compiler_params=_cparams(),
        name="na_attention",
    )(q, k, v, cache_k, cache_v, bias, gate)


def _na_bias_tables(rpb):
    qc = np.arange(GRID_W)[:, None]
    kc = np.arange(GRID_W)[None, :]
    cs = np.clip(qc - NA_COLS // 2, 0, GRID_W - NA_COLS)
    col_valid = (kc >= cs) & (kc < cs + NA_COLS)
    dc = kc - qc + NA_COLS - 1
    onehot = np.zeros((2 * NA_COLS - 1, GRID_W, GRID_W), np.float32)
    for d in range(2 * NA_COLS - 1):
        onehot[d] = (col_valid & (dc == d)).astype(np.float32)
    cols = jnp.einsum("lhrd,dqk->lhrqk", rpb.astype(F32), jnp.asarray(onehot), precision=lax.Precision.HIGHEST)
    cols = jnp.where(jnp.asarray(col_valid), cols * LOG2E, NEG)
    outside = jnp.full(cols.shape[:2] + (1, GRID_W, GRID_W), NEG, F32)
    cols = jnp.concatenate([cols, outside], axis=2)
    n_dr = 2 * NA_ROWS - 1
    idx = np.full((3, NA_GROUP_ROWS, NA_WIN_ROWS), n_dr, np.int32)
    for a in range(NA_GROUP_ROWS):
        for j in range(NA_WIN_ROWS):
            if j < NA_ROWS:
                idx[0, a, j] = j - a + NA_ROWS - 1
            if a <= j < a + NA_ROWS:
                idx[1, a, j] = j - a + NA_ROWS // 2 - 1
            if j >= NA_WIN_ROWS - NA_ROWS:
                idx[2, a, j] = j - a - (NA_WIN_ROWS - NA_ROWS) + NA_ROWS // 2 - 1
    layers = rpb.shape[0]
    return pl.pallas_call(
        functools.partial(_na_bias_kernel, idx=idx),
        out_shape=jax.ShapeDtypeStruct((layers, 3, NA_HEADS, NA_GROUP_ROWS * GRID_W, NA_WIN_ROWS * GRID_W), F32),
        grid=(layers, NA_HEADS),
        in_specs=[pl.BlockSpec((None, None, n_dr + 1, GRID_W, GRID_W), lambda l, h: (l, h, 0, 0, 0))],
        out_specs=pl.BlockSpec((None, 3, None, NA_GROUP_ROWS * GRID_W, NA_WIN_ROWS * GRID_W),
                               lambda l, h: (l, 0, h, 0, 0)),
        compiler_params=_cparams(),
        name="na_bias",
    )(cols)


def _na_bias_kernel(cols_ref, out_ref, *, idx):
    for t in range(3):
        for a in range(NA_GROUP_ROWS):
            for j in range(0, NA_WIN_ROWS, 2):
                pair = jnp.concatenate([cols_ref[int(idx[t, a, j])], cols_ref[int(idx[t, a, j + 1])]], axis=-1)
                out_ref[t, a * GRID_W:(a + 1) * GRID_W, j * GRID_W:(j + 2) * GRID_W] = pair


def _out_kernel(oa_ref, ob_ref, x_ref, mod_ref, w_ref, y_ref, *, row_base, tiles_per_batch):
    row = row_base + pl.program_id(0) // tiles_per_batch
    gate = mod_ref[0, pl.ds(row, 1), :][:, 2 * D_MODEL:]
    half = oa_ref.shape[1]
    acc = _dot(oa_ref[...], w_ref[:half, :]) + _dot(ob_ref[...], w_ref[half:, :])
    y_ref[...] = x_ref[...] + gate * acc


def _out_proj(oa, ob, x, mods, layer, w, row_base, tokens_per_batch):
    rows = x.shape[0]
    tm = ROW_TILE
    tiles_per_batch = tokens_per_batch // tm
    return pl.pallas_call(
        functools.partial(_out_kernel, row_base=row_base, tiles_per_batch=tiles_per_batch),
        out_shape=jax.ShapeDtypeStruct((rows, D_MODEL), F32),
        grid=(rows // tm,),
        in_specs=[pl.BlockSpec((tm, oa.shape[1]), lambda i: (i, 0)),
                  pl.BlockSpec((tm, ob.shape[1]), lambda i: (i, 0)),
                  pl.BlockSpec((tm, D_MODEL), lambda i: (i, 0)),
                  pl.BlockSpec((1, 8, 3 * D_MODEL), lambda i: (layer, 0, 0)),
                  _full(w.shape)],
        out_specs=pl.BlockSpec((tm, D_MODEL), lambda i: (i, 0)),
        compiler_params=_cparams(),
        name="out_proj",
    )(oa, ob, x, mods, w)


def _out_in_kernel(*refs, proj, rope, row_base, tiles_per_batch):
    oa_ref, ob_ref, x_ref, mod_prev_ref, wout_ref = refs[:5]
    n_proj_in = len(refs) - 5 - 1 - (9 if proj is _in_even_kernel else 8)
    proj_in = refs[5:5 + n_proj_in]
    y_ref = refs[5 + n_proj_in]
    slabs = refs[6 + n_proj_in:]
    _out_kernel(oa_ref, ob_ref, x_ref, mod_prev_ref, wout_ref, y_ref, row_base=row_base,
                tiles_per_batch=tiles_per_batch)
    proj(y_ref, *proj_in, *slabs, rope=rope, states=False, n_alias=0, row_base=row_base,
         tiles_per_batch=tiles_per_batch)


def _out_in_proj(proj, oa, ob, x, mods, layer, w_out, gains, weights, consts, ropes, out_widths, row_base,
                 tokens_per_batch):
    rows = x.shape[0]
    tm = ROW_TILE
    tiles_per_batch = tokens_per_batch // tm
    row = pl.BlockSpec((tm, D_MODEL), lambda i: (i, 0))
    in_specs = [pl.BlockSpec((tm, oa.shape[1]), lambda i: (i, 0)), pl.BlockSpec((tm, ob.shape[1]), lambda i: (i, 0)),
                row, pl.BlockSpec((1, 8, 3 * D_MODEL), lambda i: (layer - 1, 0, 0)), _full(w_out.shape),
                pl.BlockSpec((1, 8, 3 * D_MODEL), lambda i: (layer, 0, 0)), _full(gains.shape)]
    in_specs += [_full(a.shape) for a in (*weights, *consts)]
    in_specs += [pl.BlockSpec((tm, LANES), lambda i: (i % tiles_per_batch, 0)) for _ in ropes]
    return pl.pallas_call(
        functools.partial(_out_in_kernel, proj=proj, rope=True, row_base=row_base, tiles_per_batch=tiles_per_batch),
        out_shape=[jax.ShapeDtypeStruct((rows, D_MODEL), F32)]
        + [jax.ShapeDtypeStruct((rows, w), BF16) for w in out_widths],
        grid=(rows // tm,),
        in_specs=in_specs,
        out_specs=[row] + [pl.BlockSpec((tm, w), lambda i: (i, 0)) for w in out_widths],
        compiler_params=_cparams(),
        name="out_" + proj.__name__.strip("_"),
    )(oa, ob, x, mods, w_out, mods, gains, *weights, *consts, *ropes)


def _ctx_layer_kernel(*refs, even, n_in, n_alias, n_slabs, seq, lam_init, tiles_per_batch):
    in_refs = refs[:n_in]
    it = iter(refs[n_in:])
    wout_ref = next(it)
    if even:
        lam_ref, subln_ref = next(it), next(it)
    alias_refs = [next(it) for _ in range(n_alias)]
    y_ref = next(it)
    state_refs = [next(it) for _ in range(4)]
    slabs = [next(it) for _ in range(n_slabs)]
    oa_ref, ob_ref = next(it), next(it)
    x_ref, mod_ref = in_refs[0], in_refs[1]

    proj = _in_even_kernel if even else _in_odd_kernel
    proj(*in_refs, *alias_refs, *slabs, *state_refs, rope=False, states=True, n_alias=n_alias, row_base=0,
         tiles_per_batch=tiles_per_batch)
    for b in range(x_ref.shape[0] // seq):
        def own(ref):
            return ref.at[pl.ds(b * seq, seq), :]
        common = dict(n_pairs=4, mxu_denominator=False)
        if even:
            qcat, kn, kpe, vm, mg, dq, dk, dv, dg = (own(r) for r in slabs)
            _pair_attn_kernel(qcat, kn, kpe, vm, mg, own(oa_ref), mode="mla", kinds=(("rows",) * 3,),
                              lam_init=0.0, **common)
            _pair_attn_kernel(dq, dk, dv, dg, lam_ref, subln_ref, own(ob_ref), mode="diff", kinds=(("rows",) * 2,),
                              lam_init=lam_init, **common)
        else:
            nq, nk, nv, ng, gq, gk, gv, gg = (own(r) for r in slabs)
            _pair_attn_kernel(nq, nk, nv, ng, own(oa_ref), mode="mha", kinds=(("rows",) * 2,), lam_init=0.0, **common)
            _pair_attn_kernel(gq, gk, gv, gg, own(ob_ref), mode="gqa", kinds=(("rows",) * 2,), lam_init=0.0, **common)
    _out_kernel(oa_ref, ob_ref, x_ref, mod_ref, wout_ref, y_ref, row_base=0, tiles_per_batch=tiles_per_batch)


def _ctx_layer(even, x, mods, layer, gains, weights, consts, w_out, extras, slab_widths, state_tails, state_prev,
               seq, lam_init=0.0):
    rows = x.shape[0]
    tm = ROW_TILE
    slot = layer // 2
    in_specs = [pl.BlockSpec((tm, D_MODEL), lambda i: (i, 0)),
                pl.BlockSpec((1, 8, 3 * D_MODEL), lambda i: (layer, 0, 0)),
                _full(gains.shape)]
    in_specs += [_full(a.shape) for a in (*weights, *consts)]
    args = [x, mods, gains, *weights, *consts]
    n_in = len(args)
    in_specs += [_full(a.shape) for a in (w_out, *extras)]
    args += [w_out, *extras]
    aliases = {}
    if state_prev is not None:
        aliases = {len(args) + j: 1 + j for j in range(len(state_prev))}
        in_specs += [pl.BlockSpec(memory_space=pl.ANY) for _ in state_prev]
        args += list(state_prev)
    out_shape = [jax.ShapeDtypeStruct((rows, D_MODEL), F32)]
    out_specs = [pl.BlockSpec((tm, D_MODEL), lambda i: (i, 0))]
    out_shape += [jax.ShapeDtypeStruct((rows // seq, DEPTH // 2) + tail, F32) for tail in state_tails]
    out_specs += [pl.BlockSpec((tm // seq, None) + tail, lambda i: (i, slot, 0, 0)) for tail in state_tails]
    scratch = [pltpu.VMEM((tm, w), BF16) for w in (*slab_widths, 4 * LANES, 4 * LANES)]
    return pl.pallas_call(
        functools.partial(_ctx_layer_kernel, even=even, n_in=n_in, n_alias=len(aliases), n_slabs=len(slab_widths),
                          seq=seq, lam_init=lam_init, tiles_per_batch=rows // tm),
        out_shape=out_shape,
        grid=(rows // tm,),
        in_specs=in_specs,
        out_specs=out_specs,
        scratch_shapes=scratch,
        input_output_aliases=aliases,
        compiler_params=_cparams(),
        name="ctx_layer_even" if even else "ctx_layer_odd",
    )(*args)


def _block_diag(width, group):
    idx = np.arange(width) // group
    return jnp.asarray((idx[:, None] == idx[None, :]).astype(np.float32) / group, BF16)


def _rope_tables(t, rot_dim):
    pos = np.arange(t)
    row = (pos // GRID_W).astype(np.float64)
    col = (pos % GRID_W).astype(np.float64)
    n = rot_dim // 2
    inv = ROPE_THETA ** (-np.arange(0, n, 2, dtype=np.float64) / n)
    ang = np.concatenate([row[:, None] * inv, col[:, None] * inv], axis=-1)
    cos = np.concatenate([np.cos(ang), np.cos(ang)], axis=-1)
    sin = np.concatenate([-np.sin(ang), np.sin(ang)], axis=-1)
    reps = LANES // rot_dim
    return (jnp.asarray(np.tile(cos, (1, reps)), F32), jnp.asarray(np.tile(sin, (1, reps)), F32))


def _pad_row(v, width=D_MODEL):
    return jnp.pad(v, (0, width - v.shape[0]))


def _tile_row(v, reps):
    return _pad_row(jnp.tile(v, reps))


def _permute_heads(w, axis):
    shape = w.shape
    split = shape[:axis] + (GQA_HEADS, GQA_HD) + shape[axis + 1:]
    return jnp.take(w.reshape(split), jnp.asarray(GQA_PERM), axis=axis).reshape(shape)


def kernel(x_prompt, x_sample, cache_mla_ckv, cache_mla_kpe, cache_diff_k, cache_diff_v, cache_na_k, cache_na_v, cache_gqa_k, cache_gqa_v, c, c_ctx, norm_w, w_mod, b_mod, w_in_even, w_out_even, mla_qa_norm, mla_wqb, mla_kva_norm, mla_wkvb, mla_qn_nope, mla_qn_rope, mla_kn_nope, mla_kn_rope, diff_qn, diff_kn, diff_lq1, diff_lk1, diff_lq2, diff_lk2, diff_subln, w_in_odd, w_out_odd, na_qn, na_kn, na_rpb, gqa_qn, gqa_kn):
    batch, seq, _ = x_prompt.shape
    dec_batch, dec_seq, _ = x_sample.shape
    past = cache_mla_ckv.shape[2]
    n_even, n_odd = w_in_even.shape[0], w_in_odd.shape[0]

    def layout_in_even(w):
        return jnp.concatenate([w[:, :384], jnp.tile(w[:, 384:416], (1, 4)), w[:, 416:]], axis=-1).astype(BF16)

    def layout_wqb(w):
        w = w.reshape(Q_LORA, MLA_HEADS, MLA_QK)
        w = jnp.concatenate([w[..., :MLA_NOPE].reshape(Q_LORA, 4, 2 * MLA_NOPE),
                             w[..., MLA_NOPE:].reshape(Q_LORA, 4, 2 * MLA_ROPE),
                             jnp.zeros((Q_LORA, 4, LANES - 2 * MLA_ROPE), F32)], axis=-1)
        return w.reshape(Q_LORA, 4 * 2 * LANES).astype(BF16)

    def layout_wkvb(w):
        w = w.reshape(KV_LORA, MLA_HEADS, 2 * MLA_NOPE)
        return jnp.concatenate([w[..., :MLA_NOPE].reshape(KV_LORA, 512),
                                w[..., MLA_NOPE:].reshape(KV_LORA, 512)], axis=-1).astype(BF16)

    def layout_in_odd(w):
        return jnp.concatenate([w[:, :2048], _permute_heads(w[:, 2048:2560], 1), w[:, 2560:2816],
                                _permute_heads(w[:, 2816:], 1)], axis=-1).astype(BF16)

    def layout_out_odd(w):
        return jnp.concatenate([w[:512], _permute_heads(w[512:], 0)], axis=0).astype(BF16)

    w_in_e = [layout_in_even(w_in_even[i]) for i in range(n_even)]
    wqb = [layout_wqb(mla_wqb[i]) for i in range(n_even)]
    wkvb = [layout_wkvb(mla_wkvb[i]) for i in range(n_even)]
    w_out_e = [w_out_even[i].astype(BF16) for i in range(n_even)]
    w_in_o = [layout_in_odd(w_in_odd[i]) for i in range(n_odd)]
    w_out_o = [layout_out_odd(w_out_odd[i]) for i in range(n_odd)]

    gains_e = [jnp.stack([norm_w[2 * i], _pad_row(mla_qa_norm[i]),
                          _pad_row(jnp.concatenate([jnp.tile(mla_qn_nope[i], 2), jnp.tile(mla_qn_rope[i], 4)])),
                          _pad_row(mla_kva_norm[i]), _tile_row(mla_kn_rope[i], 4), _tile_row(mla_kn_nope[i], 4),
                          _tile_row(diff_qn[i], 4), _tile_row(diff_kn[i], 4)]) for i in range(n_even)]
    gains_o = [jnp.stack([norm_w[2 * i + 1], _tile_row(na_qn[i], 4), _tile_row(na_kn[i], 4),
                          _tile_row(gqa_qn[i], 4), _tile_row(gqa_kn[i], 2),
                          jnp.zeros((D_MODEL,), F32), jnp.zeros((D_MODEL,), F32), jnp.zeros((D_MODEL,), F32)])
               for i in range(n_odd)]
    lam_vecs = [jnp.stack([diff_lq1[i], diff_lk1[i], diff_lq2[i], diff_lk2[i]]) for i in range(n_even)]

    bd64 = _block_diag(256, 64)
    bd32 = _block_diag(LANES, 32)
    bdq = jnp.concatenate([jnp.concatenate([_block_diag(LANES, 64), jnp.zeros((LANES, LANES), BF16)], axis=1),
                           jnp.concatenate([jnp.zeros((LANES, LANES), BF16), _block_diag(LANES, 32)], axis=1)], axis=0)
    c64, s64 = _rope_tables(dec_seq, 64)
    c32, s32 = _rope_tables(dec_seq, MLA_ROPE)
    na_bias = _na_bias_tables(na_rpb)

    cvecs = jnp.concatenate([c_ctx[None, :], c, jnp.zeros((8 - 1 - dec_batch, D_MODEL), F32)], axis=0)
    mods = _modulation(cvecs, w_mod, b_mod)

    cache_kpe = jnp.tile(cache_mla_kpe, (1, 1, 1, 4))
    cache_dk = cache_diff_k.transpose(0, 1, 3, 4, 5, 2).reshape(dec_batch, n_even, 512, past)
    cache_dv = cache_diff_v.reshape(dec_batch, n_even, 4 * past, LANES)
    cache_nk = cache_na_k.transpose(0, 1, 3, 4, 2).reshape(dec_batch, n_odd, 512, past)
    cache_nv = cache_na_v.transpose(0, 1, 3, 4, 2).reshape(dec_batch, n_odd, 512, past)
    cache_gk = cache_gqa_k.transpose(0, 1, 3, 4, 2).reshape(dec_batch, n_odd, LANES, past)
    cache_gv = cache_gqa_v.transpose(0, 1, 3, 4, 2).reshape(dec_batch, n_odd, LANES, past)

    even_widths = (1024, 512, LANES, 512, 512, 512, 512, 512, 512)
    odd_widths = (512, 512, 512, 512, 512, LANES, LANES, 512)
    even_states = ((seq, KV_LORA), (MLA_ROPE, seq), (512, seq), (4 * seq, LANES))
    odd_states = ((512, seq), (512, seq), (LANES, seq), (LANES, seq))

    def lam_init(l):
        return 0.8 - 0.6 * math.exp(-0.3 * l)

    def context_pass(x):
        states = [None, None]
        for l in range(DEPTH):
            i = l // 2
            if l % 2 == 0:
                x, *states[0] = _ctx_layer(True, x, mods, l, gains_e[i], (w_in_e[i], wqb[i], wkvb[i]),
                                           (bd64, bd32, bdq), w_out_e[i], (lam_vecs[i], diff_subln[i][None, :]),
                                           even_widths, even_states, states[0], seq, lam_init(l))
            else:
                x, *states[1] = _ctx_layer(False, x, mods, l, gains_o[i], (w_in_o[i],), (bd64,), w_out_o[i], (),
                                           odd_widths, odd_states, states[1], seq)
        return x, states

    def latent_pass(x, nb, t):
        row_base = 1
        bq = Q_BLOCK
        n_pairs = 4
        oa = ob = w_out_prev = None
        for l in range(DEPTH):
            i = l // 2
            if l % 2 == 0:
                proj_args = (gains_e[i], (w_in_e[i], wqb[i], wkvb[i]), (bd64, bd32, bdq), (c64, s64, c32, s32),
                             even_widths)
                proj = _in_even_kernel
            else:
                proj_args = (gains_o[i], (w_in_o[i],), (bd64,), (c64, s64), odd_widths)
                proj = _in_odd_kernel
            if l == 0:
                outs = _in_proj(proj, x, mods, l, *proj_args, row_base, t)
            else:
                x, *outs = _out_in_proj(proj, oa, ob, x, mods, l, w_out_prev, *proj_args, row_base, t)
            if l % 2 == 0:
                qcat, kn, kpe, vm, mg, dq, dk, dv, dg = outs
                kn_c, vm_c = _mla_cache_kv(cache_mla_ckv[:, i].reshape(nb * past, KV_LORA), wkvb[i], bd64, gains_e[i])
                mla_segs = [(_self_seg(kn, t, n_pairs), _self_seg(kpe, t, n_pairs, shared=True),
                             _self_seg(vm, t, n_pairs)),
                            (_self_seg(kn_c, past, n_pairs), _cache_seg(cache_kpe, i, n_pairs, "rows", shared=True),
                             _self_seg(vm_c, past, n_pairs))]
                diff_segs = [(_self_seg(dk, t, n_pairs), _self_seg(dv, t, n_pairs)),
                             (_cache_seg(cache_dk, i, n_pairs, "cols"), _cache_seg(cache_dv, i, n_pairs, "rows4"))]
                oa = _pair_attention("mla", qcat, mla_segs, mg, nb, t, bq, n_pairs, True)
                ob = _pair_attention("diff", dq, diff_segs, dg, nb, t, bq, n_pairs, True,
                                     extra=(lam_vecs[i], diff_subln[i][None, :]), lam_init=lam_init(l))
                w_out_prev = w_out_e[i]
            else:
                nq, nk, nv, ng, gq, gk, gv, gg = outs
                oa = _na_attention(nq, nk, nv, cache_nk, cache_nv, i, na_bias, ng, nb, t)
                gqa_segs = [(_self_seg(gk, t, n_pairs, shared=True), _self_seg(gv, t, n_pairs, shared=True)),
                            (_cache_seg(cache_gk, i, n_pairs, "cols", shared=True),
                             _cache_seg(cache_gv, i, n_pairs, "cols", shared=True))]
                ob = _pair_attention("gqa", gq, gqa_segs, gg, nb, t, bq, n_pairs, True)
                w_out_prev = w_out_o[i]
        return _out_proj(oa, ob, x, mods, DEPTH - 1, w_out_prev, row_base, t)

    y_prompt, st = context_pass(x_prompt.reshape(batch * seq, D_MODEL))
    y_sample = latent_pass(x_sample.reshape(dec_batch * dec_seq, D_MODEL), dec_batch, dec_seq)

    def token_major(a, heads):
        a = a.reshape((batch, n_even) + heads + (a.shape[2] // math.prod(heads), seq))
        return jnp.moveaxis(a, -1, 2)

    ckv, kpe_t, dk_t, dv4 = st[0]
    nk_t, nv_t, gk_t, gv_t = st[1]
    return (y_prompt.reshape(batch, seq, D_MODEL), y_sample.reshape(dec_batch, dec_seq, D_MODEL),
            ckv, token_major(kpe_t, ()), token_major(dk_t, (DIFF_HEADS, 2)),
            dv4.reshape(batch, n_even, seq, DIFF_HEADS, 2 * DIFF_HD),
            token_major(nk_t, (NA_HEADS,)), token_major(nv_t, (NA_HEADS,)),
            token_major(gk_t, (GQA_KV,)), token_major(gv_t, (GQA_KV,)))
```

```python
import functools
import math
from typing import NamedTuple

import jax
import jax.numpy as jnp
import numpy as np
from jax import lax
from jax.experimental import pallas as pl
from jax.experimental.pallas import tpu as pltpu

F32 = jnp.float32
BF16 = jnp.bfloat16

D_MODEL = 1024
DEPTH = 4
GRID_W = 64
ROPE_THETA = 10000.0
EPS = 1e-6
MLA_HEADS = 8
MLA_NOPE = 64
MLA_ROPE = 32
MLA_QK = MLA_NOPE + MLA_ROPE
Q_LORA = 256
KV_LORA = 128
DIFF_HEADS = 4
DIFF_HD = 64
NA_HEADS = 8
NA_HD = 64
NA_ROWS = 8
NA_COLS = 16
GQA_HEADS = 8
GQA_KV = 2
GQA_HD = 64
LANES = 128
NA_GROUP_ROWS = 4
NA_WIN_ROWS = NA_ROWS + NA_GROUP_ROWS
NEG = -1e30
LOG2E = math.log2(math.e)
VMEM_LIMIT = 48 * 1024 * 1024
GQA_PERM = (0, 4, 1, 5, 2, 6, 3, 7)

ROW_TILE = 512
Q_BLOCK = 512


def _cparams():
    return pltpu.CompilerParams(vmem_limit_bytes=VMEM_LIMIT)


def _dot(a, b):
    return jnp.dot(a, b, preferred_element_type=F32)


def _dot_nt(a, b):
    return lax.dot_general(a, b, (((1,), (1,)), ((), ())), preferred_element_type=F32)


def _rms_full(x, g):
    ms = jnp.mean(x * x, axis=-1, keepdims=True)
    return x * lax.rsqrt(ms + EPS) * g


def _rms_group(x, bd, g):
    ms = _dot((x * x).astype(BF16), bd)
    return x * lax.rsqrt(ms + EPS) * g


def _silu(u):
    return u * (1.0 / (1.0 + jnp.exp(-u)))


def _rope(x, cos, sin, group):
    half = group // 2
    rows, width = x.shape
    lane = lax.broadcasted_iota(jnp.int32, (rows, LANES), 1)
    first = (lane & (group - 1)) < half
    outs = []
    for c in range(width // LANES):
        xc = x[:, c * LANES:(c + 1) * LANES]
        rot = jnp.where(first, pltpu.roll(xc, LANES - half, 1), pltpu.roll(xc, half, 1))
        outs.append(xc * cos + rot * sin)
    return outs[0] if len(outs) == 1 else jnp.concatenate(outs, axis=-1)


def _lane_mask(rows, width, lo, hi):
    lane = lax.broadcasted_iota(jnp.int32, (rows, width), 1)
    return (lane >= lo) & (lane < hi)


def _scores(qm, ksegs, bias0=None):
    ss = [_dot(qm, k) if transposed else _dot_nt(qm, k) for k, transposed in ksegs]
    if bias0 is not None:
        ss[0] = ss[0] + bias0
    return ss


def _softmax_pv(ss, vsegs, ones_lanes=None):
    m = jnp.max(ss[0], axis=-1, keepdims=True)
    for s in ss[1:]:
        m = jnp.maximum(m, jnp.max(s, axis=-1, keepdims=True))
    acc = None
    l = None
    for s, (v, transposed) in zip(ss, vsegs):
        p = jnp.exp2(s - m)
        if ones_lanes is None:
            ps = jnp.sum(p, axis=-1, keepdims=True)
            l = ps if l is None else l + ps
        else:
            width = lax.broadcasted_iota(jnp.int32, v.shape, 0 if transposed else 1)
            v = jnp.where((width >= ones_lanes[0]) & (width < ones_lanes[1]), jnp.ones_like(v), v)
        a = _dot_nt(p.astype(BF16), v) if transposed else _dot(p.astype(BF16), v)
        acc = a if acc is None else acc + a
    if ones_lanes is not None:
        l = pltpu.roll(acc, LANES // 2, 1)
    return acc / l


def _mod_kernel(c_ref, w_ref, b_ref, o_ref):
    c = c_ref[...]
    o_ref[0] = _dot(_silu(c).astype(BF16), w_ref[0].astype(BF16)) + b_ref[0]


def _modulation(cvecs, w_mod, b_mod):
    tn = 768
    return pl.pallas_call(
        _mod_kernel,
        out_shape=jax.ShapeDtypeStruct((DEPTH, 8, 3 * D_MODEL), F32),
        grid=(DEPTH, 3 * D_MODEL // tn),
        in_specs=[pl.BlockSpec((8, D_MODEL), lambda l, n: (0, 0)),
                  pl.BlockSpec((1, D_MODEL, tn), lambda l, n: (l, 0, n)),
                  pl.BlockSpec((1, 1, tn), lambda l, n: (l, 0, n))],
        out_specs=pl.BlockSpec((1, 8, tn), lambda l, n: (l, 0, n)),
        compiler_params=_cparams(),
        name="modulation",
    )(cvecs, w_mod, b_mod.reshape(DEPTH, 1, 3 * D_MODEL))


def _modulated_norm(x_ref, mod_ref, g_ref, row):
    mod = mod_ref[0, pl.ds(row, 1), :]
    shift = mod[:, :D_MODEL]
    scale = mod[:, D_MODEL:2 * D_MODEL]
    h = _rms_full(x_ref[...], g_ref[0:1, :]) * (1.0 + scale) + shift
    return h.astype(BF16)


def _mla_kv(cn, wkvb_ref, bd64_ref, g_kn, kn_ref, vm_ref):
    kv = _dot(cn.astype(BF16), wkvb_ref[...])
    for c in range(2):
        sl = slice(256 * c, 256 * (c + 1))
        kn_ref[:, sl] = _rms_group(kv[:, sl], bd64_ref[...], g_kn).astype(BF16)
    vm_ref[...] = kv[:, 512:].astype(BF16)


def _store_rows(ref, x):
    seq = ref.shape[1]
    for b in range(ref.shape[0]):
        ref[b] = x[b * seq:(b + 1) * seq]


def _store_transposed(ref, x, row0, keep=None):
    seq = ref.shape[2]
    for b in range(ref.shape[0]):
        xt = x[b * seq:(b + 1) * seq].T
        if keep is not None:
            xt = xt[:keep]
        ref[b, row0:row0 + xt.shape[0], :] = xt


def _store_heads4(ref, x):
    seq = ref.shape[1] // 4
    for b in range(ref.shape[0]):
        for h in range(4):
            ref[b, pl.ds(h, seq, stride=4), :] = x[b * seq:(b + 1) * seq, LANES * h:LANES * (h + 1)]


def _in_even_kernel(*refs, rope, states, n_alias, row_base, tiles_per_batch):
    it = iter(refs)
    x_ref, mod_ref, g_ref, w_ref, wqb_ref, wkvb_ref, bd64_ref, bd32_ref, bdq_ref = (next(it) for _ in range(9))
    if rope:
        c64_ref, s64_ref, c32_ref, s32_ref = (next(it) for _ in range(4))
    for _ in range(n_alias):
        next(it)
    qcat_ref, kn_ref, kpe_ref, vm_ref, mg_ref, dq_ref, dk_ref, dv_ref, dg_ref = (next(it) for _ in range(9))
    if states:
        st_ckv_ref, st_kpe_ref, st_dk_ref, st_dv_ref = (next(it) for _ in range(4))

    row = row_base + pl.program_id(0) // tiles_per_batch
    hb = _modulated_norm(x_ref, mod_ref, g_ref, row)
    bd64 = bd64_ref[...]
    mla_scale = MLA_QK ** -0.5 * LOG2E
    diff_scale = DIFF_HD ** -0.5 * LOG2E

    qa = _dot(hb, w_ref[:, 0:256])
    qa_n = _rms_full(qa, g_ref[1:2, 0:256]).astype(BF16)
    q = _dot(qa_n, wqb_ref[...])
    for p in range(4):
        qp = _rms_group(q[:, 256 * p:256 * (p + 1)], bdq_ref[...], g_ref[2:3, 0:256])
        q_nope = qp[:, :LANES]
        q_pe = qp[:, LANES:]
        if rope:
            q_pe = _rope(q_pe, c32_ref[...], s32_ref[...], MLA_ROPE)
        qcat_ref[:, 256 * p:256 * p + LANES] = (q_nope * mla_scale).astype(BF16)
        qcat_ref[:, 256 * p + LANES:256 * (p + 1)] = (q_pe * mla_scale).astype(BF16)

    kva = _dot(hb, w_ref[:, 256:512])
    c_kv = _rms_full(kva[:, :LANES], g_ref[3:4, 0:LANES])
    k_pe = _rms_group(kva[:, LANES:], bd32_ref[...], g_ref[4:5, 0:LANES])
    if states:
        _store_rows(st_ckv_ref, c_kv)
        _store_transposed(st_kpe_ref, k_pe, 0, keep=MLA_ROPE)
    if rope:
        k_pe = _rope(k_pe, c32_ref[...], s32_ref[...], MLA_ROPE)
    kpe_ref[...] = k_pe.astype(BF16)
    _mla_kv(c_kv, wkvb_ref, bd64_ref, g_ref[5:6, 0:256], kn_ref, vm_ref)

    mg_ref[...] = _silu(_dot(hb, w_ref[:, 512:1024])).astype(BF16)

    dq = _dot(hb, w_ref[:, 1024:1536])
    dk = _dot(hb, w_ref[:, 1536:2048])
    for c in range(2):
        sl = slice(256 * c, 256 * (c + 1))
        qn = _rms_group(dq[:, sl], bd64, g_ref[6:7, 0:256])
        kn = _rms_group(dk[:, sl], bd64, g_ref[7:8, 0:256])
        if states:
            _store_transposed(st_dk_ref, kn, 256 * c)
        if rope:
            qn = _rope(qn, c64_ref[...], s64_ref[...], DIFF_HD)
            kn = _rope(kn, c64_ref[...], s64_ref[...], DIFF_HD)
        dq_ref[:, sl] = (qn * diff_scale).astype(BF16)
        dk_ref[:, sl] = kn.astype(BF16)
    dv = _dot(hb, w_ref[:, 2048:2560])
    if states:
        _store_heads4(st_dv_ref, dv)
    dv_ref[...] = dv.astype(BF16)
    dg_ref[...] = _silu(_dot(hb, w_ref[:, 2560:3072])).astype(BF16)


def _in_odd_kernel(*refs, rope, states, n_alias, row_base, tiles_per_batch):
    it = iter(refs)
    x_ref, mod_ref, g_ref, w_ref, bd64_ref = (next(it) for _ in range(5))
    if rope:
        c64_ref, s64_ref = (next(it) for _ in range(2))
    for _ in range(n_alias):
        next(it)
    nq_ref, nk_ref, nv_ref, ng_ref, gq_ref, gk_ref, gv_ref, gg_ref = (next(it) for _ in range(8))
    if states:
        st_nk_ref, st_nv_ref, st_gk_ref, st_gv_ref = (next(it) for _ in range(4))

    row = row_base + pl.program_id(0) // tiles_per_batch
    hb = _modulated_norm(x_ref, mod_ref, g_ref, row)
    bd64 = bd64_ref[...]
    na_scale = NA_HD ** -0.5 * LOG2E
    gqa_scale = GQA_HD ** -0.5 * LOG2E

    nq = _dot(hb, w_ref[:, 0:512])
    nk = _dot(hb, w_ref[:, 512:1024])
    for c in range(2):
        sl = slice(256 * c, 256 * (c + 1))
        nq_ref[:, sl] = (_rms_group(nq[:, sl], bd64, g_ref[1:2, 0:256]) * na_scale).astype(BF16)
        kn = _rms_group(nk[:, sl], bd64, g_ref[2:3, 0:256])
        if states:
            _store_transposed(st_nk_ref, kn, 256 * c)
        nk_ref[:, sl] = kn.astype(BF16)
    nv = _dot(hb, w_ref[:, 1024:1536])
    if states:
        for c in range(2):
            _store_transposed(st_nv_ref, nv[:, 256 * c:256 * (c + 1)], 256 * c)
    nv_ref[...] = nv.astype(BF16)
    ng_ref[...] = _silu(_dot(hb, w_ref[:, 1536:2048])).astype(BF16)

    gq = _dot(hb, w_ref[:, 2048:2560])
    for c in range(2):
        sl = slice(256 * c, 256 * (c + 1))
        qn = _rms_group(gq[:, sl], bd64, g_ref[3:4, 0:256])
        if rope:
            qn = _rope(qn, c64_ref[...], s64_ref[...], GQA_HD)
        gq_ref[:, sl] = (qn * gqa_scale).astype(BF16)
    gkv = _dot(hb, w_ref[:, 2560:2816])
    gk = _rms_group(gkv[:, :LANES], bd64[:LANES, :LANES], g_ref[4:5, 0:LANES])
    gv = gkv[:, LANES:]
    if states:
        _store_transposed(st_gk_ref, gk, 0)
        _store_transposed(st_gv_ref, gv, 0)
    if rope:
        gk = _rope(gk, c64_ref[...], s64_ref[...], GQA_HD)
    gk_ref[...] = gk.astype(BF16)
    gv_ref[...] = gv.astype(BF16)
    gg_ref[...] = _silu(_dot(hb, w_ref[:, 2816:3328])).astype(BF16)


def _full(shape):
    zeros = (0,) * len(shape)
    return pl.BlockSpec(shape, lambda *_: zeros)


class _Layer(NamedTuple):
    stack: jax.Array
    index: int


def _spec(a):
    if isinstance(a, _Layer):
        index = (a.index,) + (0,) * (a.stack.ndim - 1)
        return pl.BlockSpec((None,) + a.stack.shape[1:], lambda *_: index)
    return _full(a.shape)


def _arr(a):
    return a.stack if isinstance(a, _Layer) else a


def _in_proj(kernel, x, mods, layer, gains, weights, consts, ropes, out_widths, row_base, tokens_per_batch):
    rows = x.shape[0]
    tm = ROW_TILE
    tiles_per_batch = tokens_per_batch // tm
    in_specs = [pl.BlockSpec((tm, D_MODEL), lambda i: (i, 0)),
                pl.BlockSpec((1, 8, 3 * D_MODEL), lambda i: (layer, 0, 0)),
                _full(gains.shape)]
    in_specs += [_spec(a) for a in (*weights, *consts)]
    in_specs += [pl.BlockSpec((tm, LANES), lambda i: (i % tiles_per_batch, 0)) for _ in ropes]
    return pl.pallas_call(
        functools.partial(kernel, rope=True, states=False, n_alias=0, row_base=row_base,
                          tiles_per_batch=tiles_per_batch),
        out_shape=[jax.ShapeDtypeStruct((rows, w), BF16) for w in out_widths],
        grid=(rows // tm,),
        in_specs=in_specs,
        out_specs=[pl.BlockSpec((tm, w), lambda i: (i, 0)) for w in out_widths],
        compiler_params=_cparams(),
        name=kernel.__name__.strip("_"),
    )(x, mods, gains, *map(_arr, weights), *consts, *ropes)


def _mla_cache_kernel(c_ref, wkvb_ref, bd64_ref, g_ref, kn_ref, vm_ref):
    _mla_kv(c_ref[...], wkvb_ref, bd64_ref, g_ref[5:6, 0:256], kn_ref, vm_ref)


def _mla_cache_kv(ckv, wkvb, bd64, gains):
    rows = ckv.shape[0]
    return pl.pallas_call(
        _mla_cache_kernel,
        out_shape=[jax.ShapeDtypeStruct((rows, 512), BF16)] * 2,
        grid=(1,),
        in_specs=[_full(ckv.shape), _spec(wkvb), _full(bd64.shape), _full(gains.shape)],
        out_specs=[_full((rows, 512))] * 2,
        compiler_params=_cparams(),
        name="mla_cache_kv",
    )(ckv, _arr(wkvb), bd64, gains)


def _pair_attn_kernel(*refs, mode, kinds, n_pairs, mxu_denominator, lam_init):
    it = iter(refs)
    q_ref = next(it)
    segs = [[next(it) for _ in seg_kinds] for seg_kinds in kinds]
    gate_ref = next(it)
    if mode == "diff":
        lam_ref, subln_ref = next(it), next(it)
    o_ref = next(it)

    qw = 2 * LANES if mode == "mla" else LANES
    rows = q_ref.shape[0]
    lo = _lane_mask(rows, LANES, 0, LANES // 2)
    if mode == "diff":
        lv = lam_ref[...]
        lam = (jnp.exp(jnp.sum(lv[0:1] * lv[1:2], axis=-1, keepdims=True))
               - jnp.exp(jnp.sum(lv[2:3] * lv[3:4], axis=-1, keepdims=True)) + lam_init)

    def load(ref, kind, p):
        if kind == "rows4":
            x = ref[pl.ds(p, ref.shape[0] // 4, stride=4), :]
        elif mode == "gqa":
            x = ref[...]
        else:
            sl = slice(LANES * p, LANES * (p + 1))
            x = ref[sl, :] if kind == "cols" else ref[:, sl]
        return x.astype(BF16), kind == "cols"

    def values(p):
        return [load(seg[-1], seg_kinds[-1], p) for seg, seg_kinds in zip(segs, kinds)]

    def scores(p, j):
        q = q_ref[:, qw * p:qw * (p + 1)]
        ksegs = []
        for seg, seg_kinds in zip(segs, kinds):
            k, transposed = load(seg[0], seg_kinds[0], p)
            if mode == "mla":
                k = jnp.concatenate([k, seg[1][...].astype(BF16)], axis=-1)
            ksegs.append((k, transposed))
        keep = _lane_mask(rows, qw, 64 * j, 64 * (j + 1))
        if mode == "mla":
            keep = keep | _lane_mask(rows, qw, LANES + 32 * j, LANES + 32 * (j + 1))
        return _scores(jnp.where(keep, q, jnp.zeros_like(q)), ksegs)

    def finish(p, outs):
        sl = slice(LANES * p, LANES * (p + 1))
        if mode == "diff":
            d = outs[0] - lam * outs[1]
            o = _rms_full(d, subln_ref[...]) * (1.0 - lam_init)
        else:
            o = jnp.where(lo, outs[0], outs[1])
        o_ref[:, sl] = (o * gate_ref[:, sl].astype(F32)).astype(BF16)

    heads = [(p, j) for p in range(n_pairs) for j in range(2)]
    ss = scores(*heads[0])
    outs = []
    for t, (p, j) in enumerate(heads):
        ss_next = scores(*heads[t + 1]) if t + 1 < len(heads) else None
        spare = (64 * (1 - j), 64 * (2 - j)) if mxu_denominator and mode != "diff" else None
        outs.append(_softmax_pv(ss, values(p), ones_lanes=spare))
        ss = ss_next
        if j == 1:
            finish(p, outs)
            outs = []


def _pair_attention(mode, q, segs, gate, batch, tq, bq, n_pairs, mxu_denominator, extra=(), lam_init=0.0):
    qw = 2 * LANES if mode == "mla" else LANES
    nq = tq // bq
    steps_p = 4 // n_pairs
    in_specs = [pl.BlockSpec((bq, qw * n_pairs), lambda b, p, i: (b * nq + i, p))]
    args = [q]
    for seg in segs:
        for arr, spec, _ in seg:
            args.append(arr)
            in_specs.append(spec)
    kinds = tuple(tuple(kind for _, _, kind in seg) for seg in segs)
    in_specs.append(pl.BlockSpec((bq, LANES * n_pairs), lambda b, p, i: (b * nq + i, p)))
    args.append(gate)
    for arr in extra:
        args.append(arr)
        in_specs.append(pl.BlockSpec(arr.shape, lambda b, p, i: (0,) * arr.ndim))
    return pl.pallas_call(
        functools.partial(_pair_attn_kernel, mode=mode, kinds=kinds, n_pairs=n_pairs,
                          mxu_denominator=mxu_denominator, lam_init=lam_init),
        out_shape=jax.ShapeDtypeStruct((batch * tq, 4 * LANES), BF16),
        grid=(batch, steps_p, nq),
        in_specs=in_specs,
        out_specs=pl.BlockSpec((bq, LANES * n_pairs), lambda b, p, i: (b * nq + i, p)),
        compiler_params=_cparams(),
        name=mode + "_attention",
    )(*args)


def _self_seg(arr, tk, n_pairs, shared=False):
    if shared:
        return arr, pl.BlockSpec((tk, arr.shape[1]), lambda b, p, i: (b, 0)), "rows"
    return arr, pl.BlockSpec((tk, LANES * n_pairs), lambda b, p, i: (b, p)), "rows"


def _cache_seg(arr, layer, n_pairs, kind, shared=False):
    r, c = arr.shape[2:]
    if shared or kind == "rows4":
        return arr, pl.BlockSpec((None, None, r, c), lambda b, p, i: (b, layer, 0, 0)), kind
    if kind == "cols":
        return arr, pl.BlockSpec((None, None, LANES * n_pairs, c), lambda b, p, i: (b, layer, p, 0)), kind
    return arr, pl.BlockSpec((None, None, r, LANES * n_pairs), lambda b, p, i: (b, layer, 0, p)), kind


def _na_kernel(q_ref, k_ref, v_ref, ck_ref, cv_ref, bias_ref, gate_ref, o_ref):
    g = pl.program_id(1)
    n_groups = pl.num_programs(1)
    first_row = jnp.clip(NA_GROUP_ROWS * g - NA_ROWS // 2, 0, NA_GROUP_ROWS * n_groups - NA_WIN_ROWS)
    start = pl.multiple_of(first_row * GRID_W, GRID_W)
    win = NA_WIN_ROWS * GRID_W
    kwin = k_ref[pl.ds(start, win), :]
    vwin = v_ref[pl.ds(start, win), :]
    ck = ck_ref[...].astype(BF16)
    cv = cv_ref[...].astype(BF16)
    q = q_ref[...]
    rows = q.shape[0]
    lo = _lane_mask(rows, LANES, 0, LANES // 2)

    def scores(h):
        sl = slice(LANES * (h // 2), LANES * (h // 2 + 1))
        keep = _lane_mask(rows, LANES, 64 * (h % 2), 64 * (h % 2 + 1))
        qm = jnp.where(keep, q[:, sl], jnp.zeros_like(q[:, sl]))
        return _scores(qm, [(kwin[:, sl], False), (ck[sl, :], True)], bias0=bias_ref[h])

    ss = scores(0)
    outs = []
    for h in range(NA_HEADS):
        ss_next = scores(h + 1) if h + 1 < NA_HEADS else None
        sl = slice(LANES * (h // 2), LANES * (h // 2 + 1))
        j = h % 2
        outs.append(_softmax_pv(ss, [(vwin[:, sl], False), (cv[sl, :], True)],
                                ones_lanes=(64 * (1 - j), 64 * (2 - j))))
        if j == 1:
            o = jnp.where(lo, outs[0], outs[1])
            o_ref[:, sl] = (o * gate_ref[:, sl].astype(F32)).astype(BF16)
            outs = []
        ss = ss_next


def _na_attention(q, k, v, cache_k, cache_v, layer, bias, gate, batch, tq):
    bq = NA_GROUP_ROWS * GRID_W
    n_groups = tq // bq
    past = cache_k.shape[3]
    width = NA_HEADS * NA_HD

    def bias_map(b, g):
        return (layer, jnp.where(g == 0, 0, jnp.where(g == n_groups - 1, 2, 1)), 0, 0, 0)

    tok = pl.BlockSpec((bq, width), lambda b, g: (b * n_groups + g, 0))
    whole = pl.BlockSpec((tq, width), lambda b, g: (b, 0))
    cache = pl.BlockSpec((None, None, width, past), lambda b, g: (b, layer, 0, 0))
    return pl.pallas_call(
        _na_kernel,
        out_shape=jax.ShapeDtypeStruct((batch * tq, width), BF16),
        grid=(batch, n_groups),
        in_specs=[tok, whole, whole, cache, cache,
                  pl.BlockSpec((None, None, NA_HEADS, bq, NA_WIN_ROWS * GRID_W), bias_map), tok],
        out_specs=tok,
        compiler_params=_cparams(),
        name="na_attention",
    )(q, k, v, cache_k, cache_v, bias, gate)


def _na_bias_tables(rpb):
    qc = np.arange(GRID_W)[:, None]
    kc = np.arange(GRID_W)[None, :]
    cs = np.clip(qc - NA_COLS // 2, 0, GRID_W - NA_COLS)
    col_valid = (kc >= cs) & (kc < cs + NA_COLS)
    dc = kc - qc + NA_COLS - 1
    onehot = np.zeros((2 * NA_COLS - 1, GRID_W, GRID_W), np.float32)
    for d in range(2 * NA_COLS - 1):
        onehot[d] = (col_valid & (dc == d)).astype(np.float32)
    cols = jnp.einsum("lhrd,dqk->lhrqk", rpb.astype(F32), jnp.asarray(onehot), precision=lax.Precision.HIGHEST)
    cols = jnp.where(jnp.asarray(col_valid), cols * LOG2E, NEG)
    outside = jnp.full(cols.shape[:2] + (1, GRID_W, GRID_W), NEG, F32)
    cols = jnp.concatenate([cols, outside], axis=2)
    n_dr = 2 * NA_ROWS - 1
    idx = np.full((3, NA_GROUP_ROWS, NA_WIN_ROWS), n_dr, np.int32)
    for a in range(NA_GROUP_ROWS):
        for j in range(NA_WIN_ROWS):
            if j < NA_ROWS:
                idx[0, a, j] = j - a + NA_ROWS - 1
            if a <= j < a + NA_ROWS:
                idx[1, a, j] = j - a + NA_ROWS // 2 - 1
            if j >= NA_WIN_ROWS - NA_ROWS:
                idx[2, a, j] = j - a - (NA_WIN_ROWS - NA_ROWS) + NA_ROWS // 2 - 1
    layers = rpb.shape[0]
    return pl.pallas_call(
        functools.partial(_na_bias_kernel, idx=idx),
        out_shape=jax.ShapeDtypeStruct((layers, 3, NA_HEADS, NA_GROUP_ROWS * GRID_W, NA_WIN_ROWS * GRID_W), F32),
        grid=(layers, NA_HEADS),
        in_specs=[pl.BlockSpec((None, None, n_dr + 1, GRID_W, GRID_W), lambda l, h: (l, h, 0, 0, 0))],
        out_specs=pl.BlockSpec((None, 3, None, NA_GROUP_ROWS * GRID_W, NA_WIN_ROWS * GRID_W),
                               lambda l, h: (l, 0, h, 0, 0)),
        compiler_params=_cparams(),
        name="na_bias",
    )(cols)


def _na_bias_kernel(cols_ref, out_ref, *, idx):
    for t in range(3):
        for a in range(NA_GROUP_ROWS):
            for j in range(0, NA_WIN_ROWS, 2):
                pair = jnp.concatenate([cols_ref[int(idx[t, a, j])], cols_ref[int(idx[t, a, j + 1])]], axis=-1)
                out_ref[t, a * GRID_W:(a + 1) * GRID_W, j * GRID_W:(j + 2) * GRID_W] = pair


def _out_kernel(oa_ref, ob_ref, x_ref, mod_ref, w_ref, y_ref, *, row_base, tiles_per_batch):
    row = row_base + pl.program_id(0) // tiles_per_batch
    gate = mod_ref[0, pl.ds(row, 1), :][:, 2 * D_MODEL:]
    half = oa_ref.shape[1]
    acc = _dot(oa_ref[...], w_ref[:half, :]) + _dot(ob_ref[...], w_ref[half:, :])
    y_ref[...] = x_ref[...] + gate * acc


def _out_proj(oa, ob, x, mods, layer, w, row_base, tokens_per_batch):
    rows = x.shape[0]
    tm = ROW_TILE
    tiles_per_batch = tokens_per_batch // tm
    return pl.pallas_call(
        functools.partial(_out_kernel, row_base=row_base, tiles_per_batch=tiles_per_batch),
        out_shape=jax.ShapeDtypeStruct((rows, D_MODEL), F32),
        grid=(rows // tm,),
        in_specs=[pl.BlockSpec((tm, oa.shape[1]), lambda i: (i, 0)),
                  pl.BlockSpec((tm, ob.shape[1]), lambda i: (i, 0)),
                  pl.BlockSpec((tm, D_MODEL), lambda i: (i, 0)),
                  pl.BlockSpec((1, 8, 3 * D_MODEL), lambda i: (layer, 0, 0)),
                  _spec(w)],
        out_specs=pl.BlockSpec((tm, D_MODEL), lambda i: (i, 0)),
        compiler_params=_cparams(),
        name="out_proj",
    )(oa, ob, x, mods, _arr(w))


def _out_in_kernel(*refs, proj, rope, row_base, tiles_per_batch):
    oa_ref, ob_ref, x_ref, mod_prev_ref, wout_ref = refs[:5]
    n_proj_in = len(refs) - 5 - 1 - (9 if proj is _in_even_kernel else 8)
    proj_in = refs[5:5 + n_proj_in]
    y_ref = refs[5 + n_proj_in]
    slabs = refs[6 + n_proj_in:]
    _out_kernel(oa_ref, ob_ref, x_ref, mod_prev_ref, wout_ref, y_ref, row_base=row_base,
                tiles_per_batch=tiles_per_batch)
    proj(y_ref, *proj_in, *slabs, rope=rope, states=False, n_alias=0, row_base=row_base,
         tiles_per_batch=tiles_per_batch)


def _out_in_proj(proj, oa, ob, x, mods, layer, w_out, gains, weights, consts, ropes, out_widths, row_base,
                 tokens_per_batch):
    rows = x.shape[0]
    tm = ROW_TILE
    tiles_per_batch = tokens_per_batch // tm
    row = pl.BlockSpec((tm, D_MODEL), lambda i: (i, 0))
    in_specs = [pl.BlockSpec((tm, oa.shape[1]), lambda i: (i, 0)), pl.BlockSpec((tm, ob.shape[1]), lambda i: (i, 0)),
                row, pl.BlockSpec((1, 8, 3 * D_MODEL), lambda i: (layer - 1, 0, 0)), _spec(w_out),
                pl.BlockSpec((1, 8, 3 * D_MODEL), lambda i: (layer, 0, 0)), _full(gains.shape)]
    in_specs += [_spec(a) for a in (*weights, *consts)]
    in_specs += [pl.BlockSpec((tm, LANES), lambda i: (i % tiles_per_batch, 0)) for _ in ropes]
    return pl.pallas_call(
        functools.partial(_out_in_kernel, proj=proj, rope=True, row_base=row_base, tiles_per_batch=tiles_per_batch),
        out_shape=[jax.ShapeDtypeStruct((rows, D_MODEL), F32)]
        + [jax.ShapeDtypeStruct((rows, w), BF16) for w in out_widths],
        grid=(rows // tm,),
        in_specs=in_specs,
        out_specs=[row] + [pl.BlockSpec((tm, w), lambda i: (i, 0)) for w in out_widths],
        compiler_params=_cparams(),
        name="out_" + proj.__name__.strip("_"),
    )(oa, ob, x, mods, _arr(w_out), mods, gains, *map(_arr, weights), *consts, *ropes)


def _ctx_layer_kernel(*refs, even, n_in, n_alias, n_slabs, seq, lam_init, tiles_per_batch):
    in_refs = refs[:n_in]
    it = iter(refs[n_in:])
    wout_ref = next(it)
    if even:
        lam_ref, subln_ref = next(it), next(it)
    alias_refs = [next(it) for _ in range(n_alias)]
    y_ref = next(it)
    state_refs = [next(it) for _ in range(4)]
    slabs = [next(it) for _ in range(n_slabs)]
    oa_ref, ob_ref = next(it), next(it)
    x_ref, mod_ref = in_refs[0], in_refs[1]

    proj = _in_even_kernel if even else _in_odd_kernel
    proj(*in_refs, *alias_refs, *slabs, *state_refs, rope=False, states=True, n_alias=n_alias, row_base=0,
         tiles_per_batch=tiles_per_batch)
    for b in range(x_ref.shape[0] // seq):
        def own(ref):
            return ref.at[pl.ds(b * seq, seq), :]
        common = dict(n_pairs=4, mxu_denominator=False)
        if even:
            qcat, kn, kpe, vm, mg, dq, dk, dv, dg = (own(r) for r in slabs)
            _pair_attn_kernel(qcat, kn, kpe, vm, mg, own(oa_ref), mode="mla", kinds=(("rows",) * 3,),
                              lam_init=0.0, **common)
            _pair_attn_kernel(dq, dk, dv, dg, lam_ref, subln_ref, own(ob_ref), mode="diff", kinds=(("rows",) * 2,),
                              lam_init=lam_init, **common)
        else:
            nq, nk, nv, ng, gq, gk, gv, gg = (own(r) for r in slabs)
            _pair_attn_kernel(nq, nk, nv, ng, own(oa_ref), mode="mha", kinds=(("rows",) * 2,), lam_init=0.0, **common)
            _pair_attn_kernel(gq, gk, gv, gg, own(ob_ref), mode="gqa", kinds=(("rows",) * 2,), lam_init=0.0, **common)
    _out_kernel(oa_ref, ob_ref, x_ref, mod_ref, wout_ref, y_ref, row_base=0, tiles_per_batch=tiles_per_batch)


def _ctx_layer(even, x, mods, layer, gains, weights, consts, w_out, extras, slab_widths, state_tails, state_prev,
               seq, lam_init=0.0):
    rows = x.shape[0]
    tm = ROW_TILE
    slot = layer // 2
    in_specs = [pl.BlockSpec((tm, D_MODEL), lambda i: (i, 0)),
                pl.BlockSpec((1, 8, 3 * D_MODEL), lambda i: (layer, 0, 0)),
                _full(gains.shape)]
    in_specs += [_spec(a) for a in (*weights, *consts)]
    args = [x, mods, gains, *map(_arr, weights), *consts]
    n_in = len(args)
    in_specs += [_spec(a) for a in (w_out, *extras)]
    args += [_arr(w_out), *extras]
    aliases = {}
    if state_prev is not None:
        aliases = {len(args) + j: 1 + j for j in range(len(state_prev))}
        in_specs += [pl.BlockSpec(memory_space=pl.ANY) for _ in state_prev]
        args += list(state_prev)
    out_shape = [jax.ShapeDtypeStruct((rows, D_MODEL), F32)]
    out_specs = [pl.BlockSpec((tm, D_MODEL), lambda i: (i, 0))]
    out_shape += [jax.ShapeDtypeStruct((rows // seq, DEPTH // 2) + tail, F32) for tail in state_tails]
    out_specs += [pl.BlockSpec((tm // seq, None) + tail, lambda i: (i, slot, 0, 0)) for tail in state_tails]
    scratch = [pltpu.VMEM((tm, w), BF16) for w in (*slab_widths, 4 * LANES, 4 * LANES)]
    return pl.pallas_call(
        functools.partial(_ctx_layer_kernel, even=even, n_in=n_in, n_alias=len(aliases), n_slabs=len(slab_widths),
                          seq=seq, lam_init=lam_init, tiles_per_batch=rows // tm),
        out_shape=out_shape,
        grid=(rows // tm,),
        in_specs=in_specs,
        out_specs=out_specs,
        scratch_shapes=scratch,
        input_output_aliases=aliases,
        compiler_params=_cparams(),
        name="ctx_layer_even" if even else "ctx_layer_odd",
    )(*args)


def _lanes_from(lo, hi, shift):
    lane = lax.broadcasted_iota(jnp.int32, lo.shape, 1)
    return jnp.where(lane < LANES - shift, pltpu.roll(lo, LANES - shift, 1), pltpu.roll(hi, LANES - shift, 1))


def _layout_in_even_kernel(w_ref, o_ref):
    o_ref[:, 0:384] = w_ref[:, 0:384].astype(BF16)
    a = w_ref[:, 384:512]
    lane = lax.broadcasted_iota(jnp.int32, a.shape, 1)
    copies = jnp.where(lane < 32, a, jnp.where(lane < 64, pltpu.roll(a, 32, 1),
                                               jnp.where(lane < 96, pltpu.roll(a, 64, 1), pltpu.roll(a, 96, 1))))
    o_ref[:, 384:512] = copies.astype(BF16)
    for c in range((o_ref.shape[1] - 512) // LANES):
        src = 384 + LANES * c
        blk = _lanes_from(w_ref[:, src:src + LANES], w_ref[:, src + LANES:src + 2 * LANES], MLA_ROPE)
        o_ref[:, 512 + LANES * c:512 + LANES * (c + 1)] = blk.astype(BF16)


def _layout_in_odd_kernel(w_ref, o_ref):
    def permuted(base):
        for k in range(GQA_HEADS // 2):
            lo_head, hi_head = GQA_PERM[2 * k], GQA_PERM[2 * k + 1]
            lo = w_ref[:, base + LANES * (lo_head // 2):base + LANES * (lo_head // 2 + 1)]
            hi = w_ref[:, base + LANES * (hi_head // 2):base + LANES * (hi_head // 2 + 1)]
            if lo_head % 2 == 1:
                lo = pltpu.roll(lo, LANES // 2, 1)
            if hi_head % 2 == 0:
                hi = pltpu.roll(hi, LANES // 2, 1)
            lane = lax.broadcasted_iota(jnp.int32, lo.shape, 1)
            o_ref[:, base + LANES * k:base + LANES * (k + 1)] = jnp.where(lane < LANES // 2, lo, hi).astype(BF16)

    o_ref[:, 0:2048] = w_ref[:, 0:2048].astype(BF16)
    permuted(2048)
    o_ref[:, 2560:2816] = w_ref[:, 2560:2816].astype(BF16)
    permuted(2816)


def _layout_w_out_kernel(w_ref, o_ref, *, permute):
    half = w_ref.shape[0] // 2
    o_ref[:half, :] = w_ref[:half, :].astype(BF16)
    if permute:
        for k, head in enumerate(GQA_PERM):
            o_ref[half + GQA_HD * k:half + GQA_HD * (k + 1), :] = (
                w_ref[half + GQA_HD * head:half + GQA_HD * (head + 1), :].astype(BF16))
    else:
        o_ref[half:, :] = w_ref[half:, :].astype(BF16)


def _layout_weights(kernel, w, out_cols, name):
    layers, rows, _ = w.shape
    rb = 256
    return pl.pallas_call(
        kernel,
        out_shape=jax.ShapeDtypeStruct((layers, rows, out_cols), BF16),
        grid=(layers, rows // rb),
        in_specs=[pl.BlockSpec((None, rb, out_cols), lambda l, r: (l, r, 0))],
        out_specs=pl.BlockSpec((None, rb, out_cols), lambda l, r: (l, r, 0)),
        compiler_params=_cparams(),
        name=name,
    )(w)


def _layout_w_out(w, permute):
    layers, rows, cols = w.shape
    return pl.pallas_call(
        functools.partial(_layout_w_out_kernel, permute=permute),
        out_shape=jax.ShapeDtypeStruct((layers, rows, cols), BF16),
        grid=(layers,),
        in_specs=[pl.BlockSpec((None, rows, cols), lambda l: (l, 0, 0))],
        out_specs=pl.BlockSpec((None, rows, cols), lambda l: (l, 0, 0)),
        compiler_params=_cparams(),
        name="layout_w_out",
    )(w)


def _block_diag(width, group):
    idx = np.arange(width) // group
    return jnp.asarray((idx[:, None] == idx[None, :]).astype(np.float32) / group, BF16)


def _rope_tables(t, rot_dim):
    pos = np.arange(t)
    row = (pos // GRID_W).astype(np.float64)
    col = (pos % GRID_W).astype(np.float64)
    n = rot_dim // 2
    inv = ROPE_THETA ** (-np.arange(0, n, 2, dtype=np.float64) / n)
    ang = np.concatenate([row[:, None] * inv, col[:, None] * inv], axis=-1)
    cos = np.concatenate([np.cos(ang), np.cos(ang)], axis=-1)
    sin = np.concatenate([-np.sin(ang), np.sin(ang)], axis=-1)
    reps = LANES // rot_dim
    return (jnp.asarray(np.tile(cos, (1, reps)), F32), jnp.asarray(np.tile(sin, (1, reps)), F32))


def _pad_row(v, width=D_MODEL):
    return jnp.pad(v, (0, width - v.shape[0]))


def _tile_row(v, reps):
    return _pad_row(jnp.tile(v, reps))


def kernel(x_prompt, x_sample, cache_mla_ckv, cache_mla_kpe, cache_diff_k, cache_diff_v, cache_na_k, cache_na_v, cache_gqa_k, cache_gqa_v, c, c_ctx, norm_w, w_mod, b_mod, w_in_even, w_out_even, mla_qa_norm, mla_wqb, mla_kva_norm, mla_wkvb, mla_qn_nope, mla_qn_rope, mla_kn_nope, mla_kn_rope, diff_qn, diff_kn, diff_lq1, diff_lk1, diff_lq2, diff_lk2, diff_subln, w_in_odd, w_out_odd, na_qn, na_kn, na_rpb, gqa_qn, gqa_kn):
    batch, seq, _ = x_prompt.shape
    dec_batch, dec_seq, _ = x_sample.shape
    past = cache_mla_ckv.shape[2]
    n_even, n_odd = w_in_even.shape[0], w_in_odd.shape[0]

    def layers(stack):
        return [_Layer(stack, i) for i in range(stack.shape[0])]

    w_in_e = layers(_layout_weights(_layout_in_even_kernel, w_in_even, 3072, "layout_in_even"))
    w_in_o = layers(_layout_weights(_layout_in_odd_kernel, w_in_odd, w_in_odd.shape[2], "layout_in_odd"))
    w_out_e = layers(_layout_w_out(w_out_even, permute=False))
    w_out_o = layers(_layout_w_out(w_out_odd, permute=True))
    wqb = mla_wqb.reshape(n_even, Q_LORA, MLA_HEADS, MLA_QK)
    wqb = jnp.concatenate([wqb[..., :MLA_NOPE].reshape(n_even, Q_LORA, 4, 2 * MLA_NOPE),
                           wqb[..., MLA_NOPE:].reshape(n_even, Q_LORA, 4, 2 * MLA_ROPE),
                           jnp.zeros((n_even, Q_LORA, 4, LANES - 2 * MLA_ROPE), F32)], axis=-1)
    wqb = layers(wqb.reshape(n_even, Q_LORA, 4 * 2 * LANES).astype(BF16))
    wkvb = mla_wkvb.reshape(n_even, KV_LORA, MLA_HEADS, 2 * MLA_NOPE)
    wkvb = layers(jnp.concatenate([wkvb[..., :MLA_NOPE].reshape(n_even, KV_LORA, 512),
                                   wkvb[..., MLA_NOPE:].reshape(n_even, KV_LORA, 512)], axis=-1).astype(BF16))

    gains_e = [jnp.stack([norm_w[2 * i], _pad_row(mla_qa_norm[i]),
                          _pad_row(jnp.concatenate([jnp.tile(mla_qn_nope[i], 2), jnp.tile(mla_qn_rope[i], 4)])),
                          _pad_row(mla_kva_norm[i]), _tile_row(mla_kn_rope[i], 4), _tile_row(mla_kn_nope[i], 4),
                          _tile_row(diff_qn[i], 4), _tile_row(diff_kn[i], 4)]) for i in range(n_even)]
    gains_o = [jnp.stack([norm_w[2 * i + 1], _tile_row(na_qn[i], 4), _tile_row(na_kn[i], 4),
                          _tile_row(gqa_qn[i], 4), _tile_row(gqa_kn[i], 2),
                          jnp.zeros((D_MODEL,), F32), jnp.zeros((D_MODEL,), F32), jnp.zeros((D_MODEL,), F32)])
               for i in range(n_odd)]
    lam_vecs = [jnp.stack([diff_lq1[i], diff_lk1[i], diff_lq2[i], diff_lk2[i]]) for i in range(n_even)]

    bd64 = _block_diag(256, 64)
    bd32 = _block_diag(LANES, 32)
    bdq = jnp.concatenate([jnp.concatenate([_block_diag(LANES, 64), jnp.zeros((LANES, LANES), BF16)], axis=1),
                           jnp.concatenate([jnp.zeros((LANES, LANES), BF16), _block_diag(LANES, 32)], axis=1)], axis=0)
    c64, s64 = _rope_tables(dec_seq, 64)
    c32, s32 = _rope_tables(dec_seq, MLA_ROPE)
    na_bias = _na_bias_tables(na_rpb)

    cvecs = jnp.concatenate([c_ctx[None, :], c, jnp.zeros((8 - 1 - dec_batch, D_MODEL), F32)], axis=0)
    mods = _modulation(cvecs, w_mod, b_mod)

    cache_kpe = jnp.tile(cache_mla_kpe, (1, 1, 1, 4))
    cache_dk = cache_diff_k.transpose(0, 1, 3, 4, 5, 2).reshape(dec_batch, n_even, 512, past)
    cache_dv = cache_diff_v.reshape(dec_batch, n_even, 4 * past, LANES)
    cache_nk = cache_na_k.transpose(0, 1, 3, 4, 2).reshape(dec_batch, n_odd, 512, past)
    cache_nv = cache_na_v.transpose(0, 1, 3, 4, 2).reshape(dec_batch, n_odd, 512, past)
    cache_gk = cache_gqa_k.transpose(0, 1, 3, 4, 2).reshape(dec_batch, n_odd, LANES, past)
    cache_gv = cache_gqa_v.transpose(0, 1, 3, 4, 2).reshape(dec_batch, n_odd, LANES, past)

    even_widths = (1024, 512, LANES, 512, 512, 512, 512, 512, 512)
    odd_widths = (512, 512, 512, 512, 512, LANES, LANES, 512)
    even_states = ((seq, KV_LORA), (MLA_ROPE, seq), (512, seq), (4 * seq, LANES))
    odd_states = ((512, seq), (512, seq), (LANES, seq), (LANES, seq))

    def lam_init(l):
        return 0.8 - 0.6 * math.exp(-0.3 * l)

    def context_pass(x):
        states = [None, None]
        for l in range(DEPTH):
            i = l // 2
            if l % 2 == 0:
                x, *states[0] = _ctx_layer(True, x, mods, l, gains_e[i], (w_in_e[i], wqb[i], wkvb[i]),
                                           (bd64, bd32, bdq), w_out_e[i], (lam_vecs[i], diff_subln[i][None, :]),
                                           even_widths, even_states, states[0], seq, lam_init(l))
            else:
                x, *states[1] = _ctx_layer(False, x, mods, l, gains_o[i], (w_in_o[i],), (bd64,), w_out_o[i], (),
                                           odd_widths, odd_states, states[1], seq)
        return x, states

    def latent_pass(x, nb, t):
        row_base = 1
        bq = Q_BLOCK
        n_pairs = 4
        oa = ob = w_out_prev = None
        for l in range(DEPTH):
            i = l // 2
            if l % 2 == 0:
                proj_args = (gains_e[i], (w_in_e[i], wqb[i], wkvb[i]), (bd64, bd32, bdq), (c64, s64, c32, s32),
                             even_widths)
                proj = _in_even_kernel
            else:
                proj_args = (gains_o[i], (w_in_o[i],), (bd64,), (c64, s64), odd_widths)
                proj = _in_odd_kernel
            if l == 0:
                outs = _in_proj(proj, x, mods, l, *proj_args, row_base, t)
            else:
                x, *outs = _out_in_proj(proj, oa, ob, x, mods, l, w_out_prev, *proj_args, row_base, t)
            if l % 2 == 0:
                qcat, kn, kpe, vm, mg, dq, dk, dv, dg = outs
                kn_c, vm_c = _mla_cache_kv(cache_mla_ckv[:, i].reshape(nb * past, KV_LORA), wkvb[i], bd64, gains_e[i])
                mla_segs = [(_self_seg(kn, t, n_pairs), _self_seg(kpe, t, n_pairs, shared=True),
                             _self_seg(vm, t, n_pairs)),
                            (_self_seg(kn_c, past, n_pairs), _cache_seg(cache_kpe, i, n_pairs, "rows", shared=True),
                             _self_seg(vm_c, past, n_pairs))]
                diff_segs = [(_self_seg(dk, t, n_pairs), _self_seg(dv, t, n_pairs)),
                             (_cache_seg(cache_dk, i, n_pairs, "cols"), _cache_seg(cache_dv, i, n_pairs, "rows4"))]
                oa = _pair_attention("mla", qcat, mla_segs, mg, nb, t, bq, n_pairs, True)
                ob = _pair_attention("diff", dq, diff_segs, dg, nb, t, bq, n_pairs, True,
                                     extra=(lam_vecs[i], diff_subln[i][None, :]), lam_init=lam_init(l))
                w_out_prev = w_out_e[i]
            else:
                nq, nk, nv, ng, gq, gk, gv, gg = outs
                oa = _na_attention(nq, nk, nv, cache_nk, cache_nv, i, na_bias, ng, nb, t)
                gqa_segs = [(_self_seg(gk, t, n_pairs, shared=True), _self_seg(gv, t, n_pairs, shared=True)),
                            (_cache_seg(cache_gk, i, n_pairs, "cols", shared=True),
                             _cache_seg(cache_gv, i, n_pairs, "cols", shared=True))]
                ob = _pair_attention("gqa", gq, gqa_segs, gg, nb, t, bq, n_pairs, True)
                w_out_prev = w_out_o[i]
        return _out_proj(oa, ob, x, mods, DEPTH - 1, w_out_prev, row_base, t)

    y_prompt, st = context_pass(x_prompt.reshape(batch * seq, D_MODEL))
    y_sample = latent_pass(x_sample.reshape(dec_batch * dec_seq, D_MODEL), dec_batch, dec_seq)

    def token_major(a, heads):
        a = a.reshape((batch, n_even) + heads + (a.shape[2] // math.prod(heads), seq))
        return jnp.moveaxis(a, -1, 2)

    ckv, kpe_t, dk_t, dv4 = st[0]
    nk_t, nv_t, gk_t, gv_t = st[1]
    return (y_prompt.reshape(batch, seq, D_MODEL), y_sample.reshape(dec_batch, dec_seq, D_MODEL),
            ckv, token_major(kpe_t, ()), token_major(dk_t, (DIFF_HEADS, 2)),
            dv4.reshape(batch, n_even, seq, DIFF_HEADS, 2 * DIFF_HD),
            token_major(nk_t, (NA_HEADS,)), token_major(nv_t, (NA_HEADS,)),
            token_major(gk_t, (GQA_KV,)), token_major(gv_t, (GQA_KV,)))
```

```python
import functools
import math
from typing import NamedTuple

import jax
import jax.numpy as jnp
import numpy as np
from jax import lax
from jax.experimental import pallas as pl
from jax.experimental.pallas import tpu as pltpu

F32 = jnp.float32
BF16 = jnp.bfloat16

D_MODEL = 1024
DEPTH = 4
GRID_W = 64
ROPE_THETA = 10000.0
EPS = 1e-6
MLA_HEADS = 8
MLA_NOPE = 64
MLA_ROPE = 32
MLA_QK = MLA_NOPE + MLA_ROPE
Q_LORA = 256
KV_LORA = 128
DIFF_HEADS = 4
DIFF_HD = 64
NA_HEADS = 8
NA_HD = 64
NA_ROWS = 8
NA_COLS = 16
GQA_HEADS = 8
GQA_KV = 2
GQA_HD = 64
LANES = 128
NA_GROUP_ROWS = 4
NA_WIN_ROWS = NA_ROWS + NA_GROUP_ROWS
NEG = -1e30
LOG2E = math.log2(math.e)
VMEM_LIMIT = 48 * 1024 * 1024
GQA_PERM = (0, 4, 1, 5, 2, 6, 3, 7)

ROW_TILE = 512
Q_BLOCK = 512


def _cparams():
    return pltpu.CompilerParams(vmem_limit_bytes=VMEM_LIMIT)


def _dot(a, b):
    return jnp.dot(a, b, preferred_element_type=F32)


def _dot_nt(a, b):
    return lax.dot_general(a, b, (((1,), (1,)), ((), ())), preferred_element_type=F32)


def _rms_full(x, g):
    ms = jnp.mean(x * x, axis=-1, keepdims=True)
    return x * lax.rsqrt(ms + EPS) * g


def _rms_group(x, bd, g):
    ms = _dot((x * x).astype(BF16), bd)
    return x * lax.rsqrt(ms + EPS) * g


def _silu(u):
    return u * (1.0 / (1.0 + jnp.exp(-u)))


def _rope(x, cos, sin, group):
    half = group // 2
    rows, width = x.shape
    lane = lax.broadcasted_iota(jnp.int32, (rows, LANES), 1)
    first = (lane & (group - 1)) < half
    outs = []
    for c in range(width // LANES):
        xc = x[:, c * LANES:(c + 1) * LANES]
        rot = jnp.where(first, pltpu.roll(xc, LANES - half, 1), pltpu.roll(xc, half, 1))
        outs.append(xc * cos + rot * sin)
    return outs[0] if len(outs) == 1 else jnp.concatenate(outs, axis=-1)


def _lane_mask(rows, width, lo, hi):
    lane = lax.broadcasted_iota(jnp.int32, (rows, width), 1)
    return (lane >= lo) & (lane < hi)


def _scores(qm, ksegs, bias0=None):
    ss = [_dot(qm, k) if transposed else _dot_nt(qm, k) for k, transposed in ksegs]
    if bias0 is not None:
        ss[0] = ss[0] + bias0
    return ss


def _softmax_pv(ss, vsegs, ones_lanes=None):
    m = jnp.max(ss[0], axis=-1, keepdims=True)
    for s in ss[1:]:
        m = jnp.maximum(m, jnp.max(s, axis=-1, keepdims=True))
    acc = None
    l = None
    for s, (v, transposed) in zip(ss, vsegs):
        p = jnp.exp2(s - m)
        if ones_lanes is None:
            ps = jnp.sum(p, axis=-1, keepdims=True)
            l = ps if l is None else l + ps
        else:
            width = lax.broadcasted_iota(jnp.int32, v.shape, 0 if transposed else 1)
            v = jnp.where((width >= ones_lanes[0]) & (width < ones_lanes[1]), jnp.ones_like(v), v)
        a = _dot_nt(p.astype(BF16), v) if transposed else _dot(p.astype(BF16), v)
        acc = a if acc is None else acc + a
    if ones_lanes is not None:
        l = pltpu.roll(acc, LANES // 2, 1)
    return acc / l


def _mod_kernel(c_ref, w_ref, b_ref, o_ref):
    c = c_ref[...]
    o_ref[0] = _dot(_silu(c).astype(BF16), w_ref[0].astype(BF16)) + b_ref[0]


def _modulation(cvecs, w_mod, b_mod):
    tn = 768
    return pl.pallas_call(
        _mod_kernel,
        out_shape=jax.ShapeDtypeStruct((DEPTH, 8, 3 * D_MODEL), F32),
        grid=(DEPTH, 3 * D_MODEL // tn),
        in_specs=[pl.BlockSpec((8, D_MODEL), lambda l, n: (0, 0)),
                  pl.BlockSpec((1, D_MODEL, tn), lambda l, n: (l, 0, n)),
                  pl.BlockSpec((1, 1, tn), lambda l, n: (l, 0, n))],
        out_specs=pl.BlockSpec((1, 8, tn), lambda l, n: (l, 0, n)),
        compiler_params=_cparams(),
        name="modulation",
    )(cvecs, w_mod, b_mod.reshape(DEPTH, 1, 3 * D_MODEL))


def _modulated_norm(x_ref, mod_ref, g_ref, row):
    mod = mod_ref[0, pl.ds(row, 1), :]
    shift = mod[:, :D_MODEL]
    scale = mod[:, D_MODEL:2 * D_MODEL]
    h = _rms_full(x_ref[...], g_ref[0:1, :]) * (1.0 + scale) + shift
    return h.astype(BF16)


def _mla_kv(cn, wkvb_ref, bd64_ref, g_kn, kn_ref, vm_ref):
    kv = _dot(cn.astype(BF16), wkvb_ref[...])
    for c in range(2):
        sl = slice(256 * c, 256 * (c + 1))
        kn_ref[:, sl] = _rms_group(kv[:, sl], bd64_ref[...], g_kn).astype(BF16)
    vm_ref[...] = kv[:, 512:].astype(BF16)


def _store_rows(ref, x):
    seq = ref.shape[1]
    for b in range(ref.shape[0]):
        ref[b] = x[b * seq:(b + 1) * seq]


def _store_transposed(ref, x, row0, keep=None):
    seq = ref.shape[2]
    for b in range(ref.shape[0]):
        xt = x[b * seq:(b + 1) * seq].T
        if keep is not None:
            xt = xt[:keep]
        ref[b, row0:row0 + xt.shape[0], :] = xt


def _store_heads4(ref, x):
    seq = ref.shape[1] // 4
    for b in range(ref.shape[0]):
        for h in range(4):
            ref[b, pl.ds(h, seq, stride=4), :] = x[b * seq:(b + 1) * seq, LANES * h:LANES * (h + 1)]


def _in_even_kernel(*refs, rope, states, n_alias, row_base, tiles_per_batch):
    it = iter(refs)
    x_ref, mod_ref, g_ref, w_ref, wqb_ref, wkvb_ref, bd64_ref, bd32_ref, bdq_ref = (next(it) for _ in range(9))
    if rope:
        c64_ref, s64_ref, c32_ref, s32_ref = (next(it) for _ in range(4))
    for _ in range(n_alias):
        next(it)
    qcat_ref, kn_ref, kpe_ref, vm_ref, mg_ref, dq_ref, dk_ref, dv_ref, dg_ref = (next(it) for _ in range(9))
    if states:
        st_ckv_ref, st_kpe_ref, st_dk_ref, st_dv_ref = (next(it) for _ in range(4))

    row = row_base + pl.program_id(0) // tiles_per_batch
    hb = _modulated_norm(x_ref, mod_ref, g_ref, row)
    bd64 = bd64_ref[...]
    mla_scale = MLA_QK ** -0.5 * LOG2E
    diff_scale = DIFF_HD ** -0.5 * LOG2E

    qa = _dot(hb, w_ref[:, 0:256])
    qa_n = _rms_full(qa, g_ref[1:2, 0:256]).astype(BF16)
    q = _dot(qa_n, wqb_ref[...])
    for p in range(4):
        qp = _rms_group(q[:, 256 * p:256 * (p + 1)], bdq_ref[...], g_ref[2:3, 0:256])
        q_nope = qp[:, :LANES]
        q_pe = qp[:, LANES:]
        if rope:
            q_pe = _rope(q_pe, c32_ref[...], s32_ref[...], MLA_ROPE)
        qcat_ref[:, 256 * p:256 * p + LANES] = (q_nope * mla_scale).astype(BF16)
        qcat_ref[:, 256 * p + LANES:256 * (p + 1)] = (q_pe * mla_scale).astype(BF16)

    kva = _dot(hb, w_ref[:, 256:512])
    c_kv = _rms_full(kva[:, :LANES], g_ref[3:4, 0:LANES])
    k_pe = _rms_group(kva[:, LANES:], bd32_ref[...], g_ref[4:5, 0:LANES])
    if states:
        _store_rows(st_ckv_ref, c_kv)
        _store_transposed(st_kpe_ref, k_pe, 0, keep=MLA_ROPE)
    if rope:
        k_pe = _rope(k_pe, c32_ref[...], s32_ref[...], MLA_ROPE)
    kpe_ref[...] = k_pe.astype(BF16)
    _mla_kv(c_kv, wkvb_ref, bd64_ref, g_ref[5:6, 0:256], kn_ref, vm_ref)

    mg_ref[...] = _silu(_dot(hb, w_ref[:, 512:1024])).astype(BF16)

    dq = _dot(hb, w_ref[:, 1024:1536])
    dk = _dot(hb, w_ref[:, 1536:2048])
    for c in range(2):
        sl = slice(256 * c, 256 * (c + 1))
        qn = _rms_group(dq[:, sl], bd64, g_ref[6:7, 0:256])
        kn = _rms_group(dk[:, sl], bd64, g_ref[7:8, 0:256])
        if states:
            _store_transposed(st_dk_ref, kn, 256 * c)
        if rope:
            qn = _rope(qn, c64_ref[...], s64_ref[...], DIFF_HD)
            kn = _rope(kn, c64_ref[...], s64_ref[...], DIFF_HD)
        dq_ref[:, sl] = (qn * diff_scale).astype(BF16)
        dk_ref[:, sl] = kn.astype(BF16)
    dv = _dot(hb, w_ref[:, 2048:2560])
    if states:
        _store_heads4(st_dv_ref, dv)
    dv_ref[...] = dv.astype(BF16)
    dg_ref[...] = _silu(_dot(hb, w_ref[:, 2560:3072])).astype(BF16)


def _in_odd_kernel(*refs, rope, states, n_alias, row_base, tiles_per_batch):
    it = iter(refs)
    x_ref, mod_ref, g_ref, w_ref, bd64_ref = (next(it) for _ in range(5))
    if rope:
        c64_ref, s64_ref = (next(it) for _ in range(2))
    for _ in range(n_alias):
        next(it)
    nq_ref, nk_ref, nv_ref, ng_ref, gq_ref, gk_ref, gv_ref, gg_ref = (next(it) for _ in range(8))
    if states:
        st_nk_ref, st_nv_ref, st_gk_ref, st_gv_ref = (next(it) for _ in range(4))

    row = row_base + pl.program_id(0) // tiles_per_batch
    hb = _modulated_norm(x_ref, mod_ref, g_ref, row)
    bd64 = bd64_ref[...]
    na_scale = NA_HD ** -0.5 * LOG2E
    gqa_scale = GQA_HD ** -0.5 * LOG2E

    nq = _dot(hb, w_ref[:, 0:512])
    nk = _dot(hb, w_ref[:, 512:1024])
    for c in range(2):
        sl = slice(256 * c, 256 * (c + 1))
        nq_ref[:, sl] = (_rms_group(nq[:, sl], bd64, g_ref[1:2, 0:256]) * na_scale).astype(BF16)
        kn = _rms_group(nk[:, sl], bd64, g_ref[2:3, 0:256])
        if states:
            _store_transposed(st_nk_ref, kn, 256 * c)
        nk_ref[:, sl] = kn.astype(BF16)
    nv = _dot(hb, w_ref[:, 1024:1536])
    if states:
        for c in range(2):
            _store_transposed(st_nv_ref, nv[:, 256 * c:256 * (c + 1)], 256 * c)
    nv_ref[...] = nv.astype(BF16)
    ng_ref[...] = _silu(_dot(hb, w_ref[:, 1536:2048])).astype(BF16)

    gq = _dot(hb, w_ref[:, 2048:2560])
    for c in range(2):
        sl = slice(256 * c, 256 * (c + 1))
        qn = _rms_group(gq[:, sl], bd64, g_ref[3:4, 0:256])
        if rope:
            qn = _rope(qn, c64_ref[...], s64_ref[...], GQA_HD)
        gq_ref[:, sl] = (qn * gqa_scale).astype(BF16)
    gkv = _dot(hb, w_ref[:, 2560:2816])
    gk = _rms_group(gkv[:, :LANES], bd64[:LANES, :LANES], g_ref[4:5, 0:LANES])
    gv = gkv[:, LANES:]
    if states:
        _store_transposed(st_gk_ref, gk, 0)
        _store_transposed(st_gv_ref, gv, 0)
    if rope:
        gk = _rope(gk, c64_ref[...], s64_ref[...], GQA_HD)
    gk_ref[...] = gk.astype(BF16)
    gv_ref[...] = gv.astype(BF16)
    gg_ref[...] = _silu(_dot(hb, w_ref[:, 2816:3328])).astype(BF16)


def _full(shape):
    zeros = (0,) * len(shape)
    return pl.BlockSpec(shape, lambda *_: zeros)


class _Layer(NamedTuple):
    stack: jax.Array
    index: int


def _spec(a):
    if isinstance(a, _Layer):
        index = (a.index,) + (0,) * (a.stack.ndim - 1)
        return pl.BlockSpec((None,) + a.stack.shape[1:], lambda *_: index)
    return _full(a.shape)


def _arr(a):
    return a.stack if isinstance(a, _Layer) else a


def _in_proj(kernel, x, mods, layer, gains, weights, consts, ropes, out_widths, row_base, tokens_per_batch):
    rows = x.shape[0]
    tm = ROW_TILE
    tiles_per_batch = tokens_per_batch // tm
    in_specs = [pl.BlockSpec((tm, D_MODEL), lambda i: (i, 0)),
                pl.BlockSpec((1, 8, 3 * D_MODEL), lambda i: (layer, 0, 0)),
                _full(gains.shape)]
    in_specs += [_spec(a) for a in (*weights, *consts)]
    in_specs += [pl.BlockSpec((tm, LANES), lambda i: (i % tiles_per_batch, 0)) for _ in ropes]
    return pl.pallas_call(
        functools.partial(kernel, rope=True, states=False, n_alias=0, row_base=row_base,
                          tiles_per_batch=tiles_per_batch),
        out_shape=[jax.ShapeDtypeStruct((rows, w), BF16) for w in out_widths],
        grid=(rows // tm,),
        in_specs=in_specs,
        out_specs=[pl.BlockSpec((tm, w), lambda i: (i, 0)) for w in out_widths],
        compiler_params=_cparams(),
        name=kernel.__name__.strip("_"),
    )(x, mods, gains, *map(_arr, weights), *consts, *ropes)


def _mla_cache_kernel(c_ref, wkvb_ref, bd64_ref, g_ref, kn_ref, vm_ref):
    _mla_kv(c_ref[...], wkvb_ref, bd64_ref, g_ref[5:6, 0:256], kn_ref, vm_ref)


def _mla_cache_kv(ckv, wkvb, bd64, gains):
    rows = ckv.shape[0]
    return pl.pallas_call(
        _mla_cache_kernel,
        out_shape=[jax.ShapeDtypeStruct((rows, 512), BF16)] * 2,
        grid=(1,),
        in_specs=[_full(ckv.shape), _spec(wkvb), _full(bd64.shape), _full(gains.shape)],
        out_specs=[_full((rows, 512))] * 2,
        compiler_params=_cparams(),
        name="mla_cache_kv",
    )(ckv, _arr(wkvb), bd64, gains)


def _pair_attn_kernel(*refs, mode, kinds, n_pairs, mxu_denominator, lam_init):
    it = iter(refs)
    q_ref = next(it)
    segs = [[next(it) for _ in seg_kinds] for seg_kinds in kinds]
    gate_ref = next(it)
    if mode == "diff":
        lam_ref, subln_ref = next(it), next(it)
    o_ref = next(it)

    qw = 2 * LANES if mode == "mla" else LANES
    rows = q_ref.shape[0]
    lo = _lane_mask(rows, LANES, 0, LANES // 2)
    if mode == "diff":
        lv = lam_ref[...]
        lam = (jnp.exp(jnp.sum(lv[0:1] * lv[1:2], axis=-1, keepdims=True))
               - jnp.exp(jnp.sum(lv[2:3] * lv[3:4], axis=-1, keepdims=True)) + lam_init)

    def load(ref, kind, p):
        if kind == "rows4":
            x = ref[pl.ds(p, ref.shape[0] // 4, stride=4), :]
        elif mode == "gqa":
            x = ref[...]
        else:
            sl = slice(LANES * p, LANES * (p + 1))
            x = ref[sl, :] if kind == "cols" else ref[:, sl]
        return x.astype(BF16), kind == "cols"

    def values(p):
        return [load(seg[-1], seg_kinds[-1], p) for seg, seg_kinds in zip(segs, kinds)]

    def scores(p, j):
        q = q_ref[:, qw * p:qw * (p + 1)]
        ksegs = []
        for seg, seg_kinds in zip(segs, kinds):
            k, transposed = load(seg[0], seg_kinds[0], p)
            if mode == "mla":
                k = jnp.concatenate([k, seg[1][...].astype(BF16)], axis=-1)
            ksegs.append((k, transposed))
        keep = _lane_mask(rows, qw, 64 * j, 64 * (j + 1))
        if mode == "mla":
            keep = keep | _lane_mask(rows, qw, LANES + 32 * j, LANES + 32 * (j + 1))
        return _scores(jnp.where(keep, q, jnp.zeros_like(q)), ksegs)

    def finish(p, outs):
        sl = slice(LANES * p, LANES * (p + 1))
        if mode == "diff":
            d = outs[0] - lam * outs[1]
            o = _rms_full(d, subln_ref[...]) * (1.0 - lam_init)
        else:
            o = jnp.where(lo, outs[0], outs[1])
        o_ref[:, sl] = (o * gate_ref[:, sl].astype(F32)).astype(BF16)

    heads = [(p, j) for p in range(n_pairs) for j in range(2)]
    ss = scores(*heads[0])
    outs = []
    for t, (p, j) in enumerate(heads):
        ss_next = scores(*heads[t + 1]) if t + 1 < len(heads) else None
        spare = (64 * (1 - j), 64 * (2 - j)) if mxu_denominator and mode != "diff" else None
        outs.append(_softmax_pv(ss, values(p), ones_lanes=spare))
        ss = ss_next
        if j == 1:
            finish(p, outs)
            outs = []


def _pair_attention(mode, q, segs, gate, batch, tq, bq, n_pairs, mxu_denominator, extra=(), lam_init=0.0):
    qw = 2 * LANES if mode == "mla" else LANES
    nq = tq // bq
    steps_p = 4 // n_pairs
    in_specs = [pl.BlockSpec((bq, qw * n_pairs), lambda b, p, i: (b * nq + i, p))]
    args = [q]
    for seg in segs:
        for arr, spec, _ in seg:
            args.append(arr)
            in_specs.append(spec)
    kinds = tuple(tuple(kind for _, _, kind in seg) for seg in segs)
    in_specs.append(pl.BlockSpec((bq, LANES * n_pairs), lambda b, p, i: (b * nq + i, p)))
    args.append(gate)
    for arr in extra:
        args.append(arr)
        in_specs.append(pl.BlockSpec(arr.shape, lambda b, p, i: (0,) * arr.ndim))
    return pl.pallas_call(
        functools.partial(_pair_attn_kernel, mode=mode, kinds=kinds, n_pairs=n_pairs,
                          mxu_denominator=mxu_denominator, lam_init=lam_init),
        out_shape=jax.ShapeDtypeStruct((batch * tq, 4 * LANES), BF16),
        grid=(batch, steps_p, nq),
        in_specs=in_specs,
        out_specs=pl.BlockSpec((bq, LANES * n_pairs), lambda b, p, i: (b * nq + i, p)),
        compiler_params=_cparams(),
        name=mode + "_attention",
    )(*args)


def _self_seg(arr, tk, n_pairs, shared=False):
    if shared:
        return arr, pl.BlockSpec((tk, arr.shape[1]), lambda b, p, i: (b, 0)), "rows"
    return arr, pl.BlockSpec((tk, LANES * n_pairs), lambda b, p, i: (b, p)), "rows"


def _cache_seg(arr, layer, n_pairs, kind, shared=False):
    r, c = arr.shape[2:]
    if shared or kind == "rows4":
        return arr, pl.BlockSpec((None, None, r, c), lambda b, p, i: (b, layer, 0, 0)), kind
    if kind == "cols":
        return arr, pl.BlockSpec((None, None, LANES * n_pairs, c), lambda b, p, i: (b, layer, p, 0)), kind
    return arr, pl.BlockSpec((None, None, r, LANES * n_pairs), lambda b, p, i: (b, layer, 0, p)), kind


def _na_kernel(q_ref, k_ref, v_ref, ck_ref, cv_ref, bias_ref, gate_ref, o_ref):
    g = pl.program_id(1)
    n_groups = pl.num_programs(1)
    first_row = jnp.clip(NA_GROUP_ROWS * g - NA_ROWS // 2, 0, NA_GROUP_ROWS * n_groups - NA_WIN_ROWS)
    start = pl.multiple_of(first_row * GRID_W, GRID_W)
    win = NA_WIN_ROWS * GRID_W
    kwin = k_ref[pl.ds(start, win), :]
    vwin = v_ref[pl.ds(start, win), :]
    ck = ck_ref[...].astype(BF16)
    cv = cv_ref[...].astype(BF16)
    q = q_ref[...]
    rows = q.shape[0]
    lo = _lane_mask(rows, LANES, 0, LANES // 2)

    def scores(h):
        sl = slice(LANES * (h // 2), LANES * (h // 2 + 1))
        keep = _lane_mask(rows, LANES, 64 * (h % 2), 64 * (h % 2 + 1))
        qm = jnp.where(keep, q[:, sl], jnp.zeros_like(q[:, sl]))
        return _scores(qm, [(kwin[:, sl], False), (ck[sl, :], True)], bias0=bias_ref[h])

    ss = scores(0)
    outs = []
    for h in range(NA_HEADS):
        ss_next = scores(h + 1) if h + 1 < NA_HEADS else None
        sl = slice(LANES * (h // 2), LANES * (h // 2 + 1))
        j = h % 2
        outs.append(_softmax_pv(ss, [(vwin[:, sl], False), (cv[sl, :], True)],
                                ones_lanes=(64 * (1 - j), 64 * (2 - j))))
        if j == 1:
            o = jnp.where(lo, outs[0], outs[1])
            o_ref[:, sl] = (o * gate_ref[:, sl].astype(F32)).astype(BF16)
            outs = []
        ss = ss_next


def _na_attention(q, k, v, cache_k, cache_v, layer, bias, gate, batch, tq):
    bq = NA_GROUP_ROWS * GRID_W
    n_groups = tq // bq
    past = cache_k.shape[3]
    width = NA_HEADS * NA_HD

    def bias_map(b, g):
        return (layer, jnp.where(g == 0, 0, jnp.where(g == n_groups - 1, 2, 1)), 0, 0, 0)

    tok = pl.BlockSpec((bq, width), lambda b, g: (b * n_groups + g, 0))
    whole = pl.BlockSpec((tq, width), lambda b, g: (b, 0))
    cache = pl.BlockSpec((None, None, width, past), lambda b, g: (b, layer, 0, 0))
    return pl.pallas_call(
        _na_kernel,
        out_shape=jax.ShapeDtypeStruct((batch * tq, width), BF16),
        grid=(batch, n_groups),
        in_specs=[tok, whole, whole, cache, cache,
                  pl.BlockSpec((None, None, NA_HEADS, bq, NA_WIN_ROWS * GRID_W), bias_map), tok],
        out_specs=tok,
        compiler_params=_cparams(),
        name="na_attention",
    )(q, k, v, cache_k, cache_v, bias, gate)


def _na_bias_tables(rpb):
    qc = np.arange(GRID_W)[:, None]
    kc = np.arange(GRID_W)[None, :]
    cs = np.clip(qc - NA_COLS // 2, 0, GRID_W - NA_COLS)
    col_valid = (kc >= cs) & (kc < cs + NA_COLS)
    dc = kc - qc + NA_COLS - 1
    onehot = np.zeros((2 * NA_COLS - 1, GRID_W, GRID_W), np.float32)
    for d in range(2 * NA_COLS - 1):
        onehot[d] = (col_valid & (dc == d)).astype(np.float32)
    cols = jnp.einsum("lhrd,dqk->lhrqk", rpb.astype(F32), jnp.asarray(onehot), precision=lax.Precision.HIGHEST)
    cols = jnp.where(jnp.asarray(col_valid), cols * LOG2E, NEG)
    outside = jnp.full(cols.shape[:2] + (1, GRID_W, GRID_W), NEG, F32)
    cols = jnp.concatenate([cols, outside], axis=2)
    n_dr = 2 * NA_ROWS - 1
    idx = np.full((3, NA_GROUP_ROWS, NA_WIN_ROWS), n_dr, np.int32)
    for a in range(NA_GROUP_ROWS):
        for j in range(NA_WIN_ROWS):
            if j < NA_ROWS:
                idx[0, a, j] = j - a + NA_ROWS - 1
            if a <= j < a + NA_ROWS:
                idx[1, a, j] = j - a + NA_ROWS // 2 - 1
            if j >= NA_WIN_ROWS - NA_ROWS:
                idx[2, a, j] = j - a - (NA_WIN_ROWS - NA_ROWS) + NA_ROWS // 2 - 1
    layers = rpb.shape[0]
    return pl.pallas_call(
        functools.partial(_na_bias_kernel, idx=idx),
        out_shape=jax.ShapeDtypeStruct((layers, 3, NA_HEADS, NA_GROUP_ROWS * GRID_W, NA_WIN_ROWS * GRID_W), F32),
        grid=(layers, NA_HEADS),
        in_specs=[pl.BlockSpec((None, None, n_dr + 1, GRID_W, GRID_W), lambda l, h: (l, h, 0, 0, 0))],
        out_specs=pl.BlockSpec((None, 3, None, NA_GROUP_ROWS * GRID_W, NA_WIN_ROWS * GRID_W),
                               lambda l, h: (l, 0, h, 0, 0)),
        compiler_params=_cparams(),
        name="na_bias",
    )(cols)


def _na_bias_kernel(cols_ref, out_ref, *, idx):
    for t in range(3):
        for a in range(NA_GROUP_ROWS):
            for j in range(0, NA_WIN_ROWS, 2):
                pair = jnp.concatenate([cols_ref[int(idx[t, a, j])], cols_ref[int(idx[t, a, j + 1])]], axis=-1)
                out_ref[t, a * GRID_W:(a + 1) * GRID_W, j * GRID_W:(j + 2) * GRID_W] = pair


def _out_kernel(oa_ref, ob_ref, x_ref, mod_ref, w_ref, y_ref, *, row_base, tiles_per_batch):
    row = row_base + pl.program_id(0) // tiles_per_batch
    gate = mod_ref[0, pl.ds(row, 1), :][:, 2 * D_MODEL:]
    half = oa_ref.shape[1]
    acc = _dot(oa_ref[...], w_ref[:half, :]) + _dot(ob_ref[...], w_ref[half:, :])
    y_ref[...] = x_ref[...] + gate * acc


def _out_proj(oa, ob, x, mods, layer, w, row_base, tokens_per_batch):
    rows = x.shape[0]
    tm = ROW_TILE
    tiles_per_batch = tokens_per_batch // tm
    return pl.pallas_call(
        functools.partial(_out_kernel, row_base=row_base, tiles_per_batch=tiles_per_batch),
        out_shape=jax.ShapeDtypeStruct((rows, D_MODEL), F32),
        grid=(rows // tm,),
        in_specs=[pl.BlockSpec((tm, oa.shape[1]), lambda i: (i, 0)),
                  pl.BlockSpec((tm, ob.shape[1]), lambda i: (i, 0)),
                  pl.BlockSpec((tm, D_MODEL), lambda i: (i, 0)),
                  pl.BlockSpec((1, 8, 3 * D_MODEL), lambda i: (layer, 0, 0)),
                  _spec(w)],
        out_specs=pl.BlockSpec((tm, D_MODEL), lambda i: (i, 0)),
        compiler_params=_cparams(),
        name="out_proj",
    )(oa, ob, x, mods, _arr(w))


def _out_in_kernel(*refs, proj, rope, row_base, tiles_per_batch):
    oa_ref, ob_ref, x_ref, mod_prev_ref, wout_ref = refs[:5]
    n_proj_in = len(refs) - 5 - 1 - (9 if proj is _in_even_kernel else 8)
    proj_in = refs[5:5 + n_proj_in]
    y_ref = refs[5 + n_proj_in]
    slabs = refs[6 + n_proj_in:]
    _out_kernel(oa_ref, ob_ref, x_ref, mod_prev_ref, wout_ref, y_ref, row_base=row_base,
                tiles_per_batch=tiles_per_batch)
    proj(y_ref, *proj_in, *slabs, rope=rope, states=False, n_alias=0, row_base=row_base,
         tiles_per_batch=tiles_per_batch)


def _out_in_proj(proj, oa, ob, x, mods, layer, w_out, gains, weights, consts, ropes, out_widths, row_base,
                 tokens_per_batch):
    rows = x.shape[0]
    tm = ROW_TILE
    tiles_per_batch = tokens_per_batch // tm
    row = pl.BlockSpec((tm, D_MODEL), lambda i: (i, 0))
    in_specs = [pl.BlockSpec((tm, oa.shape[1]), lambda i: (i, 0)), pl.BlockSpec((tm, ob.shape[1]), lambda i: (i, 0)),
                row, pl.BlockSpec((1, 8, 3 * D_MODEL), lambda i: (layer - 1, 0, 0)), _spec(w_out),
                pl.BlockSpec((1, 8, 3 * D_MODEL), lambda i: (layer, 0, 0)), _full(gains.shape)]
    in_specs += [_spec(a) for a in (*weights, *consts)]
    in_specs += [pl.BlockSpec((tm, LANES), lambda i: (i % tiles_per_batch, 0)) for _ in ropes]
    return pl.pallas_call(
        functools.partial(_out_in_kernel, proj=proj, rope=True, row_base=row_base, tiles_per_batch=tiles_per_batch),
        out_shape=[jax.ShapeDtypeStruct((rows, D_MODEL), F32)]
        + [jax.ShapeDtypeStruct((rows, w), BF16) for w in out_widths],
        grid=(rows // tm,),
        in_specs=in_specs,
        out_specs=[row] + [pl.BlockSpec((tm, w), lambda i: (i, 0)) for w in out_widths],
        compiler_params=_cparams(),
        name="out_" + proj.__name__.strip("_"),
    )(oa, ob, x, mods, _arr(w_out), mods, gains, *map(_arr, weights), *consts, *ropes)


def _ctx_layer_kernel(*refs, even, n_in, n_alias, n_slabs, seq, lam_init, tiles_per_batch):
    in_refs = refs[:n_in]
    it = iter(refs[n_in:])
    wout_ref = next(it)
    if even:
        lam_ref, subln_ref = next(it), next(it)
    alias_refs = [next(it) for _ in range(n_alias)]
    y_ref = next(it)
    state_refs = [next(it) for _ in range(4)]
    slabs = [next(it) for _ in range(n_slabs)]
    oa_ref, ob_ref = next(it), next(it)
    x_ref, mod_ref = in_refs[0], in_refs[1]

    proj = _in_even_kernel if even else _in_odd_kernel
    proj(*in_refs, *alias_refs, *slabs, *state_refs, rope=False, states=True, n_alias=n_alias, row_base=0,
         tiles_per_batch=tiles_per_batch)
    for b in range(x_ref.shape[0] // seq):
        def own(ref):
            return ref.at[pl.ds(b * seq, seq), :]
        common = dict(n_pairs=4, mxu_denominator=False)
        if even:
            qcat, kn, kpe, vm, mg, dq, dk, dv, dg = (own(r) for r in slabs)
            _pair_attn_kernel(qcat, kn, kpe, vm, mg, own(oa_ref), mode="mla", kinds=(("rows",) * 3,),
                              lam_init=0.0, **common)
            _pair_attn_kernel(dq, dk, dv, dg, lam_ref, subln_ref, own(ob_ref), mode="diff", kinds=(("rows",) * 2,),
                              lam_init=lam_init, **common)
        else:
            nq, nk, nv, ng, gq, gk, gv, gg = (own(r) for r in slabs)
            _pair_attn_kernel(nq, nk, nv, ng, own(oa_ref), mode="mha", kinds=(("rows",) * 2,), lam_init=0.0, **common)
            _pair_attn_kernel(gq, gk, gv, gg, own(ob_ref), mode="gqa", kinds=(("rows",) * 2,), lam_init=0.0, **common)
    _out_kernel(oa_ref, ob_ref, x_ref, mod_ref, wout_ref, y_ref, row_base=0, tiles_per_batch=tiles_per_batch)


def _ctx_layer(even, x, mods, layer, gains, weights, consts, w_out, extras, slab_widths, state_tails, state_prev,
               seq, lam_init=0.0):
    rows = x.shape[0]
    tm = ROW_TILE
    slot = layer // 2
    in_specs = [pl.BlockSpec((tm, D_MODEL), lambda i: (i, 0)),
                pl.BlockSpec((1, 8, 3 * D_MODEL), lambda i: (layer, 0, 0)),
                _full(gains.shape)]
    in_specs += [_spec(a) for a in (*weights, *consts)]
    args = [x, mods, gains, *map(_arr, weights), *consts]
    n_in = len(args)
    in_specs += [_spec(a) for a in (w_out, *extras)]
    args += [_arr(w_out), *extras]
    aliases = {}
    if state_prev is not None:
        aliases = {len(args) + j: 1 + j for j in range(len(state_prev))}
        in_specs += [pl.BlockSpec(memory_space=pl.ANY) for _ in state_prev]
        args += list(state_prev)
    out_shape = [jax.ShapeDtypeStruct((rows, D_MODEL), F32)]
    out_specs = [pl.BlockSpec((tm, D_MODEL), lambda i: (i, 0))]
    out_shape += [jax.ShapeDtypeStruct((rows // seq, DEPTH // 2) + tail, F32) for tail in state_tails]
    out_specs += [pl.BlockSpec((tm // seq, None) + tail, lambda i: (i, slot, 0, 0)) for tail in state_tails]
    scratch = [pltpu.VMEM((tm, w), BF16) for w in (*slab_widths, 4 * LANES, 4 * LANES)]
    return pl.pallas_call(
        functools.partial(_ctx_layer_kernel, even=even, n_in=n_in, n_alias=len(aliases), n_slabs=len(slab_widths),
                          seq=seq, lam_init=lam_init, tiles_per_batch=rows // tm),
        out_shape=out_shape,
        grid=(rows // tm,),
        in_specs=in_specs,
        out_specs=out_specs,
        scratch_shapes=scratch,
        input_output_aliases=aliases,
        compiler_params=_cparams(),
        name="ctx_layer_even" if even else "ctx_layer_odd",
    )(*args)


def _layout_in_even_kernel(wt_ref, o_ref):
    kpe0 = Q_LORA + KV_LORA
    o_ref[:, 0:kpe0] = wt_ref[0:kpe0, :].T.astype(BF16)
    kpe = wt_ref[kpe0:kpe0 + MLA_ROPE, :]
    o_ref[:, kpe0:kpe0 + LANES] = jnp.concatenate([kpe] * (LANES // MLA_ROPE), axis=0).T.astype(BF16)
    o_ref[:, kpe0 + LANES:] = wt_ref[kpe0 + MLA_ROPE:, :].T.astype(BF16)


def _layout_in_even(w):
    layers, k, cols = w.shape
    kc = 256
    return pl.pallas_call(
        _layout_in_even_kernel,
        out_shape=jax.ShapeDtypeStruct((layers, k, cols + LANES - MLA_ROPE), BF16),
        grid=(layers, k // kc),
        in_specs=[pl.BlockSpec((None, cols, kc), lambda l, r: (l, 0, r))],
        out_specs=pl.BlockSpec((None, kc, cols + LANES - MLA_ROPE), lambda l, r: (l, r, 0)),
        compiler_params=_cparams(),
        name="layout_in_even",
    )(w.transpose(0, 2, 1))


def _layout_in_odd_kernel(w_ref, o_ref):
    def permuted(base):
        for k in range(GQA_HEADS // 2):
            lo_head, hi_head = GQA_PERM[2 * k], GQA_PERM[2 * k + 1]
            lo = w_ref[:, base + LANES * (lo_head // 2):base + LANES * (lo_head // 2 + 1)]
            hi = w_ref[:, base + LANES * (hi_head // 2):base + LANES * (hi_head // 2 + 1)]
            if lo_head % 2 == 1:
                lo = pltpu.roll(lo, LANES // 2, 1)
            if hi_head % 2 == 0:
                hi = pltpu.roll(hi, LANES // 2, 1)
            lane = lax.broadcasted_iota(jnp.int32, lo.shape, 1)
            o_ref[:, base + LANES * k:base + LANES * (k + 1)] = jnp.where(lane < LANES // 2, lo, hi).astype(BF16)

    o_ref[:, 0:2048] = w_ref[:, 0:2048].astype(BF16)
    permuted(2048)
    o_ref[:, 2560:2816] = w_ref[:, 2560:2816].astype(BF16)
    permuted(2816)


def _layout_w_out_kernel(w_ref, o_ref, *, permute):
    half = w_ref.shape[0] // 2
    o_ref[:half, :] = w_ref[:half, :].astype(BF16)
    if permute:
        for k, head in enumerate(GQA_PERM):
            o_ref[half + GQA_HD * k:half + GQA_HD * (k + 1), :] = (
                w_ref[half + GQA_HD * head:half + GQA_HD * (head + 1), :].astype(BF16))
    else:
        o_ref[half:, :] = w_ref[half:, :].astype(BF16)


def _layout_in_odd(w):
    layers, rows, cols = w.shape
    rb = 256
    return pl.pallas_call(
        _layout_in_odd_kernel,
        out_shape=jax.ShapeDtypeStruct((layers, rows, cols), BF16),
        grid=(layers, rows // rb),
        in_specs=[pl.BlockSpec((None, rb, cols), lambda l, r: (l, r, 0))],
        out_specs=pl.BlockSpec((None, rb, cols), lambda l, r: (l, r, 0)),
        compiler_params=_cparams(),
        name="layout_in_odd",
    )(w)


def _layout_w_out(w, permute):
    layers, rows, cols = w.shape
    return pl.pallas_call(
        functools.partial(_layout_w_out_kernel, permute=permute),
        out_shape=jax.ShapeDtypeStruct((layers, rows, cols), BF16),
        grid=(layers,),
        in_specs=[pl.BlockSpec((None, rows, cols), lambda l: (l, 0, 0))],
        out_specs=pl.BlockSpec((None, rows, cols), lambda l: (l, 0, 0)),
        compiler_params=_cparams(),
        name="layout_w_out",
    )(w)


def _block_diag(width, group):
    idx = np.arange(width) // group
    return jnp.asarray((idx[:, None] == idx[None, :]).astype(np.float32) / group, BF16)


def _rope_tables(t, rot_dim):
    pos = np.arange(t)
    row = (pos // GRID_W).astype(np.float64)
    col = (pos % GRID_W).astype(np.float64)
    n = rot_dim // 2
    inv = ROPE_THETA ** (-np.arange(0, n, 2, dtype=np.float64) / n)
    ang = np.concatenate([row[:, None] * inv, col[:, None] * inv], axis=-1)
    cos = np.concatenate([np.cos(ang), np.cos(ang)], axis=-1)
    sin = np.concatenate([-np.sin(ang), np.sin(ang)], axis=-1)
    reps = LANES // rot_dim
    return (jnp.asarray(np.tile(cos, (1, reps)), F32), jnp.asarray(np.tile(sin, (1, reps)), F32))


def _pad_row(v, width=D_MODEL):
    return jnp.pad(v, (0, width - v.shape[0]))


def _tile_row(v, reps):
    return _pad_row(jnp.tile(v, reps))


def kernel(x_prompt, x_sample, cache_mla_ckv, cache_mla_kpe, cache_diff_k, cache_diff_v, cache_na_k, cache_na_v, cache_gqa_k, cache_gqa_v, c, c_ctx, norm_w, w_mod, b_mod, w_in_even, w_out_even, mla_qa_norm, mla_wqb, mla_kva_norm, mla_wkvb, mla_qn_nope, mla_qn_rope, mla_kn_nope, mla_kn_rope, diff_qn, diff_kn, diff_lq1, diff_lk1, diff_lq2, diff_lk2, diff_subln, w_in_odd, w_out_odd, na_qn, na_kn, na_rpb, gqa_qn, gqa_kn):
    batch, seq, _ = x_prompt.shape
    dec_batch, dec_seq, _ = x_sample.shape
    past = cache_mla_ckv.shape[2]
    n_even, n_odd = w_in_even.shape[0], w_in_odd.shape[0]

    def layers(stack):
        return [_Layer(stack, i) for i in range(stack.shape[0])]

    w_in_e = layers(_layout_in_even(w_in_even))
    w_in_o = layers(_layout_in_odd(w_in_odd))
    w_out_e = layers(_layout_w_out(w_out_even, permute=False))
    w_out_o = layers(_layout_w_out(w_out_odd, permute=True))
    wqb = mla_wqb.reshape(n_even, Q_LORA, MLA_HEADS, MLA_QK)
    wqb = jnp.concatenate([wqb[..., :MLA_NOPE].reshape(n_even, Q_LORA, 4, 2 * MLA_NOPE),
                           wqb[..., MLA_NOPE:].reshape(n_even, Q_LORA, 4, 2 * MLA_ROPE),
                           jnp.zeros((n_even, Q_LORA, 4, LANES - 2 * MLA_ROPE), F32)], axis=-1)
    wqb = layers(wqb.reshape(n_even, Q_LORA, 4 * 2 * LANES).astype(BF16))
    wkvb = mla_wkvb.reshape(n_even, KV_LORA, MLA_HEADS, 2 * MLA_NOPE)
    wkvb = layers(jnp.concatenate([wkvb[..., :MLA_NOPE].reshape(n_even, KV_LORA, 512),
                                   wkvb[..., MLA_NOPE:].reshape(n_even, KV_LORA, 512)], axis=-1).astype(BF16))

    gains_e = [jnp.stack([norm_w[2 * i], _pad_row(mla_qa_norm[i]),
                          _pad_row(jnp.concatenate([jnp.tile(mla_qn_nope[i], 2), jnp.tile(mla_qn_rope[i], 4)])),
                          _pad_row(mla_kva_norm[i]), _tile_row(mla_kn_rope[i], 4), _tile_row(mla_kn_nope[i], 4),
                          _tile_row(diff_qn[i], 4), _tile_row(diff_kn[i], 4)]) for i in range(n_even)]
    gains_o = [jnp.stack([norm_w[2 * i + 1], _tile_row(na_qn[i], 4), _tile_row(na_kn[i], 4),
                          _tile_row(gqa_qn[i], 4), _tile_row(gqa_kn[i], 2),
                          jnp.zeros((D_MODEL,), F32), jnp.zeros((D_MODEL,), F32), jnp.zeros((D_MODEL,), F32)])
               for i in range(n_odd)]
    lam_vecs = [jnp.stack([diff_lq1[i], diff_lk1[i], diff_lq2[i], diff_lk2[i]]) for i in range(n_even)]

    bd64 = _block_diag(256, 64)
    bd32 = _block_diag(LANES, 32)
    bdq = jnp.concatenate([jnp.concatenate([_block_diag(LANES, 64), jnp.zeros((LANES, LANES), BF16)], axis=1),
                           jnp.concatenate([jnp.zeros((LANES, LANES), BF16), _block_diag(LANES, 32)], axis=1)], axis=0)
    c64, s64 = _rope_tables(dec_seq, 64)
    c32, s32 = _rope_tables(dec_seq, MLA_ROPE)
    na_bias = _na_bias_tables(na_rpb)

    cvecs = jnp.concatenate([c_ctx[None, :], c, jnp.zeros((8 - 1 - dec_batch, D_MODEL), F32)], axis=0)
    mods = _modulation(cvecs, w_mod, b_mod)

    cache_kpe = jnp.tile(cache_mla_kpe, (1, 1, 1, 4))
    cache_dk = cache_diff_k.transpose(0, 1, 3, 4, 5, 2).reshape(dec_batch, n_even, 512, past)
    cache_dv = cache_diff_v.reshape(dec_batch, n_even, 4 * past, LANES)
    cache_nk = cache_na_k.transpose(0, 1, 3, 4, 2).reshape(dec_batch, n_odd, 512, past)
    cache_nv = cache_na_v.transpose(0, 1, 3, 4, 2).reshape(dec_batch, n_odd, 512, past)
    cache_gk = cache_gqa_k.transpose(0, 1, 3, 4, 2).reshape(dec_batch, n_odd, LANES, past)
    cache_gv = cache_gqa_v.transpose(0, 1, 3, 4, 2).reshape(dec_batch, n_odd, LANES, past)

    even_widths = (1024, 512, LANES, 512, 512, 512, 512, 512, 512)
    odd_widths = (512, 512, 512, 512, 512, LANES, LANES, 512)
    even_states = ((seq, KV_LORA), (MLA_ROPE, seq), (512, seq), (4 * seq, LANES))
    odd_states = ((512, seq), (512, seq), (LANES, seq), (LANES, seq))

    def lam_init(l):
        return 0.8 - 0.6 * math.exp(-0.3 * l)

    def context_pass(x):
        states = [None, None]
        for l in range(DEPTH):
            i = l // 2
            if l % 2 == 0:
                x, *states[0] = _ctx_layer(True, x, mods, l, gains_e[i], (w_in_e[i], wqb[i], wkvb[i]),
                                           (bd64, bd32, bdq), w_out_e[i], (lam_vecs[i], diff_subln[i][None, :]),
                                           even_widths, even_states, states[0], seq, lam_init(l))
            else:
                x, *states[1] = _ctx_layer(False, x, mods, l, gains_o[i], (w_in_o[i],), (bd64,), w_out_o[i], (),
                                           odd_widths, odd_states, states[1], seq)
        return x, states

    def latent_pass(x, nb, t):
        row_base = 1
        bq = Q_BLOCK
        n_pairs = 4
        oa = ob = w_out_prev = None
        for l in range(DEPTH):
            i = l // 2
            if l % 2 == 0:
                proj_args = (gains_e[i], (w_in_e[i], wqb[i], wkvb[i]), (bd64, bd32, bdq), (c64, s64, c32, s32),
                             even_widths)
                proj = _in_even_kernel
            else:
                proj_args = (gains_o[i], (w_in_o[i],), (bd64,), (c64, s64), odd_widths)
                proj = _in_odd_kernel
            if l == 0:
                outs = _in_proj(proj, x, mods, l, *proj_args, row_base, t)
            else:
                x, *outs = _out_in_proj(proj, oa, ob, x, mods, l, w_out_prev, *proj_args, row_base, t)
            if l % 2 == 0:
                qcat, kn, kpe, vm, mg, dq, dk, dv, dg = outs
                kn_c, vm_c = _mla_cache_kv(cache_mla_ckv[:, i].reshape(nb * past, KV_LORA), wkvb[i], bd64, gains_e[i])
                mla_segs = [(_self_seg(kn, t, n_pairs), _self_seg(kpe, t, n_pairs, shared=True),
                             _self_seg(vm, t, n_pairs)),
                            (_self_seg(kn_c, past, n_pairs), _cache_seg(cache_kpe, i, n_pairs, "rows", shared=True),
                             _self_seg(vm_c, past, n_pairs))]
                diff_segs = [(_self_seg(dk, t, n_pairs), _self_seg(dv, t, n_pairs)),
                             (_cache_seg(cache_dk, i, n_pairs, "cols"), _cache_seg(cache_dv, i, n_pairs, "rows4"))]
                oa = _pair_attention("mla", qcat, mla_segs, mg, nb, t, bq, n_pairs, True)
                ob = _pair_attention("diff", dq, diff_segs, dg, nb, t, bq, n_pairs, True,
                                     extra=(lam_vecs[i], diff_subln[i][None, :]), lam_init=lam_init(l))
                w_out_prev = w_out_e[i]
            else:
                nq, nk, nv, ng, gq, gk, gv, gg = outs
                oa = _na_attention(nq, nk, nv, cache_nk, cache_nv, i, na_bias, ng, nb, t)
                gqa_segs = [(_self_seg(gk, t, n_pairs, shared=True), _self_seg(gv, t, n_pairs, shared=True)),
                            (_cache_seg(cache_gk, i, n_pairs, "cols", shared=True),
                             _cache_seg(cache_gv, i, n_pairs, "cols", shared=True))]
                ob = _pair_attention("gqa", gq, gqa_segs, gg, nb, t, bq, n_pairs, True)
                w_out_prev = w_out_o[i]
        return _out_proj(oa, ob, x, mods, DEPTH - 1, w_out_prev, row_base, t)

    y_prompt, st = context_pass(x_prompt.reshape(batch * seq, D_MODEL))
    y_sample = latent_pass(x_sample.reshape(dec_batch * dec_seq, D_MODEL), dec_batch, dec_seq)

    def token_major(a, heads):
        a = a.reshape((batch, n_even) + heads + (a.shape[2] // math.prod(heads), seq))
        return jnp.moveaxis(a, -1, 2)

    ckv, kpe_t, dk_t, dv4 = st[0]
    nk_t, nv_t, gk_t, gv_t = st[1]
    return (y_prompt.reshape(batch, seq, D_MODEL), y_sample.reshape(dec_batch, dec_seq, D_MODEL),
            ckv, token_major(kpe_t, ()), token_major(dk_t, (DIFF_HEADS, 2)),
            dv4.reshape(batch, n_even, seq, DIFF_HEADS, 2 * DIFF_HD),
            token_major(nk_t, (NA_HEADS,)), token_major(nv_t, (NA_HEADS,)),
            token_major(gk_t, (GQA_KV,)), token_major(gv_t, (GQA_KV,)))
```

```python
import functools
import math
from typing import NamedTuple

import jax
import jax.numpy as jnp
import numpy as np
from jax import lax
from jax.experimental import pallas as pl
from jax.experimental.pallas import tpu as pltpu

F32 = jnp.float32
BF16 = jnp.bfloat16

D_MODEL = 1024
DEPTH = 4
GRID_W = 64
ROPE_THETA = 10000.0
EPS = 1e-6
MLA_HEADS = 8
MLA_NOPE = 64
MLA_ROPE = 32
MLA_QK = MLA_NOPE + MLA_ROPE
Q_LORA = 256
KV_LORA = 128
DIFF_HEADS = 4
DIFF_HD = 64
NA_HEADS = 8
NA_HD = 64
NA_ROWS = 8
NA_COLS = 16
GQA_HEADS = 8
GQA_KV = 2
GQA_HD = 64
LANES = 128
NA_GROUP_ROWS = 4
NA_WIN_ROWS = NA_ROWS + NA_GROUP_ROWS
NEG = -1e30
LOG2E = math.log2(math.e)
VMEM_LIMIT = 48 * 1024 * 1024
GQA_PERM = (0, 4, 1, 5, 2, 6, 3, 7)

ROW_TILE = 512
Q_BLOCK = 512


def _cparams():
    return pltpu.CompilerParams(vmem_limit_bytes=VMEM_LIMIT)


def _dot(a, b):
    return jnp.dot(a, b, preferred_element_type=F32)


def _dot_nt(a, b):
    return lax.dot_general(a, b, (((1,), (1,)), ((), ())), preferred_element_type=F32)


def _rms_full(x, g):
    ms = jnp.mean(x * x, axis=-1, keepdims=True)
    return x * lax.rsqrt(ms + EPS) * g


def _rms_group(x, bd, g):
    ms = _dot((x * x).astype(BF16), bd)
    return x * lax.rsqrt(ms + EPS) * g


def _silu(u):
    return u * (1.0 / (1.0 + jnp.exp(-u)))


def _rope(x, cos, sin, group):
    half = group // 2
    rows, width = x.shape
    lane = lax.broadcasted_iota(jnp.int32, (rows, LANES), 1)
    first = (lane & (group - 1)) < half
    outs = []
    for c in range(width // LANES):
        xc = x[:, c * LANES:(c + 1) * LANES]
        rot = jnp.where(first, pltpu.roll(xc, LANES - half, 1), pltpu.roll(xc, half, 1))
        outs.append(xc * cos + rot * sin)
    return outs[0] if len(outs) == 1 else jnp.concatenate(outs, axis=-1)


def _lane_mask(rows, width, lo, hi):
    lane = lax.broadcasted_iota(jnp.int32, (rows, width), 1)
    return (lane >= lo) & (lane < hi)


def _scores(qm, ksegs, bias0=None):
    ss = [_dot(qm, k) if transposed else _dot_nt(qm, k) for k, transposed in ksegs]
    if bias0 is not None:
        ss[0] = ss[0] + bias0
    return ss


def _softmax_pv(ss, vsegs, ones_lanes=None):
    m = jnp.max(ss[0], axis=-1, keepdims=True)
    for s in ss[1:]:
        m = jnp.maximum(m, jnp.max(s, axis=-1, keepdims=True))
    acc = None
    l = None
    for s, (v, transposed) in zip(ss, vsegs):
        p = jnp.exp2(s - m)
        if ones_lanes is None:
            ps = jnp.sum(p, axis=-1, keepdims=True)
            l = ps if l is None else l + ps
        else:
            width = lax.broadcasted_iota(jnp.int32, v.shape, 0 if transposed else 1)
            v = jnp.where((width >= ones_lanes[0]) & (width < ones_lanes[1]), jnp.ones_like(v), v)
        a = _dot_nt(p.astype(BF16), v) if transposed else _dot(p.astype(BF16), v)
        acc = a if acc is None else acc + a
    if ones_lanes is not None:
        l = pltpu.roll(acc, LANES // 2, 1)
    return acc / l


def _mod_kernel(c_ref, w_ref, b_ref, o_ref):
    c = c_ref[...]
    o_ref[0] = _dot(_silu(c).astype(BF16), w_ref[0].astype(BF16)) + b_ref[0]


def _modulation(cvecs, w_mod, b_mod):
    tn = 768
    return pl.pallas_call(
        _mod_kernel,
        out_shape=jax.ShapeDtypeStruct((DEPTH, 8, 3 * D_MODEL), F32),
        grid=(DEPTH, 3 * D_MODEL // tn),
        in_specs=[pl.BlockSpec((8, D_MODEL), lambda l, n: (0, 0)),
                  pl.BlockSpec((1, D_MODEL, tn), lambda l, n: (l, 0, n)),
                  pl.BlockSpec((1, 1, tn), lambda l, n: (l, 0, n))],
        out_specs=pl.BlockSpec((1, 8, tn), lambda l, n: (l, 0, n)),
        compiler_params=_cparams(),
        name="modulation",
    )(cvecs, w_mod, b_mod.reshape(DEPTH, 1, 3 * D_MODEL))


def _modulated_norm(x_ref, mod_ref, g_ref, row):
    mod = mod_ref[0, pl.ds(row, 1), :]
    shift = mod[:, :D_MODEL]
    scale = mod[:, D_MODEL:2 * D_MODEL]
    h = _rms_full(x_ref[...], g_ref[0:1, :]) * (1.0 + scale) + shift
    return h.astype(BF16)


def _mla_kv(cn, wkvb_ref, bd64_ref, g_kn, kn_ref, vm_ref):
    kv = _dot(cn.astype(BF16), wkvb_ref[...])
    for c in range(2):
        sl = slice(256 * c, 256 * (c + 1))
        kn_ref[:, sl] = _rms_group(kv[:, sl], bd64_ref[...], g_kn).astype(BF16)
    vm_ref[...] = kv[:, 512:].astype(BF16)


def _store_rows(ref, x):
    seq = ref.shape[1]
    for b in range(ref.shape[0]):
        ref[b] = x[b * seq:(b + 1) * seq]


def _store_transposed(ref, x, row0, keep=None):
    seq = ref.shape[2]
    for b in range(ref.shape[0]):
        xt = x[b * seq:(b + 1) * seq].T
        if keep is not None:
            xt = xt[:keep]
        ref[b, row0:row0 + xt.shape[0], :] = xt


def _store_heads4(ref, x):
    seq = ref.shape[1] // 4
    for b in range(ref.shape[0]):
        for h in range(4):
            ref[b, pl.ds(h, seq, stride=4), :] = x[b * seq:(b + 1) * seq, LANES * h:LANES * (h + 1)]


def _in_even_kernel(*refs, rope, states, n_alias, row_base, tiles_per_batch):
    it = iter(refs)
    x_ref, mod_ref, g_ref, w_ref, wqb_ref, wkvb_ref, bd64_ref, bd32_ref, bdq_ref = (next(it) for _ in range(9))
    if rope:
        c64_ref, s64_ref, c32_ref, s32_ref = (next(it) for _ in range(4))
    for _ in range(n_alias):
        next(it)
    qcat_ref, kn_ref, kpe_ref, vm_ref, mg_ref, dq_ref, dk_ref, dv_ref, dg_ref = (next(it) for _ in range(9))
    if states:
        st_ckv_ref, st_kpe_ref, st_dk_ref, st_dv_ref = (next(it) for _ in range(4))

    row = row_base + pl.program_id(0) // tiles_per_batch
    hb = _modulated_norm(x_ref, mod_ref, g_ref, row)
    bd64 = bd64_ref[...]
    mla_scale = MLA_QK ** -0.5 * LOG2E
    diff_scale = DIFF_HD ** -0.5 * LOG2E

    qa = _dot(hb, w_ref[:, 0:256])
    qa_n = _rms_full(qa, g_ref[1:2, 0:256]).astype(BF16)
    q = _dot(qa_n, wqb_ref[...])
    for p in range(4):
        qp = _rms_group(q[:, 256 * p:256 * (p + 1)], bdq_ref[...], g_ref[2:3, 0:256])
        q_nope = qp[:, :LANES]
        q_pe = qp[:, LANES:]
        if rope:
            q_pe = _rope(q_pe, c32_ref[...], s32_ref[...], MLA_ROPE)
        qcat_ref[:, 256 * p:256 * p + LANES] = (q_nope * mla_scale).astype(BF16)
        qcat_ref[:, 256 * p + LANES:256 * (p + 1)] = (q_pe * mla_scale).astype(BF16)

    kva = _dot(hb, w_ref[:, 256:512])
    c_kv = _rms_full(kva[:, :LANES], g_ref[3:4, 0:LANES])
    k_pe = _rms_group(kva[:, LANES:], bd32_ref[...], g_ref[4:5, 0:LANES])
    if states:
        _store_rows(st_ckv_ref, c_kv)
        _store_transposed(st_kpe_ref, k_pe, 0, keep=MLA_ROPE)
    if rope:
        k_pe = _rope(k_pe, c32_ref[...], s32_ref[...], MLA_ROPE)
    kpe_ref[...] = k_pe.astype(BF16)
    _mla_kv(c_kv, wkvb_ref, bd64_ref, g_ref[5:6, 0:256], kn_ref, vm_ref)

    mg_ref[...] = _silu(_dot(hb, w_ref[:, 512:1024])).astype(BF16)

    dq = _dot(hb, w_ref[:, 1024:1536])
    dk = _dot(hb, w_ref[:, 1536:2048])
    for c in range(2):
        sl = slice(256 * c, 256 * (c + 1))
        qn = _rms_group(dq[:, sl], bd64, g_ref[6:7, 0:256])
        kn = _rms_group(dk[:, sl], bd64, g_ref[7:8, 0:256])
        if states:
            _store_transposed(st_dk_ref, kn, 256 * c)
        if rope:
            qn = _rope(qn, c64_ref[...], s64_ref[...], DIFF_HD)
            kn = _rope(kn, c64_ref[...], s64_ref[...], DIFF_HD)
        dq_ref[:, sl] = (qn * diff_scale).astype(BF16)
        dk_ref[:, sl] = kn.astype(BF16)
    dv = _dot(hb, w_ref[:, 2048:2560])
    if states:
        _store_heads4(st_dv_ref, dv)
    dv_ref[...] = dv.astype(BF16)
    dg_ref[...] = _silu(_dot(hb, w_ref[:, 2560:3072])).astype(BF16)


def _in_odd_kernel(*refs, rope, states, n_alias, row_base, tiles_per_batch):
    it = iter(refs)
    x_ref, mod_ref, g_ref, w_ref, bd64_ref = (next(it) for _ in range(5))
    if rope:
        c64_ref, s64_ref = (next(it) for _ in range(2))
    for _ in range(n_alias):
        next(it)
    nq_ref, nk_ref, nv_ref, ng_ref, gq_ref, gk_ref, gv_ref, gg_ref = (next(it) for _ in range(8))
    if states:
        st_nk_ref, st_nv_ref, st_gk_ref, st_gv_ref = (next(it) for _ in range(4))

    row = row_base + pl.program_id(0) // tiles_per_batch
    hb = _modulated_norm(x_ref, mod_ref, g_ref, row)
    bd64 = bd64_ref[...]
    na_scale = NA_HD ** -0.5 * LOG2E
    gqa_scale = GQA_HD ** -0.5 * LOG2E

    nq = _dot(hb, w_ref[:, 0:512])
    nk = _dot(hb, w_ref[:, 512:1024])
    for c in range(2):
        sl = slice(256 * c, 256 * (c + 1))
        nq_ref[:, sl] = (_rms_group(nq[:, sl], bd64, g_ref[1:2, 0:256]) * na_scale).astype(BF16)
        kn = _rms_group(nk[:, sl], bd64, g_ref[2:3, 0:256])
        if states:
            _store_transposed(st_nk_ref, kn, 256 * c)
        nk_ref[:, sl] = kn.astype(BF16)
    nv = _dot(hb, w_ref[:, 1024:1536])
    if states:
        for c in range(2):
            _store_transposed(st_nv_ref, nv[:, 256 * c:256 * (c + 1)], 256 * c)
    nv_ref[...] = nv.astype(BF16)
    ng_ref[...] = _silu(_dot(hb, w_ref[:, 1536:2048])).astype(BF16)

    gq = _dot(hb, w_ref[:, 2048:2560])
    for c in range(2):
        sl = slice(256 * c, 256 * (c + 1))
        qn = _rms_group(gq[:, sl], bd64, g_ref[3:4, 0:256])
        if rope:
            qn = _rope(qn, c64_ref[...], s64_ref[...], GQA_HD)
        gq_ref[:, sl] = (qn * gqa_scale).astype(BF16)
    gkv = _dot(hb, w_ref[:, 2560:2816])
    gk = _rms_group(gkv[:, :LANES], bd64[:LANES, :LANES], g_ref[4:5, 0:LANES])
    gv = gkv[:, LANES:]
    if states:
        _store_transposed(st_gk_ref, gk, 0)
        _store_transposed(st_gv_ref, gv, 0)
    if rope:
        gk = _rope(gk, c64_ref[...], s64_ref[...], GQA_HD)
    gk_ref[...] = gk.astype(BF16)
    gv_ref[...] = gv.astype(BF16)
    gg_ref[...] = _silu(_dot(hb, w_ref[:, 2816:3328])).astype(BF16)


def _full(shape):
    zeros = (0,) * len(shape)
    return pl.BlockSpec(shape, lambda *_: zeros)


class _Layer(NamedTuple):
    stack: jax.Array
    index: int


def _spec(a):
    if isinstance(a, _Layer):
        index = (a.index,) + (0,) * (a.stack.ndim - 1)
        return pl.BlockSpec((None,) + a.stack.shape[1:], lambda *_: index)
    return _full(a.shape)


def _arr(a):
    return a.stack if isinstance(a, _Layer) else a


def _in_proj(kernel, x, mods, layer, gains, weights, consts, ropes, out_widths, row_base, tokens_per_batch):
    rows = x.shape[0]
    tm = ROW_TILE
    tiles_per_batch = tokens_per_batch // tm
    in_specs = [pl.BlockSpec((tm, D_MODEL), lambda i: (i, 0)),
                pl.BlockSpec((1, 8, 3 * D_MODEL), lambda i: (layer, 0, 0)),
                _full(gains.shape)]
    in_specs += [_spec(a) for a in (*weights, *consts)]
    in_specs += [pl.BlockSpec((tm, LANES), lambda i: (i % tiles_per_batch, 0)) for _ in ropes]
    return pl.pallas_call(
        functools.partial(kernel, rope=True, states=False, n_alias=0, row_base=row_base,
                          tiles_per_batch=tiles_per_batch),
        out_shape=[jax.ShapeDtypeStruct((rows, w), BF16) for w in out_widths],
        grid=(rows // tm,),
        in_specs=in_specs,
        out_specs=[pl.BlockSpec((tm, w), lambda i: (i, 0)) for w in out_widths],
        compiler_params=_cparams(),
        name=kernel.__name__.strip("_"),
    )(x, mods, gains, *map(_arr, weights), *consts, *ropes)


def _mla_cache_kernel(c_ref, wkvb_ref, bd64_ref, g_ref, kn_ref, vm_ref):
    _mla_kv(c_ref[...], wkvb_ref, bd64_ref, g_ref[5:6, 0:256], kn_ref, vm_ref)


def _mla_cache_kv(ckv, wkvb, bd64, gains):
    rows = ckv.shape[0]
    return pl.pallas_call(
        _mla_cache_kernel,
        out_shape=[jax.ShapeDtypeStruct((rows, 512), BF16)] * 2,
        grid=(1,),
        in_specs=[_full(ckv.shape), _spec(wkvb), _full(bd64.shape), _full(gains.shape)],
        out_specs=[_full((rows, 512))] * 2,
        compiler_params=_cparams(),
        name="mla_cache_kv",
    )(ckv, _arr(wkvb), bd64, gains)


def _pair_attn_kernel(*refs, mode, kinds, n_pairs, mxu_denominator, lam_init):
    it = iter(refs)
    q_ref = next(it)
    segs = [[next(it) for _ in seg_kinds] for seg_kinds in kinds]
    gate_ref = next(it)
    if mode == "diff":
        lam_ref, subln_ref = next(it), next(it)
    o_ref = next(it)

    qw = 2 * LANES if mode == "mla" else LANES
    rows = q_ref.shape[0]
    lo = _lane_mask(rows, LANES, 0, LANES // 2)
    if mode == "diff":
        lv = lam_ref[...]
        lam = (jnp.exp(jnp.sum(lv[0:1] * lv[1:2], axis=-1, keepdims=True))
               - jnp.exp(jnp.sum(lv[2:3] * lv[3:4], axis=-1, keepdims=True)) + lam_init)

    def load(ref, kind, p):
        if kind == "rows4":
            x = ref[pl.ds(p, ref.shape[0] // 4, stride=4), :]
        elif mode == "gqa":
            x = ref[...]
        else:
            sl = slice(LANES * p, LANES * (p + 1))
            x = ref[sl, :] if kind == "cols" else ref[:, sl]
        return x.astype(BF16), kind == "cols"

    def values(p):
        return [load(seg[-1], seg_kinds[-1], p) for seg, seg_kinds in zip(segs, kinds)]

    def scores(p, j):
        q = q_ref[:, qw * p:qw * (p + 1)]
        ksegs = []
        for seg, seg_kinds in zip(segs, kinds):
            k, transposed = load(seg[0], seg_kinds[0], p)
            if mode == "mla":
                k = jnp.concatenate([k, seg[1][...].astype(BF16)], axis=-1)
            ksegs.append((k, transposed))
        keep = _lane_mask(rows, qw, 64 * j, 64 * (j + 1))
        if mode == "mla":
            keep = keep | _lane_mask(rows, qw, LANES + 32 * j, LANES + 32 * (j + 1))
        return _scores(jnp.where(keep, q, jnp.zeros_like(q)), ksegs)

    def finish(p, outs):
        sl = slice(LANES * p, LANES * (p + 1))
        if mode == "diff":
            d = outs[0] - lam * outs[1]
            o = _rms_full(d, subln_ref[...]) * (1.0 - lam_init)
        else:
            o = jnp.where(lo, outs[0], outs[1])
        o_ref[:, sl] = (o * gate_ref[:, sl].astype(F32)).astype(BF16)

    heads = [(p, j) for p in range(n_pairs) for j in range(2)]
    ss = scores(*heads[0])
    outs = []
    for t, (p, j) in enumerate(heads):
        ss_next = scores(*heads[t + 1]) if t + 1 < len(heads) else None
        spare = (64 * (1 - j), 64 * (2 - j)) if mxu_denominator and mode != "diff" else None
        outs.append(_softmax_pv(ss, values(p), ones_lanes=spare))
        ss = ss_next
        if j == 1:
            finish(p, outs)
            outs = []


def _pair_attention(mode, q, segs, gate, batch, tq, bq, n_pairs, mxu_denominator, extra=(), lam_init=0.0):
    qw = 2 * LANES if mode == "mla" else LANES
    nq = tq // bq
    steps_p = 4 // n_pairs
    in_specs = [pl.BlockSpec((bq, qw * n_pairs), lambda b, p, i: (b * nq + i, p))]
    args = [q]
    for seg in segs:
        for arr, spec, _ in seg:
            args.append(arr)
            in_specs.append(spec)
    kinds = tuple(tuple(kind for _, _, kind in seg) for seg in segs)
    in_specs.append(pl.BlockSpec((bq, LANES * n_pairs), lambda b, p, i: (b * nq + i, p)))
    args.append(gate)
    for arr in extra:
        args.append(arr)
        in_specs.append(pl.BlockSpec(arr.shape, lambda b, p, i: (0,) * arr.ndim))
    return pl.pallas_call(
        functools.partial(_pair_attn_kernel, mode=mode, kinds=kinds, n_pairs=n_pairs,
                          mxu_denominator=mxu_denominator, lam_init=lam_init),
        out_shape=jax.ShapeDtypeStruct((batch * tq, 4 * LANES), BF16),
        grid=(batch, steps_p, nq),
        in_specs=in_specs,
        out_specs=pl.BlockSpec((bq, LANES * n_pairs), lambda b, p, i: (b * nq + i, p)),
        compiler_params=_cparams(),
        name=mode + "_attention",
    )(*args)


def _self_seg(arr, tk, n_pairs, shared=False):
    if shared:
        return arr, pl.BlockSpec((tk, arr.shape[1]), lambda b, p, i: (b, 0)), "rows"
    return arr, pl.BlockSpec((tk, LANES * n_pairs), lambda b, p, i: (b, p)), "rows"


def _cache_seg(arr, layer, n_pairs, kind, shared=False):
    r, c = arr.shape[2:]
    if shared or kind == "rows4":
        return arr, pl.BlockSpec((None, None, r, c), lambda b, p, i: (b, layer, 0, 0)), kind
    if kind == "cols":
        return arr, pl.BlockSpec((None, None, LANES * n_pairs, c), lambda b, p, i: (b, layer, p, 0)), kind
    return arr, pl.BlockSpec((None, None, r, LANES * n_pairs), lambda b, p, i: (b, layer, 0, p)), kind


def _na_kernel(q_ref, k_ref, v_ref, ck_ref, cv_ref, bias_ref, gate_ref, o_ref):
    g = pl.program_id(1)
    n_groups = pl.num_programs(1)
    first_row = jnp.clip(NA_GROUP_ROWS * g - NA_ROWS // 2, 0, NA_GROUP_ROWS * n_groups - NA_WIN_ROWS)
    start = pl.multiple_of(first_row * GRID_W, GRID_W)
    win = NA_WIN_ROWS * GRID_W
    kwin = k_ref[pl.ds(start, win), :]
    vwin = v_ref[pl.ds(start, win), :]
    ck = ck_ref[...].astype(BF16)
    cv = cv_ref[...].astype(BF16)
    q = q_ref[...]
    rows = q.shape[0]
    lo = _lane_mask(rows, LANES, 0, LANES // 2)

    def scores(h):
        sl = slice(LANES * (h // 2), LANES * (h // 2 + 1))
        keep = _lane_mask(rows, LANES, 64 * (h % 2), 64 * (h % 2 + 1))
        qm = jnp.where(keep, q[:, sl], jnp.zeros_like(q[:, sl]))
        return _scores(qm, [(kwin[:, sl], False), (ck[sl, :], True)], bias0=bias_ref[h])

    ss = scores(0)
    outs = []
    for h in range(NA_HEADS):
        ss_next = scores(h + 1) if h + 1 < NA_HEADS else None
        sl = slice(LANES * (h // 2), LANES * (h // 2 + 1))
        j = h % 2
        outs.append(_softmax_pv(ss, [(vwin[:, sl], False), (cv[sl, :], True)],
                                ones_lanes=(64 * (1 - j), 64 * (2 - j))))
        if j == 1:
            o = jnp.where(lo, outs[0], outs[1])
            o_ref[:, sl] = (o * gate_ref[:, sl].astype(F32)).astype(BF16)
            outs = []
        ss = ss_next


def _na_attention(q, k, v, cache_k, cache_v, layer, bias, gate, batch, tq):
    bq = NA_GROUP_ROWS * GRID_W
    n_groups = tq // bq
    past = cache_k.shape[3]
    width = NA_HEADS * NA_HD

    def bias_map(b, g):
        return (layer, jnp.where(g == 0, 0, jnp.where(g == n_groups - 1, 2, 1)), 0, 0, 0)

    tok = pl.BlockSpec((bq, width), lambda b, g: (b * n_groups + g, 0))
    whole = pl.BlockSpec((tq, width), lambda b, g: (b, 0))
    cache = pl.BlockSpec((None, None, width, past), lambda b, g: (b, layer, 0, 0))
    return pl.pallas_call(
        _na_kernel,
        out_shape=jax.ShapeDtypeStruct((batch * tq, width), BF16),
        grid=(batch, n_groups),
        in_specs=[tok, whole, whole, cache, cache,
                  pl.BlockSpec((None, None, NA_HEADS, bq, NA_WIN_ROWS * GRID_W), bias_map), tok],
        out_specs=tok,
        compiler_params=_cparams(),
        name="na_attention",
    )(q, k, v, cache_k, cache_v, bias, gate)


def _na_bias_tables(rpb):
    n_dr = 2 * NA_ROWS - 1
    idx = np.full((3, NA_GROUP_ROWS, NA_WIN_ROWS), n_dr, np.int32)
    for a in range(NA_GROUP_ROWS):
        for j in range(NA_WIN_ROWS):
            if j < NA_ROWS:
                idx[0, a, j] = j - a + NA_ROWS - 1
            if a <= j < a + NA_ROWS:
                idx[1, a, j] = j - a + NA_ROWS // 2 - 1
            if j >= NA_WIN_ROWS - NA_ROWS:
                idx[2, a, j] = j - a - (NA_WIN_ROWS - NA_ROWS) + NA_ROWS // 2 - 1
    layers = rpb.shape[0]
    padded = jnp.pad(rpb.astype(F32), ((0, 0), (0, 0), (0, 1), (0, LANES - rpb.shape[3])))
    return pl.pallas_call(
        functools.partial(_na_bias_kernel, idx=idx),
        out_shape=jax.ShapeDtypeStruct((layers, 3, NA_HEADS, NA_GROUP_ROWS * GRID_W, NA_WIN_ROWS * GRID_W), F32),
        grid=(layers, NA_HEADS),
        in_specs=[pl.BlockSpec((None, None, n_dr + 1, LANES), lambda l, h: (l, h, 0, 0))],
        out_specs=pl.BlockSpec((None, 3, None, NA_GROUP_ROWS * GRID_W, NA_WIN_ROWS * GRID_W),
                               lambda l, h: (l, 0, h, 0, 0)),
        compiler_params=_cparams(),
        name="na_bias",
    )(padded)


def _na_bias_kernel(rpb_ref, out_ref, *, idx):
    n_dr = rpb_ref.shape[0] - 1
    qc = lax.broadcasted_iota(jnp.int32, (GRID_W, LANES), 0)
    lane = lax.broadcasted_iota(jnp.int32, (GRID_W, LANES), 1)
    kc = lane & (GRID_W - 1)
    first = jnp.clip(qc - NA_COLS // 2, 0, GRID_W - NA_COLS)
    valid = (kc >= first) & (kc < first + NA_COLS)
    low = lane < GRID_W
    outside = jnp.full((GRID_W, LANES), NEG, F32)

    def table(r):
        if r == n_dr:
            return outside
        row = jnp.broadcast_to(rpb_ref[r:r + 1, :], (GRID_W, LANES))
        lo = pltpu.roll(row, LANES - (NA_COLS - 1), 1, stride=1, stride_axis=0)
        hi = pltpu.roll(row, GRID_W - (NA_COLS - 1), 1, stride=1, stride_axis=0)
        return jnp.where(valid, jnp.where(low, lo, hi) * LOG2E, NEG)

    tables = [table(r) for r in range(n_dr + 1)]
    for t in range(3):
        for a in range(NA_GROUP_ROWS):
            for j in range(0, NA_WIN_ROWS, 2):
                pair = jnp.where(low, tables[int(idx[t, a, j])], tables[int(idx[t, a, j + 1])])
                out_ref[t, a * GRID_W:(a + 1) * GRID_W, j * GRID_W:(j + 2) * GRID_W] = pair


def _out_kernel(oa_ref, ob_ref, x_ref, mod_ref, w_ref, y_ref, *, row_base, tiles_per_batch):
    row = row_base + pl.program_id(0) // tiles_per_batch
    gate = mod_ref[0, pl.ds(row, 1), :][:, 2 * D_MODEL:]
    half = oa_ref.shape[1]
    acc = _dot(oa_ref[...], w_ref[:half, :]) + _dot(ob_ref[...], w_ref[half:, :])
    y_ref[...] = x_ref[...] + gate * acc


def _out_proj(oa, ob, x, mods, layer, w, row_base, tokens_per_batch):
    rows = x.shape[0]
    tm = ROW_TILE
    tiles_per_batch = tokens_per_batch // tm
    return pl.pallas_call(
        functools.partial(_out_kernel, row_base=row_base, tiles_per_batch=tiles_per_batch),
        out_shape=jax.ShapeDtypeStruct((rows, D_MODEL), F32),
        grid=(rows // tm,),
        in_specs=[pl.BlockSpec((tm, oa.shape[1]), lambda i: (i, 0)),
                  pl.BlockSpec((tm, ob.shape[1]), lambda i: (i, 0)),
                  pl.BlockSpec((tm, D_MODEL), lambda i: (i, 0)),
                  pl.BlockSpec((1, 8, 3 * D_MODEL), lambda i: (layer, 0, 0)),
                  _spec(w)],
        out_specs=pl.BlockSpec((tm, D_MODEL), lambda i: (i, 0)),
        compiler_params=_cparams(),
        name="out_proj",
    )(oa, ob, x, mods, _arr(w))


def _out_in_kernel(*refs, proj, rope, row_base, tiles_per_batch):
    oa_ref, ob_ref, x_ref, mod_prev_ref, wout_ref = refs[:5]
    n_proj_in = len(refs) - 5 - 1 - (9 if proj is _in_even_kernel else 8)
    proj_in = refs[5:5 + n_proj_in]
    y_ref = refs[5 + n_proj_in]
    slabs = refs[6 + n_proj_in:]
    _out_kernel(oa_ref, ob_ref, x_ref, mod_prev_ref, wout_ref, y_ref, row_base=row_base,
                tiles_per_batch=tiles_per_batch)
    proj(y_ref, *proj_in, *slabs, rope=rope, states=False, n_alias=0, row_base=row_base,
         tiles_per_batch=tiles_per_batch)


def _out_in_proj(proj, oa, ob, x, mods, layer, w_out, gains, weights, consts, ropes, out_widths, row_base,
                 tokens_per_batch):
    rows = x.shape[0]
    tm = ROW_TILE
    tiles_per_batch = tokens_per_batch // tm
    row = pl.BlockSpec((tm, D_MODEL), lambda i: (i, 0))
    in_specs = [pl.BlockSpec((tm, oa.shape[1]), lambda i: (i, 0)), pl.BlockSpec((tm, ob.shape[1]), lambda i: (i, 0)),
                row, pl.BlockSpec((1, 8, 3 * D_MODEL), lambda i: (layer - 1, 0, 0)), _spec(w_out),
                pl.BlockSpec((1, 8, 3 * D_MODEL), lambda i: (layer, 0, 0)), _full(gains.shape)]
    in_specs += [_spec(a) for a in (*weights, *consts)]
    in_specs += [pl.BlockSpec((tm, LANES), lambda i: (i % tiles_per_batch, 0)) for _ in ropes]
    return pl.pallas_call(
        functools.partial(_out_in_kernel, proj=proj, rope=True, row_base=row_base, tiles_per_batch=tiles_per_batch),
        out_shape=[jax.ShapeDtypeStruct((rows, D_MODEL), F32)]
        + [jax.ShapeDtypeStruct((rows, w), BF16) for w in out_widths],
        grid=(rows // tm,),
        in_specs=in_specs,
        out_specs=[row] + [pl.BlockSpec((tm, w), lambda i: (i, 0)) for w in out_widths],
        compiler_params=_cparams(),
        name="out_" + proj.__name__.strip("_"),
    )(oa, ob, x, mods, _arr(w_out), mods, gains, *map(_arr, weights), *consts, *ropes)


def _ctx_layer_kernel(*refs, even, n_in, n_alias, n_slabs, seq, lam_init, tiles_per_batch):
    in_refs = refs[:n_in]
    it = iter(refs[n_in:])
    wout_ref = next(it)
    if even:
        lam_ref, subln_ref = next(it), next(it)
    alias_refs = [next(it) for _ in range(n_alias)]
    y_ref = next(it)
    state_refs = [next(it) for _ in range(4)]
    slabs = [next(it) for _ in range(n_slabs)]
    oa_ref, ob_ref = next(it), next(it)
    x_ref, mod_ref = in_refs[0], in_refs[1]

    proj = _in_even_kernel if even else _in_odd_kernel
    proj(*in_refs, *alias_refs, *slabs, *state_refs, rope=False, states=True, n_alias=n_alias, row_base=0,
         tiles_per_batch=tiles_per_batch)
    for b in range(x_ref.shape[0] // seq):
        def own(ref):
            return ref.at[pl.ds(b * seq, seq), :]
        common = dict(n_pairs=4, mxu_denominator=False)
        if even:
            qcat, kn, kpe, vm, mg, dq, dk, dv, dg = (own(r) for r in slabs)
            _pair_attn_kernel(qcat, kn, kpe, vm, mg, own(oa_ref), mode="mla", kinds=(("rows",) * 3,),
                              lam_init=0.0, **common)
            _pair_attn_kernel(dq, dk, dv, dg, lam_ref, subln_ref, own(ob_ref), mode="diff", kinds=(("rows",) * 2,),
                              lam_init=lam_init, **common)
        else:
            nq, nk, nv, ng, gq, gk, gv, gg = (own(r) for r in slabs)
            _pair_attn_kernel(nq, nk, nv, ng, own(oa_ref), mode="mha", kinds=(("rows",) * 2,), lam_init=0.0, **common)
            _pair_attn_kernel(gq, gk, gv, gg, own(ob_ref), mode="gqa", kinds=(("rows",) * 2,), lam_init=0.0, **common)
    _out_kernel(oa_ref, ob_ref, x_ref, mod_ref, wout_ref, y_ref, row_base=0, tiles_per_batch=tiles_per_batch)


def _ctx_layer(even, x, mods, layer, gains, weights, consts, w_out, extras, slab_widths, state_tails, state_prev,
               seq, lam_init=0.0):
    rows = x.shape[0]
    tm = ROW_TILE
    slot = layer // 2
    in_specs = [pl.BlockSpec((tm, D_MODEL), lambda i: (i, 0)),
                pl.BlockSpec((1, 8, 3 * D_MODEL), lambda i: (layer, 0, 0)),
                _full(gains.shape)]
    in_specs += [_spec(a) for a in (*weights, *consts)]
    args = [x, mods, gains, *map(_arr, weights), *consts]
    n_in = len(args)
    in_specs += [_spec(a) for a in (w_out, *extras)]
    args += [_arr(w_out), *extras]
    aliases = {}
    if state_prev is not None:
        aliases = {len(args) + j: 1 + j for j in range(len(state_prev))}
        in_specs += [pl.BlockSpec(memory_space=pl.ANY) for _ in state_prev]
        args += list(state_prev)
    out_shape = [jax.ShapeDtypeStruct((rows, D_MODEL), F32)]
    out_specs = [pl.BlockSpec((tm, D_MODEL), lambda i: (i, 0))]
    out_shape += [jax.ShapeDtypeStruct((rows // seq, DEPTH // 2) + tail, F32) for tail in state_tails]
    out_specs += [pl.BlockSpec((tm // seq, None) + tail, lambda i: (i, slot, 0, 0)) for tail in state_tails]
    scratch = [pltpu.VMEM((tm, w), BF16) for w in (*slab_widths, 4 * LANES, 4 * LANES)]
    return pl.pallas_call(
        functools.partial(_ctx_layer_kernel, even=even, n_in=n_in, n_alias=len(aliases), n_slabs=len(slab_widths),
                          seq=seq, lam_init=lam_init, tiles_per_batch=rows // tm),
        out_shape=out_shape,
        grid=(rows // tm,),
        in_specs=in_specs,
        out_specs=out_specs,
        scratch_shapes=scratch,
        input_output_aliases=aliases,
        compiler_params=_cparams(),
        name="ctx_layer_even" if even else "ctx_layer_odd",
    )(*args)


def _layout_in_even_kernel(wt_ref, o_ref):
    kpe0 = Q_LORA + KV_LORA
    o_ref[:, 0:kpe0] = wt_ref[0:kpe0, :].T.astype(BF16)
    kpe = wt_ref[kpe0:kpe0 + MLA_ROPE, :]
    o_ref[:, kpe0:kpe0 + LANES] = jnp.concatenate([kpe] * (LANES // MLA_ROPE), axis=0).T.astype(BF16)
    o_ref[:, kpe0 + LANES:] = wt_ref[kpe0 + MLA_ROPE:, :].T.astype(BF16)


def _layout_in_even(w):
    layers, k, cols = w.shape
    kc = 256
    return pl.pallas_call(
        _layout_in_even_kernel,
        out_shape=jax.ShapeDtypeStruct((layers, k, cols + LANES - MLA_ROPE), BF16),
        grid=(layers, k // kc),
        in_specs=[pl.BlockSpec((None, cols, kc), lambda l, r: (l, 0, r))],
        out_specs=pl.BlockSpec((None, kc, cols + LANES - MLA_ROPE), lambda l, r: (l, r, 0)),
        compiler_params=_cparams(),
        name="layout_in_even",
    )(w.transpose(0, 2, 1))


def _layout_in_odd_kernel(w_ref, o_ref):
    def permuted(base):
        for k in range(GQA_HEADS // 2):
            lo_head, hi_head = GQA_PERM[2 * k], GQA_PERM[2 * k + 1]
            lo = w_ref[:, base + LANES * (lo_head // 2):base + LANES * (lo_head // 2 + 1)]
            hi = w_ref[:, base + LANES * (hi_head // 2):base + LANES * (hi_head // 2 + 1)]
            if lo_head % 2 == 1:
                lo = pltpu.roll(lo, LANES // 2, 1)
            if hi_head % 2 == 0:
                hi = pltpu.roll(hi, LANES // 2, 1)
            lane = lax.broadcasted_iota(jnp.int32, lo.shape, 1)
            o_ref[:, base + LANES * k:base + LANES * (k + 1)] = jnp.where(lane < LANES // 2, lo, hi).astype(BF16)

    o_ref[:, 0:2048] = w_ref[:, 0:2048].astype(BF16)
    permuted(2048)
    o_ref[:, 2560:2816] = w_ref[:, 2560:2816].astype(BF16)
    permuted(2816)


def _layout_w_out_kernel(w_ref, o_ref, *, permute):
    half = w_ref.shape[0] // 2
    o_ref[:half, :] = w_ref[:half, :].astype(BF16)
    if permute:
        for k, head in enumerate(GQA_PERM):
            o_ref[half + GQA_HD * k:half + GQA_HD * (k + 1), :] = (
                w_ref[half + GQA_HD * head:half + GQA_HD * (head + 1), :].astype(BF16))
    else:
        o_ref[half:, :] = w_ref[half:, :].astype(BF16)


def _layout_in_odd(w):
    layers, rows, cols = w.shape
    rb = 256
    return pl.pallas_call(
        _layout_in_odd_kernel,
        out_shape=jax.ShapeDtypeStruct((layers, rows, cols), BF16),
        grid=(layers, rows // rb),
        in_specs=[pl.BlockSpec((None, rb, cols), lambda l, r: (l, r, 0))],
        out_specs=pl.BlockSpec((None, rb, cols), lambda l, r: (l, r, 0)),
        compiler_params=_cparams(),
        name="layout_in_odd",
    )(w)


def _layout_w_out(w, permute):
    layers, rows, cols = w.shape
    return pl.pallas_call(
        functools.partial(_layout_w_out_kernel, permute=permute),
        out_shape=jax.ShapeDtypeStruct((layers, rows, cols), BF16),
        grid=(layers,),
        in_specs=[pl.BlockSpec((None, rows, cols), lambda l: (l, 0, 0))],
        out_specs=pl.BlockSpec((None, rows, cols), lambda l: (l, 0, 0)),
        compiler_params=_cparams(),
        name="layout_w_out",
    )(w)


def _block_diag(width, group):
    idx = np.arange(width) // group
    return jnp.asarray((idx[:, None] == idx[None, :]).astype(np.float32) / group, BF16)


def _rope_tables(t, rot_dim):
    pos = np.arange(t)
    row = (pos // GRID_W).astype(np.float64)
    col = (pos % GRID_W).astype(np.float64)
    n = rot_dim // 2
    inv = ROPE_THETA ** (-np.arange(0, n, 2, dtype=np.float64) / n)
    ang = np.concatenate([row[:, None] * inv, col[:, None] * inv], axis=-1)
    cos = np.concatenate([np.cos(ang), np.cos(ang)], axis=-1)
    sin = np.concatenate([-np.sin(ang), np.sin(ang)], axis=-1)
    reps = LANES // rot_dim
    return (jnp.asarray(np.tile(cos, (1, reps)), F32), jnp.asarray(np.tile(sin, (1, reps)), F32))


def _pad_row(v, width=D_MODEL):
    return jnp.pad(v, (0, width - v.shape[0]))


def _tile_row(v, reps):
    return _pad_row(jnp.tile(v, reps))


def kernel(x_prompt, x_sample, cache_mla_ckv, cache_mla_kpe, cache_diff_k, cache_diff_v, cache_na_k, cache_na_v, cache_gqa_k, cache_gqa_v, c, c_ctx, norm_w, w_mod, b_mod, w_in_even, w_out_even, mla_qa_norm, mla_wqb, mla_kva_norm, mla_wkvb, mla_qn_nope, mla_qn_rope, mla_kn_nope, mla_kn_rope, diff_qn, diff_kn, diff_lq1, diff_lk1, diff_lq2, diff_lk2, diff_subln, w_in_odd, w_out_odd, na_qn, na_kn, na_rpb, gqa_qn, gqa_kn):
    batch, seq, _ = x_prompt.shape
    dec_batch, dec_seq, _ = x_sample.shape
    past = cache_mla_ckv.shape[2]
    n_even, n_odd = w_in_even.shape[0], w_in_odd.shape[0]

    def layers(stack):
        return [_Layer(stack, i) for i in range(stack.shape[0])]

    w_in_e = layers(_layout_in_even(w_in_even))
    w_in_o = layers(_layout_in_odd(w_in_odd))
    w_out_e = layers(_layout_w_out(w_out_even, permute=False))
    w_out_o = layers(_layout_w_out(w_out_odd, permute=True))
    wqb = mla_wqb.reshape(n_even, Q_LORA, MLA_HEADS, MLA_QK)
    wqb = jnp.concatenate([wqb[..., :MLA_NOPE].reshape(n_even, Q_LORA, 4, 2 * MLA_NOPE),
                           wqb[..., MLA_NOPE:].reshape(n_even, Q_LORA, 4, 2 * MLA_ROPE),
                           jnp.zeros((n_even, Q_LORA, 4, LANES - 2 * MLA_ROPE), F32)], axis=-1)
    wqb = layers(wqb.reshape(n_even, Q_LORA, 4 * 2 * LANES).astype(BF16))
    wkvb = mla_wkvb.reshape(n_even, KV_LORA, MLA_HEADS, 2 * MLA_NOPE)
    wkvb = layers(jnp.concatenate([wkvb[..., :MLA_NOPE].reshape(n_even, KV_LORA, 512),
                                   wkvb[..., MLA_NOPE:].reshape(n_even, KV_LORA, 512)], axis=-1).astype(BF16))

    gains_e = [jnp.stack([norm_w[2 * i], _pad_row(mla_qa_norm[i]),
                          _pad_row(jnp.concatenate([jnp.tile(mla_qn_nope[i], 2), jnp.tile(mla_qn_rope[i], 4)])),
                          _pad_row(mla_kva_norm[i]), _tile_row(mla_kn_rope[i], 4), _tile_row(mla_kn_nope[i], 4),
                          _tile_row(diff_qn[i], 4), _tile_row(diff_kn[i], 4)]) for i in range(n_even)]
    gains_o = [jnp.stack([norm_w[2 * i + 1], _tile_row(na_qn[i], 4), _tile_row(na_kn[i], 4),
                          _tile_row(gqa_qn[i], 4), _tile_row(gqa_kn[i], 2),
                          jnp.zeros((D_MODEL,), F32), jnp.zeros((D_MODEL,), F32), jnp.zeros((D_MODEL,), F32)])
               for i in range(n_odd)]
    lam_vecs = [jnp.stack([diff_lq1[i], diff_lk1[i], diff_lq2[i], diff_lk2[i]]) for i in range(n_even)]

    bd64 = _block_diag(256, 64)
    bd32 = _block_diag(LANES, 32)
    bdq = jnp.concatenate([jnp.concatenate([_block_diag(LANES, 64), jnp.zeros((LANES, LANES), BF16)], axis=1),
                           jnp.concatenate([jnp.zeros((LANES, LANES), BF16), _block_diag(LANES, 32)], axis=1)], axis=0)
    c64, s64 = _rope_tables(dec_seq, 64)
    c32, s32 = _rope_tables(dec_seq, MLA_ROPE)
    na_bias = _na_bias_tables(na_rpb)

    cvecs = jnp.concatenate([c_ctx[None, :], c, jnp.zeros((8 - 1 - dec_batch, D_MODEL), F32)], axis=0)
    mods = _modulation(cvecs, w_mod, b_mod)

    cache_kpe = jnp.tile(cache_mla_kpe, (1, 1, 1, 4))
    cache_dk = cache_diff_k.transpose(0, 1, 3, 4, 5, 2).reshape(dec_batch, n_even, 512, past)
    cache_dv = cache_diff_v.reshape(dec_batch, n_even, 4 * past, LANES)
    cache_nk = cache_na_k.transpose(0, 1, 3, 4, 2).reshape(dec_batch, n_odd, 512, past)
    cache_nv = cache_na_v.transpose(0, 1, 3, 4, 2).reshape(dec_batch, n_odd, 512, past)
    cache_gk = cache_gqa_k.transpose(0, 1, 3, 4, 2).reshape(dec_batch, n_odd, LANES, past)
    cache_gv = cache_gqa_v.transpose(0, 1, 3, 4, 2).reshape(dec_batch, n_odd, LANES, past)

    even_widths = (1024, 512, LANES, 512, 512, 512, 512, 512, 512)
    odd_widths = (512, 512, 512, 512, 512, LANES, LANES, 512)
    even_states = ((seq, KV_LORA), (MLA_ROPE, seq), (512, seq), (4 * seq, LANES))
    odd_states = ((512, seq), (512, seq), (LANES, seq), (LANES, seq))

    def lam_init(l):
        return 0.8 - 0.6 * math.exp(-0.3 * l)

    def context_pass(x):
        states = [None, None]
        for l in range(DEPTH):
            i = l // 2
            if l % 2 == 0:
                x, *states[0] = _ctx_layer(True, x, mods, l, gains_e[i], (w_in_e[i], wqb[i], wkvb[i]),
                                           (bd64, bd32, bdq), w_out_e[i], (lam_vecs[i], diff_subln[i][None, :]),
                                           even_widths, even_states, states[0], seq, lam_init(l))
            else:
                x, *states[1] = _ctx_layer(False, x, mods, l, gains_o[i], (w_in_o[i],), (bd64,), w_out_o[i], (),
                                           odd_widths, odd_states, states[1], seq)
        return x, states

    def latent_pass(x, nb, t):
        row_base = 1
        bq = Q_BLOCK
        n_pairs = 4
        oa = ob = w_out_prev = None
        for l in range(DEPTH):
            i = l // 2
            if l % 2 == 0:
                proj_args = (gains_e[i], (w_in_e[i], wqb[i], wkvb[i]), (bd64, bd32, bdq), (c64, s64, c32, s32),
                             even_widths)
                proj = _in_even_kernel
            else:
                proj_args = (gains_o[i], (w_in_o[i],), (bd64,), (c64, s64), odd_widths)
                proj = _in_odd_kernel
            if l == 0:
                outs = _in_proj(proj, x, mods, l, *proj_args, row_base, t)
            else:
                x, *outs = _out_in_proj(proj, oa, ob, x, mods, l, w_out_prev, *proj_args, row_base, t)
            if l % 2 == 0:
                qcat, kn, kpe, vm, mg, dq, dk, dv, dg = outs
                kn_c, vm_c = _mla_cache_kv(cache_mla_ckv[:, i].reshape(nb * past, KV_LORA), wkvb[i], bd64, gains_e[i])
                mla_segs = [(_self_seg(kn, t, n_pairs), _self_seg(kpe, t, n_pairs, shared=True),
                             _self_seg(vm, t, n_pairs)),
                            (_self_seg(kn_c, past, n_pairs), _cache_seg(cache_kpe, i, n_pairs, "rows", shared=True),
                             _self_seg(vm_c, past, n_pairs))]
                diff_segs = [(_self_seg(dk, t, n_pairs), _self_seg(dv, t, n_pairs)),
                             (_cache_seg(cache_dk, i, n_pairs, "cols"), _cache_seg(cache_dv, i, n_pairs, "rows4"))]
                oa = _pair_attention("mla", qcat, mla_segs, mg, nb, t, bq, n_pairs, True)
                ob = _pair_attention("diff", dq, diff_segs, dg, nb, t, bq, n_pairs, True,
                                     extra=(lam_vecs[i], diff_subln[i][None, :]), lam_init=lam_init(l))
                w_out_prev = w_out_e[i]
            else:
                nq, nk, nv, ng, gq, gk, gv, gg = outs
                oa = _na_attention(nq, nk, nv, cache_nk, cache_nv, i, na_bias, ng, nb, t)
                gqa_segs = [(_self_seg(gk, t, n_pairs, shared=True), _self_seg(gv, t, n_pairs, shared=True)),
                            (_cache_seg(cache_gk, i, n_pairs, "cols", shared=True),
                             _cache_seg(cache_gv, i, n_pairs, "cols", shared=True))]
                ob = _pair_attention("gqa", gq, gqa_segs, gg, nb, t, bq, n_pairs, True)
                w_out_prev = w_out_o[i]
        return _out_proj(oa, ob, x, mods, DEPTH - 1, w_out_prev, row_base, t)

    y_prompt, st = context_pass(x_prompt.reshape(batch * seq, D_MODEL))
    y_sample = latent_pass(x_sample.reshape(dec_batch * dec_seq, D_MODEL), dec_batch, dec_seq)

    def token_major(a, heads):
        a = a.reshape((batch, n_even) + heads + (a.shape[2] // math.prod(heads), seq))
        return jnp.moveaxis(a, -1, 2)

    ckv, kpe_t, dk_t, dv4 = st[0]
    nk_t, nv_t, gk_t, gv_t = st[1]
    return (y_prompt.reshape(batch, seq, D_MODEL), y_sample.reshape(dec_batch, dec_seq, D_MODEL),
            ckv, token_major(kpe_t, ()), token_major(dk_t, (DIFF_HEADS, 2)),
            dv4.reshape(batch, n_even, seq, DIFF_HEADS, 2 * DIFF_HD),
            token_major(nk_t, (NA_HEADS,)), token_major(nv_t, (NA_HEADS,)),
            token_major(gk_t, (GQA_KV,)), token_major(gv_t, (GQA_KV,)))
```

```python
import functools
import math
from typing import NamedTuple

import jax
import jax.numpy as jnp
import numpy as np
from jax import lax
from jax.experimental import pallas as pl
from jax.experimental.pallas import tpu as pltpu

F32 = jnp.float32
BF16 = jnp.bfloat16

D_MODEL = 1024
DEPTH = 4
GRID_W = 64
ROPE_THETA = 10000.0
EPS = 1e-6
MLA_HEADS = 8
MLA_NOPE = 64
MLA_ROPE = 32
MLA_QK = MLA_NOPE + MLA_ROPE
Q_LORA = 256
KV_LORA = 128
DIFF_HEADS = 4
DIFF_HD = 64
NA_HEADS = 8
NA_HD = 64
NA_ROWS = 8
NA_COLS = 16
GQA_HEADS = 8
GQA_KV = 2
GQA_HD = 64
LANES = 128
NA_GROUP_ROWS = 4
NA_WIN_ROWS = NA_ROWS + NA_GROUP_ROWS
NEG = -1e30
LOG2E = math.log2(math.e)
VMEM_LIMIT = 48 * 1024 * 1024
GQA_PERM = (0, 4, 1, 5, 2, 6, 3, 7)

ROW_TILE = 512
Q_BLOCK = 512


def _cparams():
    return pltpu.CompilerParams(vmem_limit_bytes=VMEM_LIMIT)


def _dot(a, b):
    return jnp.dot(a, b, preferred_element_type=F32)


def _dot_nt(a, b):
    return lax.dot_general(a, b, (((1,), (1,)), ((), ())), preferred_element_type=F32)


def _rms_full(x, g):
    ms = jnp.mean(x * x, axis=-1, keepdims=True)
    return x * lax.rsqrt(ms + EPS) * g


def _rms_group(x, bd, g):
    ms = _dot((x * x).astype(BF16), bd)
    return x * lax.rsqrt(ms + EPS) * g


def _silu(u):
    return u * (1.0 / (1.0 + jnp.exp(-u)))


def _rope(x, cos, sin, group):
    half = group // 2
    rows, width = x.shape
    lane = lax.broadcasted_iota(jnp.int32, (rows, LANES), 1)
    first = (lane & (group - 1)) < half
    outs = []
    for c in range(width // LANES):
        xc = x[:, c * LANES:(c + 1) * LANES]
        rot = jnp.where(first, pltpu.roll(xc, LANES - half, 1), pltpu.roll(xc, half, 1))
        outs.append(xc * cos + rot * sin)
    return outs[0] if len(outs) == 1 else jnp.concatenate(outs, axis=-1)


def _lane_mask(rows, width, lo, hi):
    lane = lax.broadcasted_iota(jnp.int32, (rows, width), 1)
    return (lane >= lo) & (lane < hi)


def _scores(qm, ksegs, bias0=None):
    ss = [_dot(qm, k) if transposed else _dot_nt(qm, k) for k, transposed in ksegs]
    if bias0 is not None:
        ss[0] = ss[0] + bias0
    return ss


def _softmax_pv(ss, vsegs, ones_lanes=None):
    m = jnp.max(ss[0], axis=-1, keepdims=True)
    for s in ss[1:]:
        m = jnp.maximum(m, jnp.max(s, axis=-1, keepdims=True))
    acc = None
    l = None
    for s, (v, transposed) in zip(ss, vsegs):
        p = jnp.exp2(s - m)
        if ones_lanes is None:
            ps = jnp.sum(p, axis=-1, keepdims=True)
            l = ps if l is None else l + ps
        else:
            width = lax.broadcasted_iota(jnp.int32, v.shape, 0 if transposed else 1)
            v = jnp.where((width >= ones_lanes[0]) & (width < ones_lanes[1]), jnp.ones_like(v), v)
        a = _dot_nt(p.astype(BF16), v) if transposed else _dot(p.astype(BF16), v)
        acc = a if acc is None else acc + a
    if ones_lanes is not None:
        l = pltpu.roll(acc, LANES // 2, 1)
    return acc / l


def _mod_kernel(c_ref, w_ref, b_ref, o_ref):
    c = c_ref[...]
    o_ref[0] = _dot(_silu(c).astype(BF16), w_ref[0].astype(BF16)) + b_ref[0]


def _modulation(cvecs, w_mod, b_mod):
    tn = 768
    return pl.pallas_call(
        _mod_kernel,
        out_shape=jax.ShapeDtypeStruct((DEPTH, 8, 3 * D_MODEL), F32),
        grid=(DEPTH, 3 * D_MODEL // tn),
        in_specs=[pl.BlockSpec((8, D_MODEL), lambda l, n: (0, 0)),
                  pl.BlockSpec((1, D_MODEL, tn), lambda l, n: (l, 0, n)),
                  pl.BlockSpec((1, 1, tn), lambda l, n: (l, 0, n))],
        out_specs=pl.BlockSpec((1, 8, tn), lambda l, n: (l, 0, n)),
        compiler_params=_cparams(),
        name="modulation",
    )(cvecs, w_mod, b_mod.reshape(DEPTH, 1, 3 * D_MODEL))


def _modulated_norm(x_ref, mod_ref, g_ref, row):
    mod = mod_ref[0, pl.ds(row, 1), :]
    shift = mod[:, :D_MODEL]
    scale = mod[:, D_MODEL:2 * D_MODEL]
    h = _rms_full(x_ref[...], g_ref[0:1, :]) * (1.0 + scale) + shift
    return h.astype(BF16)


def _mla_kv(cn, wkvb_ref, bd64_ref, g_kn, kn_ref, vm_ref):
    kv = _dot(cn.astype(BF16), wkvb_ref[...])
    for c in range(2):
        sl = slice(256 * c, 256 * (c + 1))
        kn_ref[:, sl] = _rms_group(kv[:, sl], bd64_ref[...], g_kn).astype(BF16)
    vm_ref[...] = kv[:, 512:].astype(BF16)


def _store_rows(ref, x):
    seq = ref.shape[1]
    for b in range(ref.shape[0]):
        ref[b] = x[b * seq:(b + 1) * seq]


def _store_transposed(ref, x, row0, keep=None):
    seq = ref.shape[2]
    for b in range(ref.shape[0]):
        xt = x[b * seq:(b + 1) * seq].T
        if keep is not None:
            xt = xt[:keep]
        ref[b, row0:row0 + xt.shape[0], :] = xt


def _store_heads4(ref, x):
    seq = ref.shape[1] // 4
    for b in range(ref.shape[0]):
        for h in range(4):
            ref[b, pl.ds(h, seq, stride=4), :] = x[b * seq:(b + 1) * seq, LANES * h:LANES * (h + 1)]


def _staggered(hb, w_ref, stages):
    pending = _dot(hb, w_ref[:, stages[0][0]:stages[0][1]])
    for t, (_, _, epilogue) in enumerate(stages):
        current = pending
        if t + 1 < len(stages):
            pending = _dot(hb, w_ref[:, stages[t + 1][0]:stages[t + 1][1]])
        epilogue(current)


def _gate_into(out_ref):
    def epilogue(u):
        out_ref[...] = _silu(u).astype(BF16)
    return epilogue


def _in_even_kernel(*refs, rope, states, n_alias, row_base, tiles_per_batch):
    it = iter(refs)
    x_ref, mod_ref, g_ref, w_ref, wqb_ref, wkvb_ref, bd64_ref, bd32_ref, bdq_ref = (next(it) for _ in range(9))
    if rope:
        c64_ref, s64_ref, c32_ref, s32_ref = (next(it) for _ in range(4))
    for _ in range(n_alias):
        next(it)
    qcat_ref, kn_ref, kpe_ref, vm_ref, mg_ref, dq_ref, dk_ref, dv_ref, dg_ref = (next(it) for _ in range(9))
    if states:
        st_ckv_ref, st_kpe_ref, st_dk_ref, st_dv_ref = (next(it) for _ in range(4))

    row = row_base + pl.program_id(0) // tiles_per_batch
    hb = _modulated_norm(x_ref, mod_ref, g_ref, row)
    bd64 = bd64_ref[...]
    mla_scale = MLA_QK ** -0.5 * LOG2E
    diff_scale = DIFF_HD ** -0.5 * LOG2E

    def mla_queries(qa):
        qa_n = _rms_full(qa, g_ref[1:2, 0:256]).astype(BF16)
        q = _dot(qa_n, wqb_ref[...])
        for p in range(4):
            qp = _rms_group(q[:, 256 * p:256 * (p + 1)], bdq_ref[...], g_ref[2:3, 0:256])
            q_nope = qp[:, :LANES]
            q_pe = qp[:, LANES:]
            if rope:
                q_pe = _rope(q_pe, c32_ref[...], s32_ref[...], MLA_ROPE)
            qcat_ref[:, 256 * p:256 * p + LANES] = (q_nope * mla_scale).astype(BF16)
            qcat_ref[:, 256 * p + LANES:256 * (p + 1)] = (q_pe * mla_scale).astype(BF16)

    def mla_keys(kva):
        c_kv = _rms_full(kva[:, :LANES], g_ref[3:4, 0:LANES])
        k_pe = _rms_group(kva[:, LANES:], bd32_ref[...], g_ref[4:5, 0:LANES])
        if states:
            _store_rows(st_ckv_ref, c_kv)
            _store_transposed(st_kpe_ref, k_pe, 0, keep=MLA_ROPE)
        if rope:
            k_pe = _rope(k_pe, c32_ref[...], s32_ref[...], MLA_ROPE)
        kpe_ref[...] = k_pe.astype(BF16)
        _mla_kv(c_kv, wkvb_ref, bd64_ref, g_ref[5:6, 0:256], kn_ref, vm_ref)

    def diff_queries(dq):
        for c in range(2):
            sl = slice(256 * c, 256 * (c + 1))
            qn = _rms_group(dq[:, sl], bd64, g_ref[6:7, 0:256])
            if rope:
                qn = _rope(qn, c64_ref[...], s64_ref[...], DIFF_HD)
            dq_ref[:, sl] = (qn * diff_scale).astype(BF16)

    def diff_keys(dk):
        for c in range(2):
            sl = slice(256 * c, 256 * (c + 1))
            kn = _rms_group(dk[:, sl], bd64, g_ref[7:8, 0:256])
            if states:
                _store_transposed(st_dk_ref, kn, 256 * c)
            if rope:
                kn = _rope(kn, c64_ref[...], s64_ref[...], DIFF_HD)
            dk_ref[:, sl] = kn.astype(BF16)

    def diff_values(dv):
        if states:
            _store_heads4(st_dv_ref, dv)
        dv_ref[...] = dv.astype(BF16)

    _staggered(hb, w_ref, [(0, 256, mla_queries), (256, 512, mla_keys), (512, 1024, _gate_into(mg_ref)),
                           (1024, 1536, diff_queries), (1536, 2048, diff_keys), (2048, 2560, diff_values),
                           (2560, 3072, _gate_into(dg_ref))])


def _in_odd_kernel(*refs, rope, states, n_alias, row_base, tiles_per_batch):
    it = iter(refs)
    x_ref, mod_ref, g_ref, w_ref, bd64_ref = (next(it) for _ in range(5))
    if rope:
        c64_ref, s64_ref = (next(it) for _ in range(2))
    for _ in range(n_alias):
        next(it)
    nq_ref, nk_ref, nv_ref, ng_ref, gq_ref, gk_ref, gv_ref, gg_ref = (next(it) for _ in range(8))
    if states:
        st_nk_ref, st_nv_ref, st_gk_ref, st_gv_ref = (next(it) for _ in range(4))

    row = row_base + pl.program_id(0) // tiles_per_batch
    hb = _modulated_norm(x_ref, mod_ref, g_ref, row)
    bd64 = bd64_ref[...]
    na_scale = NA_HD ** -0.5 * LOG2E
    gqa_scale = GQA_HD ** -0.5 * LOG2E

    def na_queries(nq):
        for c in range(2):
            sl = slice(256 * c, 256 * (c + 1))
            nq_ref[:, sl] = (_rms_group(nq[:, sl], bd64, g_ref[1:2, 0:256]) * na_scale).astype(BF16)

    def na_keys(nk):
        for c in range(2):
            sl = slice(256 * c, 256 * (c + 1))
            kn = _rms_group(nk[:, sl], bd64, g_ref[2:3, 0:256])
            if states:
                _store_transposed(st_nk_ref, kn, 256 * c)
            nk_ref[:, sl] = kn.astype(BF16)

    def na_values(nv):
        if states:
            for c in range(2):
                _store_transposed(st_nv_ref, nv[:, 256 * c:256 * (c + 1)], 256 * c)
        nv_ref[...] = nv.astype(BF16)

    def gqa_queries(gq):
        for c in range(2):
            sl = slice(256 * c, 256 * (c + 1))
            qn = _rms_group(gq[:, sl], bd64, g_ref[3:4, 0:256])
            if rope:
                qn = _rope(qn, c64_ref[...], s64_ref[...], GQA_HD)
            gq_ref[:, sl] = (qn * gqa_scale).astype(BF16)

    def gqa_keys_values(gkv):
        gk = _rms_group(gkv[:, :LANES], bd64[:LANES, :LANES], g_ref[4:5, 0:LANES])
        gv = gkv[:, LANES:]
        if states:
            _store_transposed(st_gk_ref, gk, 0)
            _store_transposed(st_gv_ref, gv, 0)
        if rope:
            gk = _rope(gk, c64_ref[...], s64_ref[...], GQA_HD)
        gk_ref[...] = gk.astype(BF16)
        gv_ref[...] = gv.astype(BF16)

    _staggered(hb, w_ref, [(0, 512, na_queries), (512, 1024, na_keys), (1024, 1536, na_values),
                           (1536, 2048, _gate_into(ng_ref)), (2048, 2560, gqa_queries),
                           (2560, 2816, gqa_keys_values), (2816, 3328, _gate_into(gg_ref))])


def _full(shape):
    zeros = (0,) * len(shape)
    return pl.BlockSpec(shape, lambda *_: zeros)


class _Layer(NamedTuple):
    stack: jax.Array
    index: int


def _spec(a):
    if isinstance(a, _Layer):
        index = (a.index,) + (0,) * (a.stack.ndim - 1)
        return pl.BlockSpec((None,) + a.stack.shape[1:], lambda *_: index)
    return _full(a.shape)


def _arr(a):
    return a.stack if isinstance(a, _Layer) else a


def _in_proj(kernel, x, mods, layer, gains, weights, consts, ropes, out_widths, row_base, tokens_per_batch):
    rows = x.shape[0]
    tm = ROW_TILE
    tiles_per_batch = tokens_per_batch // tm
    in_specs = [pl.BlockSpec((tm, D_MODEL), lambda i: (i, 0)),
                pl.BlockSpec((1, 8, 3 * D_MODEL), lambda i: (layer, 0, 0)),
                _full(gains.shape)]
    in_specs += [_spec(a) for a in (*weights, *consts)]
    in_specs += [pl.BlockSpec((tm, LANES), lambda i: (i % tiles_per_batch, 0)) for _ in ropes]
    return pl.pallas_call(
        functools.partial(kernel, rope=True, states=False, n_alias=0, row_base=row_base,
                          tiles_per_batch=tiles_per_batch),
        out_shape=[jax.ShapeDtypeStruct((rows, w), BF16) for w in out_widths],
        grid=(rows // tm,),
        in_specs=in_specs,
        out_specs=[pl.BlockSpec((tm, w), lambda i: (i, 0)) for w in out_widths],
        compiler_params=_cparams(),
        name=kernel.__name__.strip("_"),
    )(x, mods, gains, *map(_arr, weights), *consts, *ropes)


def _mla_cache_kernel(c_ref, wkvb_ref, bd64_ref, g_ref, kn_ref, vm_ref):
    _mla_kv(c_ref[...], wkvb_ref, bd64_ref, g_ref[5:6, 0:256], kn_ref, vm_ref)


def _mla_cache_kv(ckv, wkvb, bd64, gains):
    rows = ckv.shape[0]
    return pl.pallas_call(
        _mla_cache_kernel,
        out_shape=[jax.ShapeDtypeStruct((rows, 512), BF16)] * 2,
        grid=(1,),
        in_specs=[_full(ckv.shape), _spec(wkvb), _full(bd64.shape), _full(gains.shape)],
        out_specs=[_full((rows, 512))] * 2,
        compiler_params=_cparams(),
        name="mla_cache_kv",
    )(ckv, _arr(wkvb), bd64, gains)


def _pair_attn_kernel(*refs, mode, kinds, n_pairs, mxu_denominator, lam_init):
    it = iter(refs)
    q_ref = next(it)
    segs = [[next(it) for _ in seg_kinds] for seg_kinds in kinds]
    gate_ref = next(it)
    if mode == "diff":
        lam_ref, subln_ref = next(it), next(it)
    o_ref = next(it)

    qw = 2 * LANES if mode == "mla" else LANES
    rows = q_ref.shape[0]
    lo = _lane_mask(rows, LANES, 0, LANES // 2)
    if mode == "diff":
        lv = lam_ref[...]
        lam = (jnp.exp(jnp.sum(lv[0:1] * lv[1:2], axis=-1, keepdims=True))
               - jnp.exp(jnp.sum(lv[2:3] * lv[3:4], axis=-1, keepdims=True)) + lam_init)

    def load(ref, kind, p):
        if kind == "rows4":
            x = ref[pl.ds(p, ref.shape[0] // 4, stride=4), :]
        elif mode == "gqa":
            x = ref[...]
        else:
            sl = slice(LANES * p, LANES * (p + 1))
            x = ref[sl, :] if kind == "cols" else ref[:, sl]
        return x.astype(BF16), kind == "cols"

    def values(p):
        return [load(seg[-1], seg_kinds[-1], p) for seg, seg_kinds in zip(segs, kinds)]

    def scores(p, j):
        q = q_ref[:, qw * p:qw * (p + 1)]
        ksegs = []
        for seg, seg_kinds in zip(segs, kinds):
            k, transposed = load(seg[0], seg_kinds[0], p)
            if mode == "mla":
                k = jnp.concatenate([k, seg[1][...].astype(BF16)], axis=-1)
            ksegs.append((k, transposed))
        keep = _lane_mask(rows, qw, 64 * j, 64 * (j + 1))
        if mode == "mla":
            keep = keep | _lane_mask(rows, qw, LANES + 32 * j, LANES + 32 * (j + 1))
        return _scores(jnp.where(keep, q, jnp.zeros_like(q)), ksegs)

    def finish(p, outs):
        sl = slice(LANES * p, LANES * (p + 1))
        if mode == "diff":
            d = outs[0] - lam * outs[1]
            o = _rms_full(d, subln_ref[...]) * (1.0 - lam_init)
        else:
            o = jnp.where(lo, outs[0], outs[1])
        o_ref[:, sl] = (o * gate_ref[:, sl].astype(F32)).astype(BF16)

    heads = [(p, j) for p in range(n_pairs) for j in range(2)]
    ss = scores(*heads[0])
    outs = []
    for t, (p, j) in enumerate(heads):
        ss_next = scores(*heads[t + 1]) if t + 1 < len(heads) else None
        spare = (64 * (1 - j), 64 * (2 - j)) if mxu_denominator and mode != "diff" else None
        outs.append(_softmax_pv(ss, values(p), ones_lanes=spare))
        ss = ss_next
        if j == 1:
            finish(p, outs)
            outs = []


def _pair_attention(mode, q, segs, gate, batch, tq, bq, n_pairs, mxu_denominator, extra=(), lam_init=0.0):
    qw = 2 * LANES if mode == "mla" else LANES
    nq = tq // bq
    steps_p = 4 // n_pairs
    in_specs = [pl.BlockSpec((bq, qw * n_pairs), lambda b, p, i: (b * nq + i, p))]
    args = [q]
    for seg in segs:
        for arr, spec, _ in seg:
            args.append(arr)
            in_specs.append(spec)
    kinds = tuple(tuple(kind for _, _, kind in seg) for seg in segs)
    in_specs.append(pl.BlockSpec((bq, LANES * n_pairs), lambda b, p, i: (b * nq + i, p)))
    args.append(gate)
    for arr in extra:
        args.append(arr)
        in_specs.append(pl.BlockSpec(arr.shape, lambda b, p, i: (0,) * arr.ndim))
    return pl.pallas_call(
        functools.partial(_pair_attn_kernel, mode=mode, kinds=kinds, n_pairs=n_pairs,
                          mxu_denominator=mxu_denominator, lam_init=lam_init),
        out_shape=jax.ShapeDtypeStruct((batch * tq, 4 * LANES), BF16),
        grid=(batch, steps_p, nq),
        in_specs=in_specs,
        out_specs=pl.BlockSpec((bq, LANES * n_pairs), lambda b, p, i: (b * nq + i, p)),
        compiler_params=_cparams(),
        name=mode + "_attention",
    )(*args)


def _self_seg(arr, tk, n_pairs, shared=False):
    if shared:
        return arr, pl.BlockSpec((tk, arr.shape[1]), lambda b, p, i: (b, 0)), "rows"
    return arr, pl.BlockSpec((tk, LANES * n_pairs), lambda b, p, i: (b, p)), "rows"


def _cache_seg(arr, layer, n_pairs, kind, shared=False):
    r, c = arr.shape[2:]
    if shared or kind == "rows4":
        return arr, pl.BlockSpec((None, None, r, c), lambda b, p, i: (b, layer, 0, 0)), kind
    if kind == "cols":
        return arr, pl.BlockSpec((None, None, LANES * n_pairs, c), lambda b, p, i: (b, layer, p, 0)), kind
    return arr, pl.BlockSpec((None, None, r, LANES * n_pairs), lambda b, p, i: (b, layer, 0, p)), kind


def _na_kernel(q_ref, k_ref, v_ref, ck_ref, cv_ref, bias_ref, gate_ref, o_ref):
    g = pl.program_id(1)
    n_groups = pl.num_programs(1)
    first_row = jnp.clip(NA_GROUP_ROWS * g - NA_ROWS // 2, 0, NA_GROUP_ROWS * n_groups - NA_WIN_ROWS)
    start = pl.multiple_of(first_row * GRID_W, GRID_W)
    win = NA_WIN_ROWS * GRID_W
    kwin = k_ref[pl.ds(start, win), :]
    vwin = v_ref[pl.ds(start, win), :]
    ck = ck_ref[...].astype(BF16)
    cv = cv_ref[...].astype(BF16)
    q = q_ref[...]
    rows = q.shape[0]
    lo = _lane_mask(rows, LANES, 0, LANES // 2)

    def scores(h):
        sl = slice(LANES * (h // 2), LANES * (h // 2 + 1))
        keep = _lane_mask(rows, LANES, 64 * (h % 2), 64 * (h % 2 + 1))
        qm = jnp.where(keep, q[:, sl], jnp.zeros_like(q[:, sl]))
        return _scores(qm, [(kwin[:, sl], False), (ck[sl, :], True)], bias0=bias_ref[h])

    ss = scores(0)
    outs = []
    for h in range(NA_HEADS):
        ss_next = scores(h + 1) if h + 1 < NA_HEADS else None
        sl = slice(LANES * (h // 2), LANES * (h // 2 + 1))
        j = h % 2
        outs.append(_softmax_pv(ss, [(vwin[:, sl], False), (cv[sl, :], True)],
                                ones_lanes=(64 * (1 - j), 64 * (2 - j))))
        if j == 1:
            o = jnp.where(lo, outs[0], outs[1])
            o_ref[:, sl] = (o * gate_ref[:, sl].astype(F32)).astype(BF16)
            outs = []
        ss = ss_next


def _na_attention(q, k, v, cache_k, cache_v, layer, bias, gate, batch, tq):
    bq = NA_GROUP_ROWS * GRID_W
    n_groups = tq // bq
    past = cache_k.shape[3]
    width = NA_HEADS * NA_HD

    def bias_map(b, g):
        return (layer, jnp.where(g == 0, 0, jnp.where(g == n_groups - 1, 2, 1)), 0, 0, 0)

    tok = pl.BlockSpec((bq, width), lambda b, g: (b * n_groups + g, 0))
    whole = pl.BlockSpec((tq, width), lambda b, g: (b, 0))
    cache = pl.BlockSpec((None, None, width, past), lambda b, g: (b, layer, 0, 0))
    return pl.pallas_call(
        _na_kernel,
        out_shape=jax.ShapeDtypeStruct((batch * tq, width), BF16),
        grid=(batch, n_groups),
        in_specs=[tok, whole, whole, cache, cache,
                  pl.BlockSpec((None, None, NA_HEADS, bq, NA_WIN_ROWS * GRID_W), bias_map), tok],
        out_specs=tok,
        compiler_params=_cparams(),
        name="na_attention",
    )(q, k, v, cache_k, cache_v, bias, gate)


def _na_bias_tables(rpb):
    n_dr = 2 * NA_ROWS - 1
    idx = np.full((3, NA_GROUP_ROWS, NA_WIN_ROWS), n_dr, np.int32)
    for a in range(NA_GROUP_ROWS):
        for j in range(NA_WIN_ROWS):
            if j < NA_ROWS:
                idx[0, a, j] = j - a + NA_ROWS - 1
            if a <= j < a + NA_ROWS:
                idx[1, a, j] = j - a + NA_ROWS // 2 - 1
            if j >= NA_WIN_ROWS - NA_ROWS:
                idx[2, a, j] = j - a - (NA_WIN_ROWS - NA_ROWS) + NA_ROWS // 2 - 1
    layers = rpb.shape[0]
    padded = jnp.pad(rpb.astype(F32), ((0, 0), (0, 0), (0, 1), (0, LANES - rpb.shape[3])))
    return pl.pallas_call(
        functools.partial(_na_bias_kernel, idx=idx),
        out_shape=jax.ShapeDtypeStruct((layers, 3, NA_HEADS, NA_GROUP_ROWS * GRID_W, NA_WIN_ROWS * GRID_W), F32),
        grid=(layers, NA_HEADS),
        in_specs=[pl.BlockSpec((None, None, n_dr + 1, LANES), lambda l, h: (l, h, 0, 0))],
        out_specs=pl.BlockSpec((None, 3, None, NA_GROUP_ROWS * GRID_W, NA_WIN_ROWS * GRID_W),
                               lambda l, h: (l, 0, h, 0, 0)),
        compiler_params=_cparams(),
        name="na_bias",
    )(padded)


def _na_bias_kernel(rpb_ref, out_ref, *, idx):
    n_dr = rpb_ref.shape[0] - 1
    qc = lax.broadcasted_iota(jnp.int32, (GRID_W, LANES), 0)
    lane = lax.broadcasted_iota(jnp.int32, (GRID_W, LANES), 1)
    kc = lane & (GRID_W - 1)
    first = jnp.clip(qc - NA_COLS // 2, 0, GRID_W - NA_COLS)
    valid = (kc >= first) & (kc < first + NA_COLS)
    low = lane < GRID_W
    outside = jnp.full((GRID_W, LANES), NEG, F32)

    def table(r):
        if r == n_dr:
            return outside
        row = jnp.broadcast_to(rpb_ref[r:r + 1, :], (GRID_W, LANES))
        lo = pltpu.roll(row, LANES - (NA_COLS - 1), 1, stride=1, stride_axis=0)
        hi = pltpu.roll(row, GRID_W - (NA_COLS - 1), 1, stride=1, stride_axis=0)
        return jnp.where(valid, jnp.where(low, lo, hi) * LOG2E, NEG)

    tables = [table(r) for r in range(n_dr + 1)]
    for t in range(3):
        for a in range(NA_GROUP_ROWS):
            for j in range(0, NA_WIN_ROWS, 2):
                pair = jnp.where(low, tables[int(idx[t, a, j])], tables[int(idx[t, a, j + 1])])
                out_ref[t, a * GRID_W:(a + 1) * GRID_W, j * GRID_W:(j + 2) * GRID_W] = pair


def _out_kernel(oa_ref, ob_ref, x_ref, mod_ref, w_ref, y_ref, *, row_base, tiles_per_batch):
    row = row_base + pl.program_id(0) // tiles_per_batch
    gate = mod_ref[0, pl.ds(row, 1), :][:, 2 * D_MODEL:]
    half = oa_ref.shape[1]
    acc = _dot(oa_ref[...], w_ref[:half, :]) + _dot(ob_ref[...], w_ref[half:, :])
    y_ref[...] = x_ref[...] + gate * acc


def _out_proj(oa, ob, x, mods, layer, w, row_base, tokens_per_batch):
    rows = x.shape[0]
    tm = ROW_TILE
    tiles_per_batch = tokens_per_batch // tm
    return pl.pallas_call(
        functools.partial(_out_kernel, row_base=row_base, tiles_per_batch=tiles_per_batch),
        out_shape=jax.ShapeDtypeStruct((rows, D_MODEL), F32),
        grid=(rows // tm,),
        in_specs=[pl.BlockSpec((tm, oa.shape[1]), lambda i: (i, 0)),
                  pl.BlockSpec((tm, ob.shape[1]), lambda i: (i, 0)),
                  pl.BlockSpec((tm, D_MODEL), lambda i: (i, 0)),
                  pl.BlockSpec((1, 8, 3 * D_MODEL), lambda i: (layer, 0, 0)),
                  _spec(w)],
        out_specs=pl.BlockSpec((tm, D_MODEL), lambda i: (i, 0)),
        compiler_params=_cparams(),
        name="out_proj",
    )(oa, ob, x, mods, _arr(w))


def _out_in_kernel(*refs, proj, rope, row_base, tiles_per_batch):
    oa_ref, ob_ref, x_ref, mod_prev_ref, wout_ref = refs[:5]
    n_proj_in = len(refs) - 5 - 1 - (9 if proj is _in_even_kernel else 8)
    proj_in = refs[5:5 + n_proj_in]
    y_ref = refs[5 + n_proj_in]
    slabs = refs[6 + n_proj_in:]
    _out_kernel(oa_ref, ob_ref, x_ref, mod_prev_ref, wout_ref, y_ref, row_base=row_base,
                tiles_per_batch=tiles_per_batch)
    proj(y_ref, *proj_in, *slabs, rope=rope, states=False, n_alias=0, row_base=row_base,
         tiles_per_batch=tiles_per_batch)


def _out_in_proj(proj, oa, ob, x, mods, layer, w_out, gains, weights, consts, ropes, out_widths, row_base,
                 tokens_per_batch):
    rows = x.shape[0]
    tm = ROW_TILE
    tiles_per_batch = tokens_per_batch // tm
    row = pl.BlockSpec((tm, D_MODEL), lambda i: (i, 0))
    in_specs = [pl.BlockSpec((tm, oa.shape[1]), lambda i: (i, 0)), pl.BlockSpec((tm, ob.shape[1]), lambda i: (i, 0)),
                row, pl.BlockSpec((1, 8, 3 * D_MODEL), lambda i: (layer - 1, 0, 0)), _spec(w_out),
                pl.BlockSpec((1, 8, 3 * D_MODEL), lambda i: (layer, 0, 0)), _full(gains.shape)]
    in_specs += [_spec(a) for a in (*weights, *consts)]
    in_specs += [pl.BlockSpec((tm, LANES), lambda i: (i % tiles_per_batch, 0)) for _ in ropes]
    return pl.pallas_call(
        functools.partial(_out_in_kernel, proj=proj, rope=True, row_base=row_base, tiles_per_batch=tiles_per_batch),
        out_shape=[jax.ShapeDtypeStruct((rows, D_MODEL), F32)]
        + [jax.ShapeDtypeStruct((rows, w), BF16) for w in out_widths],
        grid=(rows // tm,),
        in_specs=in_specs,
        out_specs=[row] + [pl.BlockSpec((tm, w), lambda i: (i, 0)) for w in out_widths],
        compiler_params=_cparams(),
        name="out_" + proj.__name__.strip("_"),
    )(oa, ob, x, mods, _arr(w_out), mods, gains, *map(_arr, weights), *consts, *ropes)


def _ctx_layer_kernel(*refs, even, n_in, n_alias, n_slabs, seq, lam_init, tiles_per_batch):
    in_refs = refs[:n_in]
    it = iter(refs[n_in:])
    wout_ref = next(it)
    if even:
        lam_ref, subln_ref = next(it), next(it)
    alias_refs = [next(it) for _ in range(n_alias)]
    y_ref = next(it)
    state_refs = [next(it) for _ in range(4)]
    slabs = [next(it) for _ in range(n_slabs)]
    oa_ref, ob_ref = next(it), next(it)
    x_ref, mod_ref = in_refs[0], in_refs[1]

    proj = _in_even_kernel if even else _in_odd_kernel
    proj(*in_refs, *alias_refs, *slabs, *state_refs, rope=False, states=True, n_alias=n_alias, row_base=0,
         tiles_per_batch=tiles_per_batch)
    for b in range(x_ref.shape[0] // seq):
        def own(ref):
            return ref.at[pl.ds(b * seq, seq), :]
        common = dict(n_pairs=4, mxu_denominator=False)
        if even:
            qcat, kn, kpe, vm, mg, dq, dk, dv, dg = (own(r) for r in slabs)
            _pair_attn_kernel(qcat, kn, kpe, vm, mg, own(oa_ref), mode="mla", kinds=(("rows",) * 3,),
                              lam_init=0.0, **common)
            _pair_attn_kernel(dq, dk, dv, dg, lam_ref, subln_ref, own(ob_ref), mode="diff", kinds=(("rows",) * 2,),
                              lam_init=lam_init, **common)
        else:
            nq, nk, nv, ng, gq, gk, gv, gg = (own(r) for r in slabs)
            _pair_attn_kernel(nq, nk, nv, ng, own(oa_ref), mode="mha", kinds=(("rows",) * 2,), lam_init=0.0, **common)
            _pair_attn_kernel(gq, gk, gv, gg, own(ob_ref), mode="gqa", kinds=(("rows",) * 2,), lam_init=0.0, **common)
    _out_kernel(oa_ref, ob_ref, x_ref, mod_ref, wout_ref, y_ref, row_base=0, tiles_per_batch=tiles_per_batch)


def _ctx_layer(even, x, mods, layer, gains, weights, consts, w_out, extras, slab_widths, state_tails, state_prev,
               seq, lam_init=0.0):
    rows = x.shape[0]
    tm = ROW_TILE
    slot = layer // 2
    in_specs = [pl.BlockSpec((tm, D_MODEL), lambda i: (i, 0)),
                pl.BlockSpec((1, 8, 3 * D_MODEL), lambda i: (layer, 0, 0)),
                _full(gains.shape)]
    in_specs += [_spec(a) for a in (*weights, *consts)]
    args = [x, mods, gains, *map(_arr, weights), *consts]
    n_in = len(args)
    in_specs += [_spec(a) for a in (w_out, *extras)]
    args += [_arr(w_out), *extras]
    aliases = {}
    if state_prev is not None:
        aliases = {len(args) + j: 1 + j for j in range(len(state_prev))}
        in_specs += [pl.BlockSpec(memory_space=pl.ANY) for _ in state_prev]
        args += list(state_prev)
    out_shape = [jax.ShapeDtypeStruct((rows, D_MODEL), F32)]
    out_specs = [pl.BlockSpec((tm, D_MODEL), lambda i: (i, 0))]
    out_shape += [jax.ShapeDtypeStruct((rows // seq, DEPTH // 2) + tail, F32) for tail in state_tails]
    out_specs += [pl.BlockSpec((tm // seq, None) + tail, lambda i: (i, slot, 0, 0)) for tail in state_tails]
    scratch = [pltpu.VMEM((tm, w), BF16) for w in (*slab_widths, 4 * LANES, 4 * LANES)]
    return pl.pallas_call(
        functools.partial(_ctx_layer_kernel, even=even, n_in=n_in, n_alias=len(aliases), n_slabs=len(slab_widths),
                          seq=seq, lam_init=lam_init, tiles_per_batch=rows // tm),
        out_shape=out_shape,
        grid=(rows // tm,),
        in_specs=in_specs,
        out_specs=out_specs,
        scratch_shapes=scratch,
        input_output_aliases=aliases,
        compiler_params=_cparams(),
        name="ctx_layer_even" if even else "ctx_layer_odd",
    )(*args)


def _layout_in_even_kernel(wt_ref, o_ref):
    kpe0 = Q_LORA + KV_LORA
    o_ref[:, 0:kpe0] = wt_ref[0:kpe0, :].T.astype(BF16)
    kpe = wt_ref[kpe0:kpe0 + MLA_ROPE, :]
    o_ref[:, kpe0:kpe0 + LANES] = jnp.concatenate([kpe] * (LANES // MLA_ROPE), axis=0).T.astype(BF16)
    o_ref[:, kpe0 + LANES:] = wt_ref[kpe0 + MLA_ROPE:, :].T.astype(BF16)


def _layout_in_even(w):
    layers, k, cols = w.shape
    kc = 256
    return pl.pallas_call(
        _layout_in_even_kernel,
        out_shape=jax.ShapeDtypeStruct((layers, k, cols + LANES - MLA_ROPE), BF16),
        grid=(layers, k // kc),
        in_specs=[pl.BlockSpec((None, cols, kc), lambda l, r: (l, 0, r))],
        out_specs=pl.BlockSpec((None, kc, cols + LANES - MLA_ROPE), lambda l, r: (l, r, 0)),
        compiler_params=_cparams(),
        name="layout_in_even",
    )(w.transpose(0, 2, 1))


def _layout_in_odd_kernel(w_ref, o_ref):
    def permuted(base):
        for k in range(GQA_HEADS // 2):
            lo_head, hi_head = GQA_PERM[2 * k], GQA_PERM[2 * k + 1]
            lo = w_ref[:, base + LANES * (lo_head // 2):base + LANES * (lo_head // 2 + 1)]
            hi = w_ref[:, base + LANES * (hi_head // 2):base + LANES * (hi_head // 2 + 1)]
            if lo_head % 2 == 1:
                lo = pltpu.roll(lo, LANES // 2, 1)
            if hi_head % 2 == 0:
                hi = pltpu.roll(hi, LANES // 2, 1)
            lane = lax.broadcasted_iota(jnp.int32, lo.shape, 1)
            o_ref[:, base + LANES * k:base + LANES * (k + 1)] = jnp.where(lane < LANES // 2, lo, hi).astype(BF16)

    o_ref[:, 0:2048] = w_ref[:, 0:2048].astype(BF16)
    permuted(2048)
    o_ref[:, 2560:2816] = w_ref[:, 2560:2816].astype(BF16)
    permuted(2816)


def _layout_w_out_kernel(w_ref, o_ref, *, permute):
    half = w_ref.shape[0] // 2
    o_ref[:half, :] = w_ref[:half, :].astype(BF16)
    if permute:
        for k, head in enumerate(GQA_PERM):
            o_ref[half + GQA_HD * k:half + GQA_HD * (k + 1), :] = (
                w_ref[half + GQA_HD * head:half + GQA_HD * (head + 1), :].astype(BF16))
    else:
        o_ref[half:, :] = w_ref[half:, :].astype(BF16)


def _layout_in_odd(w):
    layers, rows, cols = w.shape
    rb = 256
    return pl.pallas_call(
        _layout_in_odd_kernel,
        out_shape=jax.ShapeDtypeStruct((layers, rows, cols), BF16),
        grid=(layers, rows // rb),
        in_specs=[pl.BlockSpec((None, rb, cols), lambda l, r: (l, r, 0))],
        out_specs=pl.BlockSpec((None, rb, cols), lambda l, r: (l, r, 0)),
        compiler_params=_cparams(),
        name="layout_in_odd",
    )(w)


def _layout_w_out(w, permute):
    layers, rows, cols = w.shape
    return pl.pallas_call(
        functools.partial(_layout_w_out_kernel, permute=permute),
        out_shape=jax.ShapeDtypeStruct((layers, rows, cols), BF16),
        grid=(layers,),
        in_specs=[pl.BlockSpec((None, rows, cols), lambda l: (l, 0, 0))],
        out_specs=pl.BlockSpec((None, rows, cols), lambda l: (l, 0, 0)),
        compiler_params=_cparams(),
        name="layout_w_out",
    )(w)


def _block_diag(width, group):
    idx = np.arange(width) // group
    return jnp.asarray((idx[:, None] == idx[None, :]).astype(np.float32) / group, BF16)


def _rope_tables(t, rot_dim):
    pos = np.arange(t)
    row = (pos // GRID_W).astype(np.float64)
    col = (pos % GRID_W).astype(np.float64)
    n = rot_dim // 2
    inv = ROPE_THETA ** (-np.arange(0, n, 2, dtype=np.float64) / n)
    ang = np.concatenate([row[:, None] * inv, col[:, None] * inv], axis=-1)
    cos = np.concatenate([np.cos(ang), np.cos(ang)], axis=-1)
    sin = np.concatenate([-np.sin(ang), np.sin(ang)], axis=-1)
    reps = LANES // rot_dim
    return (jnp.asarray(np.tile(cos, (1, reps)), F32), jnp.asarray(np.tile(sin, (1, reps)), F32))


def _pad_row(v, width=D_MODEL):
    return jnp.pad(v, (0, width - v.shape[0]))


def _tile_row(v, reps):
    return _pad_row(jnp.tile(v, reps))


def kernel(x_prompt, x_sample, cache_mla_ckv, cache_mla_kpe, cache_diff_k, cache_diff_v, cache_na_k, cache_na_v, cache_gqa_k, cache_gqa_v, c, c_ctx, norm_w, w_mod, b_mod, w_in_even, w_out_even, mla_qa_norm, mla_wqb, mla_kva_norm, mla_wkvb, mla_qn_nope, mla_qn_rope, mla_kn_nope, mla_kn_rope, diff_qn, diff_kn, diff_lq1, diff_lk1, diff_lq2, diff_lk2, diff_subln, w_in_odd, w_out_odd, na_qn, na_kn, na_rpb, gqa_qn, gqa_kn):
    batch, seq, _ = x_prompt.shape
    dec_batch, dec_seq, _ = x_sample.shape
    past = cache_mla_ckv.shape[2]
    n_even, n_odd = w_in_even.shape[0], w_in_odd.shape[0]

    def layers(stack):
        return [_Layer(stack, i) for i in range(stack.shape[0])]

    w_in_e = layers(_layout_in_even(w_in_even))
    w_in_o = layers(_layout_in_odd(w_in_odd))
    w_out_e = layers(_layout_w_out(w_out_even, permute=False))
    w_out_o = layers(_layout_w_out(w_out_odd, permute=True))
    wqb = mla_wqb.reshape(n_even, Q_LORA, MLA_HEADS, MLA_QK)
    wqb = jnp.concatenate([wqb[..., :MLA_NOPE].reshape(n_even, Q_LORA, 4, 2 * MLA_NOPE),
                           wqb[..., MLA_NOPE:].reshape(n_even, Q_LORA, 4, 2 * MLA_ROPE),
                           jnp.zeros((n_even, Q_LORA, 4, LANES - 2 * MLA_ROPE), F32)], axis=-1)
    wqb = layers(wqb.reshape(n_even, Q_LORA, 4 * 2 * LANES).astype(BF16))
    wkvb = mla_wkvb.reshape(n_even, KV_LORA, MLA_HEADS, 2 * MLA_NOPE)
    wkvb = layers(jnp.concatenate([wkvb[..., :MLA_NOPE].reshape(n_even, KV_LORA, 512),
                                   wkvb[..., MLA_NOPE:].reshape(n_even, KV_LORA, 512)], axis=-1).astype(BF16))

    gains_e = [jnp.stack([norm_w[2 * i], _pad_row(mla_qa_norm[i]),
                          _pad_row(jnp.concatenate([jnp.tile(mla_qn_nope[i], 2), jnp.tile(mla_qn_rope[i], 4)])),
                          _pad_row(mla_kva_norm[i]), _tile_row(mla_kn_rope[i], 4), _tile_row(mla_kn_nope[i], 4),
                          _tile_row(diff_qn[i], 4), _tile_row(diff_kn[i], 4)]) for i in range(n_even)]
    gains_o = [jnp.stack([norm_w[2 * i + 1], _tile_row(na_qn[i], 4), _tile_row(na_kn[i], 4),
                          _tile_row(gqa_qn[i], 4), _tile_row(gqa_kn[i], 2),
                          jnp.zeros((D_MODEL,), F32), jnp.zeros((D_MODEL,), F32), jnp.zeros((D_MODEL,), F32)])
               for i in range(n_odd)]
    lam_vecs = [jnp.stack([diff_lq1[i], diff_lk1[i], diff_lq2[i], diff_lk2[i]]) for i in range(n_even)]

    bd64 = _block_diag(256, 64)
    bd32 = _block_diag(LANES, 32)
    bdq = jnp.concatenate([jnp.concatenate([_block_diag(LANES, 64), jnp.zeros((LANES, LANES), BF16)], axis=1),
                           jnp.concatenate([jnp.zeros((LANES, LANES), BF16), _block_diag(LANES, 32)], axis=1)], axis=0)
    c64, s64 = _rope_tables(dec_seq, 64)
    c32, s32 = _rope_tables(dec_seq, MLA_ROPE)
    na_bias = _na_bias_tables(na_rpb)

    cvecs = jnp.concatenate([c_ctx[None, :], c, jnp.zeros((8 - 1 - dec_batch, D_MODEL), F32)], axis=0)
    mods = _modulation(cvecs, w_mod, b_mod)

    cache_kpe = jnp.tile(cache_mla_kpe, (1, 1, 1, 4))
    cache_dk = cache_diff_k.transpose(0, 1, 3, 4, 5, 2).reshape(dec_batch, n_even, 512, past)
    cache_dv = cache_diff_v.reshape(dec_batch, n_even, 4 * past, LANES)
    cache_nk = cache_na_k.transpose(0, 1, 3, 4, 2).reshape(dec_batch, n_odd, 512, past)
    cache_nv = cache_na_v.transpose(0, 1, 3, 4, 2).reshape(dec_batch, n_odd, 512, past)
    cache_gk = cache_gqa_k.transpose(0, 1, 3, 4, 2).reshape(dec_batch, n_odd, LANES, past)
    cache_gv = cache_gqa_v.transpose(0, 1, 3, 4, 2).reshape(dec_batch, n_odd, LANES, past)

    even_widths = (1024, 512, LANES, 512, 512, 512, 512, 512, 512)
    odd_widths = (512, 512, 512, 512, 512, LANES, LANES, 512)
    even_states = ((seq, KV_LORA), (MLA_ROPE, seq), (512, seq), (4 * seq, LANES))
    odd_states = ((512, seq), (512, seq), (LANES, seq), (LANES, seq))

    def lam_init(l):
        return 0.8 - 0.6 * math.exp(-0.3 * l)

    def context_pass(x):
        states = [None, None]
        for l in range(DEPTH):
            i = l // 2
            if l % 2 == 0:
                x, *states[0] = _ctx_layer(True, x, mods, l, gains_e[i], (w_in_e[i], wqb[i], wkvb[i]),
                                           (bd64, bd32, bdq), w_out_e[i], (lam_vecs[i], diff_subln[i][None, :]),
                                           even_widths, even_states, states[0], seq, lam_init(l))
            else:
                x, *states[1] = _ctx_layer(False, x, mods, l, gains_o[i], (w_in_o[i],), (bd64,), w_out_o[i], (),
                                           odd_widths, odd_states, states[1], seq)
        return x, states

    def latent_pass(x, nb, t):
        row_base = 1
        bq = Q_BLOCK
        n_pairs = 4
        oa = ob = w_out_prev = None
        for l in range(DEPTH):
            i = l // 2
            if l % 2 == 0:
                proj_args = (gains_e[i], (w_in_e[i], wqb[i], wkvb[i]), (bd64, bd32, bdq), (c64, s64, c32, s32),
                             even_widths)
                proj = _in_even_kernel
            else:
                proj_args = (gains_o[i], (w_in_o[i],), (bd64,), (c64, s64), odd_widths)
                proj = _in_odd_kernel
            if l == 0:
                outs = _in_proj(proj, x, mods, l, *proj_args, row_base, t)
            else:
                x, *outs = _out_in_proj(proj, oa, ob, x, mods, l, w_out_prev, *proj_args, row_base, t)
            if l % 2 == 0:
                qcat, kn, kpe, vm, mg, dq, dk, dv, dg = outs
                kn_c, vm_c = _mla_cache_kv(cache_mla_ckv[:, i].reshape(nb * past, KV_LORA), wkvb[i], bd64, gains_e[i])
                mla_segs = [(_self_seg(kn, t, n_pairs), _self_seg(kpe, t, n_pairs, shared=True),
                             _self_seg(vm, t, n_pairs)),
                            (_self_seg(kn_c, past, n_pairs), _cache_seg(cache_kpe, i, n_pairs, "rows", shared=True),
                             _self_seg(vm_c, past, n_pairs))]
                diff_segs = [(_self_seg(dk, t, n_pairs), _self_seg(dv, t, n_pairs)),
                             (_cache_seg(cache_dk, i, n_pairs, "cols"), _cache_seg(cache_dv, i, n_pairs, "rows4"))]
                oa = _pair_attention("mla", qcat, mla_segs, mg, nb, t, bq, n_pairs, True)
                ob = _pair_attention("diff", dq, diff_segs, dg, nb, t, bq, n_pairs, True,
                                     extra=(lam_vecs[i], diff_subln[i][None, :]), lam_init=lam_init(l))
                w_out_prev = w_out_e[i]
            else:
                nq, nk, nv, ng, gq, gk, gv, gg = outs
                oa = _na_attention(nq, nk, nv, cache_nk, cache_nv, i, na_bias, ng, nb, t)
                gqa_segs = [(_self_seg(gk, t, n_pairs, shared=True), _self_seg(gv, t, n_pairs, shared=True)),
                            (_cache_seg(cache_gk, i, n_pairs, "cols", shared=True),
                             _cache_seg(cache_gv, i, n_pairs, "cols", shared=True))]
                ob = _pair_attention("gqa", gq, gqa_segs, gg, nb, t, bq, n_pairs, True)
                w_out_prev = w_out_o[i]
        return _out_proj(oa, ob, x, mods, DEPTH - 1, w_out_prev, row_base, t)

    y_prompt, st = context_pass(x_prompt.reshape(batch * seq, D_MODEL))
    y_sample = latent_pass(x_sample.reshape(dec_batch * dec_seq, D_MODEL), dec_batch, dec_seq)

    def token_major(a, heads):
        a = a.reshape((batch, n_even) + heads + (a.shape[2] // math.prod(heads), seq))
        return jnp.moveaxis(a, -1, 2)

    ckv, kpe_t, dk_t, dv4 = st[0]
    nk_t, nv_t, gk_t, gv_t = st[1]
    return (y_prompt.reshape(batch, seq, D_MODEL), y_sample.reshape(dec_batch, dec_seq, D_MODEL),
            ckv, token_major(kpe_t, ()), token_major(dk_t, (DIFF_HEADS, 2)),
            dv4.reshape(batch, n_even, seq, DIFF_HEADS, 2 * DIFF_HD),
            token_major(nk_t, (NA_HEADS,)), token_major(nv_t, (NA_HEADS,)),
            token_major(gk_t, (GQA_KV,)), token_major(gv_t, (GQA_KV,)))
```

```python
import functools
import math
from typing import NamedTuple

import jax
import jax.numpy as jnp
import numpy as np
from jax import lax
from jax.experimental import pallas as pl
from jax.experimental.pallas import tpu as pltpu

F32 = jnp.float32
BF16 = jnp.bfloat16

D_MODEL = 1024
DEPTH = 4
GRID_W = 64
ROPE_THETA = 10000.0
EPS = 1e-6
MLA_HEADS = 8
MLA_NOPE = 64
MLA_ROPE = 32
MLA_QK = MLA_NOPE + MLA_ROPE
Q_LORA = 256
KV_LORA = 128
DIFF_HEADS = 4
DIFF_HD = 64
NA_HEADS = 8
NA_HD = 64
NA_ROWS = 8
NA_COLS = 16
GQA_HEADS = 8
GQA_KV = 2
GQA_HD = 64
LANES = 128
PAIRS = 4
NA_GROUP_ROWS = 4
NA_WIN_ROWS = NA_ROWS + NA_GROUP_ROWS
NEG = -1e30
LOG2E = math.log2(math.e)
VMEM_LIMIT = 48 * 1024 * 1024
GQA_PERM = (0, 4, 1, 5, 2, 6, 3, 7)

ROW_TILE = 512
Q_BLOCK = 512


def _cparams():
    return pltpu.CompilerParams(vmem_limit_bytes=VMEM_LIMIT)


def _dot(a, b):
    return jnp.dot(a, b, preferred_element_type=F32)


def _dot_nt(a, b):
    return lax.dot_general(a, b, (((1,), (1,)), ((), ())), preferred_element_type=F32)


def _rms_full(x, g):
    ms = jnp.mean(x * x, axis=-1, keepdims=True)
    return x * lax.rsqrt(ms + EPS) * g


def _rms_group(x, bd, g):
    ms = _dot((x * x).astype(BF16), bd)
    return x * lax.rsqrt(ms + EPS) * g


def _silu(u):
    return u * (1.0 / (1.0 + jnp.exp(-u)))


def _rope(x, cos, sin, group):
    half = group // 2
    rows, width = x.shape
    lane = lax.broadcasted_iota(jnp.int32, (rows, LANES), 1)
    first = (lane & (group - 1)) < half
    outs = []
    for c in range(width // LANES):
        xc = x[:, c * LANES:(c + 1) * LANES]
        rot = jnp.where(first, pltpu.roll(xc, LANES - half, 1), pltpu.roll(xc, half, 1))
        outs.append(xc * cos + rot * sin)
    return outs[0] if len(outs) == 1 else jnp.concatenate(outs, axis=-1)


def _lane_mask(rows, width, lo, hi):
    lane = lax.broadcasted_iota(jnp.int32, (rows, width), 1)
    return (lane >= lo) & (lane < hi)


def _scores(qm, ksegs, bias0=None):
    ss = [_dot(qm, k) if transposed else _dot_nt(qm, k) for k, transposed in ksegs]
    if bias0 is not None:
        ss[0] = ss[0] + bias0
    return ss


def _softmax_pv(ss, vsegs, ones_lanes=None):
    m = jnp.max(ss[0], axis=-1, keepdims=True)
    for s in ss[1:]:
        m = jnp.maximum(m, jnp.max(s, axis=-1, keepdims=True))
    acc = None
    l = None
    for s, (v, transposed) in zip(ss, vsegs):
        p = jnp.exp2(s - m)
        if ones_lanes is None:
            ps = jnp.sum(p, axis=-1, keepdims=True)
            l = ps if l is None else l + ps
        else:
            width = lax.broadcasted_iota(jnp.int32, v.shape, 0 if transposed else 1)
            v = jnp.where((width >= ones_lanes[0]) & (width < ones_lanes[1]), jnp.ones_like(v), v)
        a = _dot_nt(p.astype(BF16), v) if transposed else _dot(p.astype(BF16), v)
        acc = a if acc is None else acc + a
    if ones_lanes is not None:
        l = pltpu.roll(acc, LANES // 2, 1)
    return acc / l


def _mod_kernel(c_ref, w_ref, b_ref, o_ref):
    c = c_ref[...]
    o_ref[0] = _dot(_silu(c).astype(BF16), w_ref[0].astype(BF16)) + b_ref[0]


def _modulation(cvecs, w_mod, b_mod):
    tn = 1536
    return pl.pallas_call(
        _mod_kernel,
        out_shape=jax.ShapeDtypeStruct((DEPTH, 8, 3 * D_MODEL), F32),
        grid=(DEPTH, 3 * D_MODEL // tn),
        in_specs=[pl.BlockSpec((8, D_MODEL), lambda l, n: (0, 0)),
                  pl.BlockSpec((1, D_MODEL, tn), lambda l, n: (l, 0, n)),
                  pl.BlockSpec((1, 1, tn), lambda l, n: (l, 0, n))],
        out_specs=pl.BlockSpec((1, 8, tn), lambda l, n: (l, 0, n)),
        compiler_params=_cparams(),
        name="modulation",
    )(cvecs, w_mod, b_mod.reshape(DEPTH, 1, 3 * D_MODEL))


def _modulated_norm(x_ref, mod_ref, g_ref, row):
    mod = mod_ref[0, pl.ds(row, 1), :]
    shift = mod[:, :D_MODEL]
    scale = mod[:, D_MODEL:2 * D_MODEL]
    h = _rms_full(x_ref[...], g_ref[0:1, :]) * (1.0 + scale) + shift
    return h.astype(BF16)


def _mla_kv(cn, wkvb_ref, bd64_ref, g_kn, kn_ref, vm_ref):
    kv = _dot(cn.astype(BF16), wkvb_ref[...])
    for c in range(2):
        sl = slice(256 * c, 256 * (c + 1))
        kn_ref[:, sl] = _rms_group(kv[:, sl], bd64_ref[...], g_kn).astype(BF16)
    vm_ref[...] = kv[:, 512:].astype(BF16)


def _store_rows(ref, x):
    seq = ref.shape[1]
    for b in range(ref.shape[0]):
        ref[b] = x[b * seq:(b + 1) * seq]


def _store_transposed(ref, x, row0, keep=None):
    seq = ref.shape[2]
    for b in range(ref.shape[0]):
        xt = x[b * seq:(b + 1) * seq].T
        if keep is not None:
            xt = xt[:keep]
        ref[b, row0:row0 + xt.shape[0], :] = xt


def _store_heads4(ref, x):
    seq = ref.shape[1] // 4
    for b in range(ref.shape[0]):
        for h in range(4):
            ref[b, pl.ds(h, seq, stride=4), :] = x[b * seq:(b + 1) * seq, LANES * h:LANES * (h + 1)]


def _staggered(hb, w_ref, stages):
    pending = _dot(hb, w_ref[:, stages[0][0]:stages[0][1]])
    for t, (_, _, epilogue) in enumerate(stages):
        current = pending
        if t + 1 < len(stages):
            pending = _dot(hb, w_ref[:, stages[t + 1][0]:stages[t + 1][1]])
        epilogue(current)


def _gate_into(out_ref):
    def epilogue(u):
        out_ref[...] = _silu(u).astype(BF16)
    return epilogue


def _in_even_kernel(*refs, rope, states, n_alias, row_base, tiles_per_batch):
    it = iter(refs)
    x_ref, mod_ref, g_ref, w_ref, wqb_ref, wkvb_ref, bd64_ref, bd32_ref, bdq_ref = (next(it) for _ in range(9))
    if rope:
        c64_ref, s64_ref, c32_ref, s32_ref = (next(it) for _ in range(4))
    for _ in range(n_alias):
        next(it)
    qcat_ref, kn_ref, kpe_ref, vm_ref, mg_ref, dq_ref, dk_ref, dv_ref, dg_ref = (next(it) for _ in range(9))
    if states:
        st_ckv_ref, st_kpe_ref, st_dk_ref, st_dv_ref = (next(it) for _ in range(4))

    row = row_base + pl.program_id(0) // tiles_per_batch
    hb = _modulated_norm(x_ref, mod_ref, g_ref, row)
    bd64 = bd64_ref[...]
    mla_scale = MLA_QK ** -0.5 * LOG2E
    diff_scale = DIFF_HD ** -0.5 * LOG2E

    def mla_queries(qa):
        qa_n = _rms_full(qa, g_ref[1:2, 0:256]).astype(BF16)
        q = _dot(qa_n, wqb_ref[...])
        for p in range(4):
            qp = _rms_group(q[:, 256 * p:256 * (p + 1)], bdq_ref[...], g_ref[2:3, 0:256])
            q_nope = qp[:, :LANES]
            q_pe = qp[:, LANES:]
            if rope:
                q_pe = _rope(q_pe, c32_ref[...], s32_ref[...], MLA_ROPE)
            qcat_ref[:, 256 * p:256 * p + LANES] = (q_nope * mla_scale).astype(BF16)
            qcat_ref[:, 256 * p + LANES:256 * (p + 1)] = (q_pe * mla_scale).astype(BF16)

    def mla_keys(kva):
        c_kv = _rms_full(kva[:, :LANES], g_ref[3:4, 0:LANES])
        k_pe = _rms_group(kva[:, LANES:], bd32_ref[...], g_ref[4:5, 0:LANES])
        if states:
            _store_rows(st_ckv_ref, c_kv)
            _store_transposed(st_kpe_ref, k_pe, 0, keep=MLA_ROPE)
        if rope:
            k_pe = _rope(k_pe, c32_ref[...], s32_ref[...], MLA_ROPE)
        kpe_ref[...] = k_pe.astype(BF16)
        _mla_kv(c_kv, wkvb_ref, bd64_ref, g_ref[5:6, 0:256], kn_ref, vm_ref)

    def diff_queries(dq):
        for c in range(2):
            sl = slice(256 * c, 256 * (c + 1))
            qn = _rms_group(dq[:, sl], bd64, g_ref[6:7, 0:256])
            if rope:
                qn = _rope(qn, c64_ref[...], s64_ref[...], DIFF_HD)
            dq_ref[:, sl] = (qn * diff_scale).astype(BF16)

    def diff_keys(dk):
        for c in range(2):
            sl = slice(256 * c, 256 * (c + 1))
            kn = _rms_group(dk[:, sl], bd64, g_ref[7:8, 0:256])
            if states:
                _store_transposed(st_dk_ref, kn, 256 * c)
            if rope:
                kn = _rope(kn, c64_ref[...], s64_ref[...], DIFF_HD)
            dk_ref[:, sl] = kn.astype(BF16)

    def diff_values(dv):
        if states:
            _store_heads4(st_dv_ref, dv)
        dv_ref[...] = dv.astype(BF16)

    _staggered(hb, w_ref, [(0, 256, mla_queries), (256, 512, mla_keys), (512, 1024, _gate_into(mg_ref)),
                           (1024, 1536, diff_queries), (1536, 2048, diff_keys), (2048, 2560, diff_values),
                           (2560, 3072, _gate_into(dg_ref))])


def _in_odd_kernel(*refs, rope, states, n_alias, row_base, tiles_per_batch):
    it = iter(refs)
    x_ref, mod_ref, g_ref, w_ref, bd64_ref = (next(it) for _ in range(5))
    if rope:
        c64_ref, s64_ref = (next(it) for _ in range(2))
    for _ in range(n_alias):
        next(it)
    nq_ref, nk_ref, nv_ref, ng_ref, gq_ref, gk_ref, gv_ref, gg_ref = (next(it) for _ in range(8))
    if states:
        st_nk_ref, st_nv_ref, st_gk_ref, st_gv_ref = (next(it) for _ in range(4))

    row = row_base + pl.program_id(0) // tiles_per_batch
    hb = _modulated_norm(x_ref, mod_ref, g_ref, row)
    bd64 = bd64_ref[...]
    na_scale = NA_HD ** -0.5 * LOG2E
    gqa_scale = GQA_HD ** -0.5 * LOG2E

    def na_queries(nq):
        for c in range(2):
            sl = slice(256 * c, 256 * (c + 1))
            nq_ref[:, sl] = (_rms_group(nq[:, sl], bd64, g_ref[1:2, 0:256]) * na_scale).astype(BF16)

    def na_keys(nk):
        for c in range(2):
            sl = slice(256 * c, 256 * (c + 1))
            kn = _rms_group(nk[:, sl], bd64, g_ref[2:3, 0:256])
            if states:
                _store_transposed(st_nk_ref, kn, 256 * c)
            nk_ref[:, sl] = kn.astype(BF16)

    def na_values(nv):
        if states:
            for c in range(2):
                _store_transposed(st_nv_ref, nv[:, 256 * c:256 * (c + 1)], 256 * c)
        nv_ref[...] = nv.astype(BF16)

    def gqa_queries(gq):
        for c in range(2):
            sl = slice(256 * c, 256 * (c + 1))
            qn = _rms_group(gq[:, sl], bd64, g_ref[3:4, 0:256])
            if rope:
                qn = _rope(qn, c64_ref[...], s64_ref[...], GQA_HD)
            gq_ref[:, sl] = (qn * gqa_scale).astype(BF16)

    def gqa_keys_values(gkv):
        gk = _rms_group(gkv[:, :LANES], bd64[:LANES, :LANES], g_ref[4:5, 0:LANES])
        gv = gkv[:, LANES:]
        if states:
            _store_transposed(st_gk_ref, gk, 0)
            _store_transposed(st_gv_ref, gv, 0)
        if rope:
            gk = _rope(gk, c64_ref[...], s64_ref[...], GQA_HD)
        gk_ref[...] = gk.astype(BF16)
        gv_ref[...] = gv.astype(BF16)

    _staggered(hb, w_ref, [(0, 512, na_queries), (512, 1024, na_keys), (1024, 1536, na_values),
                           (1536, 2048, _gate_into(ng_ref)), (2048, 2560, gqa_queries),
                           (2560, 2816, gqa_keys_values), (2816, 3328, _gate_into(gg_ref))])


def _full(shape):
    zeros = (0,) * len(shape)
    return pl.BlockSpec(shape, lambda *_: zeros)


class _Layer(NamedTuple):
    stack: jax.Array
    index: int


def _spec(a):
    if isinstance(a, _Layer):
        index = (a.index,) + (0,) * (a.stack.ndim - 1)
        return pl.BlockSpec((None,) + a.stack.shape[1:], lambda *_: index)
    return _full(a.shape)


def _arr(a):
    return a.stack if isinstance(a, _Layer) else a


def _in_proj(kernel, x, mods, layer, gains, weights, consts, ropes, out_widths, row_base, tokens_per_batch):
    rows = x.shape[0]
    tm = ROW_TILE
    tiles_per_batch = tokens_per_batch // tm
    in_specs = [pl.BlockSpec((tm, D_MODEL), lambda i: (i, 0)),
                pl.BlockSpec((1, 8, 3 * D_MODEL), lambda i: (layer, 0, 0)),
                _full(gains.shape)]
    in_specs += [_spec(a) for a in (*weights, *consts)]
    in_specs += [pl.BlockSpec((tm, LANES), lambda i: (i % tiles_per_batch, 0)) for _ in ropes]
    return pl.pallas_call(
        functools.partial(kernel, rope=True, states=False, n_alias=0, row_base=row_base,
                          tiles_per_batch=tiles_per_batch),
        out_shape=[jax.ShapeDtypeStruct((rows, w), BF16) for w in out_widths],
        grid=(rows // tm,),
        in_specs=in_specs,
        out_specs=[pl.BlockSpec((tm, w), lambda i: (i, 0)) for w in out_widths],
        compiler_params=_cparams(),
        name=kernel.__name__.strip("_"),
    )(x, mods, gains, *map(_arr, weights), *consts, *ropes)


def _mla_cache_kernel(c_ref, wkvb_ref, bd64_ref, g_ref, kn_ref, vm_ref):
    _mla_kv(c_ref[...], wkvb_ref, bd64_ref, g_ref[5:6, 0:256], kn_ref, vm_ref)


def _mla_cache_kv(ckv, wkvb, bd64, gains):
    rows = ckv.shape[0]
    return pl.pallas_call(
        _mla_cache_kernel,
        out_shape=[jax.ShapeDtypeStruct((rows, 512), BF16)] * 2,
        grid=(1,),
        in_specs=[_full(ckv.shape), _spec(wkvb), _full(bd64.shape), _full(gains.shape)],
        out_specs=[_full((rows, 512))] * 2,
        compiler_params=_cparams(),
        name="mla_cache_kv",
    )(ckv, _arr(wkvb), bd64, gains)


def _pair_attn_kernel(*refs, mode, kinds, mxu_denominator, lam_init):
    it = iter(refs)
    q_ref = next(it)
    segs = [[next(it) for _ in seg_kinds] for seg_kinds in kinds]
    gate_ref = next(it)
    if mode == "diff":
        lam_ref, subln_ref = next(it), next(it)
    o_ref = next(it)

    qw = 2 * LANES if mode == "mla" else LANES
    rows = q_ref.shape[0]
    lo = _lane_mask(rows, LANES, 0, LANES // 2)
    if mode == "diff":
        lv = lam_ref[...]
        lam = (jnp.exp(jnp.sum(lv[0:1] * lv[1:2], axis=-1, keepdims=True))
               - jnp.exp(jnp.sum(lv[2:3] * lv[3:4], axis=-1, keepdims=True)) + lam_init)

    def load(ref, kind, p):
        if kind == "rows4":
            x = ref[pl.ds(p, ref.shape[0] // 4, stride=4), :]
        elif mode == "gqa":
            x = ref[...]
        else:
            sl = slice(LANES * p, LANES * (p + 1))
            x = ref[sl, :] if kind == "cols" else ref[:, sl]
        return x.astype(BF16), kind == "cols"

    def values(p):
        return [load(seg[-1], seg_kinds[-1], p) for seg, seg_kinds in zip(segs, kinds)]

    def scores(p, j):
        q = q_ref[:, qw * p:qw * (p + 1)]
        ksegs = []
        for seg, seg_kinds in zip(segs, kinds):
            k, transposed = load(seg[0], seg_kinds[0], p)
            if mode == "mla":
                k = jnp.concatenate([k, seg[1][...].astype(BF16)], axis=-1)
            ksegs.append((k, transposed))
        keep = _lane_mask(rows, qw, 64 * j, 64 * (j + 1))
        if mode == "mla":
            keep = keep | _lane_mask(rows, qw, LANES + 32 * j, LANES + 32 * (j + 1))
        return _scores(jnp.where(keep, q, jnp.zeros_like(q)), ksegs)

    def finish(p, outs):
        sl = slice(LANES * p, LANES * (p + 1))
        if mode == "diff":
            d = outs[0] - lam * outs[1]
            o = _rms_full(d, subln_ref[...]) * (1.0 - lam_init)
        else:
            o = jnp.where(lo, outs[0], outs[1])
        o_ref[:, sl] = (o * gate_ref[:, sl].astype(F32)).astype(BF16)

    heads = [(p, j) for p in range(PAIRS) for j in range(2)]
    ss = scores(*heads[0])
    outs = []
    for t, (p, j) in enumerate(heads):
        ss_next = scores(*heads[t + 1]) if t + 1 < len(heads) else None
        spare = (64 * (1 - j), 64 * (2 - j)) if mxu_denominator and mode != "diff" else None
        outs.append(_softmax_pv(ss, values(p), ones_lanes=spare))
        ss = ss_next
        if j == 1:
            finish(p, outs)
            outs = []


def _latent_attention(mode, q, segs, gate, batch, tq, extra=(), lam_init=0.0):
    bq = Q_BLOCK
    qw = 2 * LANES if mode == "mla" else LANES
    nq = tq // bq
    in_specs = [pl.BlockSpec((bq, qw * PAIRS), lambda b, i: (b * nq + i, 0))]
    args = [q]
    for seg in segs:
        for arr, spec, _ in seg:
            args.append(arr)
            in_specs.append(spec)
    kinds = tuple(tuple(kind for _, _, kind in seg) for seg in segs)
    in_specs.append(pl.BlockSpec((bq, LANES * PAIRS), lambda b, i: (b * nq + i, 0)))
    args.append(gate)
    for arr in extra:
        args.append(arr)
        in_specs.append(pl.BlockSpec(arr.shape, lambda b, i: (0,) * arr.ndim))
    return pl.pallas_call(
        functools.partial(_pair_attn_kernel, mode=mode, kinds=kinds, mxu_denominator=True, lam_init=lam_init),
        out_shape=jax.ShapeDtypeStruct((batch * tq, LANES * PAIRS), BF16),
        grid=(batch, nq),
        in_specs=in_specs,
        out_specs=pl.BlockSpec((bq, LANES * PAIRS), lambda b, i: (b * nq + i, 0)),
        compiler_params=_cparams(),
        name=mode + "_attention",
    )(*args)


def _self_seg(arr, tk):
    return arr, pl.BlockSpec((tk, arr.shape[1]), lambda b, i: (b, 0)), "rows"


def _cache_seg(arr, layer, kind):
    return arr, pl.BlockSpec((None, None) + arr.shape[2:], lambda b, i: (b, layer, 0, 0)), kind


def _na_kernel(q_ref, k_ref, v_ref, ck_ref, cv_ref, bias_ref, gate_ref, o_ref):
    g = pl.program_id(1)
    n_groups = pl.num_programs(1)
    first_row = jnp.clip(NA_GROUP_ROWS * g - NA_ROWS // 2, 0, NA_GROUP_ROWS * n_groups - NA_WIN_ROWS)
    start = pl.multiple_of(first_row * GRID_W, GRID_W)
    win = NA_WIN_ROWS * GRID_W
    kwin = k_ref[pl.ds(start, win), :]
    vwin = v_ref[pl.ds(start, win), :]
    ck = ck_ref[...].astype(BF16)
    cv = cv_ref[...].astype(BF16)
    q = q_ref[...]
    rows = q.shape[0]
    lo = _lane_mask(rows, LANES, 0, LANES // 2)

    def scores(h):
        sl = slice(LANES * (h // 2), LANES * (h // 2 + 1))
        keep = _lane_mask(rows, LANES, 64 * (h % 2), 64 * (h % 2 + 1))
        qm = jnp.where(keep, q[:, sl], jnp.zeros_like(q[:, sl]))
        return _scores(qm, [(kwin[:, sl], False), (ck[sl, :], True)], bias0=bias_ref[h])

    ss = scores(0)
    outs = []
    for h in range(NA_HEADS):
        ss_next = scores(h + 1) if h + 1 < NA_HEADS else None
        sl = slice(LANES * (h // 2), LANES * (h // 2 + 1))
        j = h % 2
        outs.append(_softmax_pv(ss, [(vwin[:, sl], False), (cv[sl, :], True)],
                                ones_lanes=(64 * (1 - j), 64 * (2 - j))))
        if j == 1:
            o = jnp.where(lo, outs[0], outs[1])
            o_ref[:, sl] = (o * gate_ref[:, sl].astype(F32)).astype(BF16)
            outs = []
        ss = ss_next


def _na_attention(q, k, v, cache_k, cache_v, layer, bias, gate, batch, tq):
    bq = NA_GROUP_ROWS * GRID_W
    n_groups = tq // bq
    past = cache_k.shape[3]
    width = NA_HEADS * NA_HD

    def bias_map(b, g):
        return (layer, jnp.where(g == 0, 0, jnp.where(g == n_groups - 1, 2, 1)), 0, 0, 0)

    tok = pl.BlockSpec((bq, width), lambda b, g: (b * n_groups + g, 0))
    whole = pl.BlockSpec((tq, width), lambda b, g: (b, 0))
    cache = pl.BlockSpec((None, None, width, past), lambda b, g: (b, layer, 0, 0))
    return pl.pallas_call(
        _na_kernel,
        out_shape=jax.ShapeDtypeStruct((batch * tq, width), BF16),
        grid=(batch, n_groups),
        in_specs=[tok, whole, whole, cache, cache,
                  pl.BlockSpec((None, None, NA_HEADS, bq, NA_WIN_ROWS * GRID_W), bias_map), tok],
        out_specs=tok,
        compiler_params=_cparams(),
        name="na_attention",
    )(q, k, v, cache_k, cache_v, bias, gate)


def _na_bias_tables(rpb):
    n_dr = 2 * NA_ROWS - 1
    idx = np.full((3, NA_GROUP_ROWS, NA_WIN_ROWS), n_dr, np.int32)
    for a in range(NA_GROUP_ROWS):
        for j in range(NA_WIN_ROWS):
            if j < NA_ROWS:
                idx[0, a, j] = j - a + NA_ROWS - 1
            if a <= j < a + NA_ROWS:
                idx[1, a, j] = j - a + NA_ROWS // 2 - 1
            if j >= NA_WIN_ROWS - NA_ROWS:
                idx[2, a, j] = j - a - (NA_WIN_ROWS - NA_ROWS) + NA_ROWS // 2 - 1
    layers = rpb.shape[0]
    padded = jnp.pad(rpb.astype(F32), ((0, 0), (0, 0), (0, 1), (0, LANES - rpb.shape[3])))
    return pl.pallas_call(
        functools.partial(_na_bias_kernel, idx=idx),
        out_shape=jax.ShapeDtypeStruct((layers, 3, NA_HEADS, NA_GROUP_ROWS * GRID_W, NA_WIN_ROWS * GRID_W), F32),
        grid=(layers, NA_HEADS),
        in_specs=[pl.BlockSpec((None, None, n_dr + 1, LANES), lambda l, h: (l, h, 0, 0))],
        out_specs=pl.BlockSpec((None, 3, None, NA_GROUP_ROWS * GRID_W, NA_WIN_ROWS * GRID_W),
                               lambda l, h: (l, 0, h, 0, 0)),
        compiler_params=_cparams(),
        name="na_bias",
    )(padded)


def _na_bias_kernel(rpb_ref, out_ref, *, idx):
    n_dr = rpb_ref.shape[0] - 1
    qc = lax.broadcasted_iota(jnp.int32, (GRID_W, LANES), 0)
    lane = lax.broadcasted_iota(jnp.int32, (GRID_W, LANES), 1)
    kc = lane & (GRID_W - 1)
    first = jnp.clip(qc - NA_COLS // 2, 0, GRID_W - NA_COLS)
    valid = (kc >= first) & (kc < first + NA_COLS)
    low = lane < GRID_W
    outside = jnp.full((GRID_W, LANES), NEG, F32)

    def table(r):
        if r == n_dr:
            return outside
        row = jnp.broadcast_to(rpb_ref[r:r + 1, :], (GRID_W, LANES))
        lo = pltpu.roll(row, LANES - (NA_COLS - 1), 1, stride=1, stride_axis=0)
        hi = pltpu.roll(row, GRID_W - (NA_COLS - 1), 1, stride=1, stride_axis=0)
        return jnp.where(valid, jnp.where(low, lo, hi) * LOG2E, NEG)

    tables = [table(r) for r in range(n_dr + 1)]
    for t in range(3):
        for a in range(NA_GROUP_ROWS):
            for j in range(0, NA_WIN_ROWS, 2):
                pair = jnp.where(low, tables[int(idx[t, a, j])], tables[int(idx[t, a, j + 1])])
                out_ref[t, a * GRID_W:(a + 1) * GRID_W, j * GRID_W:(j + 2) * GRID_W] = pair


def _out_kernel(oa_ref, ob_ref, x_ref, mod_ref, w_ref, y_ref, *, row_base, tiles_per_batch):
    row = row_base + pl.program_id(0) // tiles_per_batch
    gate = mod_ref[0, pl.ds(row, 1), :][:, 2 * D_MODEL:]
    half = oa_ref.shape[1]
    acc = _dot(oa_ref[...], w_ref[:half, :]) + _dot(ob_ref[...], w_ref[half:, :])
    y_ref[...] = x_ref[...] + gate * acc


def _out_proj(oa, ob, x, mods, layer, w, row_base, tokens_per_batch):
    rows = x.shape[0]
    tm = ROW_TILE
    tiles_per_batch = tokens_per_batch // tm
    return pl.pallas_call(
        functools.partial(_out_kernel, row_base=row_base, tiles_per_batch=tiles_per_batch),
        out_shape=jax.ShapeDtypeStruct((rows, D_MODEL), F32),
        grid=(rows // tm,),
        in_specs=[pl.BlockSpec((tm, oa.shape[1]), lambda i: (i, 0)),
                  pl.BlockSpec((tm, ob.shape[1]), lambda i: (i, 0)),
                  pl.BlockSpec((tm, D_MODEL), lambda i: (i, 0)),
                  pl.BlockSpec((1, 8, 3 * D_MODEL), lambda i: (layer, 0, 0)),
                  _spec(w)],
        out_specs=pl.BlockSpec((tm, D_MODEL), lambda i: (i, 0)),
        compiler_params=_cparams(),
        name="out_proj",
    )(oa, ob, x, mods, _arr(w))


def _out_in_kernel(*refs, proj, rope, row_base, tiles_per_batch):
    oa_ref, ob_ref, x_ref, mod_prev_ref, wout_ref = refs[:5]
    n_proj_in = len(refs) - 5 - 1 - (9 if proj is _in_even_kernel else 8)
    proj_in = refs[5:5 + n_proj_in]
    y_ref = refs[5 + n_proj_in]
    slabs = refs[6 + n_proj_in:]
    _out_kernel(oa_ref, ob_ref, x_ref, mod_prev_ref, wout_ref, y_ref, row_base=row_base,
                tiles_per_batch=tiles_per_batch)
    proj(y_ref, *proj_in, *slabs, rope=rope, states=False, n_alias=0, row_base=row_base,
         tiles_per_batch=tiles_per_batch)


def _out_in_proj(proj, oa, ob, x, mods, layer, w_out, gains, weights, consts, ropes, out_widths, row_base,
                 tokens_per_batch):
    rows = x.shape[0]
    tm = ROW_TILE
    tiles_per_batch = tokens_per_batch // tm
    row = pl.BlockSpec((tm, D_MODEL), lambda i: (i, 0))
    in_specs = [pl.BlockSpec((tm, oa.shape[1]), lambda i: (i, 0)), pl.BlockSpec((tm, ob.shape[1]), lambda i: (i, 0)),
                row, pl.BlockSpec((1, 8, 3 * D_MODEL), lambda i: (layer - 1, 0, 0)), _spec(w_out),
                pl.BlockSpec((1, 8, 3 * D_MODEL), lambda i: (layer, 0, 0)), _full(gains.shape)]
    in_specs += [_spec(a) for a in (*weights, *consts)]
    in_specs += [pl.BlockSpec((tm, LANES), lambda i: (i % tiles_per_batch, 0)) for _ in ropes]
    return pl.pallas_call(
        functools.partial(_out_in_kernel, proj=proj, rope=True, row_base=row_base, tiles_per_batch=tiles_per_batch),
        out_shape=[jax.ShapeDtypeStruct((rows, D_MODEL), F32)]
        + [jax.ShapeDtypeStruct((rows, w), BF16) for w in out_widths],
        grid=(rows // tm,),
        in_specs=in_specs,
        out_specs=[row] + [pl.BlockSpec((tm, w), lambda i: (i, 0)) for w in out_widths],
        compiler_params=_cparams(),
        name="out_" + proj.__name__.strip("_"),
    )(oa, ob, x, mods, _arr(w_out), mods, gains, *map(_arr, weights), *consts, *ropes)


def _ctx_layer_kernel(*refs, even, n_in, n_alias, n_slabs, seq, lam_init, tiles_per_batch):
    in_refs = refs[:n_in]
    it = iter(refs[n_in:])
    wout_ref = next(it)
    if even:
        lam_ref, subln_ref = next(it), next(it)
    alias_refs = [next(it) for _ in range(n_alias)]
    y_ref = next(it)
    state_refs = [next(it) for _ in range(4)]
    slabs = [next(it) for _ in range(n_slabs)]
    oa_ref, ob_ref = next(it), next(it)
    x_ref, mod_ref = in_refs[0], in_refs[1]

    proj = _in_even_kernel if even else _in_odd_kernel
    proj(*in_refs, *alias_refs, *slabs, *state_refs, rope=False, states=True, n_alias=n_alias, row_base=0,
         tiles_per_batch=tiles_per_batch)
    for b in range(x_ref.shape[0] // seq):
        def own(ref):
            return ref.at[pl.ds(b * seq, seq), :]
        common = dict(mxu_denominator=False)
        if even:
            qcat, kn, kpe, vm, mg, dq, dk, dv, dg = (own(r) for r in slabs)
            _pair_attn_kernel(qcat, kn, kpe, vm, mg, own(oa_ref), mode="mla", kinds=(("rows",) * 3,),
                              lam_init=0.0, **common)
            _pair_attn_kernel(dq, dk, dv, dg, lam_ref, subln_ref, own(ob_ref), mode="diff", kinds=(("rows",) * 2,),
                              lam_init=lam_init, **common)
        else:
            nq, nk, nv, ng, gq, gk, gv, gg = (own(r) for r in slabs)
            _pair_attn_kernel(nq, nk, nv, ng, own(oa_ref), mode="mha", kinds=(("rows",) * 2,), lam_init=0.0, **common)
            _pair_attn_kernel(gq, gk, gv, gg, own(ob_ref), mode="gqa", kinds=(("rows",) * 2,), lam_init=0.0, **common)
    _out_kernel(oa_ref, ob_ref, x_ref, mod_ref, wout_ref, y_ref, row_base=0, tiles_per_batch=tiles_per_batch)


def _ctx_layer(even, x, mods, layer, gains, weights, consts, w_out, extras, slab_widths, state_tails, state_prev,
               seq, lam_init=0.0):
    rows = x.shape[0]
    tm = ROW_TILE
    slot = layer // 2
    in_specs = [pl.BlockSpec((tm, D_MODEL), lambda i: (i, 0)),
                pl.BlockSpec((1, 8, 3 * D_MODEL), lambda i: (layer, 0, 0)),
                _full(gains.shape)]
    in_specs += [_spec(a) for a in (*weights, *consts)]
    args = [x, mods, gains, *map(_arr, weights), *consts]
    n_in = len(args)
    in_specs += [_spec(a) for a in (w_out, *extras)]
    args += [_arr(w_out), *extras]
    aliases = {}
    if state_prev is not None:
        aliases = {len(args) + j: 1 + j for j in range(len(state_prev))}
        in_specs += [pl.BlockSpec(memory_space=pl.ANY) for _ in state_prev]
        args += list(state_prev)
    out_shape = [jax.ShapeDtypeStruct((rows, D_MODEL), F32)]
    out_specs = [pl.BlockSpec((tm, D_MODEL), lambda i: (i, 0))]
    out_shape += [jax.ShapeDtypeStruct((rows // seq, DEPTH // 2) + tail, F32) for tail in state_tails]
    out_specs += [pl.BlockSpec((tm // seq, None) + tail, lambda i: (i, slot, 0, 0)) for tail in state_tails]
    scratch = [pltpu.VMEM((tm, w), BF16) for w in (*slab_widths, 4 * LANES, 4 * LANES)]
    return pl.pallas_call(
        functools.partial(_ctx_layer_kernel, even=even, n_in=n_in, n_alias=len(aliases), n_slabs=len(slab_widths),
                          seq=seq, lam_init=lam_init, tiles_per_batch=rows // tm),
        out_shape=out_shape,
        grid=(rows // tm,),
        in_specs=in_specs,
        out_specs=out_specs,
        scratch_shapes=scratch,
        input_output_aliases=aliases,
        compiler_params=_cparams(),
        name="ctx_layer_even" if even else "ctx_layer_odd",
    )(*args)


def _layout_in_even_kernel(wt_ref, o_ref):
    kpe0 = Q_LORA + KV_LORA
    o_ref[:, 0:kpe0] = wt_ref[0:kpe0, :].T.astype(BF16)
    kpe = wt_ref[kpe0:kpe0 + MLA_ROPE, :]
    o_ref[:, kpe0:kpe0 + LANES] = jnp.concatenate([kpe] * (LANES // MLA_ROPE), axis=0).T.astype(BF16)
    o_ref[:, kpe0 + LANES:] = wt_ref[kpe0 + MLA_ROPE:, :].T.astype(BF16)


def _layout_in_even(w):
    layers, k, cols = w.shape
    kc = 256
    return pl.pallas_call(
        _layout_in_even_kernel,
        out_shape=jax.ShapeDtypeStruct((layers, k, cols + LANES - MLA_ROPE), BF16),
        grid=(layers, k // kc),
        in_specs=[pl.BlockSpec((None, cols, kc), lambda l, r: (l, 0, r))],
        out_specs=pl.BlockSpec((None, kc, cols + LANES - MLA_ROPE), lambda l, r: (l, r, 0)),
        compiler_params=_cparams(),
        name="layout_in_even",
    )(w.transpose(0, 2, 1))


def _layout_in_odd_kernel(w_ref, o_ref):
    def permuted(base):
        for k in range(GQA_HEADS // 2):
            lo_head, hi_head = GQA_PERM[2 * k], GQA_PERM[2 * k + 1]
            lo = w_ref[:, base + LANES * (lo_head // 2):base + LANES * (lo_head // 2 + 1)]
            hi = w_ref[:, base + LANES * (hi_head // 2):base + LANES * (hi_head // 2 + 1)]
            if lo_head % 2 == 1:
                lo = pltpu.roll(lo, LANES // 2, 1)
            if hi_head % 2 == 0:
                hi = pltpu.roll(hi, LANES // 2, 1)
            lane = lax.broadcasted_iota(jnp.int32, lo.shape, 1)
            o_ref[:, base + LANES * k:base + LANES * (k + 1)] = jnp.where(lane < LANES // 2, lo, hi).astype(BF16)

    o_ref[:, 0:2048] = w_ref[:, 0:2048].astype(BF16)
    permuted(2048)
    o_ref[:, 2560:2816] = w_ref[:, 2560:2816].astype(BF16)
    permuted(2816)


def _layout_w_out_kernel(w_ref, o_ref, *, permute):
    half = w_ref.shape[0] // 2
    o_ref[:half, :] = w_ref[:half, :].astype(BF16)
    if permute:
        for k, head in enumerate(GQA_PERM):
            o_ref[half + GQA_HD * k:half + GQA_HD * (k + 1), :] = (
                w_ref[half + GQA_HD * head:half + GQA_HD * (head + 1), :].astype(BF16))
    else:
        o_ref[half:, :] = w_ref[half:, :].astype(BF16)


def _layout_in_odd(w):
    layers, rows, cols = w.shape
    rb = 256
    return pl.pallas_call(
        _layout_in_odd_kernel,
        out_shape=jax.ShapeDtypeStruct((layers, rows, cols), BF16),
        grid=(layers, rows // rb),
        in_specs=[pl.BlockSpec((None, rb, cols), lambda l, r: (l, r, 0))],
        out_specs=pl.BlockSpec((None, rb, cols), lambda l, r: (l, r, 0)),
        compiler_params=_cparams(),
        name="layout_in_odd",
    )(w)


def _layout_w_out(w, permute):
    layers, rows, cols = w.shape
    return pl.pallas_call(
        functools.partial(_layout_w_out_kernel, permute=permute),
        out_shape=jax.ShapeDtypeStruct((layers, rows, cols), BF16),
        grid=(layers,),
        in_specs=[pl.BlockSpec((None, rows, cols), lambda l: (l, 0, 0))],
        out_specs=pl.BlockSpec((None, rows, cols), lambda l: (l, 0, 0)),
        compiler_params=_cparams(),
        name="layout_w_out",
    )(w)


def _block_diag(width, group):
    idx = np.arange(width) // group
    return jnp.asarray((idx[:, None] == idx[None, :]).astype(np.float32) / group, BF16)


def _rope_tables(t, rot_dim):
    pos = np.arange(t)
    row = (pos // GRID_W).astype(np.float64)
    col = (pos % GRID_W).astype(np.float64)
    n = rot_dim // 2
    inv = ROPE_THETA ** (-np.arange(0, n, 2, dtype=np.float64) / n)
    ang = np.concatenate([row[:, None] * inv, col[:, None] * inv], axis=-1)
    cos = np.concatenate([np.cos(ang), np.cos(ang)], axis=-1)
    sin = np.concatenate([-np.sin(ang), np.sin(ang)], axis=-1)
    reps = LANES // rot_dim
    return (jnp.asarray(np.tile(cos, (1, reps)), F32), jnp.asarray(np.tile(sin, (1, reps)), F32))


def _pad_row(v, width=D_MODEL):
    return jnp.pad(v, (0, width - v.shape[0]))


def _tile_row(v, reps):
    return _pad_row(jnp.tile(v, reps))


def kernel(x_prompt, x_sample, cache_mla_ckv, cache_mla_kpe, cache_diff_k, cache_diff_v, cache_na_k, cache_na_v, cache_gqa_k, cache_gqa_v, c, c_ctx, norm_w, w_mod, b_mod, w_in_even, w_out_even, mla_qa_norm, mla_wqb, mla_kva_norm, mla_wkvb, mla_qn_nope, mla_qn_rope, mla_kn_nope, mla_kn_rope, diff_qn, diff_kn, diff_lq1, diff_lk1, diff_lq2, diff_lk2, diff_subln, w_in_odd, w_out_odd, na_qn, na_kn, na_rpb, gqa_qn, gqa_kn):
    batch, seq, _ = x_prompt.shape
    dec_batch, dec_seq, _ = x_sample.shape
    past = cache_mla_ckv.shape[2]
    n_even, n_odd = w_in_even.shape[0], w_in_odd.shape[0]

    def layers(stack):
        return [_Layer(stack, i) for i in range(stack.shape[0])]

    w_in_e = layers(_layout_in_even(w_in_even))
    w_in_o = layers(_layout_in_odd(w_in_odd))
    w_out_e = layers(_layout_w_out(w_out_even, permute=False))
    w_out_o = layers(_layout_w_out(w_out_odd, permute=True))
    wqb = mla_wqb.reshape(n_even, Q_LORA, MLA_HEADS, MLA_QK)
    wqb = jnp.concatenate([wqb[..., :MLA_NOPE].reshape(n_even, Q_LORA, 4, 2 * MLA_NOPE),
                           wqb[..., MLA_NOPE:].reshape(n_even, Q_LORA, 4, 2 * MLA_ROPE),
                           jnp.zeros((n_even, Q_LORA, 4, LANES - 2 * MLA_ROPE), F32)], axis=-1)
    wqb = layers(wqb.reshape(n_even, Q_LORA, 4 * 2 * LANES).astype(BF16))
    wkvb = mla_wkvb.reshape(n_even, KV_LORA, MLA_HEADS, 2 * MLA_NOPE)
    wkvb = layers(jnp.concatenate([wkvb[..., :MLA_NOPE].reshape(n_even, KV_LORA, 512),
                                   wkvb[..., MLA_NOPE:].reshape(n_even, KV_LORA, 512)], axis=-1).astype(BF16))

    gains_e = [jnp.stack([norm_w[2 * i], _pad_row(mla_qa_norm[i]),
                          _pad_row(jnp.concatenate([jnp.tile(mla_qn_nope[i], 2), jnp.tile(mla_qn_rope[i], 4)])),
                          _pad_row(mla_kva_norm[i]), _tile_row(mla_kn_rope[i], 4), _tile_row(mla_kn_nope[i], 4),
                          _tile_row(diff_qn[i], 4), _tile_row(diff_kn[i], 4)]) for i in range(n_even)]
    gains_o = [jnp.stack([norm_w[2 * i + 1], _tile_row(na_qn[i], 4), _tile_row(na_kn[i], 4),
                          _tile_row(gqa_qn[i], 4), _tile_row(gqa_kn[i], 2),
                          jnp.zeros((D_MODEL,), F32), jnp.zeros((D_MODEL,), F32), jnp.zeros((D_MODEL,), F32)])
               for i in range(n_odd)]
    lam_vecs = [jnp.stack([diff_lq1[i], diff_lk1[i], diff_lq2[i], diff_lk2[i]]) for i in range(n_even)]

    bd64 = _block_diag(256, 64)
    bd32 = _block_diag(LANES, 32)
    bdq = jnp.concatenate([jnp.concatenate([_block_diag(LANES, 64), jnp.zeros((LANES, LANES), BF16)], axis=1),
                           jnp.concatenate([jnp.zeros((LANES, LANES), BF16), _block_diag(LANES, 32)], axis=1)], axis=0)
    c64, s64 = _rope_tables(dec_seq, 64)
    c32, s32 = _rope_tables(dec_seq, MLA_ROPE)
    na_bias = _na_bias_tables(na_rpb)

    cvecs = jnp.concatenate([c_ctx[None, :], c, jnp.zeros((8 - 1 - dec_batch, D_MODEL), F32)], axis=0)
    mods = _modulation(cvecs, w_mod, b_mod)

    cache_kpe = jnp.tile(cache_mla_kpe, (1, 1, 1, 4))
    cache_dk = cache_diff_k.transpose(0, 1, 3, 4, 5, 2).reshape(dec_batch, n_even, 512, past)
    cache_dv = cache_diff_v.reshape(dec_batch, n_even, 4 * past, LANES)
    cache_nk = cache_na_k.transpose(0, 1, 3, 4, 2).reshape(dec_batch, n_odd, 512, past)
    cache_nv = cache_na_v.transpose(0, 1, 3, 4, 2).reshape(dec_batch, n_odd, 512, past)
    cache_gk = cache_gqa_k.transpose(0, 1, 3, 4, 2).reshape(dec_batch, n_odd, LANES, past)
    cache_gv = cache_gqa_v.transpose(0, 1, 3, 4, 2).reshape(dec_batch, n_odd, LANES, past)

    even_widths = (1024, 512, LANES, 512, 512, 512, 512, 512, 512)
    odd_widths = (512, 512, 512, 512, 512, LANES, LANES, 512)
    even_states = ((seq, KV_LORA), (MLA_ROPE, seq), (512, seq), (4 * seq, LANES))
    odd_states = ((512, seq), (512, seq), (LANES, seq), (LANES, seq))

    def lam_init(l):
        return 0.8 - 0.6 * math.exp(-0.3 * l)

    def context_pass(x):
        states = [None, None]
        for l in range(DEPTH):
            i = l // 2
            if l % 2 == 0:
                x, *states[0] = _ctx_layer(True, x, mods, l, gains_e[i], (w_in_e[i], wqb[i], wkvb[i]),
                                           (bd64, bd32, bdq), w_out_e[i], (lam_vecs[i], diff_subln[i][None, :]),
                                           even_widths, even_states, states[0], seq, lam_init(l))
            else:
                x, *states[1] = _ctx_layer(False, x, mods, l, gains_o[i], (w_in_o[i],), (bd64,), w_out_o[i], (),
                                           odd_widths, odd_states, states[1], seq)
        return x, states

    def latent_pass(x, nb, t):
        row_base = 1
        oa = ob = w_out_prev = None
        for l in range(DEPTH):
            i = l // 2
            if l % 2 == 0:
                proj_args = (gains_e[i], (w_in_e[i], wqb[i], wkvb[i]), (bd64, bd32, bdq), (c64, s64, c32, s32),
                             even_widths)
                proj = _in_even_kernel
            else:
                proj_args = (gains_o[i], (w_in_o[i],), (bd64,), (c64, s64), odd_widths)
                proj = _in_odd_kernel
            if l == 0:
                outs = _in_proj(proj, x, mods, l, *proj_args, row_base, t)
            else:
                x, *outs = _out_in_proj(proj, oa, ob, x, mods, l, w_out_prev, *proj_args, row_base, t)
            if l % 2 == 0:
                qcat, kn, kpe, vm, mg, dq, dk, dv, dg = outs
                kn_c, vm_c = _mla_cache_kv(cache_mla_ckv[:, i].reshape(nb * past, KV_LORA), wkvb[i], bd64, gains_e[i])
                mla_segs = [(_self_seg(kn, t), _self_seg(kpe, t), _self_seg(vm, t)),
                            (_self_seg(kn_c, past), _cache_seg(cache_kpe, i, "rows"), _self_seg(vm_c, past))]
                diff_segs = [(_self_seg(dk, t), _self_seg(dv, t)),
                             (_cache_seg(cache_dk, i, "cols"), _cache_seg(cache_dv, i, "rows4"))]
                oa = _latent_attention("mla", qcat, mla_segs, mg, nb, t)
                ob = _latent_attention("diff", dq, diff_segs, dg, nb, t,
                                       extra=(lam_vecs[i], diff_subln[i][None, :]), lam_init=lam_init(l))
                w_out_prev = w_out_e[i]
            else:
                nq, nk, nv, ng, gq, gk, gv, gg = outs
                oa = _na_attention(nq, nk, nv, cache_nk, cache_nv, i, na_bias, ng, nb, t)
                gqa_segs = [(_self_seg(gk, t), _self_seg(gv, t)),
                            (_cache_seg(cache_gk, i, "cols"), _cache_seg(cache_gv, i, "cols"))]
                ob = _latent_attention("gqa", gq, gqa_segs, gg, nb, t)
                w_out_prev = w_out_o[i]
        return _out_proj(oa, ob, x, mods, DEPTH - 1, w_out_prev, row_base, t)

    y_prompt, st = context_pass(x_prompt.reshape(batch * seq, D_MODEL))
    y_sample = latent_pass(x_sample.reshape(dec_batch * dec_seq, D_MODEL), dec_batch, dec_seq)

    def token_major(a, heads):
        a = a.reshape((batch, n_even) + heads + (a.shape[2] // math.prod(heads), seq))
        return jnp.moveaxis(a, -1, 2)

    ckv, kpe_t, dk_t, dv4 = st[0]
    nk_t, nv_t, gk_t, gv_t = st[1]
    return (y_prompt.reshape(batch, seq, D_MODEL), y_sample.reshape(dec_batch, dec_seq, D_MODEL),
            ckv, token_major(kpe_t, ()), token_major(dk_t, (DIFF_HEADS, 2)),
            dv4.reshape(batch, n_even, seq, DIFF_HEADS, 2 * DIFF_HD),
            token_major(nk_t, (NA_HEADS,)), token_major(nv_t, (NA_HEADS,)),
            token_major(gk_t, (GQA_KV,)), token_major(gv_t, (GQA_KV,)))
```

```python
import functools
import math
from typing import NamedTuple

import jax
import jax.numpy as jnp
import numpy as np
from jax import lax
from jax.experimental import pallas as pl
from jax.experimental.pallas import tpu as pltpu

F32 = jnp.float32
BF16 = jnp.bfloat16

D_MODEL = 1024
DEPTH = 4
GRID_W = 64
ROPE_THETA = 10000.0
EPS = 1e-6
MLA_HEADS = 8
MLA_NOPE = 64
MLA_ROPE = 32
MLA_QK = MLA_NOPE + MLA_ROPE
Q_LORA = 256
KV_LORA = 128
DIFF_HEADS = 4
DIFF_HD = 64
NA_HEADS = 8
NA_HD = 64
NA_ROWS = 8
NA_COLS = 16
GQA_HEADS = 8
GQA_KV = 2
GQA_HD = 64
LANES = 128
PAIRS = 4
NA_GROUP_ROWS = 4
NA_WIN_ROWS = NA_ROWS + NA_GROUP_ROWS
NEG = -1e30
LOG2E = math.log2(math.e)
VMEM_LIMIT = 48 * 1024 * 1024
GQA_PERM = (0, 4, 1, 5, 2, 6, 3, 7)

ROW_TILE = 512
Q_BLOCK = 512


def _cparams():
    return pltpu.CompilerParams(vmem_limit_bytes=VMEM_LIMIT)


def _dot(a, b):
    return jnp.dot(a, b, preferred_element_type=F32)


def _dot_nt(a, b):
    return lax.dot_general(a, b, (((1,), (1,)), ((), ())), preferred_element_type=F32)


def _rms_full(x, g):
    ms = jnp.mean(x * x, axis=-1, keepdims=True)
    return x * lax.rsqrt(ms + EPS) * g


def _rms_group(x, bd, g):
    ms = _dot((x * x).astype(BF16), bd)
    return x * lax.rsqrt(ms + EPS) * g


def _silu(u):
    return u * (1.0 / (1.0 + jnp.exp(-u)))


def _rope(x, cos, sin, group):
    half = group // 2
    rows, width = x.shape
    lane = lax.broadcasted_iota(jnp.int32, (rows, LANES), 1)
    first = (lane & (group - 1)) < half
    outs = []
    for c in range(width // LANES):
        xc = x[:, c * LANES:(c + 1) * LANES]
        rot = jnp.where(first, pltpu.roll(xc, LANES - half, 1), pltpu.roll(xc, half, 1))
        outs.append(xc * cos + rot * sin)
    return outs[0] if len(outs) == 1 else jnp.concatenate(outs, axis=-1)


def _lane_mask(rows, width, lo, hi):
    lane = lax.broadcasted_iota(jnp.int32, (rows, width), 1)
    return (lane >= lo) & (lane < hi)


def _scores(qm, ksegs, bias0=None):
    ss = [_dot(qm, k) if transposed else _dot_nt(qm, k) for k, transposed in ksegs]
    if bias0 is not None:
        ss[0] = ss[0] + bias0
    return ss


def _softmax_pv(ss, vsegs, ones_lanes=None):
    m = jnp.max(ss[0], axis=-1, keepdims=True)
    for s in ss[1:]:
        m = jnp.maximum(m, jnp.max(s, axis=-1, keepdims=True))
    acc = None
    l = None
    for s, (v, transposed) in zip(ss, vsegs):
        p = jnp.exp2(s - m)
        if ones_lanes is None:
            ps = jnp.sum(p, axis=-1, keepdims=True)
            l = ps if l is None else l + ps
        else:
            width = lax.broadcasted_iota(jnp.int32, v.shape, 0 if transposed else 1)
            v = jnp.where((width >= ones_lanes[0]) & (width < ones_lanes[1]), jnp.ones_like(v), v)
        a = _dot_nt(p.astype(BF16), v) if transposed else _dot(p.astype(BF16), v)
        acc = a if acc is None else acc + a
    if ones_lanes is not None:
        l = pltpu.roll(acc, LANES // 2, 1)
    return acc / l


def _mod_kernel(c_ref, w_ref, b_ref, o_ref):
    c = c_ref[...]
    o_ref[0] = _dot(_silu(c).astype(BF16), w_ref[0].astype(BF16)) + b_ref[0]


def _modulation(cvecs, w_mod, b_mod):
    tn = 1536
    return pl.pallas_call(
        _mod_kernel,
        out_shape=jax.ShapeDtypeStruct((DEPTH, 8, 3 * D_MODEL), F32),
        grid=(DEPTH, 3 * D_MODEL // tn),
        in_specs=[pl.BlockSpec((8, D_MODEL), lambda l, n: (0, 0)),
                  pl.BlockSpec((1, D_MODEL, tn), lambda l, n: (l, 0, n)),
                  pl.BlockSpec((1, 1, tn), lambda l, n: (l, 0, n))],
        out_specs=pl.BlockSpec((1, 8, tn), lambda l, n: (l, 0, n)),
        compiler_params=_cparams(),
        name="modulation",
    )(cvecs, w_mod, b_mod.reshape(DEPTH, 1, 3 * D_MODEL))


def _modulated_norm(x_ref, mod_ref, g_ref, row):
    mod = mod_ref[0, pl.ds(row, 1), :]
    shift = mod[:, :D_MODEL]
    scale = mod[:, D_MODEL:2 * D_MODEL]
    h = _rms_full(x_ref[...], g_ref[0:1, :]) * (1.0 + scale) + shift
    return h.astype(BF16)


def _mla_kv(cn, wkvb_ref, bd64_ref, g_kn, kn_ref, vm_ref):
    kv = _dot(cn.astype(BF16), wkvb_ref[...])
    for c in range(2):
        sl = slice(256 * c, 256 * (c + 1))
        kn_ref[:, sl] = _rms_group(kv[:, sl], bd64_ref[...], g_kn).astype(BF16)
    vm_ref[...] = kv[:, 512:].astype(BF16)


def _store_rows(ref, x):
    seq = ref.shape[1]
    for b in range(ref.shape[0]):
        ref[b] = x[b * seq:(b + 1) * seq]


def _store_transposed(ref, x, row0, keep=None):
    seq = ref.shape[2]
    for b in range(ref.shape[0]):
        xt = x[b * seq:(b + 1) * seq].T
        if keep is not None:
            xt = xt[:keep]
        ref[b, row0:row0 + xt.shape[0], :] = xt


def _store_heads4(ref, x):
    seq = ref.shape[1] // 4
    for b in range(ref.shape[0]):
        for h in range(4):
            ref[b, pl.ds(h, seq, stride=4), :] = x[b * seq:(b + 1) * seq, LANES * h:LANES * (h + 1)]


def _staggered(hb, w_ref, stages):
    pending = _dot(hb, w_ref[:, stages[0][0]:stages[0][1]])
    for t, (_, _, epilogue) in enumerate(stages):
        current = pending
        if t + 1 < len(stages):
            pending = _dot(hb, w_ref[:, stages[t + 1][0]:stages[t + 1][1]])
        epilogue(current)


def _gate_into(out_ref):
    def epilogue(u):
        out_ref[...] = _silu(u).astype(BF16)
    return epilogue


def _in_even_kernel(*refs, rope, states, n_alias, row_base, tiles_per_batch):
    it = iter(refs)
    x_ref, mod_ref, g_ref, w_ref, wqb_ref, wkvb_ref, bd64_ref, bd32_ref, bdq_ref = (next(it) for _ in range(9))
    if rope:
        c64_ref, s64_ref, c32_ref, s32_ref = (next(it) for _ in range(4))
    for _ in range(n_alias):
        next(it)
    qcat_ref, kn_ref, kpe_ref, vm_ref, mg_ref, dq_ref, dk_ref, dv_ref, dg_ref = (next(it) for _ in range(9))
    if states:
        st_ckv_ref, st_kpe_ref, st_dk_ref, st_dv_ref = (next(it) for _ in range(4))

    row = row_base + pl.program_id(0) // tiles_per_batch
    hb = _modulated_norm(x_ref, mod_ref, g_ref, row)
    bd64 = bd64_ref[...]
    mla_scale = MLA_QK ** -0.5 * LOG2E
    diff_scale = DIFF_HD ** -0.5 * LOG2E

    def mla_queries(qa):
        qa_n = _rms_full(qa, g_ref[1:2, 0:256]).astype(BF16)
        q = _dot(qa_n, wqb_ref[...])
        for p in range(4):
            qp = _rms_group(q[:, 256 * p:256 * (p + 1)], bdq_ref[...], g_ref[2:3, 0:256])
            q_nope = qp[:, :LANES]
            q_pe = qp[:, LANES:]
            if rope:
                q_pe = _rope(q_pe, c32_ref[...], s32_ref[...], MLA_ROPE)
            qcat_ref[:, 256 * p:256 * p + LANES] = (q_nope * mla_scale).astype(BF16)
            qcat_ref[:, 256 * p + LANES:256 * (p + 1)] = (q_pe * mla_scale).astype(BF16)

    def mla_keys(kva):
        c_kv = _rms_full(kva[:, :LANES], g_ref[3:4, 0:LANES])
        k_pe = _rms_group(kva[:, LANES:], bd32_ref[...], g_ref[4:5, 0:LANES])
        if states:
            _store_rows(st_ckv_ref, c_kv)
            _store_transposed(st_kpe_ref, k_pe, 0, keep=MLA_ROPE)
        if rope:
            k_pe = _rope(k_pe, c32_ref[...], s32_ref[...], MLA_ROPE)
        kpe_ref[...] = k_pe.astype(BF16)
        _mla_kv(c_kv, wkvb_ref, bd64_ref, g_ref[5:6, 0:256], kn_ref, vm_ref)

    def diff_queries(dq):
        for c in range(2):
            sl = slice(256 * c, 256 * (c + 1))
            qn = _rms_group(dq[:, sl], bd64, g_ref[6:7, 0:256])
            if rope:
                qn = _rope(qn, c64_ref[...], s64_ref[...], DIFF_HD)
            dq_ref[:, sl] = (qn * diff_scale).astype(BF16)

    def diff_keys(dk):
        for c in range(2):
            sl = slice(256 * c, 256 * (c + 1))
            kn = _rms_group(dk[:, sl], bd64, g_ref[7:8, 0:256])
            if states:
                _store_transposed(st_dk_ref, kn, 256 * c)
            if rope:
                kn = _rope(kn, c64_ref[...], s64_ref[...], DIFF_HD)
            dk_ref[:, sl] = kn.astype(BF16)

    def diff_values(dv):
        if states:
            _store_heads4(st_dv_ref, dv)
        dv_ref[...] = dv.astype(BF16)

    _staggered(hb, w_ref, [(0, 256, mla_queries), (256, 512, mla_keys), (512, 1024, _gate_into(mg_ref)),
                           (1024, 1536, diff_queries), (1536, 2048, diff_keys), (2048, 2560, diff_values),
                           (2560, 3072, _gate_into(dg_ref))])


def _in_odd_kernel(*refs, rope, states, n_alias, row_base, tiles_per_batch):
    it = iter(refs)
    x_ref, mod_ref, g_ref, w_ref, bd64_ref = (next(it) for _ in range(5))
    if rope:
        c64_ref, s64_ref = (next(it) for _ in range(2))
    for _ in range(n_alias):
        next(it)
    nq_ref, nk_ref, nv_ref, ng_ref, gq_ref, gk_ref, gv_ref, gg_ref = (next(it) for _ in range(8))
    if states:
        st_nk_ref, st_nv_ref, st_gk_ref, st_gv_ref = (next(it) for _ in range(4))

    row = row_base + pl.program_id(0) // tiles_per_batch
    hb = _modulated_norm(x_ref, mod_ref, g_ref, row)
    bd64 = bd64_ref[...]
    na_scale = NA_HD ** -0.5 * LOG2E
    gqa_scale = GQA_HD ** -0.5 * LOG2E

    def na_queries(nq):
        for c in range(2):
            sl = slice(256 * c, 256 * (c + 1))
            nq_ref[:, sl] = (_rms_group(nq[:, sl], bd64, g_ref[1:2, 0:256]) * na_scale).astype(BF16)

    def na_keys(nk):
        for c in range(2):
            sl = slice(256 * c, 256 * (c + 1))
            kn = _rms_group(nk[:, sl], bd64, g_ref[2:3, 0:256])
            if states:
                _store_transposed(st_nk_ref, kn, 256 * c)
            nk_ref[:, sl] = kn.astype(BF16)

    def na_values(nv):
        if states:
            for c in range(2):
                _store_transposed(st_nv_ref, nv[:, 256 * c:256 * (c + 1)], 256 * c)
        nv_ref[...] = nv.astype(BF16)

    def gqa_queries(gq):
        for c in range(2):
            sl = slice(256 * c, 256 * (c + 1))
            qn = _rms_group(gq[:, sl], bd64, g_ref[3:4, 0:256])
            if rope:
                qn = _rope(qn, c64_ref[...], s64_ref[...], GQA_HD)
            gq_ref[:, sl] = (qn * gqa_scale).astype(BF16)

    def gqa_keys_values(gkv):
        gk = _rms_group(gkv[:, :LANES], bd64[:LANES, :LANES], g_ref[4:5, 0:LANES])
        gv = gkv[:, LANES:]
        if states:
            _store_transposed(st_gk_ref, gk, 0)
            _store_transposed(st_gv_ref, gv, 0)
        if rope:
            gk = _rope(gk, c64_ref[...], s64_ref[...], GQA_HD)
        gk_ref[...] = gk.astype(BF16)
        gv_ref[...] = gv.astype(BF16)

    _staggered(hb, w_ref, [(0, 512, na_queries), (512, 1024, na_keys), (1024, 1536, na_values),
                           (1536, 2048, _gate_into(ng_ref)), (2048, 2560, gqa_queries),
                           (2560, 2816, gqa_keys_values), (2816, 3328, _gate_into(gg_ref))])


def _full(shape):
    zeros = (0,) * len(shape)
    return pl.BlockSpec(shape, lambda *_: zeros)


class _Layer(NamedTuple):
    stack: jax.Array
    index: int


def _spec(a):
    if isinstance(a, _Layer):
        index = (a.index,) + (0,) * (a.stack.ndim - 1)
        return pl.BlockSpec((None,) + a.stack.shape[1:], lambda *_: index)
    return _full(a.shape)


def _arr(a):
    return a.stack if isinstance(a, _Layer) else a


def _in_proj(kernel, x, mods, layer, gains, weights, consts, ropes, out_widths, row_base, tokens_per_batch):
    rows = x.shape[0]
    tm = ROW_TILE
    tiles_per_batch = tokens_per_batch // tm
    in_specs = [pl.BlockSpec((tm, D_MODEL), lambda i: (i, 0)),
                pl.BlockSpec((1, 8, 3 * D_MODEL), lambda i: (layer, 0, 0)),
                _full(gains.shape)]
    in_specs += [_spec(a) for a in (*weights, *consts)]
    in_specs += [pl.BlockSpec((tm, LANES), lambda i: (i % tiles_per_batch, 0)) for _ in ropes]
    return pl.pallas_call(
        functools.partial(kernel, rope=True, states=False, n_alias=0, row_base=row_base,
                          tiles_per_batch=tiles_per_batch),
        out_shape=[jax.ShapeDtypeStruct((rows, w), BF16) for w in out_widths],
        grid=(rows // tm,),
        in_specs=in_specs,
        out_specs=[pl.BlockSpec((tm, w), lambda i: (i, 0)) for w in out_widths],
        compiler_params=_cparams(),
        name=kernel.__name__.strip("_"),
    )(x, mods, gains, *map(_arr, weights), *consts, *ropes)


def _mla_cache_kernel(c_ref, wkvb_ref, bd64_ref, g_ref, kn_ref, vm_ref):
    _mla_kv(c_ref[...], wkvb_ref, bd64_ref, g_ref[5:6, 0:256], kn_ref, vm_ref)


def _mla_cache_kv(ckv, wkvb, bd64, gains):
    rows = ckv.shape[0]
    return pl.pallas_call(
        _mla_cache_kernel,
        out_shape=[jax.ShapeDtypeStruct((rows, 512), BF16)] * 2,
        grid=(1,),
        in_specs=[_full(ckv.shape), _spec(wkvb), _full(bd64.shape), _full(gains.shape)],
        out_specs=[_full((rows, 512))] * 2,
        compiler_params=_cparams(),
        name="mla_cache_kv",
    )(ckv, _arr(wkvb), bd64, gains)


def _pair_attn_kernel(*refs, mode, kinds, mxu_denominator, lam_init):
    it = iter(refs)
    q_ref = next(it)
    segs = [[next(it) for _ in seg_kinds] for seg_kinds in kinds]
    gate_ref = next(it)
    if mode == "diff":
        lam_ref, subln_ref = next(it), next(it)
    o_ref = next(it)

    qw = 2 * LANES if mode == "mla" else LANES
    rows = q_ref.shape[0]
    lo = _lane_mask(rows, LANES, 0, LANES // 2)
    if mode == "diff":
        lv = lam_ref[...]
        lam = (jnp.exp(jnp.sum(lv[0:1] * lv[1:2], axis=-1, keepdims=True))
               - jnp.exp(jnp.sum(lv[2:3] * lv[3:4], axis=-1, keepdims=True)) + lam_init)

    def load(ref, kind, p):
        if kind == "rows4":
            x = ref[pl.ds(p, ref.shape[0] // 4, stride=4), :]
        elif mode == "gqa":
            x = ref[...]
        else:
            sl = slice(LANES * p, LANES * (p + 1))
            x = ref[sl, :] if kind == "cols" else ref[:, sl]
        return x.astype(BF16), kind == "cols"

    def values(p):
        return [load(seg[-1], seg_kinds[-1], p) for seg, seg_kinds in zip(segs, kinds)]

    def scores(p, j):
        q = q_ref[:, qw * p:qw * (p + 1)]
        ksegs = []
        for seg, seg_kinds in zip(segs, kinds):
            k, transposed = load(seg[0], seg_kinds[0], p)
            if mode == "mla":
                k = jnp.concatenate([k, seg[1][...].astype(BF16)], axis=-1)
            ksegs.append((k, transposed))
        keep = _lane_mask(rows, qw, 64 * j, 64 * (j + 1))
        if mode == "mla":
            keep = keep | _lane_mask(rows, qw, LANES + 32 * j, LANES + 32 * (j + 1))
        return _scores(jnp.where(keep, q, jnp.zeros_like(q)), ksegs)

    def finish(p, outs):
        sl = slice(LANES * p, LANES * (p + 1))
        if mode == "diff":
            d = outs[0] - lam * outs[1]
            o = _rms_full(d, subln_ref[...]) * (1.0 - lam_init)
        else:
            o = jnp.where(lo, outs[0], outs[1])
        o_ref[:, sl] = (o * gate_ref[:, sl].astype(F32)).astype(BF16)

    heads = [(p, j) for p in range(PAIRS) for j in range(2)]
    ss = scores(*heads[0])
    outs = []
    for t, (p, j) in enumerate(heads):
        ss_next = scores(*heads[t + 1]) if t + 1 < len(heads) else None
        spare = (64 * (1 - j), 64 * (2 - j)) if mxu_denominator and mode != "diff" else None
        outs.append(_softmax_pv(ss, values(p), ones_lanes=spare))
        ss = ss_next
        if j == 1:
            finish(p, outs)
            outs = []


def _latent_attention(mode, q, segs, gate, batch, tq, extra=(), lam_init=0.0):
    bq = Q_BLOCK
    qw = 2 * LANES if mode == "mla" else LANES
    nq = tq // bq
    in_specs = [pl.BlockSpec((bq, qw * PAIRS), lambda b, i: (b * nq + i, 0))]
    args = [q]
    for seg in segs:
        for arr, spec, _ in seg:
            args.append(arr)
            in_specs.append(spec)
    kinds = tuple(tuple(kind for _, _, kind in seg) for seg in segs)
    in_specs.append(pl.BlockSpec((bq, LANES * PAIRS), lambda b, i: (b * nq + i, 0)))
    args.append(gate)
    for arr in extra:
        args.append(arr)
        in_specs.append(pl.BlockSpec(arr.shape, lambda b, i: (0,) * arr.ndim))
    return pl.pallas_call(
        functools.partial(_pair_attn_kernel, mode=mode, kinds=kinds, mxu_denominator=True, lam_init=lam_init),
        out_shape=jax.ShapeDtypeStruct((batch * tq, LANES * PAIRS), BF16),
        grid=(batch, nq),
        in_specs=in_specs,
        out_specs=pl.BlockSpec((bq, LANES * PAIRS), lambda b, i: (b * nq + i, 0)),
        compiler_params=_cparams(),
        name=mode + "_attention",
    )(*args)


def _self_seg(arr, tk):
    return arr, pl.BlockSpec((tk, arr.shape[1]), lambda b, i: (b, 0)), "rows"


def _cache_seg(arr, layer, kind):
    return arr, pl.BlockSpec((None, None) + arr.shape[2:], lambda b, i: (b, layer, 0, 0)), kind


def _na_kernel(q_ref, k_ref, v_ref, ck_ref, cv_ref, bias_ref, gate_ref, o_ref):
    g = pl.program_id(1)
    n_groups = pl.num_programs(1)
    first_row = jnp.clip(NA_GROUP_ROWS * g - NA_ROWS // 2, 0, NA_GROUP_ROWS * n_groups - NA_WIN_ROWS)
    start = pl.multiple_of(first_row * GRID_W, GRID_W)
    win = NA_WIN_ROWS * GRID_W
    kwin = k_ref[pl.ds(start, win), :]
    vwin = v_ref[pl.ds(start, win), :]
    ck = ck_ref[...].astype(BF16)
    cv = cv_ref[...].astype(BF16)
    q = q_ref[...]
    rows = q.shape[0]
    lo = _lane_mask(rows, LANES, 0, LANES // 2)

    def scores(h):
        sl = slice(LANES * (h // 2), LANES * (h // 2 + 1))
        keep = _lane_mask(rows, LANES, 64 * (h % 2), 64 * (h % 2 + 1))
        qm = jnp.where(keep, q[:, sl], jnp.zeros_like(q[:, sl]))
        return _scores(qm, [(kwin[:, sl], False), (ck[sl, :], True)], bias0=bias_ref[h])

    ss = scores(0)
    outs = []
    for h in range(NA_HEADS):
        ss_next = scores(h + 1) if h + 1 < NA_HEADS else None
        sl = slice(LANES * (h // 2), LANES * (h // 2 + 1))
        j = h % 2
        outs.append(_softmax_pv(ss, [(vwin[:, sl], False), (cv[sl, :], True)],
                                ones_lanes=(64 * (1 - j), 64 * (2 - j))))
        if j == 1:
            o = jnp.where(lo, outs[0], outs[1])
            o_ref[:, sl] = (o * gate_ref[:, sl].astype(F32)).astype(BF16)
            outs = []
        ss = ss_next


def _na_attention(q, k, v, cache_k, cache_v, layer, bias, gate, batch, tq):
    bq = NA_GROUP_ROWS * GRID_W
    n_groups = tq // bq
    past = cache_k.shape[3]
    width = NA_HEADS * NA_HD

    def bias_map(b, g):
        return (layer, jnp.where(g == 0, 0, jnp.where(g == n_groups - 1, 2, 1)), 0, 0, 0)

    tok = pl.BlockSpec((bq, width), lambda b, g: (b * n_groups + g, 0))
    whole = pl.BlockSpec((tq, width), lambda b, g: (b, 0))
    cache = pl.BlockSpec((None, None, width, past), lambda b, g: (b, layer, 0, 0))
    return pl.pallas_call(
        _na_kernel,
        out_shape=jax.ShapeDtypeStruct((batch * tq, width), BF16),
        grid=(batch, n_groups),
        in_specs=[tok, whole, whole, cache, cache,
                  pl.BlockSpec((None, None, NA_HEADS, bq, NA_WIN_ROWS * GRID_W), bias_map), tok],
        out_specs=tok,
        compiler_params=_cparams(),
        name="na_attention",
    )(q, k, v, cache_k, cache_v, bias, gate)


def _na_bias_tables(rpb):
    n_dr = 2 * NA_ROWS - 1
    idx = np.full((3, NA_GROUP_ROWS, NA_WIN_ROWS), n_dr, np.int32)
    for a in range(NA_GROUP_ROWS):
        for j in range(NA_WIN_ROWS):
            if j < NA_ROWS:
                idx[0, a, j] = j - a + NA_ROWS - 1
            if a <= j < a + NA_ROWS:
                idx[1, a, j] = j - a + NA_ROWS // 2 - 1
            if j >= NA_WIN_ROWS - NA_ROWS:
                idx[2, a, j] = j - a - (NA_WIN_ROWS - NA_ROWS) + NA_ROWS // 2 - 1
    layers = rpb.shape[0]
    padded = jnp.pad(rpb.astype(F32), ((0, 0), (0, 0), (0, 1), (0, LANES - rpb.shape[3])))
    return pl.pallas_call(
        functools.partial(_na_bias_kernel, idx=idx),
        out_shape=jax.ShapeDtypeStruct((layers, 3, NA_HEADS, NA_GROUP_ROWS * GRID_W, NA_WIN_ROWS * GRID_W), F32),
        grid=(layers, NA_HEADS),
        in_specs=[pl.BlockSpec((None, None, n_dr + 1, LANES), lambda l, h: (l, h, 0, 0))],
        out_specs=pl.BlockSpec((None, 3, None, NA_GROUP_ROWS * GRID_W, NA_WIN_ROWS * GRID_W),
                               lambda l, h: (l, 0, h, 0, 0)),
        compiler_params=_cparams(),
        name="na_bias",
    )(padded)


def _na_bias_kernel(rpb_ref, out_ref, *, idx):
    n_dr = rpb_ref.shape[0] - 1
    qc = lax.broadcasted_iota(jnp.int32, (GRID_W, LANES), 0)
    lane = lax.broadcasted_iota(jnp.int32, (GRID_W, LANES), 1)
    kc = lane & (GRID_W - 1)
    first = jnp.clip(qc - NA_COLS // 2, 0, GRID_W - NA_COLS)
    valid = (kc >= first) & (kc < first + NA_COLS)
    low = lane < GRID_W
    outside = jnp.full((GRID_W, LANES), NEG, F32)

    def table(r):
        if r == n_dr:
            return outside
        row = jnp.broadcast_to(rpb_ref[r:r + 1, :], (GRID_W, LANES))
        lo = pltpu.roll(row, LANES - (NA_COLS - 1), 1, stride=1, stride_axis=0)
        hi = pltpu.roll(row, GRID_W - (NA_COLS - 1), 1, stride=1, stride_axis=0)
        return jnp.where(valid, jnp.where(low, lo, hi) * LOG2E, NEG)

    tables = [table(r) for r in range(n_dr + 1)]
    for t in range(3):
        for a in range(NA_GROUP_ROWS):
            for j in range(0, NA_WIN_ROWS, 2):
                pair = jnp.where(low, tables[int(idx[t, a, j])], tables[int(idx[t, a, j + 1])])
                out_ref[t, a * GRID_W:(a + 1) * GRID_W, j * GRID_W:(j + 2) * GRID_W] = pair


def _out_kernel(oa_ref, ob_ref, x_ref, mod_ref, w_ref, y_ref, *, row_base, tiles_per_batch):
    row = row_base + pl.program_id(0) // tiles_per_batch
    gate = mod_ref[0, pl.ds(row, 1), :][:, 2 * D_MODEL:]
    half = oa_ref.shape[1]
    acc = _dot(oa_ref[...], w_ref[:half, :]) + _dot(ob_ref[...], w_ref[half:, :])
    y_ref[...] = x_ref[...] + gate * acc


def _out_proj(oa, ob, x, mods, layer, w, row_base, tokens_per_batch):
    rows = x.shape[0]
    tm = ROW_TILE
    tiles_per_batch = tokens_per_batch // tm
    return pl.pallas_call(
        functools.partial(_out_kernel, row_base=row_base, tiles_per_batch=tiles_per_batch),
        out_shape=jax.ShapeDtypeStruct((rows, D_MODEL), F32),
        grid=(rows // tm,),
        in_specs=[pl.BlockSpec((tm, oa.shape[1]), lambda i: (i, 0)),
                  pl.BlockSpec((tm, ob.shape[1]), lambda i: (i, 0)),
                  pl.BlockSpec((tm, D_MODEL), lambda i: (i, 0)),
                  pl.BlockSpec((1, 8, 3 * D_MODEL), lambda i: (layer, 0, 0)),
                  _spec(w)],
        out_specs=pl.BlockSpec((tm, D_MODEL), lambda i: (i, 0)),
        compiler_params=_cparams(),
        name="out_proj",
    )(oa, ob, x, mods, _arr(w))


def _out_in_kernel(*refs, proj, rope, row_base, tiles_per_batch):
    oa_ref, ob_ref, x_ref, mod_prev_ref, wout_ref = refs[:5]
    n_proj_in = len(refs) - 5 - 1 - (9 if proj is _in_even_kernel else 8)
    proj_in = refs[5:5 + n_proj_in]
    y_ref = refs[5 + n_proj_in]
    slabs = refs[6 + n_proj_in:]
    _out_kernel(oa_ref, ob_ref, x_ref, mod_prev_ref, wout_ref, y_ref, row_base=row_base,
                tiles_per_batch=tiles_per_batch)
    proj(y_ref, *proj_in, *slabs, rope=rope, states=False, n_alias=0, row_base=row_base,
         tiles_per_batch=tiles_per_batch)


def _out_in_proj(proj, oa, ob, x, mods, layer, w_out, gains, weights, consts, ropes, out_widths, row_base,
                 tokens_per_batch):
    rows = x.shape[0]
    tm = ROW_TILE
    tiles_per_batch = tokens_per_batch // tm
    row = pl.BlockSpec((tm, D_MODEL), lambda i: (i, 0))
    in_specs = [pl.BlockSpec((tm, oa.shape[1]), lambda i: (i, 0)), pl.BlockSpec((tm, ob.shape[1]), lambda i: (i, 0)),
                row, pl.BlockSpec((1, 8, 3 * D_MODEL), lambda i: (layer - 1, 0, 0)), _spec(w_out),
                pl.BlockSpec((1, 8, 3 * D_MODEL), lambda i: (layer, 0, 0)), _full(gains.shape)]
    in_specs += [_spec(a) for a in (*weights, *consts)]
    in_specs += [pl.BlockSpec((tm, LANES), lambda i: (i % tiles_per_batch, 0)) for _ in ropes]
    return pl.pallas_call(
        functools.partial(_out_in_kernel, proj=proj, rope=True, row_base=row_base, tiles_per_batch=tiles_per_batch),
        out_shape=[jax.ShapeDtypeStruct((rows, D_MODEL), F32)]
        + [jax.ShapeDtypeStruct((rows, w), BF16) for w in out_widths],
        grid=(rows // tm,),
        in_specs=in_specs,
        out_specs=[row] + [pl.BlockSpec((tm, w), lambda i: (i, 0)) for w in out_widths],
        compiler_params=_cparams(),
        name="out_" + proj.__name__.strip("_"),
    )(oa, ob, x, mods, _arr(w_out), mods, gains, *map(_arr, weights), *consts, *ropes)


def _ctx_layer_kernel(*refs, even, n_in, n_alias, n_slabs, seq, lam_init, tiles_per_batch):
    in_refs = refs[:n_in]
    it = iter(refs[n_in:])
    wout_ref = next(it)
    if even:
        lam_ref, subln_ref = next(it), next(it)
    alias_refs = [next(it) for _ in range(n_alias)]
    y_ref = next(it)
    state_refs = [next(it) for _ in range(4)]
    slabs = [next(it) for _ in range(n_slabs)]
    oa_ref, ob_ref = next(it), next(it)
    x_ref, mod_ref = in_refs[0], in_refs[1]

    proj = _in_even_kernel if even else _in_odd_kernel
    proj(*in_refs, *alias_refs, *slabs, *state_refs, rope=False, states=True, n_alias=n_alias, row_base=0,
         tiles_per_batch=tiles_per_batch)
    for b in range(x_ref.shape[0] // seq):
        def own(ref):
            return ref.at[pl.ds(b * seq, seq), :]
        common = dict(mxu_denominator=False)
        if even:
            qcat, kn, kpe, vm, mg, dq, dk, dv, dg = (own(r) for r in slabs)
            _pair_attn_kernel(qcat, kn, kpe, vm, mg, own(oa_ref), mode="mla", kinds=(("rows",) * 3,),
                              lam_init=0.0, **common)
            _pair_attn_kernel(dq, dk, dv, dg, lam_ref, subln_ref, own(ob_ref), mode="diff", kinds=(("rows",) * 2,),
                              lam_init=lam_init, **common)
        else:
            nq, nk, nv, ng, gq, gk, gv, gg = (own(r) for r in slabs)
            _pair_attn_kernel(nq, nk, nv, ng, own(oa_ref), mode="mha", kinds=(("rows",) * 2,), lam_init=0.0, **common)
            _pair_attn_kernel(gq, gk, gv, gg, own(ob_ref), mode="gqa", kinds=(("rows",) * 2,), lam_init=0.0, **common)
    _out_kernel(oa_ref, ob_ref, x_ref, mod_ref, wout_ref, y_ref, row_base=0, tiles_per_batch=tiles_per_batch)


def _ctx_layer(even, x, mods, layer, gains, weights, consts, w_out, extras, slab_widths, state_tails, state_prev,
               seq, lam_init=0.0):
    rows = x.shape[0]
    tm = ROW_TILE
    slot = layer // 2
    in_specs = [pl.BlockSpec((tm, D_MODEL), lambda i: (i, 0)),
                pl.BlockSpec((1, 8, 3 * D_MODEL), lambda i: (layer, 0, 0)),
                _full(gains.shape)]
    in_specs += [_spec(a) for a in (*weights, *consts)]
    args = [x, mods, gains, *map(_arr, weights), *consts]
    n_in = len(args)
    in_specs += [_spec(a) for a in (w_out, *extras)]
    args += [_arr(w_out), *extras]
    aliases = {}
    if state_prev is not None:
        aliases = {len(args) + j: 1 + j for j in range(len(state_prev))}
        in_specs += [pl.BlockSpec(memory_space=pl.ANY) for _ in state_prev]
        args += list(state_prev)
    out_shape = [jax.ShapeDtypeStruct((rows, D_MODEL), F32)]
    out_specs = [pl.BlockSpec((tm, D_MODEL), lambda i: (i, 0))]
    out_shape += [jax.ShapeDtypeStruct((rows // seq, DEPTH // 2) + tail, F32) for tail in state_tails]
    out_specs += [pl.BlockSpec((tm // seq, None) + tail, lambda i: (i, slot, 0, 0)) for tail in state_tails]
    scratch = [pltpu.VMEM((tm, w), BF16) for w in (*slab_widths, 4 * LANES, 4 * LANES)]
    return pl.pallas_call(
        functools.partial(_ctx_layer_kernel, even=even, n_in=n_in, n_alias=len(aliases), n_slabs=len(slab_widths),
                          seq=seq, lam_init=lam_init, tiles_per_batch=rows // tm),
        out_shape=out_shape,
        grid=(rows // tm,),
        in_specs=in_specs,
        out_specs=out_specs,
        scratch_shapes=scratch,
        input_output_aliases=aliases,
        compiler_params=_cparams(),
        name="ctx_layer_even" if even else "ctx_layer_odd",
    )(*args)


def _layout_in_even_kernel(wt_ref, o_ref):
    kpe0 = Q_LORA + KV_LORA
    o_ref[:, 0:kpe0] = wt_ref[0:kpe0, :].T.astype(BF16)
    kpe = wt_ref[kpe0:kpe0 + MLA_ROPE, :]
    o_ref[:, kpe0:kpe0 + LANES] = jnp.concatenate([kpe] * (LANES // MLA_ROPE), axis=0).T.astype(BF16)
    o_ref[:, kpe0 + LANES:] = wt_ref[kpe0 + MLA_ROPE:, :].T.astype(BF16)


def _layout_in_even(w):
    layers, k, cols = w.shape
    kc = 512
    return pl.pallas_call(
        _layout_in_even_kernel,
        out_shape=jax.ShapeDtypeStruct((layers, k, cols + LANES - MLA_ROPE), BF16),
        grid=(layers, k // kc),
        in_specs=[pl.BlockSpec((None, cols, kc), lambda l, r: (l, 0, r))],
        out_specs=pl.BlockSpec((None, kc, cols + LANES - MLA_ROPE), lambda l, r: (l, r, 0)),
        compiler_params=_cparams(),
        name="layout_in_even",
    )(w.transpose(0, 2, 1))


def _layout_in_odd_kernel(w_ref, o_ref):
    def permuted(base):
        for k in range(GQA_HEADS // 2):
            lo_head, hi_head = GQA_PERM[2 * k], GQA_PERM[2 * k + 1]
            lo = w_ref[:, base + LANES * (lo_head // 2):base + LANES * (lo_head // 2 + 1)]
            hi = w_ref[:, base + LANES * (hi_head // 2):base + LANES * (hi_head // 2 + 1)]
            if lo_head % 2 == 1:
                lo = pltpu.roll(lo, LANES // 2, 1)
            if hi_head % 2 == 0:
                hi = pltpu.roll(hi, LANES // 2, 1)
            lane = lax.broadcasted_iota(jnp.int32, lo.shape, 1)
            o_ref[:, base + LANES * k:base + LANES * (k + 1)] = jnp.where(lane < LANES // 2, lo, hi).astype(BF16)

    o_ref[:, 0:2048] = w_ref[:, 0:2048].astype(BF16)
    permuted(2048)
    o_ref[:, 2560:2816] = w_ref[:, 2560:2816].astype(BF16)
    permuted(2816)


def _layout_w_out_kernel(w_ref, o_ref, *, permute):
    half = w_ref.shape[0] // 2
    o_ref[:half, :] = w_ref[:half, :].astype(BF16)
    if permute:
        for k, head in enumerate(GQA_PERM):
            o_ref[half + GQA_HD * k:half + GQA_HD * (k + 1), :] = (
                w_ref[half + GQA_HD * head:half + GQA_HD * (head + 1), :].astype(BF16))
    else:
        o_ref[half:, :] = w_ref[half:, :].astype(BF16)


def _layout_in_odd(w):
    layers, rows, cols = w.shape
    rb = 512
    return pl.pallas_call(
        _layout_in_odd_kernel,
        out_shape=jax.ShapeDtypeStruct((layers, rows, cols), BF16),
        grid=(layers, rows // rb),
        in_specs=[pl.BlockSpec((None, rb, cols), lambda l, r: (l, r, 0))],
        out_specs=pl.BlockSpec((None, rb, cols), lambda l, r: (l, r, 0)),
        compiler_params=_cparams(),
        name="layout_in_odd",
    )(w)


def _layout_w_out(w, permute):
    layers, rows, cols = w.shape
    return pl.pallas_call(
        functools.partial(_layout_w_out_kernel, permute=permute),
        out_shape=jax.ShapeDtypeStruct((layers, rows, cols), BF16),
        grid=(layers,),
        in_specs=[pl.BlockSpec((None, rows, cols), lambda l: (l, 0, 0))],
        out_specs=pl.BlockSpec((None, rows, cols), lambda l: (l, 0, 0)),
        compiler_params=_cparams(),
        name="layout_w_out",
    )(w)


def _block_diag(width, group):
    idx = np.arange(width) // group
    return jnp.asarray((idx[:, None] == idx[None, :]).astype(np.float32) / group, BF16)


def _rope_tables(t, rot_dim):
    pos = np.arange(t)
    row = (pos // GRID_W).astype(np.float64)
    col = (pos % GRID_W).astype(np.float64)
    n = rot_dim // 2
    inv = ROPE_THETA ** (-np.arange(0, n, 2, dtype=np.float64) / n)
    ang = np.concatenate([row[:, None] * inv, col[:, None] * inv], axis=-1)
    cos = np.concatenate([np.cos(ang), np.cos(ang)], axis=-1)
    sin = np.concatenate([-np.sin(ang), np.sin(ang)], axis=-1)
    reps = LANES // rot_dim
    return (jnp.asarray(np.tile(cos, (1, reps)), F32), jnp.asarray(np.tile(sin, (1, reps)), F32))


def _pad_row(v, width=D_MODEL):
    return jnp.pad(v, (0, width - v.shape[0]))


def _tile_row(v, reps):
    return _pad_row(jnp.tile(v, reps))


def kernel(x_prompt, x_sample, cache_mla_ckv, cache_mla_kpe, cache_diff_k, cache_diff_v, cache_na_k, cache_na_v, cache_gqa_k, cache_gqa_v, c, c_ctx, norm_w, w_mod, b_mod, w_in_even, w_out_even, mla_qa_norm, mla_wqb, mla_kva_norm, mla_wkvb, mla_qn_nope, mla_qn_rope, mla_kn_nope, mla_kn_rope, diff_qn, diff_kn, diff_lq1, diff_lk1, diff_lq2, diff_lk2, diff_subln, w_in_odd, w_out_odd, na_qn, na_kn, na_rpb, gqa_qn, gqa_kn):
    batch, seq, _ = x_prompt.shape
    dec_batch, dec_seq, _ = x_sample.shape
    past = cache_mla_ckv.shape[2]
    n_even, n_odd = w_in_even.shape[0], w_in_odd.shape[0]

    def layers(stack):
        return [_Layer(stack, i) for i in range(stack.shape[0])]

    w_in_e = layers(_layout_in_even(w_in_even))
    w_in_o = layers(_layout_in_odd(w_in_odd))
    w_out_e = layers(_layout_w_out(w_out_even, permute=False))
    w_out_o = layers(_layout_w_out(w_out_odd, permute=True))
    wqb = mla_wqb.reshape(n_even, Q_LORA, MLA_HEADS, MLA_QK)
    wqb = jnp.concatenate([wqb[..., :MLA_NOPE].reshape(n_even, Q_LORA, 4, 2 * MLA_NOPE),
                           wqb[..., MLA_NOPE:].reshape(n_even, Q_LORA, 4, 2 * MLA_ROPE),
                           jnp.zeros((n_even, Q_LORA, 4, LANES - 2 * MLA_ROPE), F32)], axis=-1)
    wqb = layers(wqb.reshape(n_even, Q_LORA, 4 * 2 * LANES).astype(BF16))
    wkvb = mla_wkvb.reshape(n_even, KV_LORA, MLA_HEADS, 2 * MLA_NOPE)
    wkvb = layers(jnp.concatenate([wkvb[..., :MLA_NOPE].reshape(n_even, KV_LORA, 512),
                                   wkvb[..., MLA_NOPE:].reshape(n_even, KV_LORA, 512)], axis=-1).astype(BF16))

    gains_e = [jnp.stack([norm_w[2 * i], _pad_row(mla_qa_norm[i]),
                          _pad_row(jnp.concatenate([jnp.tile(mla_qn_nope[i], 2), jnp.tile(mla_qn_rope[i], 4)])),
                          _pad_row(mla_kva_norm[i]), _tile_row(mla_kn_rope[i], 4), _tile_row(mla_kn_nope[i], 4),
                          _tile_row(diff_qn[i], 4), _tile_row(diff_kn[i], 4)]) for i in range(n_even)]
    gains_o = [jnp.stack([norm_w[2 * i + 1], _tile_row(na_qn[i], 4), _tile_row(na_kn[i], 4),
                          _tile_row(gqa_qn[i], 4), _tile_row(gqa_kn[i], 2),
                          jnp.zeros((D_MODEL,), F32), jnp.zeros((D_MODEL,), F32), jnp.zeros((D_MODEL,), F32)])
               for i in range(n_odd)]
    lam_vecs = [jnp.stack([diff_lq1[i], diff_lk1[i], diff_lq2[i], diff_lk2[i]]) for i in range(n_even)]

    bd64 = _block_diag(256, 64)
    bd32 = _block_diag(LANES, 32)
    bdq = jnp.concatenate([jnp.concatenate([_block_diag(LANES, 64), jnp.zeros((LANES, LANES), BF16)], axis=1),
                           jnp.concatenate([jnp.zeros((LANES, LANES), BF16), _block_diag(LANES, 32)], axis=1)], axis=0)
    c64, s64 = _rope_tables(dec_seq, 64)
    c32, s32 = _rope_tables(dec_seq, MLA_ROPE)
    na_bias = _na_bias_tables(na_rpb)

    cvecs = jnp.concatenate([c_ctx[None, :], c, jnp.zeros((8 - 1 - dec_batch, D_MODEL), F32)], axis=0)
    mods = _modulation(cvecs, w_mod, b_mod)

    cache_kpe = jnp.tile(cache_mla_kpe, (1, 1, 1, 4))
    cache_dk = cache_diff_k.transpose(0, 1, 3, 4, 5, 2).reshape(dec_batch, n_even, 512, past)
    cache_dv = cache_diff_v.reshape(dec_batch, n_even, 4 * past, LANES)
    cache_nk = cache_na_k.transpose(0, 1, 3, 4, 2).reshape(dec_batch, n_odd, 512, past)
    cache_nv = cache_na_v.transpose(0, 1, 3, 4, 2).reshape(dec_batch, n_odd, 512, past)
    cache_gk = cache_gqa_k.transpose(0, 1, 3, 4, 2).reshape(dec_batch, n_odd, LANES, past)
    cache_gv = cache_gqa_v.transpose(0, 1, 3, 4, 2).reshape(dec_batch, n_odd, LANES, past)

    even_widths = (1024, 512, LANES, 512, 512, 512, 512, 512, 512)
    odd_widths = (512, 512, 512, 512, 512, LANES, LANES, 512)
    even_states = ((seq, KV_LORA), (MLA_ROPE, seq), (512, seq), (4 * seq, LANES))
    odd_states = ((512, seq), (512, seq), (LANES, seq), (LANES, seq))

    def lam_init(l):
        return 0.8 - 0.6 * math.exp(-0.3 * l)

    def context_pass(x):
        states = [None, None]
        for l in range(DEPTH):
            i = l // 2
            if l % 2 == 0:
                x, *states[0] = _ctx_layer(True, x, mods, l, gains_e[i], (w_in_e[i], wqb[i], wkvb[i]),
                                           (bd64, bd32, bdq), w_out_e[i], (lam_vecs[i], diff_subln[i][None, :]),
                                           even_widths, even_states, states[0], seq, lam_init(l))
            else:
                x, *states[1] = _ctx_layer(False, x, mods, l, gains_o[i], (w_in_o[i],), (bd64,), w_out_o[i], (),
                                           odd_widths, odd_states, states[1], seq)
        return x, states

    def latent_pass(x, nb, t):
        row_base = 1
        oa = ob = w_out_prev = None
        for l in range(DEPTH):
            i = l // 2
            if l % 2 == 0:
                proj_args = (gains_e[i], (w_in_e[i], wqb[i], wkvb[i]), (bd64, bd32, bdq), (c64, s64, c32, s32),
                             even_widths)
                proj = _in_even_kernel
            else:
                proj_args = (gains_o[i], (w_in_o[i],), (bd64,), (c64, s64), odd_widths)
                proj = _in_odd_kernel
            if l == 0:
                outs = _in_proj(proj, x, mods, l, *proj_args, row_base, t)
            else:
                x, *outs = _out_in_proj(proj, oa, ob, x, mods, l, w_out_prev, *proj_args, row_base, t)
            if l % 2 == 0:
                qcat, kn, kpe, vm, mg, dq, dk, dv, dg = outs
                kn_c, vm_c = _mla_cache_kv(cache_mla_ckv[:, i].reshape(nb * past, KV_LORA), wkvb[i], bd64, gains_e[i])
                mla_segs = [(_self_seg(kn, t), _self_seg(kpe, t), _self_seg(vm, t)),
                            (_self_seg(kn_c, past), _cache_seg(cache_kpe, i, "rows"), _self_seg(vm_c, past))]
                diff_segs = [(_self_seg(dk, t), _self_seg(dv, t)),
                             (_cache_seg(cache_dk, i, "cols"), _cache_seg(cache_dv, i, "rows4"))]
                oa = _latent_attention("mla", qcat, mla_segs, mg, nb, t)
                ob = _latent_attention("diff", dq, diff_segs, dg, nb, t,
                                       extra=(lam_vecs[i], diff_subln[i][None, :]), lam_init=lam_init(l))
                w_out_prev = w_out_e[i]
            else:
                nq, nk, nv, ng, gq, gk, gv, gg = outs
                oa = _na_attention(nq, nk, nv, cache_nk, cache_nv, i, na_bias, ng, nb, t)
                gqa_segs = [(_self_seg(gk, t), _self_seg(gv, t)),
                            (_cache_seg(cache_gk, i, "cols"), _cache_seg(cache_gv, i, "cols"))]
                ob = _latent_attention("gqa", gq, gqa_segs, gg, nb, t)
                w_out_prev = w_out_o[i]
        return _out_proj(oa, ob, x, mods, DEPTH - 1, w_out_prev, row_base, t)

    y_prompt, st = context_pass(x_prompt.reshape(batch * seq, D_MODEL))
    y_sample = latent_pass(x_sample.reshape(dec_batch * dec_seq, D_MODEL), dec_batch, dec_seq)

    def token_major(a, heads):
        a = a.reshape((batch, n_even) + heads + (a.shape[2] // math.prod(heads), seq))
        return jnp.moveaxis(a, -1, 2)

    ckv, kpe_t, dk_t, dv4 = st[0]
    nk_t, nv_t, gk_t, gv_t = st[1]
    return (y_prompt.reshape(batch, seq, D_MODEL), y_sample.reshape(dec_batch, dec_seq, D_MODEL),
            ckv, token_major(kpe_t, ()), token_major(dk_t, (DIFF_HEADS, 2)),
            dv4.reshape(batch, n_even, seq, DIFF_HEADS, 2 * DIFF_HD),
            token_major(nk_t, (NA_HEADS,)), token_major(nv_t, (NA_HEADS,)),
            token_major(gk_t, (GQA_KV,)), token_major(gv_t, (GQA_KV,)))
```

```python
import functools
import math
from typing import NamedTuple

import jax
import jax.numpy as jnp
import numpy as np
from jax import lax
from jax.experimental import pallas as pl
from jax.experimental.pallas import tpu as pltpu

F32 = jnp.float32
BF16 = jnp.bfloat16

D_MODEL = 1024
DEPTH = 4
GRID_W = 64
ROPE_THETA = 10000.0
EPS = 1e-6
MLA_HEADS = 8
MLA_NOPE = 64
MLA_ROPE = 32
MLA_QK = MLA_NOPE + MLA_ROPE
Q_LORA = 256
KV_LORA = 128
DIFF_HEADS = 4
DIFF_HD = 64
NA_HEADS = 8
NA_HD = 64
NA_ROWS = 8
NA_COLS = 16
GQA_HEADS = 8
GQA_KV = 2
GQA_HD = 64
LANES = 128
PAIRS = 4
NA_GROUP_ROWS = 4
NA_WIN_ROWS = NA_ROWS + NA_GROUP_ROWS
NEG = -1e30
LOG2E = math.log2(math.e)
VMEM_LIMIT = 48 * 1024 * 1024
GQA_PERM = (0, 4, 1, 5, 2, 6, 3, 7)

ROW_TILE = 512
Q_BLOCK = {"mla": 1024, "gqa": 1024, "diff": 512}


def _cparams():
    return pltpu.CompilerParams(vmem_limit_bytes=VMEM_LIMIT)


def _dot(a, b):
    return jnp.dot(a, b, preferred_element_type=F32)


def _dot_nt(a, b):
    return lax.dot_general(a, b, (((1,), (1,)), ((), ())), preferred_element_type=F32)


def _rms_full(x, g):
    ms = jnp.mean(x * x, axis=-1, keepdims=True)
    return x * lax.rsqrt(ms + EPS) * g


def _rms_group(x, bd, g):
    ms = _dot((x * x).astype(BF16), bd)
    return x * lax.rsqrt(ms + EPS) * g


def _silu(u):
    return u * (1.0 / (1.0 + jnp.exp(-u)))


def _rope(x, cos, sin, group):
    half = group // 2
    rows, width = x.shape
    lane = lax.broadcasted_iota(jnp.int32, (rows, LANES), 1)
    first = (lane & (group - 1)) < half
    outs = []
    for c in range(width // LANES):
        xc = x[:, c * LANES:(c + 1) * LANES]
        rot = jnp.where(first, pltpu.roll(xc, LANES - half, 1), pltpu.roll(xc, half, 1))
        outs.append(xc * cos + rot * sin)
    return outs[0] if len(outs) == 1 else jnp.concatenate(outs, axis=-1)


def _lane_mask(rows, width, lo, hi):
    lane = lax.broadcasted_iota(jnp.int32, (rows, width), 1)
    return (lane >= lo) & (lane < hi)


def _scores(qm, ksegs, bias0=None):
    ss = [_dot(qm, k) if transposed else _dot_nt(qm, k) for k, transposed in ksegs]
    if bias0 is not None:
        ss[0] = ss[0] + bias0
    return ss


def _softmax_pv(ss, vsegs, ones_lanes=None):
    m = jnp.max(ss[0], axis=-1, keepdims=True)
    for s in ss[1:]:
        m = jnp.maximum(m, jnp.max(s, axis=-1, keepdims=True))
    acc = None
    l = None
    for s, (v, transposed) in zip(ss, vsegs):
        p = jnp.exp2(s - m)
        if ones_lanes is None:
            ps = jnp.sum(p, axis=-1, keepdims=True)
            l = ps if l is None else l + ps
        else:
            width = lax.broadcasted_iota(jnp.int32, v.shape, 0 if transposed else 1)
            v = jnp.where((width >= ones_lanes[0]) & (width < ones_lanes[1]), jnp.ones_like(v), v)
        a = _dot_nt(p.astype(BF16), v) if transposed else _dot(p.astype(BF16), v)
        acc = a if acc is None else acc + a
    if ones_lanes is not None:
        l = pltpu.roll(acc, LANES // 2, 1)
    return acc / l


def _mod_kernel(c_ref, w_ref, b_ref, o_ref):
    c = c_ref[...]
    o_ref[0] = _dot(_silu(c).astype(BF16), w_ref[0].astype(BF16)) + b_ref[0]


def _modulation(cvecs, w_mod, b_mod):
    tn = 1536
    return pl.pallas_call(
        _mod_kernel,
        out_shape=jax.ShapeDtypeStruct((DEPTH, 8, 3 * D_MODEL), F32),
        grid=(DEPTH, 3 * D_MODEL // tn),
        in_specs=[pl.BlockSpec((8, D_MODEL), lambda l, n: (0, 0)),
                  pl.BlockSpec((1, D_MODEL, tn), lambda l, n: (l, 0, n)),
                  pl.BlockSpec((1, 1, tn), lambda l, n: (l, 0, n))],
        out_specs=pl.BlockSpec((1, 8, tn), lambda l, n: (l, 0, n)),
        compiler_params=_cparams(),
        name="modulation",
    )(cvecs, w_mod, b_mod.reshape(DEPTH, 1, 3 * D_MODEL))


def _modulated_norm(x_ref, mod_ref, g_ref, row):
    mod = mod_ref[0, pl.ds(row, 1), :]
    shift = mod[:, :D_MODEL]
    scale = mod[:, D_MODEL:2 * D_MODEL]
    h = _rms_full(x_ref[...], g_ref[0:1, :]) * (1.0 + scale) + shift
    return h.astype(BF16)


def _mla_kv(cn, wkvb_ref, bd64_ref, g_kn, kn_ref, vm_ref):
    kv = _dot(cn.astype(BF16), wkvb_ref[...])
    for c in range(2):
        sl = slice(256 * c, 256 * (c + 1))
        kn_ref[:, sl] = _rms_group(kv[:, sl], bd64_ref[...], g_kn).astype(BF16)
    vm_ref[...] = kv[:, 512:].astype(BF16)


def _store_rows(ref, x):
    seq = ref.shape[1]
    for b in range(ref.shape[0]):
        ref[b] = x[b * seq:(b + 1) * seq]


def _store_transposed(ref, x, row0, keep=None):
    seq = ref.shape[2]
    for b in range(ref.shape[0]):
        xt = x[b * seq:(b + 1) * seq].T
        if keep is not None:
            xt = xt[:keep]
        ref[b, row0:row0 + xt.shape[0], :] = xt


def _store_heads4(ref, x):
    seq = ref.shape[1] // 4
    for b in range(ref.shape[0]):
        for h in range(4):
            ref[b, pl.ds(h, seq, stride=4), :] = x[b * seq:(b + 1) * seq, LANES * h:LANES * (h + 1)]


def _staggered(hb, w_ref, stages):
    pending = _dot(hb, w_ref[:, stages[0][0]:stages[0][1]])
    for t, (_, _, epilogue) in enumerate(stages):
        current = pending
        if t + 1 < len(stages):
            pending = _dot(hb, w_ref[:, stages[t + 1][0]:stages[t + 1][1]])
        epilogue(current)


def _gate_into(out_ref):
    def epilogue(u):
        out_ref[...] = _silu(u).astype(BF16)
    return epilogue


def _in_even_kernel(*refs, rope, states, n_alias, row_base, tiles_per_batch):
    it = iter(refs)
    x_ref, mod_ref, g_ref, w_ref, wqb_ref, wkvb_ref, bd64_ref, bd32_ref, bdq_ref = (next(it) for _ in range(9))
    if rope:
        c64_ref, s64_ref, c32_ref, s32_ref = (next(it) for _ in range(4))
    for _ in range(n_alias):
        next(it)
    qcat_ref, kn_ref, kpe_ref, vm_ref, mg_ref, dq_ref, dk_ref, dv_ref, dg_ref = (next(it) for _ in range(9))
    if states:
        st_ckv_ref, st_kpe_ref, st_dk_ref, st_dv_ref = (next(it) for _ in range(4))

    row = row_base + pl.program_id(0) // tiles_per_batch
    hb = _modulated_norm(x_ref, mod_ref, g_ref, row)
    bd64 = bd64_ref[...]
    mla_scale = MLA_QK ** -0.5 * LOG2E
    diff_scale = DIFF_HD ** -0.5 * LOG2E

    def mla_queries(qa):
        qa_n = _rms_full(qa, g_ref[1:2, 0:256]).astype(BF16)
        q = _dot(qa_n, wqb_ref[...])
        for p in range(4):
            qp = _rms_group(q[:, 256 * p:256 * (p + 1)], bdq_ref[...], g_ref[2:3, 0:256])
            q_nope = qp[:, :LANES]
            q_pe = qp[:, LANES:]
            if rope:
                q_pe = _rope(q_pe, c32_ref[...], s32_ref[...], MLA_ROPE)
            qcat_ref[:, 256 * p:256 * p + LANES] = (q_nope * mla_scale).astype(BF16)
            qcat_ref[:, 256 * p + LANES:256 * (p + 1)] = (q_pe * mla_scale).astype(BF16)

    def mla_keys(kva):
        c_kv = _rms_full(kva[:, :LANES], g_ref[3:4, 0:LANES])
        k_pe = _rms_group(kva[:, LANES:], bd32_ref[...], g_ref[4:5, 0:LANES])
        if states:
            _store_rows(st_ckv_ref, c_kv)
            _store_transposed(st_kpe_ref, k_pe, 0, keep=MLA_ROPE)
        if rope:
            k_pe = _rope(k_pe, c32_ref[...], s32_ref[...], MLA_ROPE)
        kpe_ref[...] = k_pe.astype(BF16)
        _mla_kv(c_kv, wkvb_ref, bd64_ref, g_ref[5:6, 0:256], kn_ref, vm_ref)

    def diff_queries(dq):
        for c in range(2):
            sl = slice(256 * c, 256 * (c + 1))
            qn = _rms_group(dq[:, sl], bd64, g_ref[6:7, 0:256])
            if rope:
                qn = _rope(qn, c64_ref[...], s64_ref[...], DIFF_HD)
            dq_ref[:, sl] = (qn * diff_scale).astype(BF16)

    def diff_keys(dk):
        for c in range(2):
            sl = slice(256 * c, 256 * (c + 1))
            kn = _rms_group(dk[:, sl], bd64, g_ref[7:8, 0:256])
            if states:
                _store_transposed(st_dk_ref, kn, 256 * c)
            if rope:
                kn = _rope(kn, c64_ref[...], s64_ref[...], DIFF_HD)
            dk_ref[:, sl] = kn.astype(BF16)

    def diff_values(dv):
        if states:
            _store_heads4(st_dv_ref, dv)
        dv_ref[...] = dv.astype(BF16)

    _staggered(hb, w_ref, [(0, 256, mla_queries), (256, 512, mla_keys), (512, 1024, _gate_into(mg_ref)),
                           (1024, 1536, diff_queries), (1536, 2048, diff_keys), (2048, 2560, diff_values),
                           (2560, 3072, _gate_into(dg_ref))])


def _in_odd_kernel(*refs, rope, states, n_alias, row_base, tiles_per_batch):
    it = iter(refs)
    x_ref, mod_ref, g_ref, w_ref, bd64_ref = (next(it) for _ in range(5))
    if rope:
        c64_ref, s64_ref = (next(it) for _ in range(2))
    for _ in range(n_alias):
        next(it)
    nq_ref, nk_ref, nv_ref, ng_ref, gq_ref, gk_ref, gv_ref, gg_ref = (next(it) for _ in range(8))
    if states:
        st_nk_ref, st_nv_ref, st_gk_ref, st_gv_ref = (next(it) for _ in range(4))

    row = row_base + pl.program_id(0) // tiles_per_batch
    hb = _modulated_norm(x_ref, mod_ref, g_ref, row)
    bd64 = bd64_ref[...]
    na_scale = NA_HD ** -0.5 * LOG2E
    gqa_scale = GQA_HD ** -0.5 * LOG2E

    def na_queries(nq):
        for c in range(2):
            sl = slice(256 * c, 256 * (c + 1))
            nq_ref[:, sl] = (_rms_group(nq[:, sl], bd64, g_ref[1:2, 0:256]) * na_scale).astype(BF16)

    def na_keys(nk):
        for c in range(2):
            sl = slice(256 * c, 256 * (c + 1))
            kn = _rms_group(nk[:, sl], bd64, g_ref[2:3, 0:256])
            if states:
                _store_transposed(st_nk_ref, kn, 256 * c)
            nk_ref[:, sl] = kn.astype(BF16)

    def na_values(nv):
        if states:
            for c in range(2):
                _store_transposed(st_nv_ref, nv[:, 256 * c:256 * (c + 1)], 256 * c)
        nv_ref[...] = nv.astype(BF16)

    def gqa_queries(gq):
        for c in range(2):
            sl = slice(256 * c, 256 * (c + 1))
            qn = _rms_group(gq[:, sl], bd64, g_ref[3:4, 0:256])
            if rope:
                qn = _rope(qn, c64_ref[...], s64_ref[...], GQA_HD)
            gq_ref[:, sl] = (qn * gqa_scale).astype(BF16)

    def gqa_keys_values(gkv):
        gk = _rms_group(gkv[:, :LANES], bd64[:LANES, :LANES], g_ref[4:5, 0:LANES])
        gv = gkv[:, LANES:]
        if states:
            _store_transposed(st_gk_ref, gk, 0)
            _store_transposed(st_gv_ref, gv, 0)
        if rope:
            gk = _rope(gk, c64_ref[...], s64_ref[...], GQA_HD)
        gk_ref[...] = gk.astype(BF16)
        gv_ref[...] = gv.astype(BF16)

    _staggered(hb, w_ref, [(0, 512, na_queries), (512, 1024, na_keys), (1024, 1536, na_values),
                           (1536, 2048, _gate_into(ng_ref)), (2048, 2560, gqa_queries),
                           (2560, 2816, gqa_keys_values), (2816, 3328, _gate_into(gg_ref))])


def _full(shape):
    zeros = (0,) * len(shape)
    return pl.BlockSpec(shape, lambda *_: zeros)


class _Layer(NamedTuple):
    stack: jax.Array
    index: int


def _spec(a):
    if isinstance(a, _Layer):
        index = (a.index,) + (0,) * (a.stack.ndim - 1)
        return pl.BlockSpec((None,) + a.stack.shape[1:], lambda *_: index)
    return _full(a.shape)


def _arr(a):
    return a.stack if isinstance(a, _Layer) else a


def _in_proj(kernel, x, mods, layer, gains, weights, consts, ropes, out_widths, row_base, tokens_per_batch):
    rows = x.shape[0]
    tm = ROW_TILE
    tiles_per_batch = tokens_per_batch // tm
    in_specs = [pl.BlockSpec((tm, D_MODEL), lambda i: (i, 0)),
                pl.BlockSpec((1, 8, 3 * D_MODEL), lambda i: (layer, 0, 0)),
                _full(gains.shape)]
    in_specs += [_spec(a) for a in (*weights, *consts)]
    in_specs += [pl.BlockSpec((tm, LANES), lambda i: (i % tiles_per_batch, 0)) for _ in ropes]
    return pl.pallas_call(
        functools.partial(kernel, rope=True, states=False, n_alias=0, row_base=row_base,
                          tiles_per_batch=tiles_per_batch),
        out_shape=[jax.ShapeDtypeStruct((rows, w), BF16) for w in out_widths],
        grid=(rows // tm,),
        in_specs=in_specs,
        out_specs=[pl.BlockSpec((tm, w), lambda i: (i, 0)) for w in out_widths],
        compiler_params=_cparams(),
        name=kernel.__name__.strip("_"),
    )(x, mods, gains, *map(_arr, weights), *consts, *ropes)


def _mla_cache_kernel(c_ref, wkvb_ref, bd64_ref, g_ref, kn_ref, vm_ref):
    _mla_kv(c_ref[...], wkvb_ref, bd64_ref, g_ref[5:6, 0:256], kn_ref, vm_ref)


def _mla_cache_kv(ckv, wkvb, bd64, gains):
    rows = ckv.shape[0]
    return pl.pallas_call(
        _mla_cache_kernel,
        out_shape=[jax.ShapeDtypeStruct((rows, 512), BF16)] * 2,
        grid=(1,),
        in_specs=[_full(ckv.shape), _spec(wkvb), _full(bd64.shape), _full(gains.shape)],
        out_specs=[_full((rows, 512))] * 2,
        compiler_params=_cparams(),
        name="mla_cache_kv",
    )(ckv, _arr(wkvb), bd64, gains)


def _pair_attn_kernel(*refs, mode, kinds, mxu_denominator, lam_init):
    it = iter(refs)
    q_ref = next(it)
    segs = [[next(it) for _ in seg_kinds] for seg_kinds in kinds]
    gate_ref = next(it)
    if mode == "diff":
        lam_ref, subln_ref = next(it), next(it)
    o_ref = next(it)

    qw = 2 * LANES if mode == "mla" else LANES
    rows = q_ref.shape[0]
    lo = _lane_mask(rows, LANES, 0, LANES // 2)
    if mode == "diff":
        lv = lam_ref[...]
        lam = (jnp.exp(jnp.sum(lv[0:1] * lv[1:2], axis=-1, keepdims=True))
               - jnp.exp(jnp.sum(lv[2:3] * lv[3:4], axis=-1, keepdims=True)) + lam_init)

    def load(ref, kind, p):
        if kind == "rows4":
            x = ref[pl.ds(p, ref.shape[0] // 4, stride=4), :]
        elif mode == "gqa":
            x = ref[...]
        else:
            sl = slice(LANES * p, LANES * (p + 1))
            x = ref[sl, :] if kind == "cols" else ref[:, sl]
        return x.astype(BF16), kind == "cols"

    def values(p):
        return [load(seg[-1], seg_kinds[-1], p) for seg, seg_kinds in zip(segs, kinds)]

    def scores(p, j):
        q = q_ref[:, qw * p:qw * (p + 1)]
        ksegs = []
        for seg, seg_kinds in zip(segs, kinds):
            k, transposed = load(seg[0], seg_kinds[0], p)
            if mode == "mla":
                k = jnp.concatenate([k, seg[1][...].astype(BF16)], axis=-1)
            ksegs.append((k, transposed))
        keep = _lane_mask(rows, qw, 64 * j, 64 * (j + 1))
        if mode == "mla":
            keep = keep | _lane_mask(rows, qw, LANES + 32 * j, LANES + 32 * (j + 1))
        return _scores(jnp.where(keep, q, jnp.zeros_like(q)), ksegs)

    def finish(p, outs):
        sl = slice(LANES * p, LANES * (p + 1))
        if mode == "diff":
            d = outs[0] - lam * outs[1]
            o = _rms_full(d, subln_ref[...]) * (1.0 - lam_init)
        else:
            o = jnp.where(lo, outs[0], outs[1])
        o_ref[:, sl] = (o * gate_ref[:, sl].astype(F32)).astype(BF16)

    heads = [(p, j) for p in range(PAIRS) for j in range(2)]
    ss = scores(*heads[0])
    outs = []
    for t, (p, j) in enumerate(heads):
        ss_next = scores(*heads[t + 1]) if t + 1 < len(heads) else None
        spare = (64 * (1 - j), 64 * (2 - j)) if mxu_denominator and mode != "diff" else None
        outs.append(_softmax_pv(ss, values(p), ones_lanes=spare))
        ss = ss_next
        if j == 1:
            finish(p, outs)
            outs = []


def _latent_attention(mode, q, segs, gate, batch, tq, extra=(), lam_init=0.0):
    bq = Q_BLOCK[mode]
    qw = 2 * LANES if mode == "mla" else LANES
    nq = tq // bq
    in_specs = [pl.BlockSpec((bq, qw * PAIRS), lambda b, i: (b * nq + i, 0))]
    args = [q]
    for seg in segs:
        for arr, spec, _ in seg:
            args.append(arr)
            in_specs.append(spec)
    kinds = tuple(tuple(kind for _, _, kind in seg) for seg in segs)
    in_specs.append(pl.BlockSpec((bq, LANES * PAIRS), lambda b, i: (b * nq + i, 0)))
    args.append(gate)
    for arr in extra:
        args.append(arr)
        in_specs.append(pl.BlockSpec(arr.shape, lambda b, i: (0,) * arr.ndim))
    return pl.pallas_call(
        functools.partial(_pair_attn_kernel, mode=mode, kinds=kinds, mxu_denominator=True, lam_init=lam_init),
        out_shape=jax.ShapeDtypeStruct((batch * tq, LANES * PAIRS), BF16),
        grid=(batch, nq),
        in_specs=in_specs,
        out_specs=pl.BlockSpec((bq, LANES * PAIRS), lambda b, i: (b * nq + i, 0)),
        compiler_params=_cparams(),
        name=mode + "_attention",
    )(*args)


def _self_seg(arr, tk):
    return arr, pl.BlockSpec((tk, arr.shape[1]), lambda b, i: (b, 0)), "rows"


def _cache_seg(arr, layer, kind):
    return arr, pl.BlockSpec((None, None) + arr.shape[2:], lambda b, i: (b, layer, 0, 0)), kind


def _na_kernel(q_ref, k_ref, v_ref, ck_ref, cv_ref, bias_ref, gate_ref, o_ref):
    g = pl.program_id(1)
    n_groups = pl.num_programs(1)
    first_row = jnp.clip(NA_GROUP_ROWS * g - NA_ROWS // 2, 0, NA_GROUP_ROWS * n_groups - NA_WIN_ROWS)
    start = pl.multiple_of(first_row * GRID_W, GRID_W)
    win = NA_WIN_ROWS * GRID_W
    kwin = k_ref[pl.ds(start, win), :]
    vwin = v_ref[pl.ds(start, win), :]
    ck = ck_ref[...].astype(BF16)
    cv = cv_ref[...].astype(BF16)
    q = q_ref[...]
    rows = q.shape[0]
    lo = _lane_mask(rows, LANES, 0, LANES // 2)

    def scores(h):
        sl = slice(LANES * (h // 2), LANES * (h // 2 + 1))
        keep = _lane_mask(rows, LANES, 64 * (h % 2), 64 * (h % 2 + 1))
        qm = jnp.where(keep, q[:, sl], jnp.zeros_like(q[:, sl]))
        return _scores(qm, [(kwin[:, sl], False), (ck[sl, :], True)], bias0=bias_ref[h])

    ss = scores(0)
    outs = []
    for h in range(NA_HEADS):
        ss_next = scores(h + 1) if h + 1 < NA_HEADS else None
        sl = slice(LANES * (h // 2), LANES * (h // 2 + 1))
        j = h % 2
        outs.append(_softmax_pv(ss, [(vwin[:, sl], False), (cv[sl, :], True)],
                                ones_lanes=(64 * (1 - j), 64 * (2 - j))))
        if j == 1:
            o = jnp.where(lo, outs[0], outs[1])
            o_ref[:, sl] = (o * gate_ref[:, sl].astype(F32)).astype(BF16)
            outs = []
        ss = ss_next


def _na_attention(q, k, v, cache_k, cache_v, layer, bias, gate, batch, tq):
    bq = NA_GROUP_ROWS * GRID_W
    n_groups = tq // bq
    past = cache_k.shape[3]
    width = NA_HEADS * NA_HD

    def bias_map(b, g):
        return (layer, jnp.where(g == 0, 0, jnp.where(g == n_groups - 1, 2, 1)), 0, 0, 0)

    tok = pl.BlockSpec((bq, width), lambda b, g: (b * n_groups + g, 0))
    whole = pl.BlockSpec((tq, width), lambda b, g: (b, 0))
    cache = pl.BlockSpec((None, None, width, past), lambda b, g: (b, layer, 0, 0))
    return pl.pallas_call(
        _na_kernel,
        out_shape=jax.ShapeDtypeStruct((batch * tq, width), BF16),
        grid=(batch, n_groups),
        in_specs=[tok, whole, whole, cache, cache,
                  pl.BlockSpec((None, None, NA_HEADS, bq, NA_WIN_ROWS * GRID_W), bias_map), tok],
        out_specs=tok,
        compiler_params=_cparams(),
        name="na_attention",
    )(q, k, v, cache_k, cache_v, bias, gate)


def _na_bias_tables(rpb):
    n_dr = 2 * NA_ROWS - 1
    idx = np.full((3, NA_GROUP_ROWS, NA_WIN_ROWS), n_dr, np.int32)
    for a in range(NA_GROUP_ROWS):
        for j in range(NA_WIN_ROWS):
            if j < NA_ROWS:
                idx[0, a, j] = j - a + NA_ROWS - 1
            if a <= j < a + NA_ROWS:
                idx[1, a, j] = j - a + NA_ROWS // 2 - 1
            if j >= NA_WIN_ROWS - NA_ROWS:
                idx[2, a, j] = j - a - (NA_WIN_ROWS - NA_ROWS) + NA_ROWS // 2 - 1
    layers = rpb.shape[0]
    padded = jnp.pad(rpb.astype(F32), ((0, 0), (0, 0), (0, 1), (0, LANES - rpb.shape[3])))
    return pl.pallas_call(
        functools.partial(_na_bias_kernel, idx=idx),
        out_shape=jax.ShapeDtypeStruct((layers, 3, NA_HEADS, NA_GROUP_ROWS * GRID_W, NA_WIN_ROWS * GRID_W), F32),
        grid=(layers, NA_HEADS),
        in_specs=[pl.BlockSpec((None, None, n_dr + 1, LANES), lambda l, h: (l, h, 0, 0))],
        out_specs=pl.BlockSpec((None, 3, None, NA_GROUP_ROWS * GRID_W, NA_WIN_ROWS * GRID_W),
                               lambda l, h: (l, 0, h, 0, 0)),
        compiler_params=_cparams(),
        name="na_bias",
    )(padded)


def _na_bias_kernel(rpb_ref, out_ref, *, idx):
    n_dr = rpb_ref.shape[0] - 1
    qc = lax.broadcasted_iota(jnp.int32, (GRID_W, LANES), 0)
    lane = lax.broadcasted_iota(jnp.int32, (GRID_W, LANES), 1)
    kc = lane & (GRID_W - 1)
    first = jnp.clip(qc - NA_COLS // 2, 0, GRID_W - NA_COLS)
    valid = (kc >= first) & (kc < first + NA_COLS)
    low = lane < GRID_W
    outside = jnp.full((GRID_W, LANES), NEG, F32)

    def table(r):
        if r == n_dr:
            return outside
        row = jnp.broadcast_to(rpb_ref[r:r + 1, :], (GRID_W, LANES))
        lo = pltpu.roll(row, LANES - (NA_COLS - 1), 1, stride=1, stride_axis=0)
        hi = pltpu.roll(row, GRID_W - (NA_COLS - 1), 1, stride=1, stride_axis=0)
        return jnp.where(valid, jnp.where(low, lo, hi) * LOG2E, NEG)

    tables = [table(r) for r in range(n_dr + 1)]
    for t in range(3):
        for a in range(NA_GROUP_ROWS):
            for j in range(0, NA_WIN_ROWS, 2):
                pair = jnp.where(low, tables[int(idx[t, a, j])], tables[int(idx[t, a, j + 1])])
                out_ref[t, a * GRID_W:(a + 1) * GRID_W, j * GRID_W:(j + 2) * GRID_W] = pair


def _out_kernel(oa_ref, ob_ref, x_ref, mod_ref, w_ref, y_ref, *, row_base, tiles_per_batch):
    row = row_base + pl.program_id(0) // tiles_per_batch
    gate = mod_ref[0, pl.ds(row, 1), :][:, 2 * D_MODEL:]
    half = oa_ref.shape[1]
    acc = _dot(oa_ref[...], w_ref[:half, :]) + _dot(ob_ref[...], w_ref[half:, :])
    y_ref[...] = x_ref[...] + gate * acc


def _out_proj(oa, ob, x, mods, layer, w, row_base, tokens_per_batch):
    rows = x.shape[0]
    tm = ROW_TILE
    tiles_per_batch = tokens_per_batch // tm
    return pl.pallas_call(
        functools.partial(_out_kernel, row_base=row_base, tiles_per_batch=tiles_per_batch),
        out_shape=jax.ShapeDtypeStruct((rows, D_MODEL), F32),
        grid=(rows // tm,),
        in_specs=[pl.BlockSpec((tm, oa.shape[1]), lambda i: (i, 0)),
                  pl.BlockSpec((tm, ob.shape[1]), lambda i: (i, 0)),
                  pl.BlockSpec((tm, D_MODEL), lambda i: (i, 0)),
                  pl.BlockSpec((1, 8, 3 * D_MODEL), lambda i: (layer, 0, 0)),
                  _spec(w)],
        out_specs=pl.BlockSpec((tm, D_MODEL), lambda i: (i, 0)),
        compiler_params=_cparams(),
        name="out_proj",
    )(oa, ob, x, mods, _arr(w))


def _out_in_kernel(*refs, proj, rope, row_base, tiles_per_batch):
    oa_ref, ob_ref, x_ref, mod_prev_ref, wout_ref = refs[:5]
    n_proj_in = len(refs) - 5 - 1 - (9 if proj is _in_even_kernel else 8)
    proj_in = refs[5:5 + n_proj_in]
    y_ref = refs[5 + n_proj_in]
    slabs = refs[6 + n_proj_in:]
    _out_kernel(oa_ref, ob_ref, x_ref, mod_prev_ref, wout_ref, y_ref, row_base=row_base,
                tiles_per_batch=tiles_per_batch)
    proj(y_ref, *proj_in, *slabs, rope=rope, states=False, n_alias=0, row_base=row_base,
         tiles_per_batch=tiles_per_batch)


def _out_in_proj(proj, oa, ob, x, mods, layer, w_out, gains, weights, consts, ropes, out_widths, row_base,
                 tokens_per_batch):
    rows = x.shape[0]
    tm = ROW_TILE
    tiles_per_batch = tokens_per_batch // tm
    row = pl.BlockSpec((tm, D_MODEL), lambda i: (i, 0))
    in_specs = [pl.BlockSpec((tm, oa.shape[1]), lambda i: (i, 0)), pl.BlockSpec((tm, ob.shape[1]), lambda i: (i, 0)),
                row, pl.BlockSpec((1, 8, 3 * D_MODEL), lambda i: (layer - 1, 0, 0)), _spec(w_out),
                pl.BlockSpec((1, 8, 3 * D_MODEL), lambda i: (layer, 0, 0)), _full(gains.shape)]
    in_specs += [_spec(a) for a in (*weights, *consts)]
    in_specs += [pl.BlockSpec((tm, LANES), lambda i: (i % tiles_per_batch, 0)) for _ in ropes]
    return pl.pallas_call(
        functools.partial(_out_in_kernel, proj=proj, rope=True, row_base=row_base, tiles_per_batch=tiles_per_batch),
        out_shape=[jax.ShapeDtypeStruct((rows, D_MODEL), F32)]
        + [jax.ShapeDtypeStruct((rows, w), BF16) for w in out_widths],
        grid=(rows // tm,),
        in_specs=in_specs,
        out_specs=[row] + [pl.BlockSpec((tm, w), lambda i: (i, 0)) for w in out_widths],
        compiler_params=_cparams(),
        name="out_" + proj.__name__.strip("_"),
    )(oa, ob, x, mods, _arr(w_out), mods, gains, *map(_arr, weights), *consts, *ropes)


def _ctx_layer_kernel(*refs, even, n_in, n_alias, n_slabs, seq, lam_init, tiles_per_batch):
    in_refs = refs[:n_in]
    it = iter(refs[n_in:])
    wout_ref = next(it)
    if even:
        lam_ref, subln_ref = next(it), next(it)
    alias_refs = [next(it) for _ in range(n_alias)]
    y_ref = next(it)
    state_refs = [next(it) for _ in range(4)]
    slabs = [next(it) for _ in range(n_slabs)]
    oa_ref, ob_ref = next(it), next(it)
    x_ref, mod_ref = in_refs[0], in_refs[1]

    proj = _in_even_kernel if even else _in_odd_kernel
    proj(*in_refs, *alias_refs, *slabs, *state_refs, rope=False, states=True, n_alias=n_alias, row_base=0,
         tiles_per_batch=tiles_per_batch)
    for b in range(x_ref.shape[0] // seq):
        def own(ref):
            return ref.at[pl.ds(b * seq, seq), :]
        common = dict(mxu_denominator=False)
        if even:
            qcat, kn, kpe, vm, mg, dq, dk, dv, dg = (own(r) for r in slabs)
            _pair_attn_kernel(qcat, kn, kpe, vm, mg, own(oa_ref), mode="mla", kinds=(("rows",) * 3,),
                              lam_init=0.0, **common)
            _pair_attn_kernel(dq, dk, dv, dg, lam_ref, subln_ref, own(ob_ref), mode="diff", kinds=(("rows",) * 2,),
                              lam_init=lam_init, **common)
        else:
            nq, nk, nv, ng, gq, gk, gv, gg = (own(r) for r in slabs)
            _pair_attn_kernel(nq, nk, nv, ng, own(oa_ref), mode="mha", kinds=(("rows",) * 2,), lam_init=0.0, **common)
            _pair_attn_kernel(gq, gk, gv, gg, own(ob_ref), mode="gqa", kinds=(("rows",) * 2,), lam_init=0.0, **common)
    _out_kernel(oa_ref, ob_ref, x_ref, mod_ref, wout_ref, y_ref, row_base=0, tiles_per_batch=tiles_per_batch)


def _ctx_layer(even, x, mods, layer, gains, weights, consts, w_out, extras, slab_widths, state_tails, state_prev,
               seq, lam_init=0.0):
    rows = x.shape[0]
    tm = ROW_TILE
    slot = layer // 2
    in_specs = [pl.BlockSpec((tm, D_MODEL), lambda i: (i, 0)),
                pl.BlockSpec((1, 8, 3 * D_MODEL), lambda i: (layer, 0, 0)),
                _full(gains.shape)]
    in_specs += [_spec(a) for a in (*weights, *consts)]
    args = [x, mods, gains, *map(_arr, weights), *consts]
    n_in = len(args)
    in_specs += [_spec(a) for a in (w_out, *extras)]
    args += [_arr(w_out), *extras]
    aliases = {}
    if state_prev is not None:
        aliases = {len(args) + j: 1 + j for j in range(len(state_prev))}
        in_specs += [pl.BlockSpec(memory_space=pl.ANY) for _ in state_prev]
        args += list(state_prev)
    out_shape = [jax.ShapeDtypeStruct((rows, D_MODEL), F32)]
    out_specs = [pl.BlockSpec((tm, D_MODEL), lambda i: (i, 0))]
    out_shape += [jax.ShapeDtypeStruct((rows // seq, DEPTH // 2) + tail, F32) for tail in state_tails]
    out_specs += [pl.BlockSpec((tm // seq, None) + tail, lambda i: (i, slot, 0, 0)) for tail in state_tails]
    scratch = [pltpu.VMEM((tm, w), BF16) for w in (*slab_widths, 4 * LANES, 4 * LANES)]
    return pl.pallas_call(
        functools.partial(_ctx_layer_kernel, even=even, n_in=n_in, n_alias=len(aliases), n_slabs=len(slab_widths),
                          seq=seq, lam_init=lam_init, tiles_per_batch=rows // tm),
        out_shape=out_shape,
        grid=(rows // tm,),
        in_specs=in_specs,
        out_specs=out_specs,
        scratch_shapes=scratch,
        input_output_aliases=aliases,
        compiler_params=_cparams(),
        name="ctx_layer_even" if even else "ctx_layer_odd",
    )(*args)


def _layout_in_even_kernel(wt_ref, o_ref):
    kpe0 = Q_LORA + KV_LORA
    o_ref[:, 0:kpe0] = wt_ref[0:kpe0, :].T.astype(BF16)
    kpe = wt_ref[kpe0:kpe0 + MLA_ROPE, :]
    o_ref[:, kpe0:kpe0 + LANES] = jnp.concatenate([kpe] * (LANES // MLA_ROPE), axis=0).T.astype(BF16)
    o_ref[:, kpe0 + LANES:] = wt_ref[kpe0 + MLA_ROPE:, :].T.astype(BF16)


def _layout_in_even(w):
    layers, k, cols = w.shape
    kc = 512
    return pl.pallas_call(
        _layout_in_even_kernel,
        out_shape=jax.ShapeDtypeStruct((layers, k, cols + LANES - MLA_ROPE), BF16),
        grid=(layers, k // kc),
        in_specs=[pl.BlockSpec((None, cols, kc), lambda l, r: (l, 0, r))],
        out_specs=pl.BlockSpec((None, kc, cols + LANES - MLA_ROPE), lambda l, r: (l, r, 0)),
        compiler_params=_cparams(),
        name="layout_in_even",
    )(w.transpose(0, 2, 1))


def _layout_in_odd_kernel(w_ref, o_ref):
    def permuted(base):
        for k in range(GQA_HEADS // 2):
            lo_head, hi_head = GQA_PERM[2 * k], GQA_PERM[2 * k + 1]
            lo = w_ref[:, base + LANES * (lo_head // 2):base + LANES * (lo_head // 2 + 1)]
            hi = w_ref[:, base + LANES * (hi_head // 2):base + LANES * (hi_head // 2 + 1)]
            if lo_head % 2 == 1:
                lo = pltpu.roll(lo, LANES // 2, 1)
            if hi_head % 2 == 0:
                hi = pltpu.roll(hi, LANES // 2, 1)
            lane = lax.broadcasted_iota(jnp.int32, lo.shape, 1)
            o_ref[:, base + LANES * k:base + LANES * (k + 1)] = jnp.where(lane < LANES // 2, lo, hi).astype(BF16)

    o_ref[:, 0:2048] = w_ref[:, 0:2048].astype(BF16)
    permuted(2048)
    o_ref[:, 2560:2816] = w_ref[:, 2560:2816].astype(BF16)
    permuted(2816)


def _layout_w_out_kernel(w_ref, o_ref, *, permute):
    half = w_ref.shape[0] // 2
    o_ref[:half, :] = w_ref[:half, :].astype(BF16)
    if permute:
        for k, head in enumerate(GQA_PERM):
            o_ref[half + GQA_HD * k:half + GQA_HD * (k + 1), :] = (
                w_ref[half + GQA_HD * head:half + GQA_HD * (head + 1), :].astype(BF16))
    else:
        o_ref[half:, :] = w_ref[half:, :].astype(BF16)


def _layout_in_odd(w):
    layers, rows, cols = w.shape
    rb = 512
    return pl.pallas_call(
        _layout_in_odd_kernel,
        out_shape=jax.ShapeDtypeStruct((layers, rows, cols), BF16),
        grid=(layers, rows // rb),
        in_specs=[pl.BlockSpec((None, rb, cols), lambda l, r: (l, r, 0))],
        out_specs=pl.BlockSpec((None, rb, cols), lambda l, r: (l, r, 0)),
        compiler_params=_cparams(),
        name="layout_in_odd",
    )(w)


def _layout_w_out(w, permute):
    layers, rows, cols = w.shape
    return pl.pallas_call(
        functools.partial(_layout_w_out_kernel, permute=permute),
        out_shape=jax.ShapeDtypeStruct((layers, rows, cols), BF16),
        grid=(layers,),
        in_specs=[pl.BlockSpec((None, rows, cols), lambda l: (l, 0, 0))],
        out_specs=pl.BlockSpec((None, rows, cols), lambda l: (l, 0, 0)),
        compiler_params=_cparams(),
        name="layout_w_out",
    )(w)


def _block_diag(width, group):
    idx = np.arange(width) // group
    return jnp.asarray((idx[:, None] == idx[None, :]).astype(np.float32) / group, BF16)


def _rope_tables(t, rot_dim):
    pos = np.arange(t)
    row = (pos // GRID_W).astype(np.float64)
    col = (pos % GRID_W).astype(np.float64)
    n = rot_dim // 2
    inv = ROPE_THETA ** (-np.arange(0, n, 2, dtype=np.float64) / n)
    ang = np.concatenate([row[:, None] * inv, col[:, None] * inv], axis=-1)
    cos = np.concatenate([np.cos(ang), np.cos(ang)], axis=-1)
    sin = np.concatenate([-np.sin(ang), np.sin(ang)], axis=-1)
    reps = LANES // rot_dim
    return (jnp.asarray(np.tile(cos, (1, reps)), F32), jnp.asarray(np.tile(sin, (1, reps)), F32))


def _pad_row(v, width=D_MODEL):
    return jnp.pad(v, (0, width - v.shape[0]))


def _tile_row(v, reps):
    return _pad_row(jnp.tile(v, reps))


def kernel(x_prompt, x_sample, cache_mla_ckv, cache_mla_kpe, cache_diff_k, cache_diff_v, cache_na_k, cache_na_v, cache_gqa_k, cache_gqa_v, c, c_ctx, norm_w, w_mod, b_mod, w_in_even, w_out_even, mla_qa_norm, mla_wqb, mla_kva_norm, mla_wkvb, mla_qn_nope, mla_qn_rope, mla_kn_nope, mla_kn_rope, diff_qn, diff_kn, diff_lq1, diff_lk1, diff_lq2, diff_lk2, diff_subln, w_in_odd, w_out_odd, na_qn, na_kn, na_rpb, gqa_qn, gqa_kn):
    batch, seq, _ = x_prompt.shape
    dec_batch, dec_seq, _ = x_sample.shape
    past = cache_mla_ckv.shape[2]
    n_even, n_odd = w_in_even.shape[0], w_in_odd.shape[0]

    def layers(stack):
        return [_Layer(stack, i) for i in range(stack.shape[0])]

    w_in_e = layers(_layout_in_even(w_in_even))
    w_in_o = layers(_layout_in_odd(w_in_odd))
    w_out_e = layers(_layout_w_out(w_out_even, permute=False))
    w_out_o = layers(_layout_w_out(w_out_odd, permute=True))
    wqb = mla_wqb.reshape(n_even, Q_LORA, MLA_HEADS, MLA_QK)
    wqb = jnp.concatenate([wqb[..., :MLA_NOPE].reshape(n_even, Q_LORA, 4, 2 * MLA_NOPE),
                           wqb[..., MLA_NOPE:].reshape(n_even, Q_LORA, 4, 2 * MLA_ROPE),
                           jnp.zeros((n_even, Q_LORA, 4, LANES - 2 * MLA_ROPE), F32)], axis=-1)
    wqb = layers(wqb.reshape(n_even, Q_LORA, 4 * 2 * LANES).astype(BF16))
    wkvb = mla_wkvb.reshape(n_even, KV_LORA, MLA_HEADS, 2 * MLA_NOPE)
    wkvb = layers(jnp.concatenate([wkvb[..., :MLA_NOPE].reshape(n_even, KV_LORA, 512),
                                   wkvb[..., MLA_NOPE:].reshape(n_even, KV_LORA, 512)], axis=-1).astype(BF16))

    gains_e = [jnp.stack([norm_w[2 * i], _pad_row(mla_qa_norm[i]),
                          _pad_row(jnp.concatenate([jnp.tile(mla_qn_nope[i], 2), jnp.tile(mla_qn_rope[i], 4)])),
                          _pad_row(mla_kva_norm[i]), _tile_row(mla_kn_rope[i], 4), _tile_row(mla_kn_nope[i], 4),
                          _tile_row(diff_qn[i], 4), _tile_row(diff_kn[i], 4)]) for i in range(n_even)]
    gains_o = [jnp.stack([norm_w[2 * i + 1], _tile_row(na_qn[i], 4), _tile_row(na_kn[i], 4),
                          _tile_row(gqa_qn[i], 4), _tile_row(gqa_kn[i], 2),
                          jnp.zeros((D_MODEL,), F32), jnp.zeros((D_MODEL,), F32), jnp.zeros((D_MODEL,), F32)])
               for i in range(n_odd)]
    lam_vecs = [jnp.stack([diff_lq1[i], diff_lk1[i], diff_lq2[i], diff_lk2[i]]) for i in range(n_even)]

    bd64 = _block_diag(256, 64)
    bd32 = _block_diag(LANES, 32)
    bdq = jnp.concatenate([jnp.concatenate([_block_diag(LANES, 64), jnp.zeros((LANES, LANES), BF16)], axis=1),
                           jnp.concatenate([jnp.zeros((LANES, LANES), BF16), _block_diag(LANES, 32)], axis=1)], axis=0)
    c64, s64 = _rope_tables(dec_seq, 64)
    c32, s32 = _rope_tables(dec_seq, MLA_ROPE)
    na_bias = _na_bias_tables(na_rpb)

    cvecs = jnp.concatenate([c_ctx[None, :], c, jnp.zeros((8 - 1 - dec_batch, D_MODEL), F32)], axis=0)
    mods = _modulation(cvecs, w_mod, b_mod)

    cache_kpe = jnp.tile(cache_mla_kpe, (1, 1, 1, 4))
    cache_dk = cache_diff_k.transpose(0, 1, 3, 4, 5, 2).reshape(dec_batch, n_even, 512, past)
    cache_dv = cache_diff_v.reshape(dec_batch, n_even, 4 * past, LANES)
    cache_nk = cache_na_k.transpose(0, 1, 3, 4, 2).reshape(dec_batch, n_odd, 512, past)
    cache_nv = cache_na_v.transpose(0, 1, 3, 4, 2).reshape(dec_batch, n_odd, 512, past)
    cache_gk = cache_gqa_k.transpose(0, 1, 3, 4, 2).reshape(dec_batch, n_odd, LANES, past)
    cache_gv = cache_gqa_v.transpose(0, 1, 3, 4, 2).reshape(dec_batch, n_odd, LANES, past)

    even_widths = (1024, 512, LANES, 512, 512, 512, 512, 512, 512)
    odd_widths = (512, 512, 512, 512, 512, LANES, LANES, 512)
    even_states = ((seq, KV_LORA), (MLA_ROPE, seq), (512, seq), (4 * seq, LANES))
    odd_states = ((512, seq), (512, seq), (LANES, seq), (LANES, seq))

    def lam_init(l):
        return 0.8 - 0.6 * math.exp(-0.3 * l)

    def context_pass(x):
        states = [None, None]
        for l in range(DEPTH):
            i = l // 2
            if l % 2 == 0:
                x, *states[0] = _ctx_layer(True, x, mods, l, gains_e[i], (w_in_e[i], wqb[i], wkvb[i]),
                                           (bd64, bd32, bdq), w_out_e[i], (lam_vecs[i], diff_subln[i][None, :]),
                                           even_widths, even_states, states[0], seq, lam_init(l))
            else:
                x, *states[1] = _ctx_layer(False, x, mods, l, gains_o[i], (w_in_o[i],), (bd64,), w_out_o[i], (),
                                           odd_widths, odd_states, states[1], seq)
        return x, states

    def latent_pass(x, nb, t):
        row_base = 1
        oa = ob = w_out_prev = None
        for l in range(DEPTH):
            i = l // 2
            if l % 2 == 0:
                proj_args = (gains_e[i], (w_in_e[i], wqb[i], wkvb[i]), (bd64, bd32, bdq), (c64, s64, c32, s32),
                             even_widths)
                proj = _in_even_kernel
            else:
                proj_args = (gains_o[i], (w_in_o[i],), (bd64,), (c64, s64), odd_widths)
                proj = _in_odd_kernel
            if l == 0:
                outs = _in_proj(proj, x, mods, l, *proj_args, row_base, t)
            else:
                x, *outs = _out_in_proj(proj, oa, ob, x, mods, l, w_out_prev, *proj_args, row_base, t)
            if l % 2 == 0:
                qcat, kn, kpe, vm, mg, dq, dk, dv, dg = outs
                kn_c, vm_c = _mla_cache_kv(cache_mla_ckv[:, i].reshape(nb * past, KV_LORA), wkvb[i], bd64, gains_e[i])
                mla_segs = [(_self_seg(kn, t), _self_seg(kpe, t), _self_seg(vm, t)),
                            (_self_seg(kn_c, past), _cache_seg(cache_kpe, i, "rows"), _self_seg(vm_c, past))]
                diff_segs = [(_self_seg(dk, t), _self_seg(dv, t)),
                             (_cache_seg(cache_dk, i, "cols"), _cache_seg(cache_dv, i, "rows4"))]
                oa = _latent_attention("mla", qcat, mla_segs, mg, nb, t)
                ob = _latent_attention("diff", dq, diff_segs, dg, nb, t,
                                       extra=(lam_vecs[i], diff_subln[i][None, :]), lam_init=lam_init(l))
                w_out_prev = w_out_e[i]
            else:
                nq, nk, nv, ng, gq, gk, gv, gg = outs
                oa = _na_attention(nq, nk, nv, cache_nk, cache_nv, i, na_bias, ng, nb, t)
                gqa_segs = [(_self_seg(gk, t), _self_seg(gv, t)),
                            (_cache_seg(cache_gk, i, "cols"), _cache_seg(cache_gv, i, "cols"))]
                ob = _latent_attention("gqa", gq, gqa_segs, gg, nb, t)
                w_out_prev = w_out_o[i]
        return _out_proj(oa, ob, x, mods, DEPTH - 1, w_out_prev, row_base, t)

    y_prompt, st = context_pass(x_prompt.reshape(batch * seq, D_MODEL))
    y_sample = latent_pass(x_sample.reshape(dec_batch * dec_seq, D_MODEL), dec_batch, dec_seq)

    def token_major(a, heads):
        a = a.reshape((batch, n_even) + heads + (a.shape[2] // math.prod(heads), seq))
        return jnp.moveaxis(a, -1, 2)

    ckv, kpe_t, dk_t, dv4 = st[0]
    nk_t, nv_t, gk_t, gv_t = st[1]
    return (y_prompt.reshape(batch, seq, D_MODEL), y_sample.reshape(dec_batch, dec_seq, D_MODEL),
            ckv, token_major(kpe_t, ()), token_major(dk_t, (DIFF_HEADS, 2)),
            dv4.reshape(batch, n_even, seq, DIFF_HEADS, 2 * DIFF_HD),
            token_major(nk_t, (NA_HEADS,)), token_major(nv_t, (NA_HEADS,)),
            token_major(gk_t, (GQA_KV,)), token_major(gv_t, (GQA_KV,)))
```

```python
import functools
import math
from typing import NamedTuple

import jax
import jax.numpy as jnp
import numpy as np
from jax import lax
from jax.experimental import pallas as pl
from jax.experimental.pallas import tpu as pltpu

F32 = jnp.float32
BF16 = jnp.bfloat16

D_MODEL = 1024
DEPTH = 4
GRID_W = 64
ROPE_THETA = 10000.0
EPS = 1e-6
MLA_HEADS = 8
MLA_NOPE = 64
MLA_ROPE = 32
MLA_QK = MLA_NOPE + MLA_ROPE
Q_LORA = 256
KV_LORA = 128
DIFF_HEADS = 4
DIFF_HD = 64
NA_HEADS = 8
NA_HD = 64
NA_ROWS = 8
NA_COLS = 16
GQA_HEADS = 8
GQA_KV = 2
GQA_HD = 64
LANES = 128
PAIRS = 4
NA_GROUP_ROWS = 4
NA_WIN_ROWS = NA_ROWS + NA_GROUP_ROWS
NEG = -1e30
LOG2E = math.log2(math.e)
VMEM_LIMIT = 48 * 1024 * 1024
GQA_PERM = (0, 4, 1, 5, 2, 6, 3, 7)

ROW_TILE = 512
Q_BLOCK = {"mla": 1024, "gqa": 1024, "diff": 512}


def _cparams():
    return pltpu.CompilerParams(vmem_limit_bytes=VMEM_LIMIT)


def _dot(a, b):
    return jnp.dot(a, b, preferred_element_type=F32)


def _dot_nt(a, b):
    return lax.dot_general(a, b, (((1,), (1,)), ((), ())), preferred_element_type=F32)


def _rms_full(x, g):
    ms = jnp.mean(x * x, axis=-1, keepdims=True)
    return x * lax.rsqrt(ms + EPS) * g


def _rms_group(x, bd, g):
    ms = _dot((x * x).astype(BF16), bd)
    return x * lax.rsqrt(ms + EPS) * g


def _silu(u):
    return u * (1.0 / (1.0 + jnp.exp(-u)))


def _rope(x, cos, sin, group):
    half = group // 2
    rows, width = x.shape
    lane = lax.broadcasted_iota(jnp.int32, (rows, LANES), 1)
    first = (lane & (group - 1)) < half
    outs = []
    for c in range(width // LANES):
        xc = x[:, c * LANES:(c + 1) * LANES]
        rot = jnp.where(first, pltpu.roll(xc, LANES - half, 1), pltpu.roll(xc, half, 1))
        outs.append(xc * cos + rot * sin)
    return outs[0] if len(outs) == 1 else jnp.concatenate(outs, axis=-1)


def _lane_mask(rows, width, lo, hi):
    lane = lax.broadcasted_iota(jnp.int32, (rows, width), 1)
    return (lane >= lo) & (lane < hi)


def _scores(qm, ksegs, bias0=None):
    ss = [_dot(qm, k) if transposed else _dot_nt(qm, k) for k, transposed in ksegs]
    if bias0 is not None:
        ss[0] = ss[0] + bias0
    return ss


def _softmax_pv(ss, vsegs, ones_lanes=None):
    m = jnp.max(ss[0], axis=-1, keepdims=True)
    for s in ss[1:]:
        m = jnp.maximum(m, jnp.max(s, axis=-1, keepdims=True))
    acc = None
    l = None
    for s, (v, transposed) in zip(ss, vsegs):
        p = jnp.exp2(s - m)
        if ones_lanes is None:
            ps = jnp.sum(p, axis=-1, keepdims=True)
            l = ps if l is None else l + ps
        else:
            width = lax.broadcasted_iota(jnp.int32, v.shape, 0 if transposed else 1)
            v = jnp.where((width >= ones_lanes[0]) & (width < ones_lanes[1]), jnp.ones_like(v), v)
        a = _dot_nt(p.astype(BF16), v) if transposed else _dot(p.astype(BF16), v)
        acc = a if acc is None else acc + a
    if ones_lanes is not None:
        l = pltpu.roll(acc, LANES // 2, 1)
    return acc / l


def _mod_kernel(c_ref, w_ref, b_ref, o_ref):
    c = c_ref[...]
    o_ref[0] = _dot(_silu(c).astype(BF16), w_ref[0].astype(BF16)) + b_ref[0]


def _modulation(cvecs, w_mod, b_mod):
    tn = 1536
    return pl.pallas_call(
        _mod_kernel,
        out_shape=jax.ShapeDtypeStruct((DEPTH, 8, 3 * D_MODEL), F32),
        grid=(DEPTH, 3 * D_MODEL // tn),
        in_specs=[pl.BlockSpec((8, D_MODEL), lambda l, n: (0, 0)),
                  pl.BlockSpec((1, D_MODEL, tn), lambda l, n: (l, 0, n)),
                  pl.BlockSpec((1, 1, tn), lambda l, n: (l, 0, n))],
        out_specs=pl.BlockSpec((1, 8, tn), lambda l, n: (l, 0, n)),
        compiler_params=_cparams(),
        name="modulation",
    )(cvecs, w_mod, b_mod.reshape(DEPTH, 1, 3 * D_MODEL))


def _modulated_norm(x_ref, mod_ref, g_ref, row):
    mod = mod_ref[0, pl.ds(row, 1), :]
    shift = mod[:, :D_MODEL]
    scale = mod[:, D_MODEL:2 * D_MODEL]
    h = _rms_full(x_ref[...], g_ref[0:1, :]) * (1.0 + scale) + shift
    return h.astype(BF16)


def _mla_kv(cn, wkvb_ref, bd64_ref, g_kn, kn_ref, vm_ref):
    kv = _dot(cn.astype(BF16), wkvb_ref[...])
    for c in range(2):
        sl = slice(256 * c, 256 * (c + 1))
        kn_ref[:, sl] = _rms_group(kv[:, sl], bd64_ref[...], g_kn).astype(BF16)
    vm_ref[...] = kv[:, 512:].astype(BF16)


def _store_rows(ref, x):
    seq = ref.shape[1]
    for b in range(ref.shape[0]):
        ref[b] = x[b * seq:(b + 1) * seq]


def _store_transposed(ref, x, row0, keep=None):
    seq = ref.shape[2]
    for b in range(ref.shape[0]):
        xt = x[b * seq:(b + 1) * seq].T
        if keep is not None:
            xt = xt[:keep]
        ref[b, row0:row0 + xt.shape[0], :] = xt


def _store_heads4(ref, x):
    seq = ref.shape[1] // 4
    for b in range(ref.shape[0]):
        for h in range(4):
            ref[b, pl.ds(h, seq, stride=4), :] = x[b * seq:(b + 1) * seq, LANES * h:LANES * (h + 1)]


def _run_staggered(items):
    pending = items[0][0]()
    for t, (_, rest) in enumerate(items):
        current = pending
        if t + 1 < len(items):
            pending = items[t + 1][0]()
        rest(current)


def _segments(hb, w_ref, stages):
    def matmul(a, b):
        return _dot(hb, w_ref[:, a:b])
    return [(functools.partial(matmul, a, b), epilogue) for a, b, epilogue in stages]


def _gate_into(out_ref):
    def epilogue(u):
        out_ref[...] = _silu(u).astype(BF16)
    return epilogue


def _in_even_kernel(*refs, **static):
    _run_staggered(_in_even_segments(*refs, **static))


def _in_odd_kernel(*refs, **static):
    _run_staggered(_in_odd_segments(*refs, **static))


def _in_even_segments(*refs, rope, states, n_alias, row_base, tiles_per_batch):
    it = iter(refs)
    x_ref, mod_ref, g_ref, w_ref, wqb_ref, wkvb_ref, bd64_ref, bd32_ref, bdq_ref = (next(it) for _ in range(9))
    if rope:
        c64_ref, s64_ref, c32_ref, s32_ref = (next(it) for _ in range(4))
    for _ in range(n_alias):
        next(it)
    qcat_ref, kn_ref, kpe_ref, vm_ref, mg_ref, dq_ref, dk_ref, dv_ref, dg_ref = (next(it) for _ in range(9))
    if states:
        st_ckv_ref, st_kpe_ref, st_dk_ref, st_dv_ref = (next(it) for _ in range(4))

    row = row_base + pl.program_id(0) // tiles_per_batch
    hb = _modulated_norm(x_ref, mod_ref, g_ref, row)
    bd64 = bd64_ref[...]
    mla_scale = MLA_QK ** -0.5 * LOG2E
    diff_scale = DIFF_HD ** -0.5 * LOG2E

    def mla_queries(qa):
        qa_n = _rms_full(qa, g_ref[1:2, 0:256]).astype(BF16)
        q = _dot(qa_n, wqb_ref[...])
        for p in range(4):
            qp = _rms_group(q[:, 256 * p:256 * (p + 1)], bdq_ref[...], g_ref[2:3, 0:256])
            q_nope = qp[:, :LANES]
            q_pe = qp[:, LANES:]
            if rope:
                q_pe = _rope(q_pe, c32_ref[...], s32_ref[...], MLA_ROPE)
            qcat_ref[:, 256 * p:256 * p + LANES] = (q_nope * mla_scale).astype(BF16)
            qcat_ref[:, 256 * p + LANES:256 * (p + 1)] = (q_pe * mla_scale).astype(BF16)

    def mla_keys(kva):
        c_kv = _rms_full(kva[:, :LANES], g_ref[3:4, 0:LANES])
        k_pe = _rms_group(kva[:, LANES:], bd32_ref[...], g_ref[4:5, 0:LANES])
        if states:
            _store_rows(st_ckv_ref, c_kv)
            _store_transposed(st_kpe_ref, k_pe, 0, keep=MLA_ROPE)
        if rope:
            k_pe = _rope(k_pe, c32_ref[...], s32_ref[...], MLA_ROPE)
        kpe_ref[...] = k_pe.astype(BF16)
        _mla_kv(c_kv, wkvb_ref, bd64_ref, g_ref[5:6, 0:256], kn_ref, vm_ref)

    def diff_queries(dq):
        for c in range(2):
            sl = slice(256 * c, 256 * (c + 1))
            qn = _rms_group(dq[:, sl], bd64, g_ref[6:7, 0:256])
            if rope:
                qn = _rope(qn, c64_ref[...], s64_ref[...], DIFF_HD)
            dq_ref[:, sl] = (qn * diff_scale).astype(BF16)

    def diff_keys(dk):
        for c in range(2):
            sl = slice(256 * c, 256 * (c + 1))
            kn = _rms_group(dk[:, sl], bd64, g_ref[7:8, 0:256])
            if states:
                _store_transposed(st_dk_ref, kn, 256 * c)
            if rope:
                kn = _rope(kn, c64_ref[...], s64_ref[...], DIFF_HD)
            dk_ref[:, sl] = kn.astype(BF16)

    def diff_values(dv):
        if states:
            _store_heads4(st_dv_ref, dv)
        dv_ref[...] = dv.astype(BF16)

    return _segments(hb, w_ref, [(0, 256, mla_queries), (256, 512, mla_keys), (512, 1024, _gate_into(mg_ref)),
                                 (1024, 1536, diff_queries), (1536, 2048, diff_keys), (2048, 2560, diff_values),
                                 (2560, 3072, _gate_into(dg_ref))])


def _in_odd_segments(*refs, rope, states, n_alias, row_base, tiles_per_batch):
    it = iter(refs)
    x_ref, mod_ref, g_ref, w_ref, bd64_ref = (next(it) for _ in range(5))
    if rope:
        c64_ref, s64_ref = (next(it) for _ in range(2))
    for _ in range(n_alias):
        next(it)
    nq_ref, nk_ref, nv_ref, ng_ref, gq_ref, gk_ref, gv_ref, gg_ref = (next(it) for _ in range(8))
    if states:
        st_nk_ref, st_nv_ref, st_gk_ref, st_gv_ref = (next(it) for _ in range(4))

    row = row_base + pl.program_id(0) // tiles_per_batch
    hb = _modulated_norm(x_ref, mod_ref, g_ref, row)
    bd64 = bd64_ref[...]
    na_scale = NA_HD ** -0.5 * LOG2E
    gqa_scale = GQA_HD ** -0.5 * LOG2E

    def na_queries(nq):
        for c in range(2):
            sl = slice(256 * c, 256 * (c + 1))
            nq_ref[:, sl] = (_rms_group(nq[:, sl], bd64, g_ref[1:2, 0:256]) * na_scale).astype(BF16)

    def na_keys(nk):
        for c in range(2):
            sl = slice(256 * c, 256 * (c + 1))
            kn = _rms_group(nk[:, sl], bd64, g_ref[2:3, 0:256])
            if states:
                _store_transposed(st_nk_ref, kn, 256 * c)
            nk_ref[:, sl] = kn.astype(BF16)

    def na_values(nv):
        if states:
            for c in range(2):
                _store_transposed(st_nv_ref, nv[:, 256 * c:256 * (c + 1)], 256 * c)
        nv_ref[...] = nv.astype(BF16)

    def gqa_queries(gq):
        for c in range(2):
            sl = slice(256 * c, 256 * (c + 1))
            qn = _rms_group(gq[:, sl], bd64, g_ref[3:4, 0:256])
            if rope:
                qn = _rope(qn, c64_ref[...], s64_ref[...], GQA_HD)
            gq_ref[:, sl] = (qn * gqa_scale).astype(BF16)

    def gqa_keys_values(gkv):
        gk = _rms_group(gkv[:, :LANES], bd64[:LANES, :LANES], g_ref[4:5, 0:LANES])
        gv = gkv[:, LANES:]
        if states:
            _store_transposed(st_gk_ref, gk, 0)
            _store_transposed(st_gv_ref, gv, 0)
        if rope:
            gk = _rope(gk, c64_ref[...], s64_ref[...], GQA_HD)
        gk_ref[...] = gk.astype(BF16)
        gv_ref[...] = gv.astype(BF16)

    return _segments(hb, w_ref, [(0, 512, na_queries), (512, 1024, na_keys), (1024, 1536, na_values),
                                 (1536, 2048, _gate_into(ng_ref)), (2048, 2560, gqa_queries),
                                 (2560, 2816, gqa_keys_values), (2816, 3328, _gate_into(gg_ref))])


def _full(shape):
    zeros = (0,) * len(shape)
    return pl.BlockSpec(shape, lambda *_: zeros)


class _Layer(NamedTuple):
    stack: jax.Array
    index: int


def _spec(a):
    if isinstance(a, _Layer):
        index = (a.index,) + (0,) * (a.stack.ndim - 1)
        return pl.BlockSpec((None,) + a.stack.shape[1:], lambda *_: index)
    return _full(a.shape)


def _arr(a):
    return a.stack if isinstance(a, _Layer) else a


def _in_proj(kernel, x, mods, layer, gains, weights, consts, ropes, out_widths, row_base, tokens_per_batch):
    rows = x.shape[0]
    tm = ROW_TILE
    tiles_per_batch = tokens_per_batch // tm
    in_specs = [pl.BlockSpec((tm, D_MODEL), lambda i: (i, 0)),
                pl.BlockSpec((1, 8, 3 * D_MODEL), lambda i: (layer, 0, 0)),
                _full(gains.shape)]
    in_specs += [_spec(a) for a in (*weights, *consts)]
    in_specs += [pl.BlockSpec((tm, LANES), lambda i: (i % tiles_per_batch, 0)) for _ in ropes]
    return pl.pallas_call(
        functools.partial(kernel, rope=True, states=False, n_alias=0, row_base=row_base,
                          tiles_per_batch=tiles_per_batch),
        out_shape=[jax.ShapeDtypeStruct((rows, w), BF16) for w in out_widths],
        grid=(rows // tm,),
        in_specs=in_specs,
        out_specs=[pl.BlockSpec((tm, w), lambda i: (i, 0)) for w in out_widths],
        compiler_params=_cparams(),
        name=kernel.__name__.strip("_"),
    )(x, mods, gains, *map(_arr, weights), *consts, *ropes)


def _mla_cache_kernel(c_ref, wkvb_ref, bd64_ref, g_ref, kn_ref, vm_ref):
    _mla_kv(c_ref[...], wkvb_ref, bd64_ref, g_ref[5:6, 0:256], kn_ref, vm_ref)


def _mla_cache_kv(ckv, wkvb, bd64, gains):
    rows = ckv.shape[0]
    return pl.pallas_call(
        _mla_cache_kernel,
        out_shape=[jax.ShapeDtypeStruct((rows, 512), BF16)] * 2,
        grid=(1,),
        in_specs=[_full(ckv.shape), _spec(wkvb), _full(bd64.shape), _full(gains.shape)],
        out_specs=[_full((rows, 512))] * 2,
        compiler_params=_cparams(),
        name="mla_cache_kv",
    )(ckv, _arr(wkvb), bd64, gains)


def _pair_attn_heads(*refs, mode, kinds, mxu_denominator, lam_init):
    it = iter(refs)
    q_ref = next(it)
    segs = [[next(it) for _ in seg_kinds] for seg_kinds in kinds]
    gate_ref = next(it)
    if mode == "diff":
        lam_ref, subln_ref = next(it), next(it)
    o_ref = next(it)

    qw = 2 * LANES if mode == "mla" else LANES
    rows = q_ref.shape[0]
    lo = _lane_mask(rows, LANES, 0, LANES // 2)
    if mode == "diff":
        lv = lam_ref[...]
        lam = (jnp.exp(jnp.sum(lv[0:1] * lv[1:2], axis=-1, keepdims=True))
               - jnp.exp(jnp.sum(lv[2:3] * lv[3:4], axis=-1, keepdims=True)) + lam_init)

    def load(ref, kind, p):
        if kind == "rows4":
            x = ref[pl.ds(p, ref.shape[0] // 4, stride=4), :]
        elif mode == "gqa":
            x = ref[...]
        else:
            sl = slice(LANES * p, LANES * (p + 1))
            x = ref[sl, :] if kind == "cols" else ref[:, sl]
        return x.astype(BF16), kind == "cols"

    def values(p):
        return [load(seg[-1], seg_kinds[-1], p) for seg, seg_kinds in zip(segs, kinds)]

    def scores(p, j):
        q = q_ref[:, qw * p:qw * (p + 1)]
        ksegs = []
        for seg, seg_kinds in zip(segs, kinds):
            k, transposed = load(seg[0], seg_kinds[0], p)
            if mode == "mla":
                k = jnp.concatenate([k, seg[1][...].astype(BF16)], axis=-1)
            ksegs.append((k, transposed))
        keep = _lane_mask(rows, qw, 64 * j, 64 * (j + 1))
        if mode == "mla":
            keep = keep | _lane_mask(rows, qw, LANES + 32 * j, LANES + 32 * (j + 1))
        return _scores(jnp.where(keep, q, jnp.zeros_like(q)), ksegs)

    def finish(p, outs):
        sl = slice(LANES * p, LANES * (p + 1))
        if mode == "diff":
            d = outs[0] - lam * outs[1]
            o = _rms_full(d, subln_ref[...]) * (1.0 - lam_init)
        else:
            o = jnp.where(lo, outs[0], outs[1])
        o_ref[:, sl] = (o * gate_ref[:, sl].astype(F32)).astype(BF16)

    outs = []

    def head(p, j):
        def rest(ss):
            spare = (64 * (1 - j), 64 * (2 - j)) if mxu_denominator and mode != "diff" else None
            outs.append(_softmax_pv(ss, values(p), ones_lanes=spare))
            if j == 1:
                finish(p, list(outs))
                outs.clear()
        return functools.partial(scores, p, j), rest

    return [head(p, j) for p in range(PAIRS) for j in range(2)]


def _pair_attn_kernel(*refs, **static):
    _run_staggered(_pair_attn_heads(*refs, **static))


def _latent_attention(mode, q, segs, gate, batch, tq, extra=(), lam_init=0.0):
    bq = Q_BLOCK[mode]
    qw = 2 * LANES if mode == "mla" else LANES
    nq = tq // bq
    in_specs = [pl.BlockSpec((bq, qw * PAIRS), lambda b, i: (b * nq + i, 0))]
    args = [q]
    for seg in segs:
        for arr, spec, _ in seg:
            args.append(arr)
            in_specs.append(spec)
    kinds = tuple(tuple(kind for _, _, kind in seg) for seg in segs)
    in_specs.append(pl.BlockSpec((bq, LANES * PAIRS), lambda b, i: (b * nq + i, 0)))
    args.append(gate)
    for arr in extra:
        args.append(arr)
        in_specs.append(pl.BlockSpec(arr.shape, lambda b, i: (0,) * arr.ndim))
    return pl.pallas_call(
        functools.partial(_pair_attn_kernel, mode=mode, kinds=kinds, mxu_denominator=True, lam_init=lam_init),
        out_shape=jax.ShapeDtypeStruct((batch * tq, LANES * PAIRS), BF16),
        grid=(batch, nq),
        in_specs=in_specs,
        out_specs=pl.BlockSpec((bq, LANES * PAIRS), lambda b, i: (b * nq + i, 0)),
        compiler_params=_cparams(),
        name=mode + "_attention",
    )(*args)


def _self_seg(arr, tk):
    return arr, pl.BlockSpec((tk, arr.shape[1]), lambda b, i: (b, 0)), "rows"


def _cache_seg(arr, layer, kind):
    return arr, pl.BlockSpec((None, None) + arr.shape[2:], lambda b, i: (b, layer, 0, 0)), kind


def _na_kernel(q_ref, k_ref, v_ref, ck_ref, cv_ref, bias_ref, gate_ref, o_ref):
    g = pl.program_id(1)
    n_groups = pl.num_programs(1)
    first_row = jnp.clip(NA_GROUP_ROWS * g - NA_ROWS // 2, 0, NA_GROUP_ROWS * n_groups - NA_WIN_ROWS)
    start = pl.multiple_of(first_row * GRID_W, GRID_W)
    win = NA_WIN_ROWS * GRID_W
    kwin = k_ref[pl.ds(start, win), :]
    vwin = v_ref[pl.ds(start, win), :]
    ck = ck_ref[...].astype(BF16)
    cv = cv_ref[...].astype(BF16)
    q = q_ref[...]
    rows = q.shape[0]
    lo = _lane_mask(rows, LANES, 0, LANES // 2)

    def scores(h):
        sl = slice(LANES * (h // 2), LANES * (h // 2 + 1))
        keep = _lane_mask(rows, LANES, 64 * (h % 2), 64 * (h % 2 + 1))
        qm = jnp.where(keep, q[:, sl], jnp.zeros_like(q[:, sl]))
        return _scores(qm, [(kwin[:, sl], False), (ck[sl, :], True)], bias0=bias_ref[h])

    ss = scores(0)
    outs = []
    for h in range(NA_HEADS):
        ss_next = scores(h + 1) if h + 1 < NA_HEADS else None
        sl = slice(LANES * (h // 2), LANES * (h // 2 + 1))
        j = h % 2
        outs.append(_softmax_pv(ss, [(vwin[:, sl], False), (cv[sl, :], True)],
                                ones_lanes=(64 * (1 - j), 64 * (2 - j))))
        if j == 1:
            o = jnp.where(lo, outs[0], outs[1])
            o_ref[:, sl] = (o * gate_ref[:, sl].astype(F32)).astype(BF16)
            outs = []
        ss = ss_next


def _na_attention(q, k, v, cache_k, cache_v, layer, bias, gate, batch, tq):
    bq = NA_GROUP_ROWS * GRID_W
    n_groups = tq // bq
    past = cache_k.shape[3]
    width = NA_HEADS * NA_HD

    def bias_map(b, g):
        return (layer, jnp.where(g == 0, 0, jnp.where(g == n_groups - 1, 2, 1)), 0, 0, 0)

    tok = pl.BlockSpec((bq, width), lambda b, g: (b * n_groups + g, 0))
    whole = pl.BlockSpec((tq, width), lambda b, g: (b, 0))
    cache = pl.BlockSpec((None, None, width, past), lambda b, g: (b, layer, 0, 0))
    return pl.pallas_call(
        _na_kernel,
        out_shape=jax.ShapeDtypeStruct((batch * tq, width), BF16),
        grid=(batch, n_groups),
        in_specs=[tok, whole, whole, cache, cache,
                  pl.BlockSpec((None, None, NA_HEADS, bq, NA_WIN_ROWS * GRID_W), bias_map), tok],
        out_specs=tok,
        compiler_params=_cparams(),
        name="na_attention",
    )(q, k, v, cache_k, cache_v, bias, gate)


def _na_bias_tables(rpb):
    n_dr = 2 * NA_ROWS - 1
    idx = np.full((3, NA_GROUP_ROWS, NA_WIN_ROWS), n_dr, np.int32)
    for a in range(NA_GROUP_ROWS):
        for j in range(NA_WIN_ROWS):
            if j < NA_ROWS:
                idx[0, a, j] = j - a + NA_ROWS - 1
            if a <= j < a + NA_ROWS:
                idx[1, a, j] = j - a + NA_ROWS // 2 - 1
            if j >= NA_WIN_ROWS - NA_ROWS:
                idx[2, a, j] = j - a - (NA_WIN_ROWS - NA_ROWS) + NA_ROWS // 2 - 1
    layers = rpb.shape[0]
    padded = jnp.pad(rpb.astype(F32), ((0, 0), (0, 0), (0, 1), (0, LANES - rpb.shape[3])))
    return pl.pallas_call(
        functools.partial(_na_bias_kernel, idx=idx),
        out_shape=jax.ShapeDtypeStruct((layers, 3, NA_HEADS, NA_GROUP_ROWS * GRID_W, NA_WIN_ROWS * GRID_W), F32),
        grid=(layers, NA_HEADS),
        in_specs=[pl.BlockSpec((None, None, n_dr + 1, LANES), lambda l, h: (l, h, 0, 0))],
        out_specs=pl.BlockSpec((None, 3, None, NA_GROUP_ROWS * GRID_W, NA_WIN_ROWS * GRID_W),
                               lambda l, h: (l, 0, h, 0, 0)),
        compiler_params=_cparams(),
        name="na_bias",
    )(padded)


def _na_bias_kernel(rpb_ref, out_ref, *, idx):
    n_dr = rpb_ref.shape[0] - 1
    qc = lax.broadcasted_iota(jnp.int32, (GRID_W, LANES), 0)
    lane = lax.broadcasted_iota(jnp.int32, (GRID_W, LANES), 1)
    kc = lane & (GRID_W - 1)
    first = jnp.clip(qc - NA_COLS // 2, 0, GRID_W - NA_COLS)
    valid = (kc >= first) & (kc < first + NA_COLS)
    low = lane < GRID_W
    outside = jnp.full((GRID_W, LANES), NEG, F32)

    def table(r):
        if r == n_dr:
            return outside
        row = jnp.broadcast_to(rpb_ref[r:r + 1, :], (GRID_W, LANES))
        lo = pltpu.roll(row, LANES - (NA_COLS - 1), 1, stride=1, stride_axis=0)
        hi = pltpu.roll(row, GRID_W - (NA_COLS - 1), 1, stride=1, stride_axis=0)
        return jnp.where(valid, jnp.where(low, lo, hi) * LOG2E, NEG)

    tables = [table(r) for r in range(n_dr + 1)]
    for t in range(3):
        for a in range(NA_GROUP_ROWS):
            for j in range(0, NA_WIN_ROWS, 2):
                pair = jnp.where(low, tables[int(idx[t, a, j])], tables[int(idx[t, a, j + 1])])
                out_ref[t, a * GRID_W:(a + 1) * GRID_W, j * GRID_W:(j + 2) * GRID_W] = pair


def _out_kernel(oa_ref, ob_ref, x_ref, mod_ref, w_ref, y_ref, *, row_base, tiles_per_batch):
    row = row_base + pl.program_id(0) // tiles_per_batch
    gate = mod_ref[0, pl.ds(row, 1), :][:, 2 * D_MODEL:]
    half = oa_ref.shape[1]
    acc = _dot(oa_ref[...], w_ref[:half, :]) + _dot(ob_ref[...], w_ref[half:, :])
    y_ref[...] = x_ref[...] + gate * acc


def _out_proj(oa, ob, x, mods, layer, w, row_base, tokens_per_batch):
    rows = x.shape[0]
    tm = ROW_TILE
    tiles_per_batch = tokens_per_batch // tm
    return pl.pallas_call(
        functools.partial(_out_kernel, row_base=row_base, tiles_per_batch=tiles_per_batch),
        out_shape=jax.ShapeDtypeStruct((rows, D_MODEL), F32),
        grid=(rows // tm,),
        in_specs=[pl.BlockSpec((tm, oa.shape[1]), lambda i: (i, 0)),
                  pl.BlockSpec((tm, ob.shape[1]), lambda i: (i, 0)),
                  pl.BlockSpec((tm, D_MODEL), lambda i: (i, 0)),
                  pl.BlockSpec((1, 8, 3 * D_MODEL), lambda i: (layer, 0, 0)),
                  _spec(w)],
        out_specs=pl.BlockSpec((tm, D_MODEL), lambda i: (i, 0)),
        compiler_params=_cparams(),
        name="out_proj",
    )(oa, ob, x, mods, _arr(w))


def _out_in_kernel(*refs, proj, rope, row_base, tiles_per_batch):
    oa_ref, ob_ref, x_ref, mod_prev_ref, wout_ref = refs[:5]
    n_proj_in = len(refs) - 5 - 1 - (9 if proj is _in_even_kernel else 8)
    proj_in = refs[5:5 + n_proj_in]
    y_ref = refs[5 + n_proj_in]
    slabs = refs[6 + n_proj_in:]
    _out_kernel(oa_ref, ob_ref, x_ref, mod_prev_ref, wout_ref, y_ref, row_base=row_base,
                tiles_per_batch=tiles_per_batch)
    proj(y_ref, *proj_in, *slabs, rope=rope, states=False, n_alias=0, row_base=row_base,
         tiles_per_batch=tiles_per_batch)


def _out_in_proj(proj, oa, ob, x, mods, layer, w_out, gains, weights, consts, ropes, out_widths, row_base,
                 tokens_per_batch):
    rows = x.shape[0]
    tm = ROW_TILE
    tiles_per_batch = tokens_per_batch // tm
    row = pl.BlockSpec((tm, D_MODEL), lambda i: (i, 0))
    in_specs = [pl.BlockSpec((tm, oa.shape[1]), lambda i: (i, 0)), pl.BlockSpec((tm, ob.shape[1]), lambda i: (i, 0)),
                row, pl.BlockSpec((1, 8, 3 * D_MODEL), lambda i: (layer - 1, 0, 0)), _spec(w_out),
                pl.BlockSpec((1, 8, 3 * D_MODEL), lambda i: (layer, 0, 0)), _full(gains.shape)]
    in_specs += [_spec(a) for a in (*weights, *consts)]
    in_specs += [pl.BlockSpec((tm, LANES), lambda i: (i % tiles_per_batch, 0)) for _ in ropes]
    return pl.pallas_call(
        functools.partial(_out_in_kernel, proj=proj, rope=True, row_base=row_base, tiles_per_batch=tiles_per_batch),
        out_shape=[jax.ShapeDtypeStruct((rows, D_MODEL), F32)]
        + [jax.ShapeDtypeStruct((rows, w), BF16) for w in out_widths],
        grid=(rows // tm,),
        in_specs=in_specs,
        out_specs=[row] + [pl.BlockSpec((tm, w), lambda i: (i, 0)) for w in out_widths],
        compiler_params=_cparams(),
        name="out_" + proj.__name__.strip("_"),
    )(oa, ob, x, mods, _arr(w_out), mods, gains, *map(_arr, weights), *consts, *ropes)


def _ctx_layer_kernel(*refs, even, n_in, n_alias, n_slabs, seq, lam_init, tiles_per_batch):
    in_refs = refs[:n_in]
    it = iter(refs[n_in:])
    wout_ref = next(it)
    if even:
        lam_ref, subln_ref = next(it), next(it)
    alias_refs = [next(it) for _ in range(n_alias)]
    y_ref = next(it)
    state_refs = [next(it) for _ in range(4)]
    slabs = [next(it) for _ in range(n_slabs)]
    oa_ref, ob_ref = next(it), next(it)
    x_ref, mod_ref = in_refs[0], in_refs[1]

    proj_segments = _in_even_segments if even else _in_odd_segments
    segments = proj_segments(*in_refs, *alias_refs, *slabs, *state_refs, rope=False, states=True, n_alias=n_alias,
                             row_base=0, tiles_per_batch=tiles_per_batch)
    heads = []
    for b in range(x_ref.shape[0] // seq):
        def own(ref):
            return ref.at[pl.ds(b * seq, seq), :]
        common = dict(mxu_denominator=False)
        if even:
            qcat, kn, kpe, vm, mg, dq, dk, dv, dg = (own(r) for r in slabs)
            heads += _pair_attn_heads(qcat, kn, kpe, vm, mg, own(oa_ref), mode="mla", kinds=(("rows",) * 3,),
                                      lam_init=0.0, **common)
            heads += _pair_attn_heads(dq, dk, dv, dg, lam_ref, subln_ref, own(ob_ref), mode="diff",
                                      kinds=(("rows",) * 2,), lam_init=lam_init, **common)
        else:
            nq, nk, nv, ng, gq, gk, gv, gg = (own(r) for r in slabs)
            heads += _pair_attn_heads(nq, nk, nv, ng, own(oa_ref), mode="mha", kinds=(("rows",) * 2,),
                                      lam_init=0.0, **common)
            heads += _pair_attn_heads(gq, gk, gv, gg, own(ob_ref), mode="gqa", kinds=(("rows",) * 2,),
                                      lam_init=0.0, **common)
    _run_staggered(segments)
    _run_staggered(heads)
    _out_kernel(oa_ref, ob_ref, x_ref, mod_ref, wout_ref, y_ref, row_base=0, tiles_per_batch=tiles_per_batch)


def _ctx_layer(even, x, mods, layer, gains, weights, consts, w_out, extras, slab_widths, state_tails, state_prev,
               seq, lam_init=0.0):
    rows = x.shape[0]
    tm = ROW_TILE
    slot = layer // 2
    in_specs = [pl.BlockSpec((tm, D_MODEL), lambda i: (i, 0)),
                pl.BlockSpec((1, 8, 3 * D_MODEL), lambda i: (layer, 0, 0)),
                _full(gains.shape)]
    in_specs += [_spec(a) for a in (*weights, *consts)]
    args = [x, mods, gains, *map(_arr, weights), *consts]
    n_in = len(args)
    in_specs += [_spec(a) for a in (w_out, *extras)]
    args += [_arr(w_out), *extras]
    aliases = {}
    if state_prev is not None:
        aliases = {len(args) + j: 1 + j for j in range(len(state_prev))}
        in_specs += [pl.BlockSpec(memory_space=pl.ANY) for _ in state_prev]
        args += list(state_prev)
    out_shape = [jax.ShapeDtypeStruct((rows, D_MODEL), F32)]
    out_specs = [pl.BlockSpec((tm, D_MODEL), lambda i: (i, 0))]
    out_shape += [jax.ShapeDtypeStruct((rows // seq, DEPTH // 2) + tail, F32) for tail in state_tails]
    out_specs += [pl.BlockSpec((tm // seq, None) + tail, lambda i: (i, slot, 0, 0)) for tail in state_tails]
    scratch = [pltpu.VMEM((tm, w), BF16) for w in (*slab_widths, 4 * LANES, 4 * LANES)]
    return pl.pallas_call(
        functools.partial(_ctx_layer_kernel, even=even, n_in=n_in, n_alias=len(aliases), n_slabs=len(slab_widths),
                          seq=seq, lam_init=lam_init, tiles_per_batch=rows // tm),
        out_shape=out_shape,
        grid=(rows // tm,),
        in_specs=in_specs,
        out_specs=out_specs,
        scratch_shapes=scratch,
        input_output_aliases=aliases,
        compiler_params=_cparams(),
        name="ctx_layer_even" if even else "ctx_layer_odd",
    )(*args)


def _layout_in_even_kernel(wt_ref, o_ref):
    kpe0 = Q_LORA + KV_LORA
    o_ref[:, 0:kpe0] = wt_ref[0:kpe0, :].T.astype(BF16)
    kpe = wt_ref[kpe0:kpe0 + MLA_ROPE, :]
    o_ref[:, kpe0:kpe0 + LANES] = jnp.concatenate([kpe] * (LANES // MLA_ROPE), axis=0).T.astype(BF16)
    o_ref[:, kpe0 + LANES:] = wt_ref[kpe0 + MLA_ROPE:, :].T.astype(BF16)


def _layout_in_even(w):
    layers, k, cols = w.shape
    kc = 512
    return pl.pallas_call(
        _layout_in_even_kernel,
        out_shape=jax.ShapeDtypeStruct((layers, k, cols + LANES - MLA_ROPE), BF16),
        grid=(layers, k // kc),
        in_specs=[pl.BlockSpec((None, cols, kc), lambda l, r: (l, 0, r))],
        out_specs=pl.BlockSpec((None, kc, cols + LANES - MLA_ROPE), lambda l, r: (l, r, 0)),
        compiler_params=_cparams(),
        name="layout_in_even",
    )(w.transpose(0, 2, 1))


def _layout_in_odd_kernel(w_ref, o_ref):
    def permuted(base):
        for k in range(GQA_HEADS // 2):
            lo_head, hi_head = GQA_PERM[2 * k], GQA_PERM[2 * k + 1]
            lo = w_ref[:, base + LANES * (lo_head // 2):base + LANES * (lo_head // 2 + 1)]
            hi = w_ref[:, base + LANES * (hi_head // 2):base + LANES * (hi_head // 2 + 1)]
            if lo_head % 2 == 1:
                lo = pltpu.roll(lo, LANES // 2, 1)
            if hi_head % 2 == 0:
                hi = pltpu.roll(hi, LANES // 2, 1)
            lane = lax.broadcasted_iota(jnp.int32, lo.shape, 1)
            o_ref[:, base + LANES * k:base + LANES * (k + 1)] = jnp.where(lane < LANES // 2, lo, hi).astype(BF16)

    o_ref[:, 0:2048] = w_ref[:, 0:2048].astype(BF16)
    permuted(2048)
    o_ref[:, 2560:2816] = w_ref[:, 2560:2816].astype(BF16)
    permuted(2816)


def _layout_w_out_kernel(w_ref, o_ref, *, permute):
    half = w_ref.shape[0] // 2
    o_ref[:half, :] = w_ref[:half, :].astype(BF16)
    if permute:
        for k, head in enumerate(GQA_PERM):
            o_ref[half + GQA_HD * k:half + GQA_HD * (k + 1), :] = (
                w_ref[half + GQA_HD * head:half + GQA_HD * (head + 1), :].astype(BF16))
    else:
        o_ref[half:, :] = w_ref[half:, :].astype(BF16)


def _layout_in_odd(w):
    layers, rows, cols = w.shape
    rb = 512
    return pl.pallas_call(
        _layout_in_odd_kernel,
        out_shape=jax.ShapeDtypeStruct((layers, rows, cols), BF16),
        grid=(layers, rows // rb),
        in_specs=[pl.BlockSpec((None, rb, cols), lambda l, r: (l, r, 0))],
        out_specs=pl.BlockSpec((None, rb, cols), lambda l, r: (l, r, 0)),
        compiler_params=_cparams(),
        name="layout_in_odd",
    )(w)


def _layout_w_out(w, permute):
    layers, rows, cols = w.shape
    return pl.pallas_call(
        functools.partial(_layout_w_out_kernel, permute=permute),
        out_shape=jax.ShapeDtypeStruct((layers, rows, cols), BF16),
        grid=(layers,),
        in_specs=[pl.BlockSpec((None, rows, cols), lambda l: (l, 0, 0))],
        out_specs=pl.BlockSpec((None, rows, cols), lambda l: (l, 0, 0)),
        compiler_params=_cparams(),
        name="layout_w_out",
    )(w)


def _block_diag(width, group):
    idx = np.arange(width) // group
    return jnp.asarray((idx[:, None] == idx[None, :]).astype(np.float32) / group, BF16)


def _rope_tables(t, rot_dim):
    pos = np.arange(t)
    row = (pos // GRID_W).astype(np.float64)
    col = (pos % GRID_W).astype(np.float64)
    n = rot_dim // 2
    inv = ROPE_THETA ** (-np.arange(0, n, 2, dtype=np.float64) / n)
    ang = np.concatenate([row[:, None] * inv, col[:, None] * inv], axis=-1)
    cos = np.concatenate([np.cos(ang), np.cos(ang)], axis=-1)
    sin = np.concatenate([-np.sin(ang), np.sin(ang)], axis=-1)
    reps = LANES // rot_dim
    return (jnp.asarray(np.tile(cos, (1, reps)), F32), jnp.asarray(np.tile(sin, (1, reps)), F32))


def _pad_row(v, width=D_MODEL):
    return jnp.pad(v, (0, width - v.shape[0]))


def _tile_row(v, reps):
    return _pad_row(jnp.tile(v, reps))


def kernel(x_prompt, x_sample, cache_mla_ckv, cache_mla_kpe, cache_diff_k, cache_diff_v, cache_na_k, cache_na_v, cache_gqa_k, cache_gqa_v, c, c_ctx, norm_w, w_mod, b_mod, w_in_even, w_out_even, mla_qa_norm, mla_wqb, mla_kva_norm, mla_wkvb, mla_qn_nope, mla_qn_rope, mla_kn_nope, mla_kn_rope, diff_qn, diff_kn, diff_lq1, diff_lk1, diff_lq2, diff_lk2, diff_subln, w_in_odd, w_out_odd, na_qn, na_kn, na_rpb, gqa_qn, gqa_kn):
    batch, seq, _ = x_prompt.shape
    dec_batch, dec_seq, _ = x_sample.shape
    past = cache_mla_ckv.shape[2]
    n_even, n_odd = w_in_even.shape[0], w_in_odd.shape[0]

    def layers(stack):
        return [_Layer(stack, i) for i in range(stack.shape[0])]

    w_in_e = layers(_layout_in_even(w_in_even))
    w_in_o = layers(_layout_in_odd(w_in_odd))
    w_out_e = layers(_layout_w_out(w_out_even, permute=False))
    w_out_o = layers(_layout_w_out(w_out_odd, permute=True))
    wqb = mla_wqb.reshape(n_even, Q_LORA, MLA_HEADS, MLA_QK)
    wqb = jnp.concatenate([wqb[..., :MLA_NOPE].reshape(n_even, Q_LORA, 4, 2 * MLA_NOPE),
                           wqb[..., MLA_NOPE:].reshape(n_even, Q_LORA, 4, 2 * MLA_ROPE),
                           jnp.zeros((n_even, Q_LORA, 4, LANES - 2 * MLA_ROPE), F32)], axis=-1)
    wqb = layers(wqb.reshape(n_even, Q_LORA, 4 * 2 * LANES).astype(BF16))
    wkvb = mla_wkvb.reshape(n_even, KV_LORA, MLA_HEADS, 2 * MLA_NOPE)
    wkvb = layers(jnp.concatenate([wkvb[..., :MLA_NOPE].reshape(n_even, KV_LORA, 512),
                                   wkvb[..., MLA_NOPE:].reshape(n_even, KV_LORA, 512)], axis=-1).astype(BF16))

    gains_e = [jnp.stack([norm_w[2 * i], _pad_row(mla_qa_norm[i]),
                          _pad_row(jnp.concatenate([jnp.tile(mla_qn_nope[i], 2), jnp.tile(mla_qn_rope[i], 4)])),
                          _pad_row(mla_kva_norm[i]), _tile_row(mla_kn_rope[i], 4), _tile_row(mla_kn_nope[i], 4),
                          _tile_row(diff_qn[i], 4), _tile_row(diff_kn[i], 4)]) for i in range(n_even)]
    gains_o = [jnp.stack([norm_w[2 * i + 1], _tile_row(na_qn[i], 4), _tile_row(na_kn[i], 4),
                          _tile_row(gqa_qn[i], 4), _tile_row(gqa_kn[i], 2),
                          jnp.zeros((D_MODEL,), F32), jnp.zeros((D_MODEL,), F32), jnp.zeros((D_MODEL,), F32)])
               for i in range(n_odd)]
    lam_vecs = [jnp.stack([diff_lq1[i], diff_lk1[i], diff_lq2[i], diff_lk2[i]]) for i in range(n_even)]

    bd64 = _block_diag(256, 64)
    bd32 = _block_diag(LANES, 32)
    bdq = jnp.concatenate([jnp.concatenate([_block_diag(LANES, 64), jnp.zeros((LANES, LANES), BF16)], axis=1),
                           jnp.concatenate([jnp.zeros((LANES, LANES), BF16), _block_diag(LANES, 32)], axis=1)], axis=0)
    c64, s64 = _rope_tables(dec_seq, 64)
    c32, s32 = _rope_tables(dec_seq, MLA_ROPE)
    na_bias = _na_bias_tables(na_rpb)

    cvecs = jnp.concatenate([c_ctx[None, :], c, jnp.zeros((8 - 1 - dec_batch, D_MODEL), F32)], axis=0)
    mods = _modulation(cvecs, w_mod, b_mod)

    cache_kpe = jnp.tile(cache_mla_kpe, (1, 1, 1, 4))
    cache_dk = cache_diff_k.transpose(0, 1, 3, 4, 5, 2).reshape(dec_batch, n_even, 512, past)
    cache_dv = cache_diff_v.reshape(dec_batch, n_even, 4 * past, LANES)
    cache_nk = cache_na_k.transpose(0, 1, 3, 4, 2).reshape(dec_batch, n_odd, 512, past)
    cache_nv = cache_na_v.transpose(0, 1, 3, 4, 2).reshape(dec_batch, n_odd, 512, past)
    cache_gk = cache_gqa_k.transpose(0, 1, 3, 4, 2).reshape(dec_batch, n_odd, LANES, past)
    cache_gv = cache_gqa_v.transpose(0, 1, 3, 4, 2).reshape(dec_batch, n_odd, LANES, past)

    even_widths = (1024, 512, LANES, 512, 512, 512, 512, 512, 512)
    odd_widths = (512, 512, 512, 512, 512, LANES, LANES, 512)
    even_states = ((seq, KV_LORA), (MLA_ROPE, seq), (512, seq), (4 * seq, LANES))
    odd_states = ((512, seq), (512, seq), (LANES, seq), (LANES, seq))

    def lam_init(l):
        return 0.8 - 0.6 * math.exp(-0.3 * l)

    def context_pass(x):
        states = [None, None]
        for l in range(DEPTH):
            i = l // 2
            if l % 2 == 0:
                x, *states[0] = _ctx_layer(True, x, mods, l, gains_e[i], (w_in_e[i], wqb[i], wkvb[i]),
                                           (bd64, bd32, bdq), w_out_e[i], (lam_vecs[i], diff_subln[i][None, :]),
                                           even_widths, even_states, states[0], seq, lam_init(l))
            else:
                x, *states[1] = _ctx_layer(False, x, mods, l, gains_o[i], (w_in_o[i],), (bd64,), w_out_o[i], (),
                                           odd_widths, odd_states, states[1], seq)
        return x, states

    def latent_pass(x, nb, t):
        row_base = 1
        oa = ob = w_out_prev = None
        for l in range(DEPTH):
            i = l // 2
            if l % 2 == 0:
                proj_args = (gains_e[i], (w_in_e[i], wqb[i], wkvb[i]), (bd64, bd32, bdq), (c64, s64, c32, s32),
                             even_widths)
                proj = _in_even_kernel
            else:
                proj_args = (gains_o[i], (w_in_o[i],), (bd64,), (c64, s64), odd_widths)
                proj = _in_odd_kernel
            if l == 0:
                outs = _in_proj(proj, x, mods, l, *proj_args, row_base, t)
            else:
                x, *outs = _out_in_proj(proj, oa, ob, x, mods, l, w_out_prev, *proj_args, row_base, t)
            if l % 2 == 0:
                qcat, kn, kpe, vm, mg, dq, dk, dv, dg = outs
                kn_c, vm_c = _mla_cache_kv(cache_mla_ckv[:, i].reshape(nb * past, KV_LORA), wkvb[i], bd64, gains_e[i])
                mla_segs = [(_self_seg(kn, t), _self_seg(kpe, t), _self_seg(vm, t)),
                            (_self_seg(kn_c, past), _cache_seg(cache_kpe, i, "rows"), _self_seg(vm_c, past))]
                diff_segs = [(_self_seg(dk, t), _self_seg(dv, t)),
                             (_cache_seg(cache_dk, i, "cols"), _cache_seg(cache_dv, i, "rows4"))]
                oa = _latent_attention("mla", qcat, mla_segs, mg, nb, t)
                ob = _latent_attention("diff", dq, diff_segs, dg, nb, t,
                                       extra=(lam_vecs[i], diff_subln[i][None, :]), lam_init=lam_init(l))
                w_out_prev = w_out_e[i]
            else:
                nq, nk, nv, ng, gq, gk, gv, gg = outs
                oa = _na_attention(nq, nk, nv, cache_nk, cache_nv, i, na_bias, ng, nb, t)
                gqa_segs = [(_self_seg(gk, t), _self_seg(gv, t)),
                            (_cache_seg(cache_gk, i, "cols"), _cache_seg(cache_gv, i, "cols"))]
                ob = _latent_attention("gqa", gq, gqa_segs, gg, nb, t)
                w_out_prev = w_out_o[i]
        return _out_proj(oa, ob, x, mods, DEPTH - 1, w_out_prev, row_base, t)

    y_prompt, st = context_pass(x_prompt.reshape(batch * seq, D_MODEL))
    y_sample = latent_pass(x_sample.reshape(dec_batch * dec_seq, D_MODEL), dec_batch, dec_seq)

    def token_major(a, heads):
        a = a.reshape((batch, n_even) + heads + (a.shape[2] // math.prod(heads), seq))
        return jnp.moveaxis(a, -1, 2)

    ckv, kpe_t, dk_t, dv4 = st[0]
    nk_t, nv_t, gk_t, gv_t = st[1]
    return (y_prompt.reshape(batch, seq, D_MODEL), y_sample.reshape(dec_batch, dec_seq, D_MODEL),
            ckv, token_major(kpe_t, ()), token_major(dk_t, (DIFF_HEADS, 2)),
            dv4.reshape(batch, n_even, seq, DIFF_HEADS, 2 * DIFF_HD),
            token_major(nk_t, (NA_HEADS,)), token_major(nv_t, (NA_HEADS,)),
            token_major(gk_t, (GQA_KV,)), token_major(gv_t, (GQA_KV,)))
```
